```python
import math
import jax, jax.numpy as jnp
from jax import lax
import numpy as np

D_MODEL = 4096
BATCH = 4
SEQ = 2048
DEPTH = 2
DEC_BATCH = 8
DEC_SEQ = 8
PAST_LEN = 16384
PAGE_SIZE = 128

N_MIXERS = 2
N_A_LAYERS = (DEPTH + 1) // 2
N_B_LAYERS = DEPTH // 2
EPS = 1e-6

A_HEADS = 32
A_HEAD_DIM = D_MODEL // A_HEADS
A_KV = 4
A_HPG = A_HEADS // A_KV
A_WIDTH = A_HEADS * A_HEAD_DIM
A_KV_WIDTH = A_KV * A_HEAD_DIM
CMP_STRIDE = 16
CMP_LEN = 2 * CMP_STRIDE
SEL_BLOCK = 64
SEL_RATIO = SEL_BLOCK // CMP_STRIDE
SEL_TOPK = 16
WINDOW = 512
Q_BLOCK = 128
FORCE_SCORE = 1e4
NEG = -1e30
TINY = 1e-30
A_SPLIT_SIZES = (A_WIDTH,) + (A_KV_WIDTH,) * 6 + (A_WIDTH,) * 3 + (3 * A_HEADS,)
A_IN = sum(A_SPLIT_SIZES)
A_SPLITS = tuple(int(v) for v in np.cumsum(A_SPLIT_SIZES)[:-1])

R_HEADS = 16
R_DK = D_MODEL // R_HEADS
R_DV = 2 * R_DK
R_KW = R_HEADS * R_DK
R_VW = R_HEADS * R_DV
R_IN = 2 * R_KW + 2 * R_VW
R_SPLITS = (R_KW, 2 * R_KW, 2 * R_KW + R_VW)
R_CHUNK = 128
ROPE_BASE = 10000.0

N_BUCKETS = 32
MAX_DISTANCE = 1024

kernel_name = 'nsa_retention_hybrid_step'


def rmsnorm(x, g=None):
    xf = x.astype(jnp.float32)
    y = xf * lax.rsqrt(jnp.mean(jnp.square(xf), axis=-1, keepdims=True) + EPS)
    if g is not None:
        y = y * g.astype(jnp.float32)
    return y.astype(x.dtype)


def ada_pre(x, c, g, w, b):
    shift, scale, gate = jnp.split(jax.nn.silu(c) @ w + b, 3, axis=-1)
    h = rmsnorm(x, g) * (1 + scale[:, None]) + shift[:, None]
    return h, gate[:, None]


def t5_bucket(dist):
    n = jnp.maximum(dist, 0)
    exact = N_BUCKETS // 2
    logv = jnp.log(jnp.maximum(n, 1).astype(jnp.float32) / exact) / math.log(MAX_DISTANCE / exact)
    large = jnp.minimum(exact + (logv * (N_BUCKETS - exact)).astype(jnp.int32), N_BUCKETS - 1)
    return jnp.where(n < exact, n, large)


def masked_softmax(s, mask):
    s = jnp.where(mask, s, NEG)
    m = jnp.max(s, axis=-1, keepdims=True)
    e = jnp.where(mask, jnp.exp(s - m), 0.0)
    return e / jnp.maximum(e.sum(-1, keepdims=True), TINY)


def rotary(x, pos):
    half = x.shape[-1] // 2
    inv = jnp.power(ROPE_BASE, -jnp.arange(half, dtype=jnp.float32) / half)
    ang = pos.astype(jnp.float32)[:, None] * inv[None, :]
    cos = jnp.cos(ang)[None, :, None, :]
    sin = jnp.sin(ang)[None, :, None, :]
    xf = x.astype(jnp.float32)
    x1, x2 = xf[..., :half], xf[..., half:]
    return jnp.concatenate([x1 * cos - x2 * sin, x1 * sin + x2 * cos], axis=-1)


def compress(x, pe, w1, w2):
    B, L, G, hd = x.shape
    n_ch = -(-L // CMP_STRIDE)
    x = jnp.pad(x, ((0, 0), (0, n_ch * CMP_STRIDE - L), (0, 0), (0, 0))).reshape(B, n_ch, CMP_STRIDE, G, hd)
    blk = jnp.concatenate([x[:, :-1], x[:, 1:]], axis=2) + pe[:, None, :]
    flat = jnp.swapaxes(blk, 2, 3).reshape(B, n_ch - 1, G, CMP_LEN * hd)
    return jax.nn.silu(flat @ w1) @ w2


def nsa_mixer(h, q0, past_kv, win_buf, rel_bias, w_in, w_out, cmp_pe, cmp_w1, cmp_w2, qk_g):
    B, T, _ = h.shape
    hd = A_HEAD_DIM
    scale = hd ** -0.5
    q, kc, vc, ks, vs, kw, vw, zc, zs, zw, gl = jnp.split(h @ w_in, A_SPLITS, axis=-1)
    q = rmsnorm(q.reshape(B, T, A_KV, A_HPG, hd), qk_g[0])
    kvh = lambda t: t.reshape(B, T, A_KV, hd)
    ks = rmsnorm(kvh(ks), qk_g[2])
    kw = rmsnorm(kvh(kw), qk_g[3])
    new_rows = jnp.stack([kvh(kc), kvh(vc), ks, kvh(vs)], axis=2)
    full = new_rows if past_kv is None else jnp.concatenate([past_kv, new_rows], axis=1)
    L = full.shape[1]
    kcmp = rmsnorm(compress(full[:, :, 0], cmp_pe[0], cmp_w1[0], cmp_w2[0]), qk_g[1])
    vcmp = compress(full[:, :, 1], cmp_pe[1], cmp_w1[1], cmp_w2[1])
    NC = kcmp.shape[1]
    cmp_end = jnp.arange(NC) * CMP_STRIDE + (CMP_LEN - 1)
    n_sel = -(-L // SEL_BLOCK)
    k_top = min(SEL_TOPK, n_sel)
    sel = jnp.pad(full[:, :, 2:4], ((0, 0), (0, n_sel * SEL_BLOCK - L), (0, 0), (0, 0), (0, 0)))
    sel = sel.reshape(B, n_sel, SEL_BLOCK, 2, A_KV, hd).transpose(3, 0, 4, 1, 2, 5)
    new_win = jnp.stack([kw, kvh(vw)], axis=2)
    if win_buf is None:
        win_all = new_win
        win_state = new_win[:, T - min(WINDOW, T):]
    else:
        win_all = jnp.concatenate([win_buf, new_win], axis=1)
        win_state = win_all[:, T:]
    n_pre = win_all.shape[1] - T
    QB = min(Q_BLOCK, T)
    n_blk = -(-T // QB)
    Tp = n_blk * QB
    win_all = jnp.pad(win_all, ((0, 0), (WINDOW - n_pre, Tp - T), (0, 0), (0, 0), (0, 0)))
    qpad = jnp.pad(q, ((0, 0), (0, Tp - T), (0, 0), (0, 0), (0, 0)))
    rb = rel_bias.reshape(N_BUCKETS, A_KV, A_HPG)
    bi = jnp.arange(B)[:, None, None, None]
    gi = jnp.arange(A_KV)[None, None, :, None]
    jj = jnp.arange(n_sel)

    def block(i0):
        qb = lax.dynamic_slice_in_dim(qpad, i0, QB, axis=1)
        qpos = q0 + i0 + jnp.arange(QB)
        s = jnp.einsum('bqghd,bcgd->bqghc', qb, kcmp, preferred_element_type=jnp.float32) * scale
        s = s + rb[t5_bucket(qpos[:, None] - cmp_end[None, :])].transpose(0, 2, 3, 1)
        p_c = masked_softmax(s, (cmp_end[None, :] <= qpos[:, None])[None, :, None, None, :])
        o_c = jnp.einsum('bqghc,bcgd->bqghd', p_c.astype(vcmp.dtype), vcmp)
        imp = p_c.sum(3)
        ip = jnp.pad(imp, ((0, 0), (0, 0), (0, 0), (1, SEL_RATIO * n_sel - NC)))
        imp_sel = (ip[..., 1:] + ip[..., :-1]).reshape(B, QB, A_KV, n_sel, SEL_RATIO).sum(-1)
        cur = (qpos // SEL_BLOCK)[:, None, None]
        forced = (jj == 0) | (jj == cur) | (jj == cur - 1)
        valid = jj * SEL_BLOCK <= qpos[:, None, None]
        score = jnp.where(valid, jnp.where(forced, FORCE_SCORE, imp_sel), -1.0)
        _, idx = lax.top_k(score, k_top)
        kb = sel[0][bi, gi, idx].reshape(B, QB, A_KV, k_top * SEL_BLOCK, hd)
        vb = sel[1][bi, gi, idx].reshape(B, QB, A_KV, k_top * SEL_BLOCK, hd)
        kpos = (idx[..., None] * SEL_BLOCK + jnp.arange(SEL_BLOCK)).reshape(B, QB, A_KV, k_top * SEL_BLOCK)
        dsel = qpos[None, :, None, None] - kpos
        s = jnp.einsum('bqghd,bqgnd->bqghn', qb, kb, preferred_element_type=jnp.float32) * scale
        s = s + jnp.moveaxis(rb[t5_bucket(dsel), gi], -1, 3)
        p_s = masked_softmax(s, (dsel >= 0)[:, :, :, None, :])
        o_s = jnp.einsum('bqghn,bqgnd->bqghd', p_s.astype(vb.dtype), vb)
        wk = lax.dynamic_slice_in_dim(win_all, i0, WINDOW + QB, axis=1)
        wpos = q0 - WINDOW + i0 + jnp.arange(WINDOW + QB)
        d = qpos[:, None] - wpos[None, :]
        wmask = (wpos[None, :] >= 0) & (d >= 0) & (d < WINDOW)
        s = jnp.einsum('bqghd,bkgd->bqghk', qb, wk[:, :, 0], preferred_element_type=jnp.float32) * scale
        s = s + rb[t5_bucket(d)].transpose(0, 2, 3, 1)
        p_w = masked_softmax(s, wmask[None, :, None, None, :])
        o_w = jnp.einsum('bqghk,bkgd->bqghd', p_w.astype(wk.dtype), wk[:, :, 1])
        return jnp.stack([o_c, o_s, o_w], axis=2)

    o = lax.map(block, jnp.arange(n_blk) * QB)
    o = jnp.moveaxis(o, 0, 1).reshape(B, Tp, 3, A_HEADS, hd)[:, :T]
    z = jnp.stack([zc, zs, zw], axis=2).reshape(B, T, 3, A_HEADS, hd)
    gates = jax.nn.sigmoid(gl.reshape(B, T, 3, A_HEADS, 1))
    mixed = (gates * o.astype(z.dtype) * jax.nn.silu(z)).sum(2).reshape(B, T, A_WIDTH)
    return mixed @ w_out, new_rows, win_state


def retention_mixer(h, q0, state0, w_in, w_out):
    B, T, _ = h.shape
    q, k, v, g = jnp.split(h @ w_in, R_SPLITS, axis=-1)
    pos = q0 + jnp.arange(T)
    q = rotary(q.reshape(B, T, R_HEADS, R_DK), pos)
    k = rotary(k.reshape(B, T, R_HEADS, R_DK), pos) * (R_DK ** -0.5)
    v = v.reshape(B, T, R_HEADS, R_DV).astype(jnp.float32)
    C = R_CHUNK if T % R_CHUNK == 0 else T
    n = T // C
    chunks = lambda t: t.reshape(B, n, C, R_HEADS, t.shape[-1]).transpose(1, 0, 3, 2, 4)
    log_g = jnp.log1p(-jnp.exp2(-5.0 - jnp.arange(R_HEADS, dtype=jnp.float32)))
    ii = jnp.arange(C, dtype=jnp.float32)
    rel = ii[:, None] - ii[None, :]
    causal = rel >= 0
    dmat = jnp.where(causal, jnp.exp(jnp.where(causal, rel, 0.0) * log_g[:, None, None]), 0.0)
    q_dec = jnp.exp((ii + 1.0) * log_g[:, None])
    k_dec = jnp.exp((C - 1.0 - ii) * log_g[:, None])
    c_dec = jnp.exp(C * log_g)

    def step(S, inp):
        qc, kc, vc = inp
        att = jnp.einsum('bhid,bhjd->bhij', qc, kc) * dmat
        o = jnp.einsum('bhij,bhjv->bhiv', att, vc) + jnp.einsum('bhid,bhdv->bhiv', qc, S) * q_dec[:, :, None]
        S = S * c_dec[:, None, None] + jnp.einsum('bhjd,bhjv->bhdv', kc * k_dec[:, :, None], vc)
        return S, o

    S, o = lax.scan(step, state0.astype(jnp.float32), (chunks(q), chunks(k), chunks(v)))
    o = o.transpose(1, 0, 3, 2, 4).reshape(B, T, R_HEADS, R_DV)
    o = rmsnorm(o).reshape(B, T, R_VW).astype(h.dtype)
    return (o * jax.nn.silu(g)) @ w_out, S.astype(state0.dtype)


def setup_inputs(seed: int = 0) -> dict:
    key = jax.random.key(seed)
    ks = jax.random.split(key, 20)
    n_pages = PAST_LEN // PAGE_SIZE
    n_pool = (5 * DEC_BATCH * n_pages + 3) // 4
    w_buf = min(WINDOW, PAST_LEN)
    nrm = lambda k, shape, s: s * jax.random.normal(k, shape, jnp.float32)
    page_table = jax.random.permutation(ks[7], n_pool)[: DEC_BATCH * n_pages].reshape(DEC_BATCH, n_pages).astype(jnp.int32)
    return {
        'x_prompt': nrm(ks[0], (BATCH, SEQ, D_MODEL), 1.0),
        'x_sample': nrm(ks[1], (DEC_BATCH, DEC_SEQ, D_MODEL), 1.0),
        'c_prompt': nrm(ks[2], (BATCH, D_MODEL), 1.0),
        'c_sample': nrm(ks[3], (DEC_BATCH, D_MODEL), 1.0),
        'cache_nsa_kv': nrm(ks[4], (N_A_LAYERS, n_pool, PAGE_SIZE, 4, A_KV, A_HEAD_DIM), 1.0),
        'cache_nsa_win': nrm(ks[5], (N_A_LAYERS, DEC_BATCH, w_buf, 2, A_KV, A_HEAD_DIM), 1.0),
        'state_ret': nrm(ks[6], (N_B_LAYERS, DEC_BATCH, R_HEADS, R_DK, R_DV), 0.1),
        'page_table': page_table,
        'norm_g': 1.0 + nrm(ks[8], (DEPTH, D_MODEL), 0.1),
        'ada_w': nrm(ks[9], (DEPTH, D_MODEL, 3 * D_MODEL), 0.5 * D_MODEL ** -0.5),
        'ada_b': nrm(ks[10], (DEPTH, 3 * D_MODEL), 0.02),
        'rel_bias': nrm(ks[11], (N_BUCKETS, A_HEADS), 0.3),
        'a_w_in': nrm(ks[12], (N_A_LAYERS, D_MODEL, A_IN), D_MODEL ** -0.5),
        'a_w_out': nrm(ks[13], (N_A_LAYERS, A_WIDTH, D_MODEL), A_WIDTH ** -0.5),
        'a_cmp_pe': nrm(ks[14], (N_A_LAYERS, 2, CMP_LEN, A_HEAD_DIM), 0.02),
        'a_cmp_w1': nrm(ks[15], (N_A_LAYERS, 2, CMP_LEN * A_HEAD_DIM, A_HEAD_DIM), (CMP_LEN * A_HEAD_DIM) ** -0.5),
        'a_cmp_w2': nrm(ks[16], (N_A_LAYERS, 2, A_HEAD_DIM, A_HEAD_DIM), A_HEAD_DIM ** -0.5),
        'a_qk_g': 1.0 + nrm(ks[17], (N_A_LAYERS, 4, A_HEAD_DIM), 0.1),
        'r_w_in': nrm(ks[18], (N_B_LAYERS, D_MODEL, R_IN), D_MODEL ** -0.5),
        'r_w_out': nrm(ks[19], (N_B_LAYERS, R_VW, D_MODEL), R_VW ** -0.5),
    }


def reference(x_prompt, x_sample, c_prompt, c_sample, cache_nsa_kv, cache_nsa_win, state_ret, page_table,
              norm_g, ada_w, ada_b, rel_bias, a_w_in, a_w_out, a_cmp_pe, a_cmp_w1, a_cmp_w2, a_qk_g,
              r_w_in, r_w_out):
    past_len = page_table.shape[1] * PAGE_SIZE
    xp, xs = x_prompt, x_sample
    kv_p, kv_s, win_p, win_s, ret_p, ret_s = [], [], [], [], [], []
    for layer in range(DEPTH):
        hp, gp = ada_pre(xp, c_prompt, norm_g[layer], ada_w[layer], ada_b[layer])
        hs, gs = ada_pre(xs, c_sample, norm_g[layer], ada_w[layer], ada_b[layer])
        li = layer // N_MIXERS
        if layer % N_MIXERS == 0:
            wa = (rel_bias, a_w_in[li], a_w_out[li], a_cmp_pe[li], a_cmp_w1[li], a_cmp_w2[li], a_qk_g[li])
            yp, rows_p, wst_p = nsa_mixer(hp, 0, None, None, *wa)
            past = cache_nsa_kv[li][page_table]
            past = past.reshape(past.shape[0], past_len, 4, A_KV, A_HEAD_DIM)
            ys, rows_s, wst_s = nsa_mixer(hs, past_len, past, cache_nsa_win[li], *wa)
            kv_p.append(rows_p)
            kv_s.append(rows_s)
            win_p.append(wst_p)
            win_s.append(wst_s)
        else:
            s0 = jnp.zeros((xp.shape[0], R_HEADS, R_DK, R_DV), jnp.float32)
            yp, sp = retention_mixer(hp, 0, s0, r_w_in[li], r_w_out[li])
            ys, ss = retention_mixer(hs, past_len, state_ret[li], r_w_in[li], r_w_out[li])
            ret_p.append(sp)
            ret_s.append(ss)
        xp = xp + gp * yp
        xs = xs + gs * ys
    return (xp, xs, jnp.stack(kv_p), jnp.stack(kv_s), jnp.stack(win_p), jnp.stack(win_s), jnp.stack(ret_p), jnp.stack(ret_s))
```

```python
import functools
import math

import numpy as np
import jax
import jax.numpy as jnp
from jax import lax
from jax.experimental import pallas as pl
from jax.experimental.pallas import tpu as pltpu

EPS = 1e-6
HD = 128
A_KV = 4
A_HPG = 8
A_HEADS = A_KV * A_HPG
CMP_STRIDE = 16
CMP_LEN = 32
SEL_BLOCK = 64
SEL_RATIO = SEL_BLOCK // CMP_STRIDE
SEL_TOPK = 16
WINDOW = 512
Q_BLOCK = 128
FORCE_SCORE = 1e4
NEG = -1e30
TINY = 1e-30
N_BUCKETS = 32
MAX_DISTANCE = 1024
R_HEADS = 16
R_CHUNK = 128
ROPE_BASE = 10000.0
PAGE = 128
PAGES_PER_STEP = 16

LANE = 128
VMEM_LIMIT = 56 * 1024 * 1024

F32 = jnp.float32
BF16 = jnp.bfloat16


def _cparams(sem):
    return pltpu.CompilerParams(dimension_semantics=sem, vmem_limit_bytes=VMEM_LIMIT)


def _dot(a, b):
    return jnp.dot(a, b, preferred_element_type=F32)


def _dot_nt(a, b):
    return lax.dot_general(a, b, (((1,), (1,)), ((), ())), preferred_element_type=F32)


def _dot_tn(a, b):
    return lax.dot_general(a, b, (((0,), (0,)), ((), ())), preferred_element_type=F32)


def _silu(x):
    return x / (1.0 + jnp.exp(-x))


def _sigmoid(x):
    return 1.0 / (1.0 + jnp.exp(-x))


def _split3_dot(x, m_bf16):
    hi = x.astype(BF16)
    r1 = x - hi.astype(F32)
    mid = r1.astype(BF16)
    lo = (r1 - mid.astype(F32)).astype(BF16)
    return _dot(hi, m_bf16) + _dot(mid, m_bf16) + _dot(lo, m_bf16)


def _ada_kernel(c_ref, w_ref, b_ref, o_ref):
    a = _silu(c_ref[...]).astype(BF16)
    o_ref[...] = _dot(a, w_ref[...].astype(BF16)) + b_ref[...]


def ada_modulation(c_all, ada_w, ada_b):
    depth, d, n = ada_w.shape
    rows = c_all.shape[0]
    tn = 512
    return pl.pallas_call(
        _ada_kernel,
        grid=(depth, n // tn),
        in_specs=[
            pl.BlockSpec((rows, d), lambda l, j: (0, 0)),
            pl.BlockSpec((None, d, tn), lambda l, j: (l, 0, j)),
            pl.BlockSpec((None, 1, tn), lambda l, j: (l, 0, j)),
        ],
        out_specs=pl.BlockSpec((None, rows, tn), lambda l, j: (l, 0, j)),
        out_shape=jax.ShapeDtypeStruct((depth, rows, n), F32),
        compiler_params=_cparams(("arbitrary", "arbitrary")),
        name="ada_modulation",
    )(c_all, ada_w, ada_b.reshape(depth, 1, n))


def _norm_mod_kernel(x_ref, g_ref, sc_ref, sh_ref, o_ref):
    x = x_ref[...]
    y = x * lax.rsqrt(jnp.mean(x * x, axis=-1, keepdims=True) + EPS)
    y = y * g_ref[...]
    o_ref[...] = (y * (1.0 + sc_ref[...]) + sh_ref[...]).astype(o_ref.dtype)


def norm_modulate(x, g, scale, shift):
    b, t, d = x.shape
    tt = min(t, 256)
    out = pl.pallas_call(
        _norm_mod_kernel,
        grid=(b, t // tt),
        in_specs=[
            pl.BlockSpec((None, tt, d), lambda i, j: (i, j, 0)),
            pl.BlockSpec((1, d), lambda i, j: (0, 0)),
            pl.BlockSpec((None, 1, d), lambda i, j: (i, 0, 0)),
            pl.BlockSpec((None, 1, d), lambda i, j: (i, 0, 0)),
        ],
        out_specs=pl.BlockSpec((None, tt, d), lambda i, j: (i, j, 0)),
        out_shape=jax.ShapeDtypeStruct((b, t, d), BF16),
        compiler_params=_cparams(("arbitrary", "arbitrary")),
        name="norm_modulate",
    )(x, g.reshape(1, d), scale.reshape(b, 1, d), shift.reshape(b, 1, d))
    return out.reshape(b * t, d)


def _proj_kernel(xp_ref, xs_ref, w_ref, op_ref, os_ref, wb_ref):
    @pl.when(pl.program_id(1) == 0)
    def _():
        wb_ref[...] = w_ref[...].astype(BF16)
        os_ref[...] = _dot(xs_ref[...], wb_ref[...])

    op_ref[...] = _dot(xp_ref[...], wb_ref[...])


def _proj_res_kernel(xp_ref, xs_ref, w_ref, rp_ref, gp_ref, rs_ref, gs_ref, op_ref, os_ref, wb_ref):
    @pl.when(pl.program_id(1) == 0)
    def _():
        wb_ref[...] = w_ref[...].astype(BF16)
        os_ref[...] = rs_ref[...] + gs_ref[...] * _dot(xs_ref[...], wb_ref[...])

    op_ref[...] = rp_ref[...] + gp_ref[...] * _dot(xp_ref[...], wb_ref[...])


def _proj_tiles(mp, k, n):
    tm = 1024 if k <= 4096 else 512
    tn = 512 if k <= 4096 else 256
    tm = min(tm, mp)
    while mp % tm:
        tm //= 2
    tn = min(tn, n)
    while n % tn:
        tn //= 2
    return tm, tn


def project(xp, xs, w, n_cols=None):
    mp, k = xp.shape
    ms = xs.shape[0]
    n = w.shape[1] if n_cols is None else n_cols
    tm, tn = _proj_tiles(mp, k, n)
    return pl.pallas_call(
        _proj_kernel,
        grid=(n // tn, mp // tm),
        in_specs=[
            pl.BlockSpec((tm, k), lambda j, i: (i, 0)),
            pl.BlockSpec((ms, k), lambda j, i: (0, 0)),
            pl.BlockSpec((k, tn), lambda j, i: (0, j)),
        ],
        out_specs=[
            pl.BlockSpec((tm, tn), lambda j, i: (i, j)),
            pl.BlockSpec((ms, tn), lambda j, i: (0, j)),
        ],
        out_shape=[jax.ShapeDtypeStruct((mp, n), F32), jax.ShapeDtypeStruct((ms, n), F32)],
        scratch_shapes=[pltpu.VMEM((k, tn), BF16)],
        compiler_params=_cparams(("arbitrary", "arbitrary")),
        name="project",
    )(xp, xs, w)


def project_residual(xp, xs, w, res_p, gate_p, res_s, gate_s, rows_per_batch):
    mp, k = xp.shape
    ms = xs.shape[0]
    n = w.shape[1]
    tm, tn = _proj_tiles(mp, k, n)
    tm = min(tm, rows_per_batch)
    tpb = rows_per_batch // tm
    nb = gate_p.shape[0]
    return pl.pallas_call(
        _proj_res_kernel,
        grid=(n // tn, mp // tm),
        in_specs=[
            pl.BlockSpec((tm, k), lambda j, i: (i, 0)),
            pl.BlockSpec((ms, k), lambda j, i: (0, 0)),
            pl.BlockSpec((k, tn), lambda j, i: (0, j)),
            pl.BlockSpec((tm, tn), lambda j, i: (i, j)),
            pl.BlockSpec((None, 1, tn), lambda j, i: (i // tpb, 0, j)),
            pl.BlockSpec((ms, tn), lambda j, i: (0, j)),
            pl.BlockSpec((ms, tn), lambda j, i: (0, j)),
        ],
        out_specs=[
            pl.BlockSpec((tm, tn), lambda j, i: (i, j)),
            pl.BlockSpec((ms, tn), lambda j, i: (0, j)),
        ],
        out_shape=[jax.ShapeDtypeStruct((mp, n), F32), jax.ShapeDtypeStruct((ms, n), F32)],
        scratch_shapes=[pltpu.VMEM((k, tn), BF16)],
        compiler_params=_cparams(("arbitrary", "arbitrary")),
        name="project_residual",
    )(xp, xs, w, res_p, gate_p.reshape(nb, 1, n), res_s, gate_s)


def _nsa_prep_kernel(q_ref, a_ref, b_ref, c_ref, g_ref, qn_ref, rows_ref, win_ref):
    def hnorm(x, g):
        return x * lax.rsqrt(jnp.mean(x * x, axis=-1, keepdims=True) + EPS) * g

    g = g_ref[...]
    for j in range(A_HEADS):
        sl = slice(j * HD, (j + 1) * HD)
        qn_ref[:, sl] = hnorm(q_ref[:, sl], g[0:1]).astype(qn_ref.dtype)
    kvw = A_KV * HD
    rows_ref[:, : 2 * kvw] = a_ref[...]
    rows_ref[:, 3 * kvw:] = b_ref[:, kvw:]
    win_ref[:, kvw:] = c_ref[:, kvw:]
    for j in range(A_KV):
        sl = slice(j * HD, (j + 1) * HD)
        rows_ref[:, 2 * kvw + j * HD: 2 * kvw + (j + 1) * HD] = hnorm(b_ref[:, sl], g[2:3])
        win_ref[:, sl] = hnorm(c_ref[:, sl], g[3:4])


def nsa_prep(proj, qk_g):
    m = proj.shape[0]
    tm = min(m, 256)
    qw = A_HEADS * HD
    kv2 = 2 * A_KV * HD
    base = qw // kv2
    return pl.pallas_call(
        _nsa_prep_kernel,
        grid=(m // tm,),
        in_specs=[
            pl.BlockSpec((tm, qw), lambda i: (i, 0)),
            pl.BlockSpec((tm, kv2), lambda i: (i, base)),
            pl.BlockSpec((tm, kv2), lambda i: (i, base + 1)),
            pl.BlockSpec((tm, kv2), lambda i: (i, base + 2)),
            pl.BlockSpec((4, HD), lambda i: (0, 0)),
        ],
        out_specs=[
            pl.BlockSpec((tm, qw), lambda i: (i, 0)),
            pl.BlockSpec((tm, 2 * kv2), lambda i: (i, 0)),
            pl.BlockSpec((tm, kv2), lambda i: (i, 0)),
        ],
        out_shape=[
            jax.ShapeDtypeStruct((m, qw), BF16),
            jax.ShapeDtypeStruct((m, 2 * kv2), F32),
            jax.ShapeDtypeStruct((m, kv2), F32),
        ],
        compiler_params=_cparams(("arbitrary",)),
        name="nsa_prep",
    )(proj, proj, proj, proj, qk_g)


NCK = 2 * A_KV


def _cmp_a_kernel(pt_ref, *refs, n_pages):
    del pt_ref
    pages = refs[:n_pages]
    w_ref = refs[n_pages]
    o_ref = refs[n_pages + 1]
    cpp = PAGE // CMP_STRIDE
    rows = n_pages * cpp * NCK
    acc = None
    for tp in range(CMP_STRIDE // 2):
        pieces = []
        for pg in pages:
            for ch in range(cpp):
                t0 = ch * CMP_STRIDE + 2 * tp
                pieces.append(jnp.concatenate([pg[t0], pg[t0 + 1]], axis=1))
        lhs = jnp.concatenate(pieces, axis=0).astype(BF16)
        d = _dot(lhs, w_ref[tp])
        acc = d if acc is None else acc + d
    is_k = (lax.broadcasted_iota(jnp.int32, (rows, 2 * HD), 0) & (NCK - 1)) < A_KV
    out = jnp.where(is_k, acc[:, :2 * HD], acc[:, 2 * HD:])
    o_ref[...] = out.reshape(n_pages * cpp, NCK, 2 * HD)


def compress_stage_a(store, page_table, w_pairs, n_pages):
    nb, n_tab = page_table.shape
    steps = n_tab // n_pages
    cpp = PAGE // CMP_STRIDE

    def page_spec(k):
        return pl.BlockSpec((None, PAGE, NCK, HD), lambda b, s, pt: (pt[b, s * n_pages + k], 0, 0, 0))

    grid_spec = pltpu.PrefetchScalarGridSpec(
        num_scalar_prefetch=1,
        grid=(nb, steps),
        in_specs=[page_spec(k) for k in range(n_pages)]
        + [pl.BlockSpec(w_pairs.shape, lambda b, s, pt: (0, 0, 0))],
        out_specs=pl.BlockSpec((None, n_pages * cpp, NCK, 2 * HD), lambda b, s, pt: (b, s, 0, 0)),
    )
    return pl.pallas_call(
        functools.partial(_cmp_a_kernel, n_pages=n_pages),
        grid_spec=grid_spec,
        out_shape=jax.ShapeDtypeStruct((nb, n_tab * cpp, NCK, 2 * HD), F32),
        compiler_params=_cparams(("arbitrary", "arbitrary")),
        name="compress_stage_a",
    )(page_table, *([store] * n_pages), w_pairs)


def _cmp_b_kernel(ab_ref, nx_ref, tail_ref, pe_ref, w1_ref, w2_ref, g_ref, o_ref, const_ref):
    j = pl.program_id(1)
    nch = ab_ref.shape[0]
    rows = nch * NCK

    @pl.when((pl.program_id(0) == 0) & (j == 0))
    def _():
        for kind in range(2):
            const = jnp.zeros((8, HD), F32)
            for t in range(CMP_LEN):
                row = jnp.broadcast_to(pe_ref[kind, t:t + 1, :], (8, HD)).astype(BF16)
                const = const + _dot(row, w1_ref[kind, t].astype(BF16))
            const_ref[kind * A_KV:(kind + 1) * A_KV, :] = const[:A_KV]

    ab = ab_ref[...].reshape(rows, 2 * HD)
    nxt = pltpu.roll(ab[:, HD:], rows - NCK, 0).reshape(nch, NCK, HD)
    first_next = jnp.where(j == pl.num_programs(1) - 1, tail_ref[0, :, HD:], nx_ref[0, :, HD:])
    cid = lax.broadcasted_iota(jnp.int32, (nch, NCK, HD), 0)
    nxt = jnp.where(cid == nch - 1, first_next[None], nxt)
    pre = ab[:, :HD].reshape(nch, NCK, HD) + nxt + const_ref[...][None]
    act = _silu(pre).reshape(rows, HD).astype(BF16)
    w2 = jnp.concatenate([w2_ref[0], w2_ref[1]], axis=1).astype(BF16)
    y2 = _dot(act, w2)
    is_k = (lax.broadcasted_iota(jnp.int32, (rows, HD), 0) & (NCK - 1)) < A_KV
    y = jnp.where(is_k, y2[:, :HD], y2[:, HD:])
    yn = y * lax.rsqrt(jnp.mean(y * y, axis=-1, keepdims=True) + EPS) * g_ref[1:2]
    o_ref[...] = jnp.where(is_k, yn, y).reshape(nch, NCK, HD)


def compress_stage_b(ab, tail, pe, w1, w2, qk_g):
    nb, n_ch = ab.shape[:2]
    tc = min(256, n_ch)
    assert n_ch % tc == 0
    nt = n_ch // tc
    out = pl.pallas_call(
        _cmp_b_kernel,
        grid=(nb, nt),
        in_specs=[
            pl.BlockSpec((None, tc, NCK, 2 * HD), lambda b, j: (b, j, 0, 0)),
            pl.BlockSpec((None, 1, NCK, 2 * HD), lambda b, j: (b, jnp.minimum((j + 1) * tc, n_ch - 1), 0, 0)),
            pl.BlockSpec((None, 1, NCK, 2 * HD), lambda b, j: (b, 0, 0, 0)),
            pl.BlockSpec(pe.shape, lambda b, j: (0, 0, 0)),
            pl.BlockSpec(w1.shape, lambda b, j: (0, 0, 0, 0)),
            pl.BlockSpec(w2.shape, lambda b, j: (0, 0, 0)),
            pl.BlockSpec((4, HD), lambda b, j: (0, 0)),
        ],
        out_specs=pl.BlockSpec((None, tc, NCK, HD), lambda b, j: (b, j, 0, 0)),
        out_shape=jax.ShapeDtypeStruct((nb, n_ch, NCK, HD), F32),
        scratch_shapes=[pltpu.VMEM((NCK, HD), F32)],
        compiler_params=_cparams(("arbitrary", "arbitrary")),
        name="compress_stage_b",
    )(ab, ab, tail, pe, w1, w2, qk_g)
    return out.reshape(nb, n_ch, NCK * HD)


def _select_topk(score, blk, k_top):
    sel = jnp.zeros(score.shape, F32)
    for _ in range(k_top):
        m = jnp.max(score, axis=-1, keepdims=True)
        cand = jnp.where(score == m, blk, jnp.int32(1 << 30))
        first = jnp.min(cand, axis=-1, keepdims=True)
        pick = blk == first
        sel = jnp.where(pick, 1.0, sel)
        score = jnp.where(pick, -jnp.inf, score)
    return sel


def _nsa_prompt_kernel(q_ref, kc_ref, vc_ref, ks_ref, vs_ref, kw_ref, vw_ref, zc_ref, zs_ref, zw_ref,
                       gt_ref, tsel_ref, tcmp_ref, msel_ref, exp_ref, o_ref,
                       qh_ref, madd_ref, mix_ref, m_ref, l_ref, acc_ref, *, n_sel, k_top, scale):
    i = pl.program_id(2)
    qb = Q_BLOCK
    rows = A_HPG * qb
    kt = 2 * LANE
    n_kt = madd_ref.shape[0]

    for h in range(A_HPG):
        qh_ref[h * qb:(h + 1) * qb, :] = q_ref[:, h * HD:(h + 1) * HD]
    qh = qh_ref[...]
    gates = _sigmoid(gt_ref[...])

    nc = kc_ref.shape[0]
    row = lax.broadcasted_iota(jnp.int32, (qb, LANE), 0)
    lane = lax.broadcasted_iota(jnp.int32, (qb, LANE), 1)
    qpos = i * qb + row
    rowc = lax.broadcasted_iota(jnp.int32, (qb, nc), 0)
    lanec = lax.broadcasted_iota(jnp.int32, (qb, nc), 1)

    def mix_in(branch, o, z_ref, first):
        for h in range(A_HPG):
            sl = slice(h * HD, (h + 1) * HD)
            gcol = gates[:, branch * A_HPG + h: branch * A_HPG + h + 1]
            val = gcol * o[h * qb:(h + 1) * qb] * _silu(z_ref[:, sl])
            if first:
                mix_ref[:, sl] = val
            else:
                mix_ref[:, sl] = mix_ref[:, sl] + val

    kc = kc_ref[...].astype(BF16)
    vc = vc_ref[...].astype(BF16)
    cvalid = (lanec * CMP_STRIDE + (CMP_LEN - 1)) <= (i * qb + rowc)
    cvalid_all = jnp.concatenate([cvalid] * A_HPG, axis=0)
    s = _dot_nt(qh, kc) * scale + tcmp_ref[...].reshape(rows, nc)
    s = jnp.where(cvalid_all, s, NEG)
    mx = jnp.max(s, axis=-1, keepdims=True)
    e = jnp.where(cvalid_all, jnp.exp(s - mx), 0.0)
    p = e / jnp.maximum(jnp.sum(e, axis=-1, keepdims=True), TINY)
    mix_in(0, _dot(p.astype(BF16), vc), zc_ref, True)
    imp = p[0:qb]
    for h in range(1, A_HPG):
        imp = imp + p[h * qb:(h + 1) * qb]

    imp_sel = _split3_dot(imp, msel_ref[...])
    cur = jnp.right_shift(qpos, int(math.log2(SEL_BLOCK)))
    forced = (lane == 0) | (lane == cur) | (lane == cur - 1)
    valid = lane * SEL_BLOCK <= qpos
    score = jnp.where(valid, jnp.where(forced, FORCE_SCORE, imp_sel), -1.0)
    score = jnp.where(lane < n_sel, score, -jnp.inf)
    sel = _select_topk(score, lane, k_top).astype(BF16)
    row2 = lax.broadcasted_iota(jnp.int32, (qb, kt), 0)
    lane2 = lax.broadcasted_iota(jnp.int32, (qb, kt), 1)
    for jj in range(n_kt):
        hit = _dot(sel, exp_ref[:, jj * kt:(jj + 1) * kt])
        ok = (hit > 0.5) & ((jj * kt + lane2) <= (i * qb + row2))
        madd_ref[jj] = jnp.where(ok, 0.0, NEG)

    def online_update(s, v):
        m_old = m_ref[...]
        m_new = jnp.maximum(m_old, jnp.max(s, axis=-1, keepdims=True))
        alpha = jnp.exp(m_old - m_new)
        p_ = jnp.exp(s - m_new)
        l_ref[...] = alpha * l_ref[...] + jnp.sum(p_, axis=-1, keepdims=True)
        acc_ref[...] = alpha * acc_ref[...] + _dot(p_.astype(BF16), v)
        m_ref[...] = m_new

    def reset():
        m_ref[...] = jnp.full(m_ref.shape, NEG, F32)
        l_ref[...] = jnp.zeros(l_ref.shape, F32)
        acc_ref[...] = jnp.zeros(acc_ref.shape, F32)

    reset()

    def sel_body(jj, carry):
        k0 = pl.multiple_of(jj * kt, kt)
        k = ks_ref[pl.ds(k0, kt), :].astype(BF16)
        v = vs_ref[pl.ds(k0, kt), :].astype(BF16)
        off = i - 2 * jj
        bias = jnp.concatenate([tsel_ref[off + 1].reshape(rows, LANE),
                                tsel_ref[off].reshape(rows, LANE)], axis=1)
        madd = madd_ref[jj]
        s_ = _dot_nt(qh, k) * scale + bias + jnp.concatenate([madd] * A_HPG, axis=0)
        online_update(s_, v)
        return carry

    lax.fori_loop(0, i // 2 + 1, sel_body, 0)
    mix_in(1, acc_ref[...] / l_ref[...], zs_ref, False)

    reset()

    def win_body(t, carry):
        j = i - t
        k0 = pl.multiple_of(j * qb, qb)
        k = kw_ref[pl.ds(k0, qb), :].astype(BF16)
        v = vw_ref[pl.ds(k0, qb), :].astype(BF16)
        dist = (t * qb + row) - lane
        ok = (dist >= 0) & (dist < WINDOW)
        madd = jnp.where(ok, 0.0, NEG)
        s_ = (_dot_nt(qh, k) * scale + tsel_ref[t + 1].reshape(rows, LANE)
              + jnp.concatenate([madd] * A_HPG, axis=0))
        online_update(s_, v)
        return carry

    lax.fori_loop(0, jnp.minimum(i, WINDOW // qb) + 1, win_body, 0)
    mix_in(2, acc_ref[...] / l_ref[...], zw_ref, False)
    o_ref[...] = mix_ref[...].astype(o_ref.dtype)


def nsa_prompt_attention(qn, cmp_kv, rows, win, proj, gates_g, tsel, tcmp, msel, expand, nb, t):
    nq = t // Q_BLOCK
    qw = A_HPG * HD
    n_sel = -(-t // SEL_BLOCK)
    k_top = min(SEL_TOPK, n_sel)
    n_kt = t // (2 * LANE)
    n_cmp = cmp_kv.shape[1]
    assert n_sel <= LANE
    zbase = (A_HEADS * HD + 6 * A_KV * HD) // qw
    nz = A_HEADS * HD // qw
    kernel = functools.partial(_nsa_prompt_kernel, n_sel=n_sel, k_top=k_top, scale=HD ** -0.5)
    return pl.pallas_call(
        kernel,
        grid=(A_KV, nb, nq),
        in_specs=[
            pl.BlockSpec((Q_BLOCK, qw), lambda g, b, i: (b * nq + i, g)),
            pl.BlockSpec((None, n_cmp, HD), lambda g, b, i: (b, 0, g)),
            pl.BlockSpec((None, n_cmp, HD), lambda g, b, i: (b, 0, A_KV + g)),
            pl.BlockSpec((t, HD), lambda g, b, i: (b, 2 * A_KV + g)),
            pl.BlockSpec((t, HD), lambda g, b, i: (b, 3 * A_KV + g)),
            pl.BlockSpec((t, HD), lambda g, b, i: (b, g)),
            pl.BlockSpec((t, HD), lambda g, b, i: (b, A_KV + g)),
            pl.BlockSpec((Q_BLOCK, qw), lambda g, b, i: (b * nq + i, zbase + g)),
            pl.BlockSpec((Q_BLOCK, qw), lambda g, b, i: (b * nq + i, zbase + nz + g)),
            pl.BlockSpec((Q_BLOCK, qw), lambda g, b, i: (b * nq + i, zbase + 2 * nz + g)),
            pl.BlockSpec((None, Q_BLOCK, 3 * A_HPG), lambda g, b, i: (g, b * nq + i, 0)),
            pl.BlockSpec((None, nq + 1, A_HPG, Q_BLOCK, LANE), lambda g, b, i: (g, 0, 0, 0, 0)),
            pl.BlockSpec((None, None, A_HPG, Q_BLOCK, n_cmp), lambda g, b, i: (g, i, 0, 0, 0)),
            pl.BlockSpec((n_cmp, LANE), lambda g, b, i: (0, 0)),
            pl.BlockSpec((LANE, t), lambda g, b, i: (0, 0)),
        ],
        out_specs=pl.BlockSpec((Q_BLOCK, qw), lambda g, b, i: (b * nq + i, g)),
        out_shape=jax.ShapeDtypeStruct((nb * t, A_HEADS * HD), BF16),
        scratch_shapes=[
            pltpu.VMEM((A_HPG * Q_BLOCK, HD), BF16),
            pltpu.VMEM((n_kt, Q_BLOCK, 2 * LANE), F32),
            pltpu.VMEM((Q_BLOCK, qw), F32),
            pltpu.VMEM((A_HPG * Q_BLOCK, 1), F32),
            pltpu.VMEM((A_HPG * Q_BLOCK, 1), F32),
            pltpu.VMEM((A_HPG * Q_BLOCK, HD), F32),
        ],
        compiler_params=_cparams(("arbitrary", "arbitrary", "arbitrary")),
        name="nsa_prompt_attention",
    )(qn, cmp_kv, cmp_kv, rows, rows, win, win, proj, proj, proj, gates_g, tsel, tcmp, msel, expand)


def _diag_blocks(o_full, rows_per_group):
    return jnp.concatenate(
        [o_full[g * rows_per_group:(g + 1) * rows_per_group, g * HD:(g + 1) * HD] for g in range(A_KV)], axis=0)


def _nsa_sample_cmp_kernel(q_ref, kc_ref, vc_ref, bias_ref, msel_ref, oc_ref, sel_ref,
                           *, past_len, n_sel, k_top, n_q, scale, blocks_per_step):
    n_cmp = kc_ref.shape[0]
    rows = q_ref.shape[0]
    rpg = A_HPG * n_q
    q = q_ref[...]
    s = _dot_nt(q, kc_ref[...].astype(BF16)) * scale + bias_ref[...]
    rowi = lax.broadcasted_iota(jnp.int32, (rows, n_cmp), 0)
    ci = lax.broadcasted_iota(jnp.int32, (rows, n_cmp), 1)
    qpos = past_len + (rowi & (n_q - 1))
    valid = (ci * CMP_STRIDE + (CMP_LEN - 1)) <= qpos
    s = jnp.where(valid, s, NEG)
    mx = jnp.max(s, axis=-1, keepdims=True)
    e = jnp.where(valid, jnp.exp(s - mx), 0.0)
    p = e / jnp.maximum(jnp.sum(e, axis=-1, keepdims=True), TINY)
    oc_ref[...] = _diag_blocks(_dot(p.astype(BF16), vc_ref[...].astype(BF16)), rpg)
    imps = []
    for g in range(A_KV):
        acc = p[g * rpg: g * rpg + n_q]
        for h in range(1, A_HPG):
            acc = acc + p[g * rpg + h * n_q: g * rpg + (h + 1) * n_q]
        imps.append(acc)
    imp = jnp.concatenate(imps, axis=0)
    imp_sel = _split3_dot(imp, msel_ref[...])
    shp = imp_sel.shape
    blk = lax.broadcasted_iota(jnp.int32, shp, 1)
    qp = past_len + (lax.broadcasted_iota(jnp.int32, shp, 0) & (n_q - 1))
    cur = jnp.right_shift(qp, int(math.log2(SEL_BLOCK)))
    forced = (blk == 0) | (blk == cur) | (blk == cur - 1)
    ok = blk * SEL_BLOCK <= qp
    score = jnp.where(ok, jnp.where(forced, FORCE_SCORE, imp_sel), -1.0)
    score = jnp.where(blk < n_sel, score, -jnp.inf)
    sel = _select_topk(score, blk, k_top)
    for st in range(sel_ref.shape[0]):
        sel_ref[st] = sel[:, st * blocks_per_step:(st + 1) * blocks_per_step]


def _nsa_sample_sel_kernel(pt_ref, *refs, n_pages, n_steps, past_len, n_q, scale):
    del pt_ref
    pages = refs[:n_pages]
    q_ref, tail_ref, bias_ref, sel_ref, exp_ref, o_ref, m_ref, l_ref, acc_ref = refs[n_pages:]
    s_id = pl.program_id(1)
    kvw = A_KV * HD
    rows = q_ref.shape[0]
    rpg = A_HPG * n_q

    @pl.when(s_id == 0)
    def _():
        m_ref[...] = jnp.full(m_ref.shape, NEG, F32)
        l_ref[...] = jnp.zeros(l_ref.shape, F32)
        acc_ref[...] = jnp.zeros(acc_ref.shape, F32)

    def update(k, v, nk, key0):
        hit = _dot(sel_ref[...].astype(BF16), exp_ref[:, :nk])
        kpos = key0 + lax.broadcasted_iota(jnp.int32, hit.shape, 1)
        qpos = past_len + (lax.broadcasted_iota(jnp.int32, hit.shape, 0) & (n_q - 1))
        madd = jnp.where((hit > 0.5) & (kpos <= qpos), 0.0, NEG)
        madd = jnp.concatenate(
            [madd[g * n_q:(g + 1) * n_q] for g in range(A_KV) for _ in range(A_HPG)], axis=0)
        s = _dot_nt(q_ref[...], k) * scale + bias_ref[:, :nk] + madd
        m_old = m_ref[...]
        m_new = jnp.maximum(m_old, jnp.max(s, axis=-1, keepdims=True))
        alpha = jnp.exp(m_old - m_new)
        p = jnp.exp(s - m_new)
        l_ref[...] = alpha * l_ref[...] + jnp.sum(p, axis=-1, keepdims=True)
        acc_ref[...] = alpha * acc_ref[...] + _dot(p.astype(BF16), v)
        m_ref[...] = m_new

    @pl.when(s_id < n_steps)
    def _():
        k = jnp.concatenate([pg[:, :kvw] for pg in pages], axis=0).astype(BF16)
        v = jnp.concatenate([pg[:, kvw:] for pg in pages], axis=0).astype(BF16)
        update(k, v, n_pages * PAGE, s_id * (n_pages * PAGE))

    @pl.when(s_id == n_steps)
    def _():
        update(tail_ref[:, :kvw].astype(BF16), tail_ref[:, kvw:].astype(BF16), PAGE, past_len)
        o_ref[...] = _diag_blocks(acc_ref[...] / l_ref[...], rpg)


def _nsa_sample_mix_kernel(q_ref, win_ref, bias_ref, oc_ref, os_ref, z_ref, gt_ref, o_ref, *, n_q, scale):
    kvw = A_KV * HD
    rows = q_ref.shape[0]
    rpg = A_HPG * n_q
    nk = win_ref.shape[0]
    s = _dot_nt(q_ref[...], win_ref[:, :kvw].astype(BF16)) * scale + bias_ref[...]
    qi = lax.broadcasted_iota(jnp.int32, (rows, nk), 0) & (n_q - 1)
    ki = lax.broadcasted_iota(jnp.int32, (rows, nk), 1)
    dist = qi + WINDOW - ki
    s = s + jnp.where((dist >= 0) & (dist < WINDOW), 0.0, NEG)
    mx = jnp.max(s, axis=-1, keepdims=True)
    e = jnp.exp(s - mx)
    p = e / jnp.sum(e, axis=-1, keepdims=True)
    ow = _diag_blocks(_dot(p.astype(BF16), win_ref[:, kvw:].astype(BF16)), rpg)
    gates = _sigmoid(gt_ref[...])
    width = A_HEADS * HD
    for hh in range(A_HEADS):
        r = slice(hh * n_q, (hh + 1) * n_q)
        c = slice(hh * HD, (hh + 1) * HD)
        val = None
        for br, o in enumerate((oc_ref, os_ref, ow)):
            zc = slice(br * width + hh * HD, br * width + (hh + 1) * HD)
            gcol = gates[:, br * A_HEADS + hh: br * A_HEADS + hh + 1]
            term = gcol * o[r, :] * _silu(z_ref[:, zc])
            val = term if val is None else val + term
        o_ref[:, c] = val.astype(o_ref.dtype)


def nsa_sample_attention(qbd, cmp_kv, cache2d, page_table, tail_sel, win_all, z_s, gl_s,
                         b_cmp, b_sel, b_win, msel, expand, past_len, n_q):
    nb, rows, kvw = qbd.shape
    n_cmp = cmp_kv.shape[1]
    n_tab = page_table.shape[1]
    n_pages = min(PAGES_PER_STEP, n_tab)
    n_steps = n_tab // n_pages
    bps = n_pages * PAGE // SEL_BLOCK
    n_sel = -(-(past_len + n_q) // SEL_BLOCK)
    k_top = min(SEL_TOPK, n_sel)
    gq = A_KV * n_q
    scale = HD ** -0.5

    oc, selmask = pl.pallas_call(
        functools.partial(_nsa_sample_cmp_kernel, past_len=past_len, n_sel=n_sel, k_top=k_top,
                          n_q=n_q, scale=scale, blocks_per_step=bps),
        grid=(nb,),
        in_specs=[
            pl.BlockSpec((None, rows, kvw), lambda b: (b, 0, 0)),
            pl.BlockSpec((None, n_cmp, kvw), lambda b: (b, 0, 0)),
            pl.BlockSpec((None, n_cmp, kvw), lambda b: (b, 0, 1)),
            pl.BlockSpec(b_cmp.shape, lambda b: (0, 0)),
            pl.BlockSpec(msel.shape, lambda b: (0, 0)),
        ],
        out_specs=[
            pl.BlockSpec((None, rows, HD), lambda b: (b, 0, 0)),
            pl.BlockSpec((None, n_steps + 1, gq, bps), lambda b: (b, 0, 0, 0)),
        ],
        out_shape=[
            jax.ShapeDtypeStruct((nb, rows, HD), F32),
            jax.ShapeDtypeStruct((nb, n_steps + 1, gq, bps), F32),
        ],
        compiler_params=_cparams(("arbitrary",)),
        name="nsa_sample_cmp",
    )(qbd, cmp_kv, cmp_kv, b_cmp, msel)

    half = 2 * kvw
    last = n_steps - 1

    def page_spec(k):
        return pl.BlockSpec(
            (None, PAGE, half),
            lambda b, s, pt: (pt[b, jnp.minimum(s, last) * n_pages + k], 0, 1))

    nkeys = n_pages * PAGE
    grid_spec = pltpu.PrefetchScalarGridSpec(
        num_scalar_prefetch=1,
        grid=(nb, n_steps + 1),
        in_specs=[page_spec(k) for k in range(n_pages)] + [
            pl.BlockSpec((None, rows, kvw), lambda b, s, pt: (b, 0, 0)),
            pl.BlockSpec((None, PAGE, half), lambda b, s, pt: (b, 0, 0)),
            pl.BlockSpec((rows, nkeys), lambda b, s, pt: (0, s)),
            pl.BlockSpec((None, None, gq, bps), lambda b, s, pt: (b, s, 0, 0)),
            pl.BlockSpec(expand.shape, lambda b, s, pt: (0, 0)),
        ],
        out_specs=pl.BlockSpec((None, rows, HD), lambda b, s, pt: (b, 0, 0)),
        scratch_shapes=[
            pltpu.VMEM((rows, 1), F32),
            pltpu.VMEM((rows, 1), F32),
            pltpu.VMEM((rows, kvw), F32),
        ],
    )
    osel = pl.pallas_call(
        functools.partial(_nsa_sample_sel_kernel, n_pages=n_pages, n_steps=n_steps,
                          past_len=past_len, n_q=n_q, scale=scale),
        grid_spec=grid_spec,
        out_shape=jax.ShapeDtypeStruct((nb, rows, HD), F32),
        compiler_params=_cparams(("arbitrary", "arbitrary")),
        name="nsa_sample_sel",
    )(page_table, *([cache2d] * n_pages), qbd, tail_sel, b_sel, selmask, expand)

    nkw = win_all.shape[1]
    zw = z_s.shape[1]
    return pl.pallas_call(
        functools.partial(_nsa_sample_mix_kernel, n_q=n_q, scale=scale),
        grid=(nb,),
        in_specs=[
            pl.BlockSpec((None, rows, kvw), lambda b: (b, 0, 0)),
            pl.BlockSpec((None, nkw, 2 * kvw), lambda b: (b, 0, 0)),
            pl.BlockSpec(b_win.shape, lambda b: (0, 0)),
            pl.BlockSpec((None, rows, HD), lambda b: (b, 0, 0)),
            pl.BlockSpec((None, rows, HD), lambda b: (b, 0, 0)),
            pl.BlockSpec((n_q, zw), lambda b: (b, 0)),
            pl.BlockSpec((n_q, 3 * A_HEADS), lambda b: (b, 0)),
        ],
        out_specs=pl.BlockSpec((n_q, A_HEADS * HD), lambda b: (b, 0)),
        out_shape=jax.ShapeDtypeStruct((nb * n_q, A_HEADS * HD), F32),
        compiler_params=_cparams(("arbitrary",)),
        name="nsa_sample_mix",
    )(qbd, win_all, b_win, oc, osel, z_s, gl_s)


def _retention_kernel(q_ref, k_ref, v_ref, g_ref, cos_ref, sin_ref, dm_ref, qd_ref, kd_ref, cd_ref,
                      s0_ref, o_ref, s_ref, *, dk):
    c = pl.program_id(2)
    half = dk // 2

    @pl.when(c == 0)
    def _():
        s_ref[...] = s0_ref[...]

    cos = cos_ref[...]
    sin = sin_ref[...]

    def rot(x):
        x1, x2 = x[:, :half], x[:, half:]
        return jnp.concatenate([x1 * cos - x2 * sin, x1 * sin + x2 * cos], axis=1)

    q = rot(q_ref[...])
    k = rot(k_ref[...]) * (dk ** -0.5)
    v = v_ref[...].astype(BF16)
    qb = q.astype(BF16)
    att = _dot_nt(qb, k.astype(BF16)) * dm_ref[...]
    state = s_ref[...]
    o = _dot(att.astype(BF16), v) + _dot(qb, state.astype(BF16)) * qd_ref[...]
    s_ref[...] = state * cd_ref[...] + _dot_tn((k * kd_ref[...]).astype(BF16), v)
    o = o * lax.rsqrt(jnp.mean(o * o, axis=-1, keepdims=True) + EPS)
    o_ref[...] = (o * _silu(g_ref[...])).astype(o_ref.dtype)


def retention(proj, state0, q0, nb, t):
    _, nh, dk, dv = state0.shape
    cs = R_CHUNK if t % R_CHUNK == 0 else t
    n = t // cs
    half = dk // 2
    pos = (q0 + jnp.arange(t)).astype(F32)
    inv = jnp.power(ROPE_BASE, -jnp.arange(half, dtype=F32) / half)
    ang = pos[:, None] * inv[None, :]
    cos, sin = jnp.cos(ang), jnp.sin(ang)
    log_g = jnp.log1p(-jnp.exp2(-5.0 - jnp.arange(nh, dtype=F32)))
    ii = jnp.arange(cs, dtype=F32)
    rel = ii[:, None] - ii[None, :]
    causal = rel >= 0
    dmat = jnp.where(causal, jnp.exp(jnp.where(causal, rel, 0.0) * log_g[:, None, None]), 0.0)
    q_dec = jnp.exp((ii + 1.0) * log_g[:, None])[:, :, None]
    k_dec = jnp.exp((cs - 1.0 - ii) * log_g[:, None])[:, :, None]
    c_dec = jnp.exp(cs * log_g)[:, None, None]
    kb = nh * dk // dk
    vb0 = 2 * nh * dk // dv
    return pl.pallas_call(
        functools.partial(_retention_kernel, dk=dk),
        grid=(nb, nh, n),
        in_specs=[
            pl.BlockSpec((cs, dk), lambda b, h, c: (b * n + c, h)),
            pl.BlockSpec((cs, dk), lambda b, h, c: (b * n + c, kb + h)),
            pl.BlockSpec((cs, dv), lambda b, h, c: (b * n + c, vb0 + h)),
            pl.BlockSpec((cs, dv), lambda b, h, c: (b * n + c, vb0 + nh + h)),
            pl.BlockSpec((cs, half), lambda b, h, c: (c, 0)),
            pl.BlockSpec((cs, half), lambda b, h, c: (c, 0)),
            pl.BlockSpec((None, cs, cs), lambda b, h, c: (h, 0, 0)),
            pl.BlockSpec((None, cs, 1), lambda b, h, c: (h, 0, 0)),
            pl.BlockSpec((None, cs, 1), lambda b, h, c: (h, 0, 0)),
            pl.BlockSpec((None, 1, 1), lambda b, h, c: (h, 0, 0)),
            pl.BlockSpec((None, None, dk, dv), lambda b, h, c: (b, h, 0, 0)),
        ],
        out_specs=[
            pl.BlockSpec((cs, dv), lambda b, h, c: (b * n + c, h)),
            pl.BlockSpec((None, None, dk, dv), lambda b, h, c: (b, h, 0, 0)),
        ],
        out_shape=[
            jax.ShapeDtypeStruct((nb * t, nh * dv), BF16 if cs % 16 == 0 else F32),
            jax.ShapeDtypeStruct(state0.shape, F32),
        ],
        compiler_params=_cparams(("arbitrary", "arbitrary", "arbitrary")),
        name="retention",
    )(proj, proj, proj, proj, cos, sin, dmat, q_dec, k_dec, c_dec, state0)


def _t5_bucket_np(dist):
    n = np.maximum(dist, 0)
    exact = N_BUCKETS // 2
    logv = np.log(np.maximum(n, 1).astype(np.float32) / np.float32(exact)) / np.float32(
        math.log(MAX_DISTANCE / exact))
    large = np.minimum(exact + (logv * np.float32(N_BUCKETS - exact)).astype(np.int32), N_BUCKETS - 1)
    return np.where(n < exact, n, large).astype(np.int32)


def _bias_lookup(rel_bias, dist):
    tab = jnp.take(rel_bias, jnp.asarray(_t5_bucket_np(dist)), axis=0)
    tab = jnp.moveaxis(tab, -1, 0)
    return tab.reshape((A_KV, A_HPG) + dist.shape)


def _sel_matrix(n_cmp, n_cmp_pad, n_sel_pad):
    c = np.arange(n_cmp_pad)[:, None]
    j = np.arange(n_sel_pad)[None, :]
    a = (c >= SEL_RATIO * j) & (c < SEL_RATIO * j + SEL_RATIO)
    b = (c >= SEL_RATIO * j - 1) & (c < SEL_RATIO * j + SEL_RATIO - 1)
    m = (a.astype(np.float32) + b.astype(np.float32)) * (c < n_cmp)
    return jnp.asarray(m, BF16)


def _expand_matrix(n_blk_pad, n_keys):
    blk = np.arange(n_blk_pad)[:, None]
    key = np.arange(n_keys)[None, :]
    return jnp.asarray((key // SEL_BLOCK == blk).astype(np.float32), BF16)


def _w1_pairs(w1):
    w = w1.reshape(2, 2, CMP_STRIDE // 2, 2, HD, HD)
    w = jnp.transpose(w, (2, 3, 4, 0, 1, 5))
    return w.reshape(CMP_STRIDE // 2, 2 * HD, 4 * HD).astype(BF16)


def kernel(x_prompt, x_sample, c_prompt, c_sample, cache_nsa_kv, cache_nsa_win, state_ret, page_table,
           norm_g, ada_w, ada_b, rel_bias, a_w_in, a_w_out, a_cmp_pe, a_cmp_w1, a_cmp_w2, a_qk_g,
           r_w_in, r_w_out):
    nbp, t, d = x_prompt.shape
    nbs, ts, _ = x_sample.shape
    n_tab = page_table.shape[1]
    past_len = n_tab * PAGE
    kvw = A_KV * HD
    qw = A_HEADS * HD
    assert t % (2 * LANE) == 0 and t >= WINDOW and ts == 8 and past_len >= WINDOW
    assert cache_nsa_win.shape[2] == WINDOW

    n_c = nbp + nbs
    pad_c = (-n_c) % 8
    c_all = jnp.concatenate([c_prompt, c_sample, jnp.zeros((pad_c, d), F32)], axis=0)
    mod = ada_modulation(c_all, ada_w, ada_b).reshape(ada_w.shape[0], n_c + pad_c, 3, d)

    def mods(layer):
        m = mod[layer]
        return (m[:nbp, 0], m[:nbp, 1], m[:nbp, 2]), (m[nbp:n_c, 0], m[nbp:n_c, 1], m[nbp:n_c, 2])

    (sh_p, sc_p, gt_p), (sh_s, sc_s, gt_s) = mods(0)
    hp = norm_modulate(x_prompt, norm_g[0], sc_p, sh_p)
    hs = norm_modulate(x_sample, norm_g[0], sc_s, sh_s)
    w_in = a_w_in[0]
    n_main = w_in.shape[1] - 3 * A_HEADS
    proj_p, proj_s = project(hp, hs, w_in, n_cols=n_main)
    w_gate = jnp.pad(w_in[:, n_main:], ((0, 0), (0, LANE - 3 * A_HEADS)))
    gl_p, gl_s = project(hp, hs, w_gate)
    gl_p, gl_s = gl_p[:, :3 * A_HEADS], gl_s[:, :3 * A_HEADS]
    qk_g = a_qk_g[0]

    qn_p, rows_p, win_p = nsa_prep(proj_p, qk_g)
    qn_s, rows_s, win_s = nsa_prep(proj_s, qk_g)

    w_pairs = _w1_pairs(a_cmp_w1[0])
    pe = a_cmp_pe[0]
    w1 = a_cmp_w1[0].reshape(2, CMP_LEN, HD, HD)
    w2 = a_cmp_w2[0]
    cpp = PAGE // CMP_STRIDE

    pages_p = t // PAGE
    ident = jnp.arange(nbp * pages_p, dtype=jnp.int32).reshape(nbp, pages_p)
    ab_p = compress_stage_a(rows_p.reshape(nbp * pages_p, PAGE, 4 * A_KV, HD), ident, w_pairs,
                            min(PAGES_PER_STEP, pages_p))
    zero_tail = jnp.zeros((nbp, cpp, NCK, 2 * HD), F32)
    cmp_p = compress_stage_b(ab_p, zero_tail, pe, w1, w2, qk_g)

    nq = t // Q_BLOCK
    ql = np.arange(Q_BLOCK)[:, None]
    kl = np.arange(LANE)[None, :]
    n_ch_p = t // CMP_STRIDE
    cl = np.arange(n_ch_p)[None, :]
    offs = (np.arange(nq + 1) - 1)[:, None, None]
    tsel = _bias_lookup(rel_bias, offs * Q_BLOCK + ql - kl)
    tsel = jnp.transpose(tsel, (0, 2, 1, 3, 4))
    qi = np.arange(nq)[:, None, None]
    tcmp = _bias_lookup(rel_bias, qi * Q_BLOCK + ql - (cl * CMP_STRIDE + CMP_LEN - 1))
    tcmp = jnp.transpose(tcmp, (0, 2, 1, 3, 4))
    n_cmp_p = t // CMP_STRIDE - 1
    msel_p = _sel_matrix(n_cmp_p, n_ch_p, LANE)
    expand_p = _expand_matrix(LANE, t)
    gates_g = jnp.transpose(gl_p.reshape(nbp * t, 3, A_KV, A_HPG), (2, 0, 1, 3)).reshape(A_KV, nbp * t, 3 * A_HPG)
    mixed_p = nsa_prompt_attention(qn_p, cmp_p, rows_p, win_p, proj_p, gates_g,
                                   tsel, tcmp, msel_p, expand_p, nbp, t)

    n_pool = cache_nsa_kv.shape[1]
    cache2d = cache_nsa_kv[0].reshape(n_pool, PAGE, 4 * kvw)
    n_pages = min(PAGES_PER_STEP, n_tab)
    ab_s = compress_stage_a(cache_nsa_kv[0].reshape(n_pool, PAGE, 4 * A_KV, HD), page_table, w_pairs, n_pages)
    tail_rows = jnp.pad(rows_s.reshape(nbs, ts, 4 * kvw), ((0, 0), (0, PAGE - ts), (0, 0)))
    ident_s = jnp.arange(nbs, dtype=jnp.int32).reshape(nbs, 1)
    ab_tail = compress_stage_a(tail_rows.reshape(nbs, PAGE, 4 * A_KV, HD), ident_s, w_pairs, 1)
    cmp_s = compress_stage_b(ab_s, ab_tail, pe, w1, w2, qk_g)

    n_cmp_s = cmp_s.shape[1]
    n_sel_s = -(-(past_len + ts) // SEL_BLOCK)
    n_steps = n_tab // n_pages
    bps = n_pages * PAGE // SEL_BLOCK
    n_sel_pad = (n_steps + 1) * bps
    qv = np.arange(ts)[:, None]
    b_cmp = _bias_lookup(rel_bias, past_len + qv - (np.arange(n_cmp_s)[None, :] * CMP_STRIDE + CMP_LEN - 1))
    b_cmp = b_cmp.reshape(A_HEADS * ts, n_cmp_s)
    nkeys = (n_steps + 1) * n_pages * PAGE
    b_sel = _bias_lookup(rel_bias, past_len + qv - np.arange(nkeys)[None, :]).reshape(A_HEADS * ts, nkeys)
    nkw = WINDOW + LANE
    b_win = _bias_lookup(rel_bias, qv + WINDOW - np.arange(nkw)[None, :]).reshape(A_HEADS * ts, nkw)
    msel_s = _sel_matrix(n_cmp_s, n_cmp_s, n_sel_pad)
    expand_s = _expand_matrix(bps, n_pages * PAGE)
    q5 = qn_s.reshape(nbs, ts, A_KV, A_HPG, HD)
    eye = jnp.eye(A_KV, dtype=BF16)
    qbd = jnp.einsum('bqghd,ge->bghqed', q5, eye).reshape(nbs, A_HEADS * ts, kvw)
    win_new = win_s.reshape(nbs, ts, 2 * kvw)
    win_cache = cache_nsa_win[0].reshape(nbs, WINDOW, 2 * kvw)
    win_all = jnp.concatenate([win_cache, win_new], axis=1)
    win_pad = jnp.pad(win_all, ((0, 0), (0, nkw - WINDOW - ts), (0, 0)))
    tail_sel = tail_rows[:, :, 2 * kvw:]
    z_s = proj_s[:, qw + 6 * kvw:]
    mixed_s = nsa_sample_attention(qbd, cmp_s, cache2d, page_table, tail_sel, win_pad, z_s, gl_s,
                                   b_cmp, b_sel, b_win, msel_s, expand_s, past_len, ts).astype(BF16)

    xs_flat = x_sample.reshape(nbs * ts, d)
    gate_s_rows = jnp.repeat(gt_s, ts, axis=0)
    x1p, x1s = project_residual(mixed_p, mixed_s, a_w_out[0], x_prompt.reshape(nbp * t, d), gt_p,
                                xs_flat, gate_s_rows, t)

    (sh_p, sc_p, gt_p), (sh_s, sc_s, gt_s) = mods(1)
    hp = norm_modulate(x1p.reshape(nbp, t, d), norm_g[1], sc_p, sh_p)
    hs = norm_modulate(x1s.reshape(nbs, ts, d), norm_g[1], sc_s, sh_s)
    rp, rs = project(hp, hs, r_w_in[0])
    s0 = jnp.zeros((nbp,) + state_ret.shape[2:], F32)
    op, ret_p = retention(rp, s0, 0, nbp, t)
    os_, ret_s = retention(rs, state_ret[0], past_len, nbs, ts)
    os_ = os_.astype(BF16)
    gate_s_rows = jnp.repeat(gt_s, ts, axis=0)
    x2p, x2s = project_residual(op, os_, r_w_out[0], x1p, gt_p, x1s, gate_s_rows, t)

    kv_p = rows_p.reshape(1, nbp, t, 4, A_KV, HD)
    kv_s = rows_s.reshape(1, nbs, ts, 4, A_KV, HD)
    wst_p = win_p.reshape(nbp, t, 2, A_KV, HD)[None, :, t - WINDOW:]
    wst_s = win_all[:, ts:].reshape(1, nbs, WINDOW, 2, A_KV, HD)
    return (x2p.reshape(nbp, t, d), x2s.reshape(nbs, ts, d), kv_p, kv_s, wst_p, wst_s,
            ret_p[None], ret_s[None])
```

```python
import functools
import math

import numpy as np
import jax
import jax.numpy as jnp
from jax import lax
from jax.experimental import pallas as pl
from jax.experimental.pallas import tpu as pltpu

EPS = 1e-6
HD = 128
A_KV = 4
A_HPG = 8
A_HEADS = A_KV * A_HPG
CMP_STRIDE = 16
CMP_LEN = 32
SEL_BLOCK = 64
SEL_RATIO = SEL_BLOCK // CMP_STRIDE
SEL_TOPK = 16
WINDOW = 512
Q_BLOCK = 128
FORCE_SCORE = 1e4
NEG = -1e30
TINY = 1e-30
N_BUCKETS = 32
MAX_DISTANCE = 1024
R_HEADS = 16
R_CHUNK = 128
ROPE_BASE = 10000.0
PAGE = 128
PAGES_PER_STEP = 16

LANE = 128
VMEM_LIMIT = 56 * 1024 * 1024

F32 = jnp.float32
BF16 = jnp.bfloat16


def _cparams(sem):
    return pltpu.CompilerParams(dimension_semantics=sem, vmem_limit_bytes=VMEM_LIMIT)


def _dot(a, b):
    return jnp.dot(a, b, preferred_element_type=F32)


def _dot_nt(a, b):
    return lax.dot_general(a, b, (((1,), (1,)), ((), ())), preferred_element_type=F32)


def _dot_tn(a, b):
    return lax.dot_general(a, b, (((0,), (0,)), ((), ())), preferred_element_type=F32)


def _silu(x):
    return x / (1.0 + jnp.exp(-x))


def _sigmoid(x):
    return 1.0 / (1.0 + jnp.exp(-x))


def _split3_dot(x, m_bf16):
    hi = x.astype(BF16)
    r1 = x - hi.astype(F32)
    mid = r1.astype(BF16)
    lo = (r1 - mid.astype(F32)).astype(BF16)
    return _dot(hi, m_bf16) + _dot(mid, m_bf16) + _dot(lo, m_bf16)


def _ada_kernel(c_ref, w_ref, b_ref, o_ref):
    a = _silu(c_ref[...]).astype(BF16)
    o_ref[...] = _dot(a, w_ref[...].astype(BF16)) + b_ref[...]


def ada_modulation(c_all, ada_w, ada_b):
    depth, d, n = ada_w.shape
    rows = c_all.shape[0]
    tn = 512
    return pl.pallas_call(
        _ada_kernel,
        grid=(depth, n // tn),
        in_specs=[
            pl.BlockSpec((rows, d), lambda l, j: (0, 0)),
            pl.BlockSpec((None, d, tn), lambda l, j: (l, 0, j)),
            pl.BlockSpec((None, 1, tn), lambda l, j: (l, 0, j)),
        ],
        out_specs=pl.BlockSpec((None, rows, tn), lambda l, j: (l, 0, j)),
        out_shape=jax.ShapeDtypeStruct((depth, rows, n), F32),
        compiler_params=_cparams(("arbitrary", "arbitrary")),
        name="ada_modulation",
    )(c_all, ada_w, ada_b.reshape(depth, 1, n))


def _norm_mod_kernel(x_ref, g_ref, sc_ref, sh_ref, o_ref):
    x = x_ref[...]
    y = x * lax.rsqrt(jnp.mean(x * x, axis=-1, keepdims=True) + EPS)
    y = y * g_ref[...]
    o_ref[...] = (y * (1.0 + sc_ref[...]) + sh_ref[...]).astype(o_ref.dtype)


def norm_modulate(x, g, scale, shift):
    b, t, d = x.shape
    tt = min(t, 256)
    out = pl.pallas_call(
        _norm_mod_kernel,
        grid=(b, t // tt),
        in_specs=[
            pl.BlockSpec((None, tt, d), lambda i, j: (i, j, 0)),
            pl.BlockSpec((1, d), lambda i, j: (0, 0)),
            pl.BlockSpec((None, 1, d), lambda i, j: (i, 0, 0)),
            pl.BlockSpec((None, 1, d), lambda i, j: (i, 0, 0)),
        ],
        out_specs=pl.BlockSpec((None, tt, d), lambda i, j: (i, j, 0)),
        out_shape=jax.ShapeDtypeStruct((b, t, d), BF16),
        compiler_params=_cparams(("arbitrary", "arbitrary")),
        name="norm_modulate",
    )(x, g.reshape(1, d), scale.reshape(b, 1, d), shift.reshape(b, 1, d))
    return out.reshape(b * t, d)


def _proj_kernel(xp_ref, xs_ref, w_ref, op_ref, os_ref, wb_ref):
    @pl.when(pl.program_id(1) == 0)
    def _():
        wb_ref[...] = w_ref[...].astype(BF16)
        os_ref[...] = _dot(xs_ref[...], wb_ref[...])

    op_ref[...] = _dot(xp_ref[...], wb_ref[...])


def _proj_res_kernel(xp_ref, xs_ref, w_ref, rp_ref, gp_ref, rs_ref, gs_ref, op_ref, os_ref, wb_ref):
    @pl.when(pl.program_id(1) == 0)
    def _():
        wb_ref[...] = w_ref[...].astype(BF16)
        os_ref[...] = rs_ref[...] + gs_ref[...] * _dot(xs_ref[...], wb_ref[...])

    op_ref[...] = rp_ref[...] + gp_ref[...] * _dot(xp_ref[...], wb_ref[...])


def _proj_tiles(mp, k, n):
    tm = 1024 if k <= 4096 else 512
    tn = 512 if k <= 4096 else 256
    tm = min(tm, mp)
    while mp % tm:
        tm //= 2
    tn = min(tn, n)
    while n % tn:
        tn //= 2
    return tm, tn


def project(xp, xs, w, n_cols=None):
    mp, k = xp.shape
    ms = xs.shape[0]
    n = w.shape[1] if n_cols is None else n_cols
    tm, tn = _proj_tiles(mp, k, n)
    return pl.pallas_call(
        _proj_kernel,
        grid=(n // tn, mp // tm),
        in_specs=[
            pl.BlockSpec((tm, k), lambda j, i: (i, 0)),
            pl.BlockSpec((ms, k), lambda j, i: (0, 0)),
            pl.BlockSpec((k, tn), lambda j, i: (0, j)),
        ],
        out_specs=[
            pl.BlockSpec((tm, tn), lambda j, i: (i, j)),
            pl.BlockSpec((ms, tn), lambda j, i: (0, j)),
        ],
        out_shape=[jax.ShapeDtypeStruct((mp, n), F32), jax.ShapeDtypeStruct((ms, n), F32)],
        scratch_shapes=[pltpu.VMEM((k, tn), BF16)],
        compiler_params=_cparams(("arbitrary", "arbitrary")),
        name="project",
    )(xp, xs, w)


def project_residual(xp, xs, w, res_p, gate_p, res_s, gate_s, rows_per_batch):
    mp, k = xp.shape
    ms = xs.shape[0]
    n = w.shape[1]
    tm, tn = _proj_tiles(mp, k, n)
    tm = min(tm, rows_per_batch)
    tpb = rows_per_batch // tm
    nb = gate_p.shape[0]
    return pl.pallas_call(
        _proj_res_kernel,
        grid=(n // tn, mp // tm),
        in_specs=[
            pl.BlockSpec((tm, k), lambda j, i: (i, 0)),
            pl.BlockSpec((ms, k), lambda j, i: (0, 0)),
            pl.BlockSpec((k, tn), lambda j, i: (0, j)),
            pl.BlockSpec((tm, tn), lambda j, i: (i, j)),
            pl.BlockSpec((None, 1, tn), lambda j, i: (i // tpb, 0, j)),
            pl.BlockSpec((ms, tn), lambda j, i: (0, j)),
            pl.BlockSpec((ms, tn), lambda j, i: (0, j)),
        ],
        out_specs=[
            pl.BlockSpec((tm, tn), lambda j, i: (i, j)),
            pl.BlockSpec((ms, tn), lambda j, i: (0, j)),
        ],
        out_shape=[jax.ShapeDtypeStruct((mp, n), F32), jax.ShapeDtypeStruct((ms, n), F32)],
        scratch_shapes=[pltpu.VMEM((k, tn), BF16)],
        compiler_params=_cparams(("arbitrary", "arbitrary")),
        name="project_residual",
    )(xp, xs, w, res_p, gate_p.reshape(nb, 1, n), res_s, gate_s)


def _nsa_prep_kernel(q_ref, a_ref, b_ref, c_ref, g_ref, qn_ref, rows_ref, win_ref):
    def hnorm(x, g):
        return x * lax.rsqrt(jnp.mean(x * x, axis=-1, keepdims=True) + EPS) * g

    g = g_ref[...]
    for j in range(A_HEADS):
        sl = slice(j * HD, (j + 1) * HD)
        qn_ref[:, sl] = hnorm(q_ref[:, sl], g[0:1]).astype(qn_ref.dtype)
    kvw = A_KV * HD
    rows_ref[:, : 2 * kvw] = a_ref[...]
    rows_ref[:, 3 * kvw:] = b_ref[:, kvw:]
    win_ref[:, kvw:] = c_ref[:, kvw:]
    for j in range(A_KV):
        sl = slice(j * HD, (j + 1) * HD)
        rows_ref[:, 2 * kvw + j * HD: 2 * kvw + (j + 1) * HD] = hnorm(b_ref[:, sl], g[2:3])
        win_ref[:, sl] = hnorm(c_ref[:, sl], g[3:4])


def nsa_prep(proj, qk_g):
    m = proj.shape[0]
    tm = min(m, 256)
    qw = A_HEADS * HD
    kv2 = 2 * A_KV * HD
    base = qw // kv2
    return pl.pallas_call(
        _nsa_prep_kernel,
        grid=(m // tm,),
        in_specs=[
            pl.BlockSpec((tm, qw), lambda i: (i, 0)),
            pl.BlockSpec((tm, kv2), lambda i: (i, base)),
            pl.BlockSpec((tm, kv2), lambda i: (i, base + 1)),
            pl.BlockSpec((tm, kv2), lambda i: (i, base + 2)),
            pl.BlockSpec((4, HD), lambda i: (0, 0)),
        ],
        out_specs=[
            pl.BlockSpec((tm, qw), lambda i: (i, 0)),
            pl.BlockSpec((tm, 2 * kv2), lambda i: (i, 0)),
            pl.BlockSpec((tm, kv2), lambda i: (i, 0)),
        ],
        out_shape=[
            jax.ShapeDtypeStruct((m, qw), BF16),
            jax.ShapeDtypeStruct((m, 2 * kv2), F32),
            jax.ShapeDtypeStruct((m, kv2), F32),
        ],
        compiler_params=_cparams(("arbitrary",)),
        name="nsa_prep",
    )(proj, proj, proj, proj, qk_g)


NCK = 2 * A_KV


def _cmp_a_kernel(pt_ref, *refs, n_pages):
    del pt_ref
    pages = refs[:n_pages]
    w_ref = refs[n_pages]
    o_ref = refs[n_pages + 1]
    cpp = PAGE // CMP_STRIDE
    rows = n_pages * cpp * NCK
    acc = None
    for tp in range(CMP_STRIDE // 2):
        pieces = []
        for pg in pages:
            for ch in range(cpp):
                t0 = ch * CMP_STRIDE + 2 * tp
                pieces.append(jnp.concatenate([pg[t0], pg[t0 + 1]], axis=1))
        lhs = jnp.concatenate(pieces, axis=0).astype(BF16)
        d = _dot(lhs, w_ref[tp])
        acc = d if acc is None else acc + d
    is_k = (lax.broadcasted_iota(jnp.int32, (rows, 2 * HD), 0) & (NCK - 1)) < A_KV
    out = jnp.where(is_k, acc[:, :2 * HD], acc[:, 2 * HD:])
    o_ref[...] = out.reshape(n_pages * cpp, NCK, 2 * HD)


def compress_stage_a(store, page_table, w_pairs, n_pages):
    nb, n_tab = page_table.shape
    steps = n_tab // n_pages
    cpp = PAGE // CMP_STRIDE

    def page_spec(k):
        return pl.BlockSpec((None, PAGE, NCK, HD), lambda b, s, pt: (pt[b, s * n_pages + k], 0, 0, 0))

    grid_spec = pltpu.PrefetchScalarGridSpec(
        num_scalar_prefetch=1,
        grid=(nb, steps),
        in_specs=[page_spec(k) for k in range(n_pages)]
        + [pl.BlockSpec(w_pairs.shape, lambda b, s, pt: (0, 0, 0))],
        out_specs=pl.BlockSpec((None, n_pages * cpp, NCK, 2 * HD), lambda b, s, pt: (b, s, 0, 0)),
    )
    return pl.pallas_call(
        functools.partial(_cmp_a_kernel, n_pages=n_pages),
        grid_spec=grid_spec,
        out_shape=jax.ShapeDtypeStruct((nb, n_tab * cpp, NCK, 2 * HD), F32),
        compiler_params=_cparams(("arbitrary", "arbitrary")),
        name="compress_stage_a",
    )(page_table, *([store] * n_pages), w_pairs)


def _cmp_b_kernel(ab_ref, nx_ref, tail_ref, pe_ref, w1_ref, w2_ref, g_ref, o_ref, const_ref):
    j = pl.program_id(1)
    nch = ab_ref.shape[0]
    rows = nch * NCK

    @pl.when((pl.program_id(0) == 0) & (j == 0))
    def _():
        for kind in range(2):
            const = jnp.zeros((8, HD), F32)
            for t in range(CMP_LEN):
                row = jnp.broadcast_to(pe_ref[kind, t:t + 1, :], (8, HD)).astype(BF16)
                const = const + _dot(row, w1_ref[kind, t].astype(BF16))
            const_ref[kind * A_KV:(kind + 1) * A_KV, :] = const[:A_KV]

    ab = ab_ref[...].reshape(rows, 2 * HD)
    nxt = pltpu.roll(ab[:, HD:], rows - NCK, 0).reshape(nch, NCK, HD)
    first_next = jnp.where(j == pl.num_programs(1) - 1, tail_ref[0, :, HD:], nx_ref[0, :, HD:])
    cid = lax.broadcasted_iota(jnp.int32, (nch, NCK, HD), 0)
    nxt = jnp.where(cid == nch - 1, first_next[None], nxt)
    pre = ab[:, :HD].reshape(nch, NCK, HD) + nxt + const_ref[...][None]
    act = _silu(pre).reshape(rows, HD).astype(BF16)
    w2 = jnp.concatenate([w2_ref[0], w2_ref[1]], axis=1).astype(BF16)
    y2 = _dot(act, w2)
    is_k = (lax.broadcasted_iota(jnp.int32, (rows, HD), 0) & (NCK - 1)) < A_KV
    y = jnp.where(is_k, y2[:, :HD], y2[:, HD:])
    yn = y * lax.rsqrt(jnp.mean(y * y, axis=-1, keepdims=True) + EPS) * g_ref[1:2]
    o_ref[...] = jnp.where(is_k, yn, y).reshape(nch, NCK, HD)


def compress_stage_b(ab, tail, pe, w1, w2, qk_g):
    nb, n_ch = ab.shape[:2]
    tc = min(256, n_ch)
    assert n_ch % tc == 0
    nt = n_ch // tc
    out = pl.pallas_call(
        _cmp_b_kernel,
        grid=(nb, nt),
        in_specs=[
            pl.BlockSpec((None, tc, NCK, 2 * HD), lambda b, j: (b, j, 0, 0)),
            pl.BlockSpec((None, 1, NCK, 2 * HD), lambda b, j: (b, jnp.minimum((j + 1) * tc, n_ch - 1), 0, 0)),
            pl.BlockSpec((None, 1, NCK, 2 * HD), lambda b, j: (b, 0, 0, 0)),
            pl.BlockSpec(pe.shape, lambda b, j: (0, 0, 0)),
            pl.BlockSpec(w1.shape, lambda b, j: (0, 0, 0, 0)),
            pl.BlockSpec(w2.shape, lambda b, j: (0, 0, 0)),
            pl.BlockSpec((4, HD), lambda b, j: (0, 0)),
        ],
        out_specs=pl.BlockSpec((None, tc, NCK, HD), lambda b, j: (b, j, 0, 0)),
        out_shape=jax.ShapeDtypeStruct((nb, n_ch, NCK, HD), F32),
        scratch_shapes=[pltpu.VMEM((NCK, HD), F32)],
        compiler_params=_cparams(("arbitrary", "arbitrary")),
        name="compress_stage_b",
    )(ab, ab, tail, pe, w1, w2, qk_g)
    return out.reshape(nb, n_ch, NCK * HD)


def _select_topk(score, blk, k_top):
    sel = jnp.zeros(score.shape, F32)
    for _ in range(k_top):
        m = jnp.max(score, axis=-1, keepdims=True)
        cand = jnp.where(score == m, blk, jnp.int32(1 << 30))
        first = jnp.min(cand, axis=-1, keepdims=True)
        pick = blk == first
        sel = jnp.where(pick, 1.0, sel)
        score = jnp.where(pick, -jnp.inf, score)
    return sel


def _select_topk_t(score, blk, k_top):
    sel = jnp.zeros(score.shape, F32)
    for _ in range(k_top):
        m = jnp.max(score, axis=0, keepdims=True)
        cand = jnp.where(score == m, blk, jnp.int32(1 << 30))
        first = jnp.min(cand, axis=0, keepdims=True)
        pick = blk == first
        sel = jnp.where(pick, 1.0, sel)
        score = jnp.where(pick, -jnp.inf, score)
    return sel


def _bias_select(dist, dmin, dmax, value_of):
    lo = int(_t5_bucket_np(np.array([max(dmin, 0)]))[0])
    hi = int(_t5_bucket_np(np.array([max(dmax, 0)]))[0])
    out = jnp.full(dist.shape, value_of(lo), F32)
    for b in range(lo + 1, hi + 1):
        out = jnp.where(dist >= _BUCKET_START[b], value_of(b), out)
    return out


def _bias_tables_kernel(rb_ref, tsel_ref, tcmp_ref):
    head = pl.program_id(0)
    value_of = lambda b: rb_ref[b, head]
    n_off, nq, nc = tsel_ref.shape[0], tcmp_ref.shape[0], tcmp_ref.shape[1]
    kl = lax.broadcasted_iota(jnp.int32, (LANE, Q_BLOCK), 0)
    ql = lax.broadcasted_iota(jnp.int32, (LANE, Q_BLOCK), 1)
    for o in range(n_off):
        base = (o - 1) * Q_BLOCK
        tsel_ref[o] = _bias_select(base + ql - kl, base - (LANE - 1), base + Q_BLOCK - 1, value_of)
    cend = lax.broadcasted_iota(jnp.int32, (nc, Q_BLOCK), 0) * CMP_STRIDE + (CMP_LEN - 1)
    qc = lax.broadcasted_iota(jnp.int32, (nc, Q_BLOCK), 1)
    for i in range(nq):
        base = i * Q_BLOCK
        tcmp_ref[i] = _bias_select(base + qc - cend, base - ((nc - 1) * CMP_STRIDE + CMP_LEN - 1),
                                   base + Q_BLOCK - 1, value_of)


def prompt_bias_tables(rel_bias, nq, n_cmp):
    return pl.pallas_call(
        _bias_tables_kernel,
        grid=(A_HEADS,),
        in_specs=[pl.BlockSpec(memory_space=pltpu.SMEM)],
        out_specs=[
            pl.BlockSpec((None, nq + 1, None, LANE, Q_BLOCK), lambda h: (h // A_HPG, 0, h % A_HPG, 0, 0)),
            pl.BlockSpec((None, nq, None, n_cmp, Q_BLOCK), lambda h: (h // A_HPG, 0, h % A_HPG, 0, 0)),
        ],
        out_shape=[
            jax.ShapeDtypeStruct((A_KV, nq + 1, A_HPG, LANE, Q_BLOCK), F32),
            jax.ShapeDtypeStruct((A_KV, nq, A_HPG, n_cmp, Q_BLOCK), F32),
        ],
        compiler_params=_cparams(("arbitrary",)),
        name="prompt_bias_tables",
    )(rel_bias)


def _bias_rows_kernel(rb_ref, cmp_ref, sel_ref, win_ref, *, past_len, n_q):
    head = pl.program_id(0)
    value_of = lambda b: rb_ref[b, head]

    def fill(ref, key_pos, kmin, kmax, q0):
        q = lax.broadcasted_iota(jnp.int32, ref.shape, 0)
        k = lax.broadcasted_iota(jnp.int32, ref.shape, 1)
        ref[...] = _bias_select(q0 + q - key_pos(k), q0 - kmax, q0 + n_q - 1 - kmin, value_of)

    nc, nk, nw = cmp_ref.shape[1], sel_ref.shape[1], win_ref.shape[1]
    fill(cmp_ref, lambda k: k * CMP_STRIDE + (CMP_LEN - 1), CMP_LEN - 1, (nc - 1) * CMP_STRIDE + CMP_LEN - 1, past_len)
    fill(sel_ref, lambda k: k, 0, nk - 1, past_len)
    fill(win_ref, lambda k: k, 0, nw - 1, WINDOW)


def sample_bias_rows(rel_bias, past_len, n_q, n_cmp, n_keys, n_win):
    shapes = [(A_HEADS * n_q, n) for n in (n_cmp, n_keys, n_win)]
    return pl.pallas_call(
        functools.partial(_bias_rows_kernel, past_len=past_len, n_q=n_q),
        grid=(A_HEADS,),
        in_specs=[pl.BlockSpec(memory_space=pltpu.SMEM)],
        out_specs=[pl.BlockSpec((n_q, s[1]), lambda h: (h, 0)) for s in shapes],
        out_shape=[jax.ShapeDtypeStruct(s, F32) for s in shapes],
        compiler_params=_cparams(("arbitrary",)),
        name="sample_bias_rows",
    )(rel_bias)


def _nsa_prompt_kernel(q_ref, kc_ref, vc_ref, ks_ref, vs_ref, kw_ref, vw_ref, zc_ref, zs_ref, zw_ref,
                       gt_ref, tsel_ref, tcmp_ref, msel_ref, exp_ref, o_ref,
                       pt_ref, acc_ref, m_ref, l_ref, madd_ref, mix_ref, *, n_sel_pad, n_sel, k_top, scale):
    i = pl.program_id(2)
    qb = Q_BLOCK
    kt = 2 * LANE
    n_kt = madd_ref.shape[0]
    nc = kc_ref.shape[0]
    gates = _sigmoid(gt_ref[...])
    hs = [slice(h * HD, (h + 1) * HD) for h in range(A_HPG)]

    def emit(branch, h, o_t, first):
        z_ref = (zc_ref, zs_ref, zw_ref)[branch]
        val = o_t.T * _silu(z_ref[:, hs[h]])
        if first:
            mix_ref[:, hs[h]] = val
        else:
            mix_ref[:, hs[h]] = mix_ref[:, hs[h]] + val

    kc = kc_ref[...].astype(BF16)
    vc = vc_ref[...].astype(BF16)
    cend = lax.broadcasted_iota(jnp.int32, (nc, qb), 0) * CMP_STRIDE + (CMP_LEN - 1)
    qpos_c = i * qb + lax.broadcasted_iota(jnp.int32, (nc, qb), 1)
    cvalid = cend <= qpos_c
    imp = None
    for h in range(A_HPG):
        s = _dot_nt(kc, q_ref[:, hs[h]]) * scale + tcmp_ref[h]
        s = jnp.where(cvalid, s, NEG)
        mx = jnp.max(s, axis=0, keepdims=True)
        e = jnp.where(cvalid, jnp.exp(s - mx), 0.0)
        p = e / jnp.maximum(jnp.sum(e, axis=0, keepdims=True), TINY)
        imp = p if imp is None else imp + p
        emit(0, h, _dot_tn(vc, p.astype(BF16)) * gates[h:h + 1, :], True)

    hi = imp.astype(BF16)
    r1 = imp - hi.astype(F32)
    mid = r1.astype(BF16)
    lo = (r1 - mid.astype(F32)).astype(BF16)
    msel = msel_ref[...]
    imp_sel = _dot(msel, hi) + _dot(msel, mid) + _dot(msel, lo)
    blk = lax.broadcasted_iota(jnp.int32, (n_sel_pad, qb), 0)
    qpos = i * qb + lax.broadcasted_iota(jnp.int32, (n_sel_pad, qb), 1)
    cur = jnp.right_shift(qpos, int(math.log2(SEL_BLOCK)))
    forced = (blk == 0) | (blk == cur) | (blk == cur - 1)
    valid = blk * SEL_BLOCK <= qpos
    score = jnp.where(valid, jnp.where(forced, FORCE_SCORE, imp_sel), -1.0)
    score = jnp.where(blk < n_sel, score, -jnp.inf)
    sel = _select_topk_t(score, blk, k_top).astype(BF16)
    kpos2 = lax.broadcasted_iota(jnp.int32, (kt, qb), 0)
    qpos2 = i * qb + lax.broadcasted_iota(jnp.int32, (kt, qb), 1)
    for jj in range(n_kt):
        hit = _dot(exp_ref[jj * kt:(jj + 1) * kt, :], sel)
        ok = (hit > 0.5) & ((jj * kt + kpos2) <= qpos2)
        madd_ref[jj] = jnp.where(ok, 0.0, NEG)

    def reset():
        m_ref[...] = jnp.full(m_ref.shape, NEG, F32)
        l_ref[...] = jnp.zeros(l_ref.shape, F32)
        acc_ref[...] = jnp.zeros(acc_ref.shape, F32)

    def tile_update(k, v, nk, bias_of, madd):
        alphas = []
        for h in range(A_HPG):
            s = _dot_nt(k, q_ref[:, hs[h]]) * scale + bias_of(h) + madd
            m_old = m_ref[h:h + 1, :]
            m_new = jnp.maximum(m_old, jnp.max(s, axis=0, keepdims=True))
            alpha = jnp.exp(m_old - m_new)
            p = jnp.exp(s - m_new)
            l_ref[h:h + 1, :] = alpha * l_ref[h:h + 1, :] + jnp.sum(p, axis=0, keepdims=True)
            m_ref[h:h + 1, :] = m_new
            pt_ref[0:nk, hs[h]] = p.astype(BF16)
            alphas.append(alpha)
        alpha_all = jnp.concatenate(alphas, axis=1)
        acc_ref[...] = acc_ref[...] * alpha_all + _dot_tn(v, pt_ref[0:nk, :])

    def finish(branch):
        for h in range(A_HPG):
            w = gates[branch * A_HPG + h: branch * A_HPG + h + 1, :] / l_ref[h:h + 1, :]
            emit(branch, h, acc_ref[:, hs[h]] * w, False)

    reset()

    def sel_body(jj, carry):
        k0 = pl.multiple_of(jj * kt, kt)
        k = ks_ref[pl.ds(k0, kt), :].astype(BF16)
        v = vs_ref[pl.ds(k0, kt), :].astype(BF16)
        off = i - 2 * jj
        bias_of = lambda h: jnp.concatenate([tsel_ref[off + 1, h], tsel_ref[off, h]], axis=0)
        tile_update(k, v, kt, bias_of, madd_ref[jj])
        return carry

    lax.fori_loop(0, i // 2 + 1, sel_body, 0)
    finish(1)

    reset()
    kl = lax.broadcasted_iota(jnp.int32, (qb, qb), 0)
    ql = lax.broadcasted_iota(jnp.int32, (qb, qb), 1)

    def win_body(t, carry):
        k0 = pl.multiple_of((i - t) * qb, qb)
        k = kw_ref[pl.ds(k0, qb), :].astype(BF16)
        v = vw_ref[pl.ds(k0, qb), :].astype(BF16)
        dist = (t * qb + ql) - kl
        madd = jnp.where((dist >= 0) & (dist < WINDOW), 0.0, NEG)
        tile_update(k, v, qb, lambda h: tsel_ref[t + 1, h], madd)
        return carry

    lax.fori_loop(0, jnp.minimum(i, WINDOW // qb) + 1, win_body, 0)
    finish(2)
    o_ref[...] = mix_ref[...].astype(o_ref.dtype)


def nsa_prompt_attention(qn, cmp_kv, rows, win, proj, gates_g, tsel, tcmp, msel, expand, nb, t):
    nq = t // Q_BLOCK
    qw = A_HPG * HD
    n_sel = -(-t // SEL_BLOCK)
    k_top = min(SEL_TOPK, n_sel)
    n_kt = t // (2 * LANE)
    n_cmp = cmp_kv.shape[1]
    n_sel_pad = msel.shape[0]
    zbase = (A_HEADS * HD + 6 * A_KV * HD) // qw
    nz = A_HEADS * HD // qw
    kernel = functools.partial(_nsa_prompt_kernel, n_sel_pad=n_sel_pad, n_sel=n_sel, k_top=k_top,
                               scale=HD ** -0.5)
    return pl.pallas_call(
        kernel,
        grid=(A_KV, nb, nq),
        in_specs=[
            pl.BlockSpec((Q_BLOCK, qw), lambda g, b, i: (b * nq + i, g)),
            pl.BlockSpec((None, n_cmp, HD), lambda g, b, i: (b, 0, g)),
            pl.BlockSpec((None, n_cmp, HD), lambda g, b, i: (b, 0, A_KV + g)),
            pl.BlockSpec((t, HD), lambda g, b, i: (b, 2 * A_KV + g)),
            pl.BlockSpec((t, HD), lambda g, b, i: (b, 3 * A_KV + g)),
            pl.BlockSpec((t, HD), lambda g, b, i: (b, g)),
            pl.BlockSpec((t, HD), lambda g, b, i: (b, A_KV + g)),
            pl.BlockSpec((Q_BLOCK, qw), lambda g, b, i: (b * nq + i, zbase + g)),
            pl.BlockSpec((Q_BLOCK, qw), lambda g, b, i: (b * nq + i, zbase + nz + g)),
            pl.BlockSpec((Q_BLOCK, qw), lambda g, b, i: (b * nq + i, zbase + 2 * nz + g)),
            pl.BlockSpec((None, 3 * A_HPG, Q_BLOCK), lambda g, b, i: (g, 0, b * nq + i)),
            pl.BlockSpec((None, nq + 1, A_HPG, LANE, Q_BLOCK), lambda g, b, i: (g, 0, 0, 0, 0)),
            pl.BlockSpec((None, None, A_HPG, n_cmp, Q_BLOCK), lambda g, b, i: (g, i, 0, 0, 0)),
            pl.BlockSpec((n_sel_pad, n_cmp), lambda g, b, i: (0, 0)),
            pl.BlockSpec((t, n_sel_pad), lambda g, b, i: (0, 0)),
        ],
        out_specs=pl.BlockSpec((Q_BLOCK, qw), lambda g, b, i: (b * nq + i, g)),
        out_shape=jax.ShapeDtypeStruct((nb * t, A_HEADS * HD), BF16),
        scratch_shapes=[
            pltpu.VMEM((2 * LANE, qw), BF16),
            pltpu.VMEM((HD, qw), F32),
            pltpu.VMEM((A_HPG, Q_BLOCK), F32),
            pltpu.VMEM((A_HPG, Q_BLOCK), F32),
            pltpu.VMEM((n_kt, 2 * LANE, Q_BLOCK), F32),
            pltpu.VMEM((Q_BLOCK, qw), F32),
        ],
        compiler_params=_cparams(("arbitrary", "arbitrary", "arbitrary")),
        name="nsa_prompt_attention",
    )(qn, cmp_kv, cmp_kv, rows, rows, win, win, proj, proj, proj, gates_g, tsel, tcmp, msel, expand)


def _diag_blocks(o_full, rows_per_group):
    return jnp.concatenate(
        [o_full[g * rows_per_group:(g + 1) * rows_per_group, g * HD:(g + 1) * HD] for g in range(A_KV)], axis=0)


def _nsa_sample_cmp_kernel(q_ref, kc_ref, vc_ref, bias_ref, msel_ref, oc_ref, sel_ref,
                           *, past_len, n_sel, k_top, n_q, scale, blocks_per_step):
    n_cmp = kc_ref.shape[0]
    rows = q_ref.shape[0]
    rpg = A_HPG * n_q
    q = q_ref[...]
    s = _dot_nt(q, kc_ref[...].astype(BF16)) * scale + bias_ref[...]
    rowi = lax.broadcasted_iota(jnp.int32, (rows, n_cmp), 0)
    ci = lax.broadcasted_iota(jnp.int32, (rows, n_cmp), 1)
    qpos = past_len + (rowi & (n_q - 1))
    valid = (ci * CMP_STRIDE + (CMP_LEN - 1)) <= qpos
    s = jnp.where(valid, s, NEG)
    mx = jnp.max(s, axis=-1, keepdims=True)
    e = jnp.where(valid, jnp.exp(s - mx), 0.0)
    p = e / jnp.maximum(jnp.sum(e, axis=-1, keepdims=True), TINY)
    oc_ref[...] = _diag_blocks(_dot(p.astype(BF16), vc_ref[...].astype(BF16)), rpg)
    imps = []
    for g in range(A_KV):
        acc = p[g * rpg: g * rpg + n_q]
        for h in range(1, A_HPG):
            acc = acc + p[g * rpg + h * n_q: g * rpg + (h + 1) * n_q]
        imps.append(acc)
    imp = jnp.concatenate(imps, axis=0)
    imp_sel = _split3_dot(imp, msel_ref[...])
    shp = imp_sel.shape
    blk = lax.broadcasted_iota(jnp.int32, shp, 1)
    qp = past_len + (lax.broadcasted_iota(jnp.int32, shp, 0) & (n_q - 1))
    cur = jnp.right_shift(qp, int(math.log2(SEL_BLOCK)))
    forced = (blk == 0) | (blk == cur) | (blk == cur - 1)
    ok = blk * SEL_BLOCK <= qp
    score = jnp.where(ok, jnp.where(forced, FORCE_SCORE, imp_sel), -1.0)
    score = jnp.where(blk < n_sel, score, -jnp.inf)
    sel = _select_topk(score, blk, k_top)
    for st in range(sel_ref.shape[0]):
        sel_ref[st] = sel[:, st * blocks_per_step:(st + 1) * blocks_per_step]


def _nsa_sample_sel_kernel(pt_ref, *refs, n_pages, n_steps, past_len, n_q, scale):
    del pt_ref
    pages = refs[:n_pages]
    q_ref, tail_ref, bias_ref, sel_ref, exp_ref, o_ref, m_ref, l_ref, acc_ref = refs[n_pages:]
    s_id = pl.program_id(1)
    kvw = A_KV * HD
    rows = q_ref.shape[0]
    rpg = A_HPG * n_q

    @pl.when(s_id == 0)
    def _():
        m_ref[...] = jnp.full(m_ref.shape, NEG, F32)
        l_ref[...] = jnp.zeros(l_ref.shape, F32)
        acc_ref[...] = jnp.zeros(acc_ref.shape, F32)

    def update(k, v, nk, key0):
        hit = _dot(sel_ref[...].astype(BF16), exp_ref[:, :nk])
        kpos = key0 + lax.broadcasted_iota(jnp.int32, hit.shape, 1)
        qpos = past_len + (lax.broadcasted_iota(jnp.int32, hit.shape, 0) & (n_q - 1))
        madd = jnp.where((hit > 0.5) & (kpos <= qpos), 0.0, NEG)
        madd = jnp.concatenate(
            [madd[g * n_q:(g + 1) * n_q] for g in range(A_KV) for _ in range(A_HPG)], axis=0)
        s = _dot_nt(q_ref[...], k) * scale + bias_ref[:, :nk] + madd
        m_old = m_ref[...]
        m_new = jnp.maximum(m_old, jnp.max(s, axis=-1, keepdims=True))
        alpha = jnp.exp(m_old - m_new)
        p = jnp.exp(s - m_new)
        l_ref[...] = alpha * l_ref[...] + jnp.sum(p, axis=-1, keepdims=True)
        acc_ref[...] = alpha * acc_ref[...] + _dot(p.astype(BF16), v)
        m_ref[...] = m_new

    @pl.when(s_id < n_steps)
    def _():
        def gather(c0):
            cols = [jnp.concatenate([pg[:, c0 + g, :] for pg in pages], axis=0) for g in range(A_KV)]
            return jnp.concatenate(cols, axis=1).astype(BF16)

        update(gather(0), gather(A_KV), n_pages * PAGE, s_id * (n_pages * PAGE))

    @pl.when(s_id == n_steps)
    def _():
        update(tail_ref[:, :kvw].astype(BF16), tail_ref[:, kvw:].astype(BF16), PAGE, past_len)
        o_ref[...] = _diag_blocks(acc_ref[...] / l_ref[...], rpg)


def _nsa_sample_mix_kernel(q_ref, win_ref, bias_ref, oc_ref, os_ref, z_ref, gt_ref, o_ref, *, n_q, scale):
    kvw = A_KV * HD
    rows = q_ref.shape[0]
    rpg = A_HPG * n_q
    nk = win_ref.shape[0]
    s = _dot_nt(q_ref[...], win_ref[:, :kvw].astype(BF16)) * scale + bias_ref[...]
    qi = lax.broadcasted_iota(jnp.int32, (rows, nk), 0) & (n_q - 1)
    ki = lax.broadcasted_iota(jnp.int32, (rows, nk), 1)
    dist = qi + WINDOW - ki
    s = s + jnp.where((dist >= 0) & (dist < WINDOW), 0.0, NEG)
    mx = jnp.max(s, axis=-1, keepdims=True)
    e = jnp.exp(s - mx)
    p = e / jnp.sum(e, axis=-1, keepdims=True)
    ow = _diag_blocks(_dot(p.astype(BF16), win_ref[:, kvw:].astype(BF16)), rpg)
    gates = _sigmoid(gt_ref[...])
    width = A_HEADS * HD
    for hh in range(A_HEADS):
        r = slice(hh * n_q, (hh + 1) * n_q)
        c = slice(hh * HD, (hh + 1) * HD)
        val = None
        for br, o in enumerate((oc_ref, os_ref, ow)):
            zc = slice(br * width + hh * HD, br * width + (hh + 1) * HD)
            gcol = gates[:, br * A_HEADS + hh: br * A_HEADS + hh + 1]
            term = gcol * o[r, :] * _silu(z_ref[:, zc])
            val = term if val is None else val + term
        o_ref[:, c] = val.astype(o_ref.dtype)


def nsa_sample_attention(qbd, cmp_kv, cache2d, page_table, tail_sel, win_all, z_s, gl_s,
                         b_cmp, b_sel, b_win, msel, expand, past_len, n_q):
    nb, rows, kvw = qbd.shape
    n_cmp = cmp_kv.shape[1]
    n_tab = page_table.shape[1]
    n_pages = min(PAGES_PER_STEP, n_tab)
    n_steps = n_tab // n_pages
    bps = n_pages * PAGE // SEL_BLOCK
    n_sel = -(-(past_len + n_q) // SEL_BLOCK)
    k_top = min(SEL_TOPK, n_sel)
    gq = A_KV * n_q
    scale = HD ** -0.5

    oc, selmask = pl.pallas_call(
        functools.partial(_nsa_sample_cmp_kernel, past_len=past_len, n_sel=n_sel, k_top=k_top,
                          n_q=n_q, scale=scale, blocks_per_step=bps),
        grid=(nb,),
        in_specs=[
            pl.BlockSpec((None, rows, kvw), lambda b: (b, 0, 0)),
            pl.BlockSpec((None, n_cmp, kvw), lambda b: (b, 0, 0)),
            pl.BlockSpec((None, n_cmp, kvw), lambda b: (b, 0, 1)),
            pl.BlockSpec(b_cmp.shape, lambda b: (0, 0)),
            pl.BlockSpec(msel.shape, lambda b: (0, 0)),
        ],
        out_specs=[
            pl.BlockSpec((None, rows, HD), lambda b: (b, 0, 0)),
            pl.BlockSpec((None, n_steps + 1, gq, bps), lambda b: (b, 0, 0, 0)),
        ],
        out_shape=[
            jax.ShapeDtypeStruct((nb, rows, HD), F32),
            jax.ShapeDtypeStruct((nb, n_steps + 1, gq, bps), F32),
        ],
        compiler_params=_cparams(("arbitrary",)),
        name="nsa_sample_cmp",
    )(qbd, cmp_kv, cmp_kv, b_cmp, msel)

    half = 2 * kvw
    last = n_steps - 1

    def page_spec(k):
        return pl.BlockSpec(
            (None, PAGE, NCK, HD),
            lambda b, s, pt: (pt[b, jnp.minimum(s, last) * n_pages + k], 0, 1, 0))

    nkeys = n_pages * PAGE
    grid_spec = pltpu.PrefetchScalarGridSpec(
        num_scalar_prefetch=1,
        grid=(nb, n_steps + 1),
        in_specs=[page_spec(k) for k in range(n_pages)] + [
            pl.BlockSpec((None, rows, kvw), lambda b, s, pt: (b, 0, 0)),
            pl.BlockSpec((None, PAGE, half), lambda b, s, pt: (b, 0, 0)),
            pl.BlockSpec((rows, nkeys), lambda b, s, pt: (0, s)),
            pl.BlockSpec((None, None, gq, bps), lambda b, s, pt: (b, s, 0, 0)),
            pl.BlockSpec(expand.shape, lambda b, s, pt: (0, 0)),
        ],
        out_specs=pl.BlockSpec((None, rows, HD), lambda b, s, pt: (b, 0, 0)),
        scratch_shapes=[
            pltpu.VMEM((rows, 1), F32),
            pltpu.VMEM((rows, 1), F32),
            pltpu.VMEM((rows, kvw), F32),
        ],
    )
    osel = pl.pallas_call(
        functools.partial(_nsa_sample_sel_kernel, n_pages=n_pages, n_steps=n_steps,
                          past_len=past_len, n_q=n_q, scale=scale),
        grid_spec=grid_spec,
        out_shape=jax.ShapeDtypeStruct((nb, rows, HD), F32),
        compiler_params=_cparams(("arbitrary", "arbitrary")),
        name="nsa_sample_sel",
    )(page_table, *([cache2d] * n_pages), qbd, tail_sel, b_sel, selmask, expand)

    nkw = win_all.shape[1]
    zw = z_s.shape[1]
    return pl.pallas_call(
        functools.partial(_nsa_sample_mix_kernel, n_q=n_q, scale=scale),
        grid=(nb,),
        in_specs=[
            pl.BlockSpec((None, rows, kvw), lambda b: (b, 0, 0)),
            pl.BlockSpec((None, nkw, 2 * kvw), lambda b: (b, 0, 0)),
            pl.BlockSpec(b_win.shape, lambda b: (0, 0)),
            pl.BlockSpec((None, rows, HD), lambda b: (b, 0, 0)),
            pl.BlockSpec((None, rows, HD), lambda b: (b, 0, 0)),
            pl.BlockSpec((n_q, zw), lambda b: (b, 0)),
            pl.BlockSpec((n_q, 3 * A_HEADS), lambda b: (b, 0)),
        ],
        out_specs=pl.BlockSpec((n_q, A_HEADS * HD), lambda b: (b, 0)),
        out_shape=jax.ShapeDtypeStruct((nb * n_q, A_HEADS * HD), F32),
        compiler_params=_cparams(("arbitrary",)),
        name="nsa_sample_mix",
    )(qbd, win_all, b_win, oc, osel, z_s, gl_s)


def _retention_kernel(q_ref, k_ref, v_ref, g_ref, cos_ref, sin_ref, dm_ref, qd_ref, kd_ref, cd_ref,
                      s0_ref, o_ref, s_ref, *, dk):
    c = pl.program_id(2)
    half = dk // 2

    @pl.when(c == 0)
    def _():
        s_ref[...] = s0_ref[...]

    cos = cos_ref[...]
    sin = sin_ref[...]

    def rot(x):
        x1, x2 = x[:, :half], x[:, half:]
        return jnp.concatenate([x1 * cos - x2 * sin, x1 * sin + x2 * cos], axis=1)

    q = rot(q_ref[...])
    k = rot(k_ref[...]) * (dk ** -0.5)
    v = v_ref[...].astype(BF16)
    qb = q.astype(BF16)
    att = _dot_nt(qb, k.astype(BF16)) * dm_ref[...]
    state = s_ref[...]
    o = _dot(att.astype(BF16), v) + _dot(qb, state.astype(BF16)) * qd_ref[...]
    s_ref[...] = state * cd_ref[...] + _dot_tn((k * kd_ref[...]).astype(BF16), v)
    o = o * lax.rsqrt(jnp.mean(o * o, axis=-1, keepdims=True) + EPS)
    o_ref[...] = (o * _silu(g_ref[...])).astype(o_ref.dtype)


def retention(proj, state0, q0, nb, t):
    _, nh, dk, dv = state0.shape
    cs = R_CHUNK if t % R_CHUNK == 0 else t
    n = t // cs
    half = dk // 2
    pos = (q0 + jnp.arange(t)).astype(F32)
    inv = jnp.power(ROPE_BASE, -jnp.arange(half, dtype=F32) / half)
    ang = pos[:, None] * inv[None, :]
    cos, sin = jnp.cos(ang), jnp.sin(ang)
    log_g = jnp.log1p(-jnp.exp2(-5.0 - jnp.arange(nh, dtype=F32)))
    ii = jnp.arange(cs, dtype=F32)
    rel = ii[:, None] - ii[None, :]
    causal = rel >= 0
    dmat = jnp.where(causal, jnp.exp(jnp.where(causal, rel, 0.0) * log_g[:, None, None]), 0.0)
    q_dec = jnp.exp((ii + 1.0) * log_g[:, None])[:, :, None]
    k_dec = jnp.exp((cs - 1.0 - ii) * log_g[:, None])[:, :, None]
    c_dec = jnp.exp(cs * log_g)[:, None, None]
    kb = nh * dk // dk
    vb0 = 2 * nh * dk // dv
    return pl.pallas_call(
        functools.partial(_retention_kernel, dk=dk),
        grid=(nb, nh, n),
        in_specs=[
            pl.BlockSpec((cs, dk), lambda b, h, c: (b * n + c, h)),
            pl.BlockSpec((cs, dk), lambda b, h, c: (b * n + c, kb + h)),
            pl.BlockSpec((cs, dv), lambda b, h, c: (b * n + c, vb0 + h)),
            pl.BlockSpec((cs, dv), lambda b, h, c: (b * n + c, vb0 + nh + h)),
            pl.BlockSpec((cs, half), lambda b, h, c: (c, 0)),
            pl.BlockSpec((cs, half), lambda b, h, c: (c, 0)),
            pl.BlockSpec((None, cs, cs), lambda b, h, c: (h, 0, 0)),
            pl.BlockSpec((None, cs, 1), lambda b, h, c: (h, 0, 0)),
            pl.BlockSpec((None, cs, 1), lambda b, h, c: (h, 0, 0)),
            pl.BlockSpec((None, 1, 1), lambda b, h, c: (h, 0, 0)),
            pl.BlockSpec((None, None, dk, dv), lambda b, h, c: (b, h, 0, 0)),
        ],
        out_specs=[
            pl.BlockSpec((cs, dv), lambda b, h, c: (b * n + c, h)),
            pl.BlockSpec((None, None, dk, dv), lambda b, h, c: (b, h, 0, 0)),
        ],
        out_shape=[
            jax.ShapeDtypeStruct((nb * t, nh * dv), BF16 if cs % 16 == 0 else F32),
            jax.ShapeDtypeStruct(state0.shape, F32),
        ],
        compiler_params=_cparams(("arbitrary", "arbitrary", "arbitrary")),
        name="retention",
    )(proj, proj, proj, proj, cos, sin, dmat, q_dec, k_dec, c_dec, state0)


def _t5_bucket_np(dist):
    n = np.maximum(dist, 0)
    exact = N_BUCKETS // 2
    logv = np.log(np.maximum(n, 1).astype(np.float32) / np.float32(exact)) / np.float32(
        math.log(MAX_DISTANCE / exact))
    large = np.minimum(exact + (logv * np.float32(N_BUCKETS - exact)).astype(np.int32), N_BUCKETS - 1)
    return np.where(n < exact, n, large).astype(np.int32)


def _bucket_starts():
    bk = _t5_bucket_np(np.arange(4 * MAX_DISTANCE))
    assert np.all(np.diff(bk) >= 0) and bk[-1] == N_BUCKETS - 1
    return [int(np.argmax(bk >= b)) for b in range(N_BUCKETS)]


_BUCKET_START = _bucket_starts()


def _sel_matrix(n_cmp, n_cmp_pad, n_sel_pad):
    c = np.arange(n_cmp_pad)[:, None]
    j = np.arange(n_sel_pad)[None, :]
    a = (c >= SEL_RATIO * j) & (c < SEL_RATIO * j + SEL_RATIO)
    b = (c >= SEL_RATIO * j - 1) & (c < SEL_RATIO * j + SEL_RATIO - 1)
    m = (a.astype(np.float32) + b.astype(np.float32)) * (c < n_cmp)
    return jnp.asarray(m, BF16)


def _expand_matrix(n_blk_pad, n_keys):
    blk = np.arange(n_blk_pad)[:, None]
    key = np.arange(n_keys)[None, :]
    return jnp.asarray((key // SEL_BLOCK == blk).astype(np.float32), BF16)


def _w1_pairs(w1):
    w = w1.reshape(2, 2, CMP_STRIDE // 2, 2, HD, HD)
    w = jnp.transpose(w, (2, 3, 4, 0, 1, 5))
    return w.reshape(CMP_STRIDE // 2, 2 * HD, 4 * HD).astype(BF16)


def kernel(x_prompt, x_sample, c_prompt, c_sample, cache_nsa_kv, cache_nsa_win, state_ret, page_table,
           norm_g, ada_w, ada_b, rel_bias, a_w_in, a_w_out, a_cmp_pe, a_cmp_w1, a_cmp_w2, a_qk_g,
           r_w_in, r_w_out):
    nbp, t, d = x_prompt.shape
    nbs, ts, _ = x_sample.shape
    n_tab = page_table.shape[1]
    past_len = n_tab * PAGE
    kvw = A_KV * HD
    qw = A_HEADS * HD
    assert t % (2 * LANE) == 0 and t >= WINDOW and ts == 8 and past_len >= WINDOW
    assert cache_nsa_win.shape[2] == WINDOW

    n_c = nbp + nbs
    pad_c = (-n_c) % 8
    c_all = jnp.concatenate([c_prompt, c_sample, jnp.zeros((pad_c, d), F32)], axis=0)
    mod = ada_modulation(c_all, ada_w, ada_b).reshape(ada_w.shape[0], n_c + pad_c, 3, d)

    def mods(layer):
        m = mod[layer]
        return (m[:nbp, 0], m[:nbp, 1], m[:nbp, 2]), (m[nbp:n_c, 0], m[nbp:n_c, 1], m[nbp:n_c, 2])

    (sh_p, sc_p, gt_p), (sh_s, sc_s, gt_s) = mods(0)
    hp = norm_modulate(x_prompt, norm_g[0], sc_p, sh_p)
    hs = norm_modulate(x_sample, norm_g[0], sc_s, sh_s)
    w_in = a_w_in[0]
    n_main = w_in.shape[1] - 3 * A_HEADS
    proj_p, proj_s = project(hp, hs, w_in, n_cols=n_main)
    w_gate = jnp.pad(w_in[:, n_main:], ((0, 0), (0, LANE - 3 * A_HEADS)))
    gl_p, gl_s = project(hp, hs, w_gate)
    gl_p, gl_s = gl_p[:, :3 * A_HEADS], gl_s[:, :3 * A_HEADS]
    qk_g = a_qk_g[0]

    qn_p, rows_p, win_p = nsa_prep(proj_p, qk_g)
    qn_s, rows_s, win_s = nsa_prep(proj_s, qk_g)

    w_pairs = _w1_pairs(a_cmp_w1[0])
    pe = a_cmp_pe[0]
    w1 = a_cmp_w1[0].reshape(2, CMP_LEN, HD, HD)
    w2 = a_cmp_w2[0]
    cpp = PAGE // CMP_STRIDE

    pages_p = t // PAGE
    ident = jnp.arange(nbp * pages_p, dtype=jnp.int32).reshape(nbp, pages_p)
    ab_p = compress_stage_a(rows_p.reshape(nbp * pages_p, PAGE, 4 * A_KV, HD), ident, w_pairs,
                            min(PAGES_PER_STEP, pages_p))
    zero_tail = jnp.zeros((nbp, cpp, NCK, 2 * HD), F32)
    cmp_p = compress_stage_b(ab_p, zero_tail, pe, w1, w2, qk_g)

    nq = t // Q_BLOCK
    n_ch_p = t // CMP_STRIDE
    tsel, tcmp = prompt_bias_tables(rel_bias, nq, n_ch_p)
    n_cmp_p = t // CMP_STRIDE - 1
    n_sel_p = -(-t // SEL_BLOCK)
    n_sel_pad_p = -(-n_sel_p // 16) * 16
    msel_p = _sel_matrix(n_cmp_p, n_ch_p, n_sel_pad_p).T
    expand_p = _expand_matrix(n_sel_pad_p, t).T
    gates_g = jnp.transpose(gl_p.reshape(nbp * t, 3, A_KV, A_HPG), (2, 1, 3, 0)).reshape(A_KV, 3 * A_HPG, nbp * t)
    mixed_p = nsa_prompt_attention(qn_p, cmp_p, rows_p, win_p, proj_p, gates_g,
                                   tsel, tcmp, msel_p, expand_p, nbp, t)

    n_pool = cache_nsa_kv.shape[1]
    cache4d = cache_nsa_kv[0].reshape(n_pool, PAGE, 4 * A_KV, HD)
    n_pages = min(PAGES_PER_STEP, n_tab)
    ab_s = compress_stage_a(cache4d, page_table, w_pairs, n_pages)
    tail_rows = jnp.pad(rows_s.reshape(nbs, ts, 4 * kvw), ((0, 0), (0, PAGE - ts), (0, 0)))
    ident_s = jnp.arange(nbs, dtype=jnp.int32).reshape(nbs, 1)
    ab_tail = compress_stage_a(tail_rows.reshape(nbs, PAGE, 4 * A_KV, HD), ident_s, w_pairs, 1)
    cmp_s = compress_stage_b(ab_s, ab_tail, pe, w1, w2, qk_g)

    n_cmp_s = cmp_s.shape[1]
    n_sel_s = -(-(past_len + ts) // SEL_BLOCK)
    n_steps = n_tab // n_pages
    bps = n_pages * PAGE // SEL_BLOCK
    n_sel_pad = (n_steps + 1) * bps
    nkeys = (n_steps + 1) * n_pages * PAGE
    nkw = WINDOW + LANE
    b_cmp, b_sel, b_win = sample_bias_rows(rel_bias, past_len, ts, n_cmp_s, nkeys, nkw)
    msel_s = _sel_matrix(n_cmp_s, n_cmp_s, n_sel_pad)
    expand_s = _expand_matrix(bps, n_pages * PAGE)
    q5 = qn_s.reshape(nbs, ts, A_KV, A_HPG, HD)
    eye = jnp.eye(A_KV, dtype=BF16)
    qbd = jnp.einsum('bqghd,ge->bghqed', q5, eye).reshape(nbs, A_HEADS * ts, kvw)
    win_new = win_s.reshape(nbs, ts, 2 * kvw)
    win_cache = cache_nsa_win[0].reshape(nbs, WINDOW, 2 * kvw)
    win_all = jnp.concatenate([win_cache, win_new], axis=1)
    win_pad = jnp.pad(win_all, ((0, 0), (0, nkw - WINDOW - ts), (0, 0)))
    tail_sel = tail_rows[:, :, 2 * kvw:]
    z_s = proj_s[:, qw + 6 * kvw:]
    mixed_s = nsa_sample_attention(qbd, cmp_s, cache4d, page_table, tail_sel, win_pad, z_s, gl_s,
                                   b_cmp, b_sel, b_win, msel_s, expand_s, past_len, ts).astype(BF16)

    xs_flat = x_sample.reshape(nbs * ts, d)
    gate_s_rows = jnp.repeat(gt_s, ts, axis=0)
    x1p, x1s = project_residual(mixed_p, mixed_s, a_w_out[0], x_prompt.reshape(nbp * t, d), gt_p,
                                xs_flat, gate_s_rows, t)

    (sh_p, sc_p, gt_p), (sh_s, sc_s, gt_s) = mods(1)
    hp = norm_modulate(x1p.reshape(nbp, t, d), norm_g[1], sc_p, sh_p)
    hs = norm_modulate(x1s.reshape(nbs, ts, d), norm_g[1], sc_s, sh_s)
    rp, rs = project(hp, hs, r_w_in[0])
    s0 = jnp.zeros((nbp,) + state_ret.shape[2:], F32)
    op, ret_p = retention(rp, s0, 0, nbp, t)
    os_, ret_s = retention(rs, state_ret[0], past_len, nbs, ts)
    os_ = os_.astype(BF16)
    gate_s_rows = jnp.repeat(gt_s, ts, axis=0)
    x2p, x2s = project_residual(op, os_, r_w_out[0], x1p, gt_p, x1s, gate_s_rows, t)

    kv_p = rows_p.reshape(1, nbp, t, 4, A_KV, HD)
    kv_s = rows_s.reshape(1, nbs, ts, 4, A_KV, HD)
    wst_p = win_p.reshape(nbp, t, 2, A_KV, HD)[None, :, t - WINDOW:]
    wst_s = win_all[:, ts:].reshape(1, nbs, WINDOW, 2, A_KV, HD)
    return (x2p.reshape(nbp, t, d), x2s.reshape(nbs, ts, d), kv_p, kv_s, wst_p, wst_s,
            ret_p[None], ret_s[None])
```

```python
import functools
import math

import numpy as np
import jax
import jax.numpy as jnp
from jax import lax
from jax.experimental import pallas as pl
from jax.experimental.pallas import tpu as pltpu

EPS = 1e-6
HD = 128
A_KV = 4
A_HPG = 8
A_HEADS = A_KV * A_HPG
CMP_STRIDE = 16
CMP_LEN = 32
SEL_BLOCK = 64
SEL_RATIO = SEL_BLOCK // CMP_STRIDE
SEL_TOPK = 16
WINDOW = 512
Q_BLOCK = 128
FORCE_SCORE = 1e4
NEG = -1e30
TINY = 1e-30
N_BUCKETS = 32
MAX_DISTANCE = 1024
R_HEADS = 16
R_CHUNK = 128
ROPE_BASE = 10000.0
LOG2E = math.log2(math.e)
Q_SCALE = HD ** -0.5 * LOG2E
PAGE = 128
PAGES_PER_STEP = 16
RET_HEADS_PER_STEP = 4

LANE = 128
VMEM_LIMIT = 60 * 1024 * 1024

F32 = jnp.float32
BF16 = jnp.bfloat16


def _cparams(sem):
    return pltpu.CompilerParams(dimension_semantics=sem, vmem_limit_bytes=VMEM_LIMIT)


def _dot(a, b):
    return jnp.dot(a, b, preferred_element_type=F32)


def _dot_nt(a, b):
    return lax.dot_general(a, b, (((1,), (1,)), ((), ())), preferred_element_type=F32)


def _dot_tn(a, b):
    return lax.dot_general(a, b, (((0,), (0,)), ((), ())), preferred_element_type=F32)


def _silu(x):
    return x / (1.0 + jnp.exp(-x))


def _sigmoid(x):
    return 1.0 / (1.0 + jnp.exp(-x))


def _split3_dot(x, m_bf16):
    hi = x.astype(BF16)
    r1 = x - hi.astype(F32)
    mid = r1.astype(BF16)
    lo = (r1 - mid.astype(F32)).astype(BF16)
    return _dot(hi, m_bf16) + _dot(mid, m_bf16) + _dot(lo, m_bf16)


def _ada_kernel(c_ref, w_ref, b_ref, o_ref):
    a = _silu(c_ref[...]).astype(BF16)
    o_ref[...] = _dot(a, w_ref[...].astype(BF16)) + b_ref[...]


def ada_modulation(c_all, ada_w, ada_b):
    depth, d, n = ada_w.shape
    rows = c_all.shape[0]
    tn = 512
    return pl.pallas_call(
        _ada_kernel,
        grid=(depth, n // tn),
        in_specs=[
            pl.BlockSpec((rows, d), lambda l, j: (0, 0)),
            pl.BlockSpec((None, d, tn), lambda l, j: (l, 0, j)),
            pl.BlockSpec((None, 1, tn), lambda l, j: (l, 0, j)),
        ],
        out_specs=pl.BlockSpec((None, rows, tn), lambda l, j: (l, 0, j)),
        out_shape=jax.ShapeDtypeStruct((depth, rows, n), F32),
        compiler_params=_cparams(("arbitrary", "arbitrary")),
        name="ada_modulation",
    )(c_all, ada_w, ada_b.reshape(depth, 1, n))


def _norm_mod_kernel(x_ref, g_ref, sc_ref, sh_ref, o_ref):
    x = x_ref[...]
    y = x * lax.rsqrt(jnp.mean(x * x, axis=-1, keepdims=True) + EPS)
    y = y * g_ref[...]
    o_ref[...] = (y * (1.0 + sc_ref[...]) + sh_ref[...]).astype(o_ref.dtype)


def norm_modulate(x, g, scale, shift):
    b, t, d = x.shape
    tt = min(t, 256)
    out = pl.pallas_call(
        _norm_mod_kernel,
        grid=(b, t // tt),
        in_specs=[
            pl.BlockSpec((None, tt, d), lambda i, j: (i, j, 0)),
            pl.BlockSpec((1, d), lambda i, j: (0, 0)),
            pl.BlockSpec((None, 1, d), lambda i, j: (i, 0, 0)),
            pl.BlockSpec((None, 1, d), lambda i, j: (i, 0, 0)),
        ],
        out_specs=pl.BlockSpec((None, tt, d), lambda i, j: (i, j, 0)),
        out_shape=jax.ShapeDtypeStruct((b, t, d), BF16),
        compiler_params=_cparams(("arbitrary", "arbitrary")),
        name="norm_modulate",
    )(x, g.reshape(1, d), scale.reshape(b, 1, d), shift.reshape(b, 1, d))
    return out.reshape(b * t, d)


def _proj_kernel(xp_ref, xs_ref, w_ref, op_ref, os_ref, wb_ref):
    @pl.when(pl.program_id(1) == 0)
    def _():
        wb_ref[...] = w_ref[...].astype(BF16)
        os_ref[...] = _dot(xs_ref[...], wb_ref[...])

    op_ref[...] = _dot(xp_ref[...], wb_ref[...])


def _proj_nk_kernel(xp_ref, xs_ref, w_ref, op_ref, os_ref, wb_ref):
    @pl.when(pl.program_id(1) == 0)
    def _():
        wb_ref[...] = w_ref[...].astype(BF16)
        os_ref[...] = _dot_nt(xs_ref[...], wb_ref[...])

    op_ref[...] = _dot_nt(xp_ref[...], wb_ref[...])


def _proj_res_kernel(xp_ref, xs_ref, w_ref, rp_ref, gp_ref, rs_ref, gs_ref, op_ref, os_ref, wb_ref):
    @pl.when(pl.program_id(1) == 0)
    def _():
        wb_ref[...] = w_ref[...].astype(BF16)
        os_ref[...] = rs_ref[...] + gs_ref[...] * _dot(xs_ref[...], wb_ref[...])

    op_ref[...] = rp_ref[...] + gp_ref[...] * _dot(xp_ref[...], wb_ref[...])


def _proj_tiles(mp, k, n, residual=False):
    if k <= 4096:
        tm, tn = (1024, 512) if residual else (512, 1024)
    else:
        tm, tn = 256, 512
    tm = min(tm, mp)
    while mp % tm:
        tm //= 2
    tn = min(tn, n)
    while n % tn:
        tn //= 2
    return tm, tn


def project(xp, xs, w, n_cols=None, w_is_nk=False):
    mp, k = xp.shape
    ms = xs.shape[0]
    n = w.shape[0 if w_is_nk else 1] if n_cols is None else n_cols
    tm, tn = _proj_tiles(mp, k, n)
    w_spec = pl.BlockSpec((tn, k), lambda j, i: (j, 0)) if w_is_nk else pl.BlockSpec((k, tn), lambda j, i: (0, j))
    return pl.pallas_call(
        _proj_nk_kernel if w_is_nk else _proj_kernel,
        grid=(n // tn, mp // tm),
        in_specs=[
            pl.BlockSpec((tm, k), lambda j, i: (i, 0)),
            pl.BlockSpec((ms, k), lambda j, i: (0, 0)),
            w_spec,
        ],
        out_specs=[
            pl.BlockSpec((tm, tn), lambda j, i: (i, j)),
            pl.BlockSpec((ms, tn), lambda j, i: (0, j)),
        ],
        out_shape=[jax.ShapeDtypeStruct((mp, n), F32), jax.ShapeDtypeStruct((ms, n), F32)],
        scratch_shapes=[pltpu.VMEM((tn, k) if w_is_nk else (k, tn), BF16)],
        compiler_params=_cparams(("arbitrary", "arbitrary")),
        name="project",
    )(xp, xs, w)


def project_residual(xp, xs, w, res_p, gate_p, res_s, gate_s, rows_per_batch):
    mp, k = xp.shape
    ms = xs.shape[0]
    n = w.shape[1]
    tm, tn = _proj_tiles(mp, k, n, residual=True)
    tm = min(tm, rows_per_batch)
    tpb = rows_per_batch // tm
    nb = gate_p.shape[0]
    return pl.pallas_call(
        _proj_res_kernel,
        grid=(n // tn, mp // tm),
        in_specs=[
            pl.BlockSpec((tm, k), lambda j, i: (i, 0)),
            pl.BlockSpec((ms, k), lambda j, i: (0, 0)),
            pl.BlockSpec((k, tn), lambda j, i: (0, j)),
            pl.BlockSpec((tm, tn), lambda j, i: (i, j)),
            pl.BlockSpec((None, 1, tn), lambda j, i: (i // tpb, 0, j)),
            pl.BlockSpec((ms, tn), lambda j, i: (0, j)),
            pl.BlockSpec((ms, tn), lambda j, i: (0, j)),
        ],
        out_specs=[
            pl.BlockSpec((tm, tn), lambda j, i: (i, j)),
            pl.BlockSpec((ms, tn), lambda j, i: (0, j)),
        ],
        out_shape=[jax.ShapeDtypeStruct((mp, n), F32), jax.ShapeDtypeStruct((ms, n), F32)],
        scratch_shapes=[pltpu.VMEM((k, tn), BF16)],
        compiler_params=_cparams(("arbitrary", "arbitrary")),
        name="project_residual",
    )(xp, xs, w, res_p, gate_p.reshape(nb, 1, n), res_s, gate_s)


def _nsa_prep_kernel(q_ref, a_ref, b_ref, c_ref, g_ref, qn_ref, rows_ref, win_ref):
    def hnorm(x, g):
        return x * lax.rsqrt(jnp.mean(x * x, axis=-1, keepdims=True) + EPS) * g

    g = g_ref[...]
    for j in range(A_HEADS):
        sl = slice(j * HD, (j + 1) * HD)
        qn_ref[:, sl] = (hnorm(q_ref[:, sl], g[0:1]) * Q_SCALE).astype(qn_ref.dtype)
    kvw = A_KV * HD
    rows_ref[:, : 2 * kvw] = a_ref[...]
    rows_ref[:, 3 * kvw:] = b_ref[:, kvw:]
    win_ref[:, kvw:] = c_ref[:, kvw:]
    for j in range(A_KV):
        sl = slice(j * HD, (j + 1) * HD)
        rows_ref[:, 2 * kvw + j * HD: 2 * kvw + (j + 1) * HD] = hnorm(b_ref[:, sl], g[2:3])
        win_ref[:, sl] = hnorm(c_ref[:, sl], g[3:4])


def nsa_prep(proj, qk_g):
    m = proj.shape[0]
    tm = min(m, 256)
    qw = A_HEADS * HD
    kv2 = 2 * A_KV * HD
    base = qw // kv2
    return pl.pallas_call(
        _nsa_prep_kernel,
        grid=(m // tm,),
        in_specs=[
            pl.BlockSpec((tm, qw), lambda i: (i, 0)),
            pl.BlockSpec((tm, kv2), lambda i: (i, base)),
            pl.BlockSpec((tm, kv2), lambda i: (i, base + 1)),
            pl.BlockSpec((tm, kv2), lambda i: (i, base + 2)),
            pl.BlockSpec((4, HD), lambda i: (0, 0)),
        ],
        out_specs=[
            pl.BlockSpec((tm, qw), lambda i: (i, 0)),
            pl.BlockSpec((tm, 2 * kv2), lambda i: (i, 0)),
            pl.BlockSpec((tm, kv2), lambda i: (i, 0)),
        ],
        out_shape=[
            jax.ShapeDtypeStruct((m, qw), BF16),
            jax.ShapeDtypeStruct((m, 2 * kv2), F32),
            jax.ShapeDtypeStruct((m, kv2), F32),
        ],
        compiler_params=_cparams(("arbitrary",)),
        name="nsa_prep",
    )(proj, proj, proj, proj, qk_g)


NCK = 2 * A_KV


def _cmp_a_kernel(pt_ref, *refs, n_pages):
    del pt_ref
    pages = refs[:n_pages]
    w_ref = refs[n_pages]
    o_ref = refs[n_pages + 1]
    cpp = PAGE // CMP_STRIDE
    rows = n_pages * cpp * NCK
    acc = None
    for tp in range(CMP_STRIDE // 2):
        pieces = []
        for pg in pages:
            for ch in range(cpp):
                t0 = ch * CMP_STRIDE + 2 * tp
                pieces.append(jnp.concatenate([pg[t0], pg[t0 + 1]], axis=1))
        lhs = jnp.concatenate(pieces, axis=0).astype(BF16)
        d = _dot(lhs, w_ref[tp])
        acc = d if acc is None else acc + d
    is_k = (lax.broadcasted_iota(jnp.int32, (rows, 2 * HD), 0) & (NCK - 1)) < A_KV
    out = jnp.where(is_k, acc[:, :2 * HD], acc[:, 2 * HD:])
    o_ref[...] = out.reshape(n_pages * cpp, NCK, 2 * HD)


def compress_stage_a(store, page_table, w_pairs, n_pages):
    nb, n_tab = page_table.shape
    steps = n_tab // n_pages
    cpp = PAGE // CMP_STRIDE

    def page_spec(k):
        return pl.BlockSpec((None, PAGE, NCK, HD), lambda b, s, pt: (pt[b, s * n_pages + k], 0, 0, 0))

    grid_spec = pltpu.PrefetchScalarGridSpec(
        num_scalar_prefetch=1,
        grid=(nb, steps),
        in_specs=[page_spec(k) for k in range(n_pages)]
        + [pl.BlockSpec(w_pairs.shape, lambda b, s, pt: (0, 0, 0))],
        out_specs=pl.BlockSpec((None, n_pages * cpp, NCK, 2 * HD), lambda b, s, pt: (b, s, 0, 0)),
    )
    return pl.pallas_call(
        functools.partial(_cmp_a_kernel, n_pages=n_pages),
        grid_spec=grid_spec,
        out_shape=jax.ShapeDtypeStruct((nb, n_tab * cpp, NCK, 2 * HD), F32),
        compiler_params=_cparams(("arbitrary", "arbitrary")),
        name="compress_stage_a",
    )(page_table, *([store] * n_pages), w_pairs)


def _cmp_b_kernel(ab_ref, nx_ref, tail_ref, pe_ref, w1_ref, w2_ref, g_ref, o_ref, const_ref):
    j = pl.program_id(1)
    nch = ab_ref.shape[0]
    rows = nch * NCK

    @pl.when((pl.program_id(0) == 0) & (j == 0))
    def _():
        for kind in range(2):
            const = jnp.zeros((8, HD), F32)
            for t in range(CMP_LEN):
                row = jnp.broadcast_to(pe_ref[kind, t:t + 1, :], (8, HD)).astype(BF16)
                const = const + _dot(row, w1_ref[kind, t].astype(BF16))
            const_ref[kind * A_KV:(kind + 1) * A_KV, :] = const[:A_KV]

    ab = ab_ref[...].reshape(rows, 2 * HD)
    nxt = pltpu.roll(ab[:, HD:], rows - NCK, 0).reshape(nch, NCK, HD)
    first_next = jnp.where(j == pl.num_programs(1) - 1, tail_ref[0, :, HD:], nx_ref[0, :, HD:])
    cid = lax.broadcasted_iota(jnp.int32, (nch, NCK, HD), 0)
    nxt = jnp.where(cid == nch - 1, first_next[None], nxt)
    pre = ab[:, :HD].reshape(nch, NCK, HD) + nxt + const_ref[...][None]
    act = _silu(pre).reshape(rows, HD).astype(BF16)
    w2 = jnp.concatenate([w2_ref[0], w2_ref[1]], axis=1).astype(BF16)
    y2 = _dot(act, w2)
    is_k = (lax.broadcasted_iota(jnp.int32, (rows, HD), 0) & (NCK - 1)) < A_KV
    y = jnp.where(is_k, y2[:, :HD], y2[:, HD:])
    yn = y * lax.rsqrt(jnp.mean(y * y, axis=-1, keepdims=True) + EPS) * g_ref[1:2]
    o_ref[...] = jnp.where(is_k, yn, y).reshape(nch, NCK, HD)


def compress_stage_b(ab, tail, pe, w1, w2, qk_g):
    nb, n_ch = ab.shape[:2]
    tc = min(256, n_ch)
    assert n_ch % tc == 0
    nt = n_ch // tc
    out = pl.pallas_call(
        _cmp_b_kernel,
        grid=(nb, nt),
        in_specs=[
            pl.BlockSpec((None, tc, NCK, 2 * HD), lambda b, j: (b, j, 0, 0)),
            pl.BlockSpec((None, 1, NCK, 2 * HD), lambda b, j: (b, jnp.minimum((j + 1) * tc, n_ch - 1), 0, 0)),
            pl.BlockSpec((None, 1, NCK, 2 * HD), lambda b, j: (b, 0, 0, 0)),
            pl.BlockSpec(pe.shape, lambda b, j: (0, 0, 0)),
            pl.BlockSpec(w1.shape, lambda b, j: (0, 0, 0, 0)),
            pl.BlockSpec(w2.shape, lambda b, j: (0, 0, 0)),
            pl.BlockSpec((4, HD), lambda b, j: (0, 0)),
        ],
        out_specs=pl.BlockSpec((None, tc, NCK, HD), lambda b, j: (b, j, 0, 0)),
        out_shape=jax.ShapeDtypeStruct((nb, n_ch, NCK, HD), F32),
        scratch_shapes=[pltpu.VMEM((NCK, HD), F32)],
        compiler_params=_cparams(("arbitrary", "arbitrary")),
        name="compress_stage_b",
    )(ab, ab, tail, pe, w1, w2, qk_g)
    return out.reshape(nb, n_ch, NCK * HD)


def _select_topk(score, blk, k_top):
    sel = jnp.zeros(score.shape, F32)
    for _ in range(k_top):
        m = jnp.max(score, axis=-1, keepdims=True)
        cand = jnp.where(score == m, blk, jnp.int32(1 << 30))
        first = jnp.min(cand, axis=-1, keepdims=True)
        pick = blk == first
        sel = jnp.where(pick, 1.0, sel)
        score = jnp.where(pick, -jnp.inf, score)
    return sel


def _select_topk_t(score, blk, k_top):
    sel = jnp.zeros(score.shape, F32)
    for _ in range(k_top):
        m = jnp.max(score, axis=0, keepdims=True)
        cand = jnp.where(score == m, blk, jnp.int32(1 << 30))
        first = jnp.min(cand, axis=0, keepdims=True)
        pick = blk == first
        sel = jnp.where(pick, 1.0, sel)
        score = jnp.where(pick, -jnp.inf, score)
    return sel


def _bias_select(dist, dmin, dmax, value_of):
    lo = int(_t5_bucket_np(np.array([max(dmin, 0)]))[0])
    hi = int(_t5_bucket_np(np.array([max(dmax, 0)]))[0])
    out = jnp.full(dist.shape, value_of(lo), F32)
    for b in range(lo + 1, hi + 1):
        out = jnp.where(dist >= _BUCKET_START[b], value_of(b), out)
    return out * LOG2E


def _bias_tables_kernel(rb_ref, tsel_ref, tcmp_ref):
    head = pl.program_id(0)
    value_of = lambda b: rb_ref[b, head]
    n_off, nq, nc = tsel_ref.shape[0], tcmp_ref.shape[0], tcmp_ref.shape[1]
    kl = lax.broadcasted_iota(jnp.int32, (LANE, Q_BLOCK), 0)
    ql = lax.broadcasted_iota(jnp.int32, (LANE, Q_BLOCK), 1)
    for o in range(n_off):
        base = (o - 1) * Q_BLOCK
        tsel_ref[o] = _bias_select(base + ql - kl, base - (LANE - 1), base + Q_BLOCK - 1, value_of)
    cend = lax.broadcasted_iota(jnp.int32, (nc, Q_BLOCK), 0) * CMP_STRIDE + (CMP_LEN - 1)
    qc = lax.broadcasted_iota(jnp.int32, (nc, Q_BLOCK), 1)
    for i in range(nq):
        base = i * Q_BLOCK
        tcmp_ref[i] = _bias_select(base + qc - cend, base - ((nc - 1) * CMP_STRIDE + CMP_LEN - 1),
                                   base + Q_BLOCK - 1, value_of)


def prompt_bias_tables(rel_bias, nq, n_cmp):
    return pl.pallas_call(
        _bias_tables_kernel,
        grid=(A_HEADS,),
        in_specs=[pl.BlockSpec(memory_space=pltpu.SMEM)],
        out_specs=[
            pl.BlockSpec((None, nq + 1, None, LANE, Q_BLOCK), lambda h: (h // A_HPG, 0, h % A_HPG, 0, 0)),
            pl.BlockSpec((None, nq, None, n_cmp, Q_BLOCK), lambda h: (h // A_HPG, 0, h % A_HPG, 0, 0)),
        ],
        out_shape=[
            jax.ShapeDtypeStruct((A_KV, nq + 1, A_HPG, LANE, Q_BLOCK), F32),
            jax.ShapeDtypeStruct((A_KV, nq, A_HPG, n_cmp, Q_BLOCK), F32),
        ],
        compiler_params=_cparams(("arbitrary",)),
        name="prompt_bias_tables",
    )(rel_bias)


def _bias_rows_kernel(rb_ref, cmp_ref, sel_ref, win_ref, *, past_len, n_q):
    head = pl.program_id(0)
    value_of = lambda b: rb_ref[b, head]

    def fill(ref, key_pos, kmin, kmax, q0):
        q = lax.broadcasted_iota(jnp.int32, ref.shape, 0)
        k = lax.broadcasted_iota(jnp.int32, ref.shape, 1)
        ref[...] = _bias_select(q0 + q - key_pos(k), q0 - kmax, q0 + n_q - 1 - kmin, value_of)

    nc, nk, nw = cmp_ref.shape[1], sel_ref.shape[1], win_ref.shape[1]
    fill(cmp_ref, lambda k: k * CMP_STRIDE + (CMP_LEN - 1), CMP_LEN - 1, (nc - 1) * CMP_STRIDE + CMP_LEN - 1, past_len)
    fill(sel_ref, lambda k: k, 0, nk - 1, past_len)
    fill(win_ref, lambda k: k, 0, nw - 1, WINDOW)


def sample_bias_rows(rel_bias, past_len, n_q, n_cmp, n_keys, n_win):
    shapes = [(A_HEADS * n_q, n) for n in (n_cmp, n_keys, n_win)]
    return pl.pallas_call(
        functools.partial(_bias_rows_kernel, past_len=past_len, n_q=n_q),
        grid=(A_HEADS,),
        in_specs=[pl.BlockSpec(memory_space=pltpu.SMEM)],
        out_specs=[pl.BlockSpec((n_q, s[1]), lambda h: (h, 0)) for s in shapes],
        out_shape=[jax.ShapeDtypeStruct(s, F32) for s in shapes],
        compiler_params=_cparams(("arbitrary",)),
        name="sample_bias_rows",
    )(rel_bias)


def _nsa_prompt_kernel(q_ref, kc_ref, vc_ref, ks_ref, vs_ref, kw_ref, vw_ref, zc_ref, zs_ref, zw_ref,
                       gt_ref, tsel_ref, tcmp_ref, msel_ref, exp_ref, o_ref,
                       pt_ref, acc_ref, m_ref, l_ref, madd_ref, mix_ref, *, n_sel_pad, n_sel, k_top):
    i = pl.program_id(2)
    qb = Q_BLOCK
    kt = 2 * LANE
    n_kt = madd_ref.shape[0]
    nc = kc_ref.shape[0]
    gates = _sigmoid(gt_ref[...])
    hs = [slice(h * HD, (h + 1) * HD) for h in range(A_HPG)]

    def emit(branch, h, o_t, first):
        z_ref = (zc_ref, zs_ref, zw_ref)[branch]
        val = o_t.T * _silu(z_ref[:, hs[h]])
        if first:
            mix_ref[:, hs[h]] = val
        else:
            mix_ref[:, hs[h]] = mix_ref[:, hs[h]] + val

    kc = kc_ref[...].astype(BF16)
    vc = vc_ref[...].astype(BF16)
    cend = lax.broadcasted_iota(jnp.int32, (nc, qb), 0) * CMP_STRIDE + (CMP_LEN - 1)
    qpos_c = i * qb + lax.broadcasted_iota(jnp.int32, (nc, qb), 1)
    cvalid = cend <= qpos_c
    imp = None
    for h in range(A_HPG):
        s = _dot_nt(kc, q_ref[:, hs[h]]) + tcmp_ref[h]
        s = jnp.where(cvalid, s, NEG)
        mx = jnp.max(s, axis=0, keepdims=True)
        e = jnp.where(cvalid, jnp.exp2(s - mx), 0.0)
        p = e * (1.0 / jnp.maximum(jnp.sum(e, axis=0, keepdims=True), TINY))
        imp = p if imp is None else imp + p
        emit(0, h, _dot_tn(vc, p.astype(BF16)) * gates[h:h + 1, :], True)

    hi = imp.astype(BF16)
    r1 = imp - hi.astype(F32)
    mid = r1.astype(BF16)
    lo = (r1 - mid.astype(F32)).astype(BF16)
    msel = msel_ref[...]
    imp_sel = _dot(msel, hi) + _dot(msel, mid) + _dot(msel, lo)
    blk = lax.broadcasted_iota(jnp.int32, (n_sel_pad, qb), 0)
    qpos = i * qb + lax.broadcasted_iota(jnp.int32, (n_sel_pad, qb), 1)
    cur = jnp.right_shift(qpos, int(math.log2(SEL_BLOCK)))
    forced = (blk == 0) | (blk == cur) | (blk == cur - 1)
    valid = blk * SEL_BLOCK <= qpos
    score = jnp.where(valid, jnp.where(forced, FORCE_SCORE, imp_sel), -1.0)
    score = jnp.where(blk < n_sel, score, -jnp.inf)
    sel = _select_topk_t(score, blk, k_top).astype(BF16)
    kpos2 = lax.broadcasted_iota(jnp.int32, (kt, qb), 0)
    qpos2 = i * qb + lax.broadcasted_iota(jnp.int32, (kt, qb), 1)
    for jj in range(n_kt):
        hit = _dot(exp_ref[jj * kt:(jj + 1) * kt, :], sel)
        ok = (hit > 0.5) & ((jj * kt + kpos2) <= qpos2)
        madd_ref[jj] = jnp.where(ok, 0.0, NEG)

    def reset():
        m_ref[...] = jnp.full(m_ref.shape, NEG, F32)
        l_ref[...] = jnp.zeros(l_ref.shape, F32)
        acc_ref[...] = jnp.zeros(acc_ref.shape, F32)

    def tile_update(k, v, nk, bias_of, madd):
        alphas = []
        for h in range(A_HPG):
            s = _dot_nt(k, q_ref[:, hs[h]]) + bias_of(h) + madd
            m_old = m_ref[h:h + 1, :]
            m_new = jnp.maximum(m_old, jnp.max(s, axis=0, keepdims=True))
            alpha = jnp.exp2(m_old - m_new)
            p = jnp.exp2(s - m_new)
            l_ref[h:h + 1, :] = alpha * l_ref[h:h + 1, :] + jnp.sum(p, axis=0, keepdims=True)
            m_ref[h:h + 1, :] = m_new
            pt_ref[0:nk, hs[h]] = p.astype(BF16)
            alphas.append(alpha)
        alpha_all = jnp.concatenate(alphas, axis=1)
        acc_ref[...] = acc_ref[...] * alpha_all + _dot_tn(v, pt_ref[0:nk, :])

    def finish(branch):
        for h in range(A_HPG):
            w = gates[branch * A_HPG + h: branch * A_HPG + h + 1, :] / l_ref[h:h + 1, :]
            emit(branch, h, acc_ref[:, hs[h]] * w, False)

    reset()

    def sel_body(jj, carry):
        k0 = pl.multiple_of(jj * kt, kt)
        k = ks_ref[pl.ds(k0, kt), :].astype(BF16)
        v = vs_ref[pl.ds(k0, kt), :].astype(BF16)
        off = i - 2 * jj
        bias_of = lambda h: jnp.concatenate([tsel_ref[off + 1, h], tsel_ref[off, h]], axis=0)
        tile_update(k, v, kt, bias_of, madd_ref[jj])
        return carry

    lax.fori_loop(0, i // 2 + 1, sel_body, 0)
    finish(1)

    reset()
    kl = lax.broadcasted_iota(jnp.int32, (qb, qb), 0)
    ql = lax.broadcasted_iota(jnp.int32, (qb, qb), 1)

    def win_body(t, carry):
        k0 = pl.multiple_of((i - t) * qb, qb)
        k = kw_ref[pl.ds(k0, qb), :].astype(BF16)
        v = vw_ref[pl.ds(k0, qb), :].astype(BF16)
        dist = (t * qb + ql) - kl
        madd = jnp.where((dist >= 0) & (dist < WINDOW), 0.0, NEG)
        tile_update(k, v, qb, lambda h: tsel_ref[t + 1, h], madd)
        return carry

    lax.fori_loop(0, jnp.minimum(i, WINDOW // qb) + 1, win_body, 0)
    finish(2)
    o_ref[...] = mix_ref[...].astype(o_ref.dtype)


def nsa_prompt_attention(qn, cmp_kv, rows, win, proj, gates_g, tsel, tcmp, msel, expand, nb, t):
    nq = t // Q_BLOCK
    qw = A_HPG * HD
    n_sel = -(-t // SEL_BLOCK)
    k_top = min(SEL_TOPK, n_sel)
    n_kt = t // (2 * LANE)
    n_cmp = cmp_kv.shape[1]
    n_sel_pad = msel.shape[0]
    zbase = (A_HEADS * HD + 6 * A_KV * HD) // qw
    nz = A_HEADS * HD // qw
    kernel = functools.partial(_nsa_prompt_kernel, n_sel_pad=n_sel_pad, n_sel=n_sel, k_top=k_top)
    return pl.pallas_call(
        kernel,
        grid=(A_KV, nb, nq),
        in_specs=[
            pl.BlockSpec((Q_BLOCK, qw), lambda g, b, i: (b * nq + i, g)),
            pl.BlockSpec((None, n_cmp, HD), lambda g, b, i: (b, 0, g)),
            pl.BlockSpec((None, n_cmp, HD), lambda g, b, i: (b, 0, A_KV + g)),
            pl.BlockSpec((t, HD), lambda g, b, i: (b, 2 * A_KV + g)),
            pl.BlockSpec((t, HD), lambda g, b, i: (b, 3 * A_KV + g)),
            pl.BlockSpec((t, HD), lambda g, b, i: (b, g)),
            pl.BlockSpec((t, HD), lambda g, b, i: (b, A_KV + g)),
            pl.BlockSpec((Q_BLOCK, qw), lambda g, b, i: (b * nq + i, zbase + g)),
            pl.BlockSpec((Q_BLOCK, qw), lambda g, b, i: (b * nq + i, zbase + nz + g)),
            pl.BlockSpec((Q_BLOCK, qw), lambda g, b, i: (b * nq + i, zbase + 2 * nz + g)),
            pl.BlockSpec((None, 3 * A_HPG, Q_BLOCK), lambda g, b, i: (g, 0, b * nq + i)),
            pl.BlockSpec((None, nq + 1, A_HPG, LANE, Q_BLOCK), lambda g, b, i: (g, 0, 0, 0, 0)),
            pl.BlockSpec((None, None, A_HPG, n_cmp, Q_BLOCK), lambda g, b, i: (g, i, 0, 0, 0)),
            pl.BlockSpec((n_sel_pad, n_cmp), lambda g, b, i: (0, 0)),
            pl.BlockSpec((t, n_sel_pad), lambda g, b, i: (0, 0)),
        ],
        out_specs=pl.BlockSpec((Q_BLOCK, qw), lambda g, b, i: (b * nq + i, g)),
        out_shape=jax.ShapeDtypeStruct((nb * t, A_HEADS * HD), BF16),
        scratch_shapes=[
            pltpu.VMEM((2 * LANE, qw), BF16),
            pltpu.VMEM((HD, qw), F32),
            pltpu.VMEM((A_HPG, Q_BLOCK), F32),
            pltpu.VMEM((A_HPG, Q_BLOCK), F32),
            pltpu.VMEM((n_kt, 2 * LANE, Q_BLOCK), F32),
            pltpu.VMEM((Q_BLOCK, qw), F32),
        ],
        compiler_params=_cparams(("arbitrary", "arbitrary", "arbitrary")),
        name="nsa_prompt_attention",
    )(qn, cmp_kv, cmp_kv, rows, rows, win, win, proj, proj, proj, gates_g, tsel, tcmp, msel, expand)


def _diag_blocks(o_full, rows_per_group):
    return jnp.concatenate(
        [o_full[g * rows_per_group:(g + 1) * rows_per_group, g * HD:(g + 1) * HD] for g in range(A_KV)], axis=0)


def _nsa_sample_cmp_kernel(q_ref, kc_ref, vc_ref, bias_ref, msel_ref, oc_ref, sel_ref,
                           *, past_len, n_sel, k_top, n_q, blocks_per_step):
    n_cmp = kc_ref.shape[0]
    rows = q_ref.shape[0]
    rpg = A_HPG * n_q
    q = q_ref[...]
    s = _dot_nt(q, kc_ref[...].astype(BF16)) + bias_ref[...]
    rowi = lax.broadcasted_iota(jnp.int32, (rows, n_cmp), 0)
    ci = lax.broadcasted_iota(jnp.int32, (rows, n_cmp), 1)
    qpos = past_len + (rowi & (n_q - 1))
    valid = (ci * CMP_STRIDE + (CMP_LEN - 1)) <= qpos
    s = jnp.where(valid, s, NEG)
    mx = jnp.max(s, axis=-1, keepdims=True)
    e = jnp.where(valid, jnp.exp2(s - mx), 0.0)
    p = e / jnp.maximum(jnp.sum(e, axis=-1, keepdims=True), TINY)
    oc_ref[...] = _diag_blocks(_dot(p.astype(BF16), vc_ref[...].astype(BF16)), rpg)
    imps = []
    for g in range(A_KV):
        acc = p[g * rpg: g * rpg + n_q]
        for h in range(1, A_HPG):
            acc = acc + p[g * rpg + h * n_q: g * rpg + (h + 1) * n_q]
        imps.append(acc)
    imp = jnp.concatenate(imps, axis=0)
    imp_sel = _split3_dot(imp, msel_ref[...])
    shp = imp_sel.shape
    blk = lax.broadcasted_iota(jnp.int32, shp, 1)
    qp = past_len + (lax.broadcasted_iota(jnp.int32, shp, 0) & (n_q - 1))
    cur = jnp.right_shift(qp, int(math.log2(SEL_BLOCK)))
    forced = (blk == 0) | (blk == cur) | (blk == cur - 1)
    ok = blk * SEL_BLOCK <= qp
    score = jnp.where(ok, jnp.where(forced, FORCE_SCORE, imp_sel), -1.0)
    score = jnp.where(blk < n_sel, score, -jnp.inf)
    sel = _select_topk(score, blk, k_top)
    for st in range(sel_ref.shape[0]):
        sel_ref[st] = sel[:, st * blocks_per_step:(st + 1) * blocks_per_step]


def _nsa_sample_sel_kernel(pt_ref, *refs, n_pages, n_steps, past_len, n_q):
    del pt_ref
    pages = refs[:n_pages]
    q_ref, tail_ref, bias_ref, sel_ref, exp_ref, o_ref, m_ref, l_ref, acc_ref = refs[n_pages:]
    s_id = pl.program_id(1)
    kvw = A_KV * HD
    rows = q_ref.shape[0]
    rpg = A_HPG * n_q

    @pl.when(s_id == 0)
    def _():
        m_ref[...] = jnp.full(m_ref.shape, NEG, F32)
        l_ref[...] = jnp.zeros(l_ref.shape, F32)
        acc_ref[...] = jnp.zeros(acc_ref.shape, F32)

    def update(k, v, nk, key0):
        hit = _dot(sel_ref[...].astype(BF16), exp_ref[:, :nk])
        kpos = key0 + lax.broadcasted_iota(jnp.int32, hit.shape, 1)
        qpos = past_len + (lax.broadcasted_iota(jnp.int32, hit.shape, 0) & (n_q - 1))
        madd = jnp.where((hit > 0.5) & (kpos <= qpos), 0.0, NEG)
        madd = jnp.concatenate(
            [madd[g * n_q:(g + 1) * n_q] for g in range(A_KV) for _ in range(A_HPG)], axis=0)
        s = _dot_nt(q_ref[...], k) + bias_ref[:, :nk] + madd
        m_old = m_ref[...]
        m_new = jnp.maximum(m_old, jnp.max(s, axis=-1, keepdims=True))
        alpha = jnp.exp2(m_old - m_new)
        p = jnp.exp2(s - m_new)
        l_ref[...] = alpha * l_ref[...] + jnp.sum(p, axis=-1, keepdims=True)
        acc_ref[...] = alpha * acc_ref[...] + _dot(p.astype(BF16), v)
        m_ref[...] = m_new

    @pl.when(s_id < n_steps)
    def _():
        by_slot = [jnp.swapaxes(pg[...], 0, 1) for pg in pages]

        def gather(c0):
            cols = [jnp.concatenate([x[c0 + g] for x in by_slot], axis=0) for g in range(A_KV)]
            return jnp.concatenate(cols, axis=1).astype(BF16)

        update(gather(0), gather(A_KV), n_pages * PAGE, s_id * (n_pages * PAGE))

    @pl.when(s_id == n_steps)
    def _():
        update(tail_ref[:, :kvw].astype(BF16), tail_ref[:, kvw:].astype(BF16), PAGE, past_len)
        o_ref[...] = _diag_blocks(acc_ref[...] / l_ref[...], rpg)


def _nsa_sample_mix_kernel(q_ref, win_ref, bias_ref, oc_ref, os_ref, z_ref, gt_ref, o_ref, *, n_q):
    kvw = A_KV * HD
    rows = q_ref.shape[0]
    rpg = A_HPG * n_q
    nk = win_ref.shape[0]
    s = _dot_nt(q_ref[...], win_ref[:, :kvw].astype(BF16)) + bias_ref[...]
    qi = lax.broadcasted_iota(jnp.int32, (rows, nk), 0) & (n_q - 1)
    ki = lax.broadcasted_iota(jnp.int32, (rows, nk), 1)
    dist = qi + WINDOW - ki
    s = s + jnp.where((dist >= 0) & (dist < WINDOW), 0.0, NEG)
    mx = jnp.max(s, axis=-1, keepdims=True)
    e = jnp.exp2(s - mx)
    p = e / jnp.sum(e, axis=-1, keepdims=True)
    ow = _diag_blocks(_dot(p.astype(BF16), win_ref[:, kvw:].astype(BF16)), rpg)
    gates = _sigmoid(gt_ref[...])
    width = A_HEADS * HD
    for hh in range(A_HEADS):
        r = slice(hh * n_q, (hh + 1) * n_q)
        c = slice(hh * HD, (hh + 1) * HD)
        val = None
        for br, o in enumerate((oc_ref, os_ref, ow)):
            zc = slice(br * width + hh * HD, br * width + (hh + 1) * HD)
            gcol = gates[:, br * A_HEADS + hh: br * A_HEADS + hh + 1]
            term = gcol * o[r, :] * _silu(z_ref[:, zc])
            val = term if val is None else val + term
        o_ref[:, c] = val.astype(o_ref.dtype)


def nsa_sample_attention(qbd, cmp_kv, cache4d, page_table, tail_sel, win_all, z_s, gl_s,
                         b_cmp, b_sel, b_win, msel, expand, past_len, n_q):
    nb, rows, kvw = qbd.shape
    n_cmp = cmp_kv.shape[1]
    n_tab = page_table.shape[1]
    n_pages = min(PAGES_PER_STEP, n_tab)
    n_steps = n_tab // n_pages
    bps = n_pages * PAGE // SEL_BLOCK
    n_sel = -(-(past_len + n_q) // SEL_BLOCK)
    k_top = min(SEL_TOPK, n_sel)
    gq = A_KV * n_q

    oc, selmask = pl.pallas_call(
        functools.partial(_nsa_sample_cmp_kernel, past_len=past_len, n_sel=n_sel, k_top=k_top,
                          n_q=n_q, blocks_per_step=bps),
        grid=(nb,),
        in_specs=[
            pl.BlockSpec((None, rows, kvw), lambda b: (b, 0, 0)),
            pl.BlockSpec((None, n_cmp, kvw), lambda b: (b, 0, 0)),
            pl.BlockSpec((None, n_cmp, kvw), lambda b: (b, 0, 1)),
            pl.BlockSpec(b_cmp.shape, lambda b: (0, 0)),
            pl.BlockSpec(msel.shape, lambda b: (0, 0)),
        ],
        out_specs=[
            pl.BlockSpec((None, rows, HD), lambda b: (b, 0, 0)),
            pl.BlockSpec((None, n_steps + 1, gq, bps), lambda b: (b, 0, 0, 0)),
        ],
        out_shape=[
            jax.ShapeDtypeStruct((nb, rows, HD), F32),
            jax.ShapeDtypeStruct((nb, n_steps + 1, gq, bps), F32),
        ],
        compiler_params=_cparams(("arbitrary",)),
        name="nsa_sample_cmp",
    )(qbd, cmp_kv, cmp_kv, b_cmp, msel)

    half = 2 * kvw
    last = n_steps - 1

    def page_spec(k):
        return pl.BlockSpec(
            (None, PAGE, NCK, HD),
            lambda b, s, pt: (pt[b, jnp.minimum(s, last) * n_pages + k], 0, 1, 0))

    nkeys = n_pages * PAGE
    grid_spec = pltpu.PrefetchScalarGridSpec(
        num_scalar_prefetch=1,
        grid=(nb, n_steps + 1),
        in_specs=[page_spec(k) for k in range(n_pages)] + [
            pl.BlockSpec((None, rows, kvw), lambda b, s, pt: (b, 0, 0)),
            pl.BlockSpec((None, PAGE, half), lambda b, s, pt: (b, 0, 0)),
            pl.BlockSpec((rows, nkeys), lambda b, s, pt: (0, s)),
            pl.BlockSpec((None, None, gq, bps), lambda b, s, pt: (b, s, 0, 0)),
            pl.BlockSpec(expand.shape, lambda b, s, pt: (0, 0)),
        ],
        out_specs=pl.BlockSpec((None, rows, HD), lambda b, s, pt: (b, 0, 0)),
        scratch_shapes=[
            pltpu.VMEM((rows, 1), F32),
            pltpu.VMEM((rows, 1), F32),
            pltpu.VMEM((rows, kvw), F32),
        ],
    )
    osel = pl.pallas_call(
        functools.partial(_nsa_sample_sel_kernel, n_pages=n_pages, n_steps=n_steps,
                          past_len=past_len, n_q=n_q),
        grid_spec=grid_spec,
        out_shape=jax.ShapeDtypeStruct((nb, rows, HD), F32),
        compiler_params=_cparams(("arbitrary", "arbitrary")),
        name="nsa_sample_sel",
    )(page_table, *([cache4d] * n_pages), qbd, tail_sel, b_sel, selmask, expand)

    nkw = win_all.shape[1]
    zw = z_s.shape[1]
    return pl.pallas_call(
        functools.partial(_nsa_sample_mix_kernel, n_q=n_q),
        grid=(nb,),
        in_specs=[
            pl.BlockSpec((None, rows, kvw), lambda b: (b, 0, 0)),
            pl.BlockSpec((None, nkw, 2 * kvw), lambda b: (b, 0, 0)),
            pl.BlockSpec(b_win.shape, lambda b: (0, 0)),
            pl.BlockSpec((None, rows, HD), lambda b: (b, 0, 0)),
            pl.BlockSpec((None, rows, HD), lambda b: (b, 0, 0)),
            pl.BlockSpec((n_q, zw), lambda b: (b, 0)),
            pl.BlockSpec((n_q, 3 * A_HEADS), lambda b: (b, 0)),
        ],
        out_specs=pl.BlockSpec((n_q, A_HEADS * HD), lambda b: (b, 0)),
        out_shape=jax.ShapeDtypeStruct((nb * n_q, A_HEADS * HD), F32),
        compiler_params=_cparams(("arbitrary",)),
        name="nsa_sample_mix",
    )(qbd, win_all, b_win, oc, osel, z_s, gl_s)


def _retention_kernel(q_ref, k_ref, v_ref, g_ref, cos_ref, sin_ref, dm_ref, qd_ref, kd_ref, cd_ref,
                      s0_ref, o_ref, s_ref, *, dk, dv, hps):
    c = pl.program_id(2)
    half = dk // 2

    @pl.when(c == 0)
    def _():
        s_ref[...] = s0_ref[...]

    cos = cos_ref[...]
    sin = sin_ref[...]

    def rot(x):
        x1, x2 = x[:, :half], x[:, half:]
        return jnp.concatenate([x1 * cos - x2 * sin, x1 * sin + x2 * cos], axis=1)

    for j in range(hps):
        ks, vs = slice(j * dk, (j + 1) * dk), slice(j * dv, (j + 1) * dv)
        q = rot(q_ref[:, ks])
        k = rot(k_ref[:, ks]) * (dk ** -0.5)
        v = v_ref[:, vs].astype(BF16)
        qb = q.astype(BF16)
        att = _dot_nt(qb, k.astype(BF16)) * dm_ref[j]
        state = s_ref[j]
        o = _dot(att.astype(BF16), v) + _dot(qb, state.astype(BF16)) * qd_ref[j]
        s_ref[j] = state * cd_ref[j] + _dot_tn((k * kd_ref[j]).astype(BF16), v)
        o = o * lax.rsqrt(jnp.mean(o * o, axis=-1, keepdims=True) + EPS)
        o_ref[:, vs] = (o * _silu(g_ref[:, vs])).astype(o_ref.dtype)


def retention(proj, state0, q0, nb, t):
    _, nh, dk, dv = state0.shape
    cs = R_CHUNK if t % R_CHUNK == 0 else t
    n = t // cs
    half = dk // 2
    pos = (q0 + jnp.arange(t)).astype(F32)
    inv = jnp.power(ROPE_BASE, -jnp.arange(half, dtype=F32) / half)
    ang = pos[:, None] * inv[None, :]
    cos, sin = jnp.cos(ang), jnp.sin(ang)
    log_g = jnp.log1p(-jnp.exp2(-5.0 - jnp.arange(nh, dtype=F32)))
    ii = jnp.arange(cs, dtype=F32)
    rel = ii[:, None] - ii[None, :]
    causal = rel >= 0
    dmat = jnp.where(causal, jnp.exp(jnp.where(causal, rel, 0.0) * log_g[:, None, None]), 0.0)
    q_dec = jnp.exp((ii + 1.0) * log_g[:, None])[:, :, None]
    k_dec = jnp.exp((cs - 1.0 - ii) * log_g[:, None])[:, :, None]
    c_dec = jnp.exp(cs * log_g)[:, None, None]
    hps = RET_HEADS_PER_STEP
    assert nh % hps == 0
    ng = nh // hps
    vb0 = 2 * nh * dk // (hps * dv)
    return pl.pallas_call(
        functools.partial(_retention_kernel, dk=dk, dv=dv, hps=hps),
        grid=(nb, ng, n),
        in_specs=[
            pl.BlockSpec((cs, hps * dk), lambda b, h, c: (b * n + c, h)),
            pl.BlockSpec((cs, hps * dk), lambda b, h, c: (b * n + c, ng + h)),
            pl.BlockSpec((cs, hps * dv), lambda b, h, c: (b * n + c, vb0 + h)),
            pl.BlockSpec((cs, hps * dv), lambda b, h, c: (b * n + c, vb0 + ng + h)),
            pl.BlockSpec((cs, half), lambda b, h, c: (c, 0)),
            pl.BlockSpec((cs, half), lambda b, h, c: (c, 0)),
            pl.BlockSpec((hps, cs, cs), lambda b, h, c: (h, 0, 0)),
            pl.BlockSpec((hps, cs, 1), lambda b, h, c: (h, 0, 0)),
            pl.BlockSpec((hps, cs, 1), lambda b, h, c: (h, 0, 0)),
            pl.BlockSpec((hps, 1, 1), lambda b, h, c: (h, 0, 0)),
            pl.BlockSpec((None, hps, dk, dv), lambda b, h, c: (b, h, 0, 0)),
        ],
        out_specs=[
            pl.BlockSpec((cs, hps * dv), lambda b, h, c: (b * n + c, h)),
            pl.BlockSpec((None, hps, dk, dv), lambda b, h, c: (b, h, 0, 0)),
        ],
        out_shape=[
            jax.ShapeDtypeStruct((nb * t, nh * dv), BF16 if cs % 16 == 0 else F32),
            jax.ShapeDtypeStruct(state0.shape, F32),
        ],
        compiler_params=_cparams(("arbitrary", "arbitrary", "arbitrary")),
        name="retention",
    )(proj, proj, proj, proj, cos, sin, dmat, q_dec, k_dec, c_dec, state0)


def _t5_bucket_np(dist):
    n = np.maximum(dist, 0)
    exact = N_BUCKETS // 2
    logv = np.log(np.maximum(n, 1).astype(np.float32) / np.float32(exact)) / np.float32(
        math.log(MAX_DISTANCE / exact))
    large = np.minimum(exact + (logv * np.float32(N_BUCKETS - exact)).astype(np.int32), N_BUCKETS - 1)
    return np.where(n < exact, n, large).astype(np.int32)


def _bucket_starts():
    bk = _t5_bucket_np(np.arange(4 * MAX_DISTANCE))
    assert np.all(np.diff(bk) >= 0) and bk[-1] == N_BUCKETS - 1
    return [int(np.argmax(bk >= b)) for b in range(N_BUCKETS)]


_BUCKET_START = _bucket_starts()


def _sel_matrix(n_cmp, n_cmp_pad, n_sel_pad):
    c = np.arange(n_cmp_pad)[:, None]
    j = np.arange(n_sel_pad)[None, :]
    a = (c >= SEL_RATIO * j) & (c < SEL_RATIO * j + SEL_RATIO)
    b = (c >= SEL_RATIO * j - 1) & (c < SEL_RATIO * j + SEL_RATIO - 1)
    m = (a.astype(np.float32) + b.astype(np.float32)) * (c < n_cmp)
    return jnp.asarray(m, BF16)


def _expand_matrix(n_blk_pad, n_keys):
    blk = np.arange(n_blk_pad)[:, None]
    key = np.arange(n_keys)[None, :]
    return jnp.asarray((key // SEL_BLOCK == blk).astype(np.float32), BF16)


def _w1_pairs(w1):
    w = w1.reshape(2, 2, CMP_STRIDE // 2, 2, HD, HD)
    w = jnp.transpose(w, (2, 3, 4, 0, 1, 5))
    return w.reshape(CMP_STRIDE // 2, 2 * HD, 4 * HD).astype(BF16)


def kernel(x_prompt, x_sample, c_prompt, c_sample, cache_nsa_kv, cache_nsa_win, state_ret, page_table,
           norm_g, ada_w, ada_b, rel_bias, a_w_in, a_w_out, a_cmp_pe, a_cmp_w1, a_cmp_w2, a_qk_g,
           r_w_in, r_w_out):
    nbp, t, d = x_prompt.shape
    nbs, ts, _ = x_sample.shape
    n_tab = page_table.shape[1]
    past_len = n_tab * PAGE
    kvw = A_KV * HD
    qw = A_HEADS * HD
    assert t % (2 * LANE) == 0 and t >= WINDOW and ts == 8 and past_len >= WINDOW
    assert cache_nsa_win.shape[2] == WINDOW

    n_c = nbp + nbs
    pad_c = (-n_c) % 8
    c_all = jnp.concatenate([c_prompt, c_sample, jnp.zeros((pad_c, d), F32)], axis=0)
    mod = ada_modulation(c_all, ada_w, ada_b).reshape(ada_w.shape[0], n_c + pad_c, 3, d)

    def mods(layer):
        m = mod[layer]
        return (m[:nbp, 0], m[:nbp, 1], m[:nbp, 2]), (m[nbp:n_c, 0], m[nbp:n_c, 1], m[nbp:n_c, 2])

    (sh_p, sc_p, gt_p), (sh_s, sc_s, gt_s) = mods(0)
    hp = norm_modulate(x_prompt, norm_g[0], sc_p, sh_p)
    hs = norm_modulate(x_sample, norm_g[0], sc_s, sh_s)
    w_in_t = jnp.swapaxes(a_w_in[0], 0, 1)
    n_main = w_in_t.shape[0] - 3 * A_HEADS
    proj_p, proj_s = project(hp, hs, w_in_t, n_cols=n_main, w_is_nk=True)
    w_gate = jnp.pad(w_in_t[n_main:], ((0, LANE - 3 * A_HEADS), (0, 0)))
    gl_p, gl_s = project(hp, hs, w_gate, w_is_nk=True)
    gl_p, gl_s = gl_p[:, :3 * A_HEADS], gl_s[:, :3 * A_HEADS]
    qk_g = a_qk_g[0]

    qn_p, rows_p, win_p = nsa_prep(proj_p, qk_g)
    qn_s, rows_s, win_s = nsa_prep(proj_s, qk_g)

    w_pairs = _w1_pairs(a_cmp_w1[0])
    pe = a_cmp_pe[0]
    w1 = a_cmp_w1[0].reshape(2, CMP_LEN, HD, HD)
    w2 = a_cmp_w2[0]
    cpp = PAGE // CMP_STRIDE

    pages_p = t // PAGE
    ident = jnp.arange(nbp * pages_p, dtype=jnp.int32).reshape(nbp, pages_p)
    ab_p = compress_stage_a(rows_p.reshape(nbp * pages_p, PAGE, 4 * A_KV, HD), ident, w_pairs,
                            min(PAGES_PER_STEP, pages_p))
    zero_tail = jnp.zeros((nbp, cpp, NCK, 2 * HD), F32)
    cmp_p = compress_stage_b(ab_p, zero_tail, pe, w1, w2, qk_g)

    nq = t // Q_BLOCK
    n_ch_p = t // CMP_STRIDE
    tsel, tcmp = prompt_bias_tables(rel_bias, nq, n_ch_p)
    n_cmp_p = t // CMP_STRIDE - 1
    n_sel_p = -(-t // SEL_BLOCK)
    n_sel_pad_p = -(-n_sel_p // 16) * 16
    msel_p = _sel_matrix(n_cmp_p, n_ch_p, n_sel_pad_p).T
    expand_p = _expand_matrix(n_sel_pad_p, t).T
    gates_g = jnp.transpose(gl_p.reshape(nbp * t, 3, A_KV, A_HPG), (2, 1, 3, 0)).reshape(A_KV, 3 * A_HPG, nbp * t)
    mixed_p = nsa_prompt_attention(qn_p, cmp_p, rows_p, win_p, proj_p, gates_g,
                                   tsel, tcmp, msel_p, expand_p, nbp, t)

    n_pool = cache_nsa_kv.shape[1]
    cache4d = cache_nsa_kv[0].reshape(n_pool, PAGE, 4 * A_KV, HD)
    n_pages = min(PAGES_PER_STEP, n_tab)
    ab_s = compress_stage_a(cache4d, page_table, w_pairs, n_pages)
    tail_rows = jnp.pad(rows_s.reshape(nbs, ts, 4 * kvw), ((0, 0), (0, PAGE - ts), (0, 0)))
    ident_s = jnp.arange(nbs, dtype=jnp.int32).reshape(nbs, 1)
    ab_tail = compress_stage_a(tail_rows.reshape(nbs, PAGE, 4 * A_KV, HD), ident_s, w_pairs, 1)
    cmp_s = compress_stage_b(ab_s, ab_tail, pe, w1, w2, qk_g)

    n_cmp_s = cmp_s.shape[1]
    n_sel_s = -(-(past_len + ts) // SEL_BLOCK)
    n_steps = n_tab // n_pages
    bps = n_pages * PAGE // SEL_BLOCK
    n_sel_pad = (n_steps + 1) * bps
    nkeys = (n_steps + 1) * n_pages * PAGE
    nkw = WINDOW + LANE
    b_cmp, b_sel, b_win = sample_bias_rows(rel_bias, past_len, ts, n_cmp_s, nkeys, nkw)
    msel_s = _sel_matrix(n_cmp_s, n_cmp_s, n_sel_pad)
    expand_s = _expand_matrix(bps, n_pages * PAGE)
    q5 = qn_s.reshape(nbs, ts, A_KV, A_HPG, HD)
    eye = jnp.eye(A_KV, dtype=BF16)
    qbd = jnp.einsum('bqghd,ge->bghqed', q5, eye).reshape(nbs, A_HEADS * ts, kvw)
    win_new = win_s.reshape(nbs, ts, 2 * kvw)
    win_cache = cache_nsa_win[0].reshape(nbs, WINDOW, 2 * kvw)
    win_all = jnp.concatenate([win_cache, win_new], axis=1)
    win_pad = jnp.pad(win_all, ((0, 0), (0, nkw - WINDOW - ts), (0, 0)))
    tail_sel = tail_rows[:, :, 2 * kvw:]
    z_s = proj_s[:, qw + 6 * kvw:]
    mixed_s = nsa_sample_attention(qbd, cmp_s, cache4d, page_table, tail_sel, win_pad, z_s, gl_s,
                                   b_cmp, b_sel, b_win, msel_s, expand_s, past_len, ts).astype(BF16)

    xs_flat = x_sample.reshape(nbs * ts, d)
    gate_s_rows = jnp.repeat(gt_s, ts, axis=0)
    x1p, x1s = project_residual(mixed_p, mixed_s, a_w_out[0], x_prompt.reshape(nbp * t, d), gt_p,
                                xs_flat, gate_s_rows, t)

    (sh_p, sc_p, gt_p), (sh_s, sc_s, gt_s) = mods(1)
    hp = norm_modulate(x1p.reshape(nbp, t, d), norm_g[1], sc_p, sh_p)
    hs = norm_modulate(x1s.reshape(nbs, ts, d), norm_g[1], sc_s, sh_s)
    rp, rs = project(hp, hs, r_w_in[0])
    s0 = jnp.zeros((nbp,) + state_ret.shape[2:], F32)
    op, ret_p = retention(rp, s0, 0, nbp, t)
    os_, ret_s = retention(rs, state_ret[0], past_len, nbs, ts)
    os_ = os_.astype(BF16)
    gate_s_rows = jnp.repeat(gt_s, ts, axis=0)
    x2p, x2s = project_residual(op, os_, r_w_out[0], x1p, gt_p, x1s, gate_s_rows, t)

    kv_p = rows_p.reshape(1, nbp, t, 4, A_KV, HD)
    kv_s = rows_s.reshape(1, nbs, ts, 4, A_KV, HD)
    wst_p = win_p.reshape(nbp, t, 2, A_KV, HD)[None, :, t - WINDOW:]
    wst_s = win_all[:, ts:].reshape(1, nbs, WINDOW, 2, A_KV, HD)
    return (x2p.reshape(nbp, t, d), x2s.reshape(nbs, ts, d), kv_p, kv_s, wst_p, wst_s,
            ret_p[None], ret_s[None])
```

```python
import functools
import math

import numpy as np
import jax
import jax.numpy as jnp
from jax import lax
from jax.experimental import pallas as pl
from jax.experimental.pallas import tpu as pltpu

EPS = 1e-6
HD = 128
A_KV = 4
A_HPG = 8
A_HEADS = A_KV * A_HPG
CMP_STRIDE = 16
CMP_LEN = 32
SEL_BLOCK = 64
SEL_RATIO = SEL_BLOCK // CMP_STRIDE
SEL_TOPK = 16
WINDOW = 512
Q_BLOCK = 128
FORCE_SCORE = 1e4
NEG = -1e30
TINY = 1e-30
N_BUCKETS = 32
MAX_DISTANCE = 1024
R_HEADS = 16
R_CHUNK = 128
ROPE_BASE = 10000.0
LOG2E = math.log2(math.e)
Q_SCALE = HD ** -0.5 * LOG2E
PAGE = 128
PAGES_PER_STEP = 16
RET_HEADS_PER_STEP = 4

LANE = 128
VMEM_LIMIT = 60 * 1024 * 1024

F32 = jnp.float32
BF16 = jnp.bfloat16


def _cparams(sem):
    return pltpu.CompilerParams(dimension_semantics=sem, vmem_limit_bytes=VMEM_LIMIT)


def _dot(a, b):
    return jnp.dot(a, b, preferred_element_type=F32)


def _dot_nt(a, b):
    return lax.dot_general(a, b, (((1,), (1,)), ((), ())), preferred_element_type=F32)


def _dot_tn(a, b):
    return lax.dot_general(a, b, (((0,), (0,)), ((), ())), preferred_element_type=F32)


def _silu(x):
    return x / (1.0 + jnp.exp(-x))


def _sigmoid(x):
    return 1.0 / (1.0 + jnp.exp(-x))


def _split3_dot(x, m_bf16):
    hi = x.astype(BF16)
    r1 = x - hi.astype(F32)
    mid = r1.astype(BF16)
    lo = (r1 - mid.astype(F32)).astype(BF16)
    return _dot(hi, m_bf16) + _dot(mid, m_bf16) + _dot(lo, m_bf16)


def _ada_kernel(c_ref, w_ref, b_ref, o_ref):
    a = _silu(c_ref[...]).astype(BF16)
    o_ref[...] = _dot(a, w_ref[...].astype(BF16)) + b_ref[...]


def ada_modulation(c_all, ada_w, ada_b):
    depth, d, n = ada_w.shape
    rows = c_all.shape[0]
    tn = 512
    return pl.pallas_call(
        _ada_kernel,
        grid=(depth, n // tn),
        in_specs=[
            pl.BlockSpec((rows, d), lambda l, j: (0, 0)),
            pl.BlockSpec((None, d, tn), lambda l, j: (l, 0, j)),
            pl.BlockSpec((None, 1, tn), lambda l, j: (l, 0, j)),
        ],
        out_specs=pl.BlockSpec((None, rows, tn), lambda l, j: (l, 0, j)),
        out_shape=jax.ShapeDtypeStruct((depth, rows, n), F32),
        compiler_params=_cparams(("arbitrary", "arbitrary")),
        name="ada_modulation",
    )(c_all, ada_w, ada_b.reshape(depth, 1, n))


def _norm_mod_kernel(x_ref, g_ref, sc_ref, sh_ref, o_ref):
    x = x_ref[...]
    y = x * lax.rsqrt(jnp.mean(x * x, axis=-1, keepdims=True) + EPS)
    y = y * g_ref[...]
    o_ref[...] = (y * (1.0 + sc_ref[...]) + sh_ref[...]).astype(o_ref.dtype)


def norm_modulate(x, g, scale, shift):
    b, t, d = x.shape
    tt = min(t, 256)
    out = pl.pallas_call(
        _norm_mod_kernel,
        grid=(b, t // tt),
        in_specs=[
            pl.BlockSpec((None, tt, d), lambda i, j: (i, j, 0)),
            pl.BlockSpec((1, d), lambda i, j: (0, 0)),
            pl.BlockSpec((None, 1, d), lambda i, j: (i, 0, 0)),
            pl.BlockSpec((None, 1, d), lambda i, j: (i, 0, 0)),
        ],
        out_specs=pl.BlockSpec((None, tt, d), lambda i, j: (i, j, 0)),
        out_shape=jax.ShapeDtypeStruct((b, t, d), BF16),
        compiler_params=_cparams(("arbitrary", "arbitrary")),
        name="norm_modulate",
    )(x, g.reshape(1, d), scale.reshape(b, 1, d), shift.reshape(b, 1, d))
    return out.reshape(b * t, d)


def _proj_kernel(xp_ref, xs_ref, w_ref, op_ref, os_ref, wb_ref):
    @pl.when(pl.program_id(1) == 0)
    def _():
        wb_ref[...] = w_ref[...].astype(BF16)
        os_ref[...] = _dot(xs_ref[...], wb_ref[...])

    op_ref[...] = _dot(xp_ref[...], wb_ref[...])


def _proj_nk_kernel(xp_ref, xs_ref, w_ref, op_ref, os_ref, wb_ref):
    @pl.when(pl.program_id(1) == 0)
    def _():
        wb_ref[...] = w_ref[...].astype(BF16)
        os_ref[...] = _dot_nt(xs_ref[...], wb_ref[...])

    op_ref[...] = _dot_nt(xp_ref[...], wb_ref[...])


def _proj_gate_t_kernel(xp_ref, xs_ref, w_ref, opt_ref, os_ref, wb_ref):
    @pl.when(pl.program_id(1) == 0)
    def _():
        wb_ref[...] = w_ref[...].astype(BF16)
        os_ref[...] = _silu(_dot_nt(xs_ref[...], wb_ref[...]))

    opt_ref[...] = _silu(_dot_nt(wb_ref[...], xp_ref[...]))


def project_gate_t(xp, xs, w_nk, row0, n_rows):
    mp, k = xp.shape
    ms = xs.shape[0]
    tm, tn = _proj_tiles(mp, k, n_rows)
    assert row0 % tn == 0
    j0 = row0 // tn
    return pl.pallas_call(
        _proj_gate_t_kernel,
        grid=(n_rows // tn, mp // tm),
        in_specs=[
            pl.BlockSpec((tm, k), lambda j, i: (i, 0)),
            pl.BlockSpec((ms, k), lambda j, i: (0, 0)),
            pl.BlockSpec((tn, k), lambda j, i: (j0 + j, 0)),
        ],
        out_specs=[
            pl.BlockSpec((tn, tm), lambda j, i: (j, i)),
            pl.BlockSpec((ms, tn), lambda j, i: (0, j)),
        ],
        out_shape=[jax.ShapeDtypeStruct((n_rows, mp), F32), jax.ShapeDtypeStruct((ms, n_rows), F32)],
        scratch_shapes=[pltpu.VMEM((tn, k), BF16)],
        compiler_params=_cparams(("arbitrary", "arbitrary")),
        name="project_gate_t",
    )(xp, xs, w_nk)


def _proj_res_kernel(xp_ref, xs_ref, w_ref, rp_ref, gp_ref, rs_ref, gs_ref, op_ref, os_ref, wb_ref):
    @pl.when(pl.program_id(1) == 0)
    def _():
        wb_ref[...] = w_ref[...].astype(BF16)
        os_ref[...] = rs_ref[...] + gs_ref[...] * _dot(xs_ref[...], wb_ref[...])

    op_ref[...] = rp_ref[...] + gp_ref[...] * _dot(xp_ref[...], wb_ref[...])


def _proj_tiles(mp, k, n, residual=False):
    if k <= 4096:
        tm, tn = (1024, 512) if residual else (512, 1024)
    else:
        tm, tn = 256, 512
    tm = min(tm, mp)
    while mp % tm:
        tm //= 2
    tn = min(tn, n)
    while n % tn:
        tn //= 2
    return tm, tn


def project(xp, xs, w, n_cols=None, w_is_nk=False):
    mp, k = xp.shape
    ms = xs.shape[0]
    n = w.shape[0 if w_is_nk else 1] if n_cols is None else n_cols
    tm, tn = _proj_tiles(mp, k, n)
    w_spec = pl.BlockSpec((tn, k), lambda j, i: (j, 0)) if w_is_nk else pl.BlockSpec((k, tn), lambda j, i: (0, j))
    return pl.pallas_call(
        _proj_nk_kernel if w_is_nk else _proj_kernel,
        grid=(n // tn, mp // tm),
        in_specs=[
            pl.BlockSpec((tm, k), lambda j, i: (i, 0)),
            pl.BlockSpec((ms, k), lambda j, i: (0, 0)),
            w_spec,
        ],
        out_specs=[
            pl.BlockSpec((tm, tn), lambda j, i: (i, j)),
            pl.BlockSpec((ms, tn), lambda j, i: (0, j)),
        ],
        out_shape=[jax.ShapeDtypeStruct((mp, n), F32), jax.ShapeDtypeStruct((ms, n), F32)],
        scratch_shapes=[pltpu.VMEM((tn, k) if w_is_nk else (k, tn), BF16)],
        compiler_params=_cparams(("arbitrary", "arbitrary")),
        name="project",
    )(xp, xs, w)


def project_residual(xp, xs, w, res_p, gate_p, res_s, gate_s, rows_per_batch):
    mp, k = xp.shape
    ms = xs.shape[0]
    n = w.shape[1]
    tm, tn = _proj_tiles(mp, k, n, residual=True)
    tm = min(tm, rows_per_batch)
    tpb = rows_per_batch // tm
    nb = gate_p.shape[0]
    return pl.pallas_call(
        _proj_res_kernel,
        grid=(n // tn, mp // tm),
        in_specs=[
            pl.BlockSpec((tm, k), lambda j, i: (i, 0)),
            pl.BlockSpec((ms, k), lambda j, i: (0, 0)),
            pl.BlockSpec((k, tn), lambda j, i: (0, j)),
            pl.BlockSpec((tm, tn), lambda j, i: (i, j)),
            pl.BlockSpec((None, 1, tn), lambda j, i: (i // tpb, 0, j)),
            pl.BlockSpec((ms, tn), lambda j, i: (0, j)),
            pl.BlockSpec((ms, tn), lambda j, i: (0, j)),
        ],
        out_specs=[
            pl.BlockSpec((tm, tn), lambda j, i: (i, j)),
            pl.BlockSpec((ms, tn), lambda j, i: (0, j)),
        ],
        out_shape=[jax.ShapeDtypeStruct((mp, n), F32), jax.ShapeDtypeStruct((ms, n), F32)],
        scratch_shapes=[pltpu.VMEM((k, tn), BF16)],
        compiler_params=_cparams(("arbitrary", "arbitrary")),
        name="project_residual",
    )(xp, xs, w, res_p, gate_p.reshape(nb, 1, n), res_s, gate_s)


def _nsa_prep_kernel(q_ref, a_ref, b_ref, c_ref, g_ref, qn_ref, rows_ref, win_ref):
    def hnorm(x, g):
        return x * lax.rsqrt(jnp.mean(x * x, axis=-1, keepdims=True) + EPS) * g

    g = g_ref[...]
    for j in range(A_HEADS):
        sl = slice(j * HD, (j + 1) * HD)
        qn_ref[:, sl] = (hnorm(q_ref[:, sl], g[0:1]) * Q_SCALE).astype(qn_ref.dtype)
    kvw = A_KV * HD
    rows_ref[:, : 2 * kvw] = a_ref[...]
    rows_ref[:, 3 * kvw:] = b_ref[:, kvw:]
    win_ref[:, kvw:] = c_ref[:, kvw:]
    for j in range(A_KV):
        sl = slice(j * HD, (j + 1) * HD)
        rows_ref[:, 2 * kvw + j * HD: 2 * kvw + (j + 1) * HD] = hnorm(b_ref[:, sl], g[2:3])
        win_ref[:, sl] = hnorm(c_ref[:, sl], g[3:4])


def nsa_prep(proj, qk_g):
    m = proj.shape[0]
    tm = min(m, 256)
    qw = A_HEADS * HD
    kv2 = 2 * A_KV * HD
    base = qw // kv2
    return pl.pallas_call(
        _nsa_prep_kernel,
        grid=(m // tm,),
        in_specs=[
            pl.BlockSpec((tm, qw), lambda i: (i, 0)),
            pl.BlockSpec((tm, kv2), lambda i: (i, base)),
            pl.BlockSpec((tm, kv2), lambda i: (i, base + 1)),
            pl.BlockSpec((tm, kv2), lambda i: (i, base + 2)),
            pl.BlockSpec((4, HD), lambda i: (0, 0)),
        ],
        out_specs=[
            pl.BlockSpec((tm, qw), lambda i: (i, 0)),
            pl.BlockSpec((tm, 2 * kv2), lambda i: (i, 0)),
            pl.BlockSpec((tm, kv2), lambda i: (i, 0)),
        ],
        out_shape=[
            jax.ShapeDtypeStruct((m, qw), BF16),
            jax.ShapeDtypeStruct((m, 2 * kv2), F32),
            jax.ShapeDtypeStruct((m, kv2), F32),
        ],
        compiler_params=_cparams(("arbitrary",)),
        name="nsa_prep",
    )(proj, proj, proj, proj, qk_g)


NCK = 2 * A_KV


def _cmp_a_kernel(pt_ref, *refs, n_pages):
    del pt_ref
    pages = refs[:n_pages]
    w_ref = refs[n_pages]
    o_ref = refs[n_pages + 1]
    cpp = PAGE // CMP_STRIDE
    rows = n_pages * cpp * NCK
    acc = None
    for tp in range(CMP_STRIDE // 2):
        pieces = []
        for pg in pages:
            for ch in range(cpp):
                t0 = ch * CMP_STRIDE + 2 * tp
                pieces.append(jnp.concatenate([pg[t0], pg[t0 + 1]], axis=1))
        lhs = jnp.concatenate(pieces, axis=0).astype(BF16)
        d = _dot(lhs, w_ref[tp])
        acc = d if acc is None else acc + d
    is_k = (lax.broadcasted_iota(jnp.int32, (rows, 2 * HD), 0) & (NCK - 1)) < A_KV
    out = jnp.where(is_k, acc[:, :2 * HD], acc[:, 2 * HD:])
    o_ref[...] = out.reshape(n_pages * cpp, NCK, 2 * HD)


def compress_stage_a(store, page_table, w_pairs, n_pages):
    nb, n_tab = page_table.shape
    steps = n_tab // n_pages
    cpp = PAGE // CMP_STRIDE

    def page_spec(k):
        return pl.BlockSpec((None, PAGE, NCK, HD), lambda b, s, pt: (pt[b, s * n_pages + k], 0, 0, 0))

    grid_spec = pltpu.PrefetchScalarGridSpec(
        num_scalar_prefetch=1,
        grid=(nb, steps),
        in_specs=[page_spec(k) for k in range(n_pages)]
        + [pl.BlockSpec(w_pairs.shape, lambda b, s, pt: (0, 0, 0))],
        out_specs=pl.BlockSpec((None, n_pages * cpp, NCK, 2 * HD), lambda b, s, pt: (b, s, 0, 0)),
    )
    return pl.pallas_call(
        functools.partial(_cmp_a_kernel, n_pages=n_pages),
        grid_spec=grid_spec,
        out_shape=jax.ShapeDtypeStruct((nb, n_tab * cpp, NCK, 2 * HD), F32),
        compiler_params=_cparams(("arbitrary", "arbitrary")),
        name="compress_stage_a",
    )(page_table, *([store] * n_pages), w_pairs)


def _cmp_b_kernel(ab_ref, nx_ref, tail_ref, pe_ref, w1_ref, w2_ref, g_ref, o_ref, const_ref):
    j = pl.program_id(1)
    nch = ab_ref.shape[0]
    rows = nch * NCK

    @pl.when((pl.program_id(0) == 0) & (j == 0))
    def _():
        for kind in range(2):
            const = jnp.zeros((8, HD), F32)
            for t in range(CMP_LEN):
                row = jnp.broadcast_to(pe_ref[kind, t:t + 1, :], (8, HD)).astype(BF16)
                const = const + _dot(row, w1_ref[kind, t].astype(BF16))
            const_ref[kind * A_KV:(kind + 1) * A_KV, :] = const[:A_KV]

    ab = ab_ref[...].reshape(rows, 2 * HD)
    nxt = pltpu.roll(ab[:, HD:], rows - NCK, 0).reshape(nch, NCK, HD)
    first_next = jnp.where(j == pl.num_programs(1) - 1, tail_ref[0, :, HD:], nx_ref[0, :, HD:])
    cid = lax.broadcasted_iota(jnp.int32, (nch, NCK, HD), 0)
    nxt = jnp.where(cid == nch - 1, first_next[None], nxt)
    pre = ab[:, :HD].reshape(nch, NCK, HD) + nxt + const_ref[...][None]
    act = _silu(pre).reshape(rows, HD).astype(BF16)
    w2 = jnp.concatenate([w2_ref[0], w2_ref[1]], axis=1).astype(BF16)
    y2 = _dot(act, w2)
    is_k = (lax.broadcasted_iota(jnp.int32, (rows, HD), 0) & (NCK - 1)) < A_KV
    y = jnp.where(is_k, y2[:, :HD], y2[:, HD:])
    yn = y * lax.rsqrt(jnp.mean(y * y, axis=-1, keepdims=True) + EPS) * g_ref[1:2]
    o_ref[...] = jnp.where(is_k, yn, y).reshape(nch, NCK, HD)


def compress_stage_b(ab, tail, pe, w1, w2, qk_g):
    nb, n_ch = ab.shape[:2]
    tc = min(256, n_ch)
    assert n_ch % tc == 0
    nt = n_ch // tc
    out = pl.pallas_call(
        _cmp_b_kernel,
        grid=(nb, nt),
        in_specs=[
            pl.BlockSpec((None, tc, NCK, 2 * HD), lambda b, j: (b, j, 0, 0)),
            pl.BlockSpec((None, 1, NCK, 2 * HD), lambda b, j: (b, jnp.minimum((j + 1) * tc, n_ch - 1), 0, 0)),
            pl.BlockSpec((None, 1, NCK, 2 * HD), lambda b, j: (b, 0, 0, 0)),
            pl.BlockSpec(pe.shape, lambda b, j: (0, 0, 0)),
            pl.BlockSpec(w1.shape, lambda b, j: (0, 0, 0, 0)),
            pl.BlockSpec(w2.shape, lambda b, j: (0, 0, 0)),
            pl.BlockSpec((4, HD), lambda b, j: (0, 0)),
        ],
        out_specs=pl.BlockSpec((None, tc, NCK, HD), lambda b, j: (b, j, 0, 0)),
        out_shape=jax.ShapeDtypeStruct((nb, n_ch, NCK, HD), F32),
        scratch_shapes=[pltpu.VMEM((NCK, HD), F32)],
        compiler_params=_cparams(("arbitrary", "arbitrary")),
        name="compress_stage_b",
    )(ab, ab, tail, pe, w1, w2, qk_g)
    return out.reshape(nb, n_ch, NCK * HD)


def _select_topk(score, blk, k_top):
    sel = jnp.zeros(score.shape, F32)
    for _ in range(k_top):
        m = jnp.max(score, axis=-1, keepdims=True)
        cand = jnp.where(score == m, blk, jnp.int32(1 << 30))
        first = jnp.min(cand, axis=-1, keepdims=True)
        pick = blk == first
        sel = jnp.where(pick, 1.0, sel)
        score = jnp.where(pick, -jnp.inf, score)
    return sel


def _select_topk_t(score, blk, k_top):
    sel = jnp.zeros(score.shape, F32)
    for _ in range(k_top):
        m = jnp.max(score, axis=0, keepdims=True)
        cand = jnp.where(score == m, blk, jnp.int32(1 << 30))
        first = jnp.min(cand, axis=0, keepdims=True)
        pick = blk == first
        sel = jnp.where(pick, 1.0, sel)
        score = jnp.where(pick, -jnp.inf, score)
    return sel


def _bias_select(dist, dmin, dmax, value_of):
    lo = int(_t5_bucket_np(np.array([max(dmin, 0)]))[0])
    hi = int(_t5_bucket_np(np.array([max(dmax, 0)]))[0])
    out = jnp.full(dist.shape, value_of(lo), F32)
    for b in range(lo + 1, hi + 1):
        out = jnp.where(dist >= _BUCKET_START[b], value_of(b), out)
    return out * LOG2E


def _bias_tables_kernel(rb_ref, tsel_ref, tcmp_ref):
    head = pl.program_id(0)
    value_of = lambda b: rb_ref[b, head]
    n_off, nq, nc = tsel_ref.shape[0], tcmp_ref.shape[0], tcmp_ref.shape[1]
    kl = lax.broadcasted_iota(jnp.int32, (LANE, Q_BLOCK), 0)
    ql = lax.broadcasted_iota(jnp.int32, (LANE, Q_BLOCK), 1)
    for o in range(n_off):
        base = (o - 1) * Q_BLOCK
        tsel_ref[o] = _bias_select(base + ql - kl, base - (LANE - 1), base + Q_BLOCK - 1, value_of)
    cend = lax.broadcasted_iota(jnp.int32, (nc, Q_BLOCK), 0) * CMP_STRIDE + (CMP_LEN - 1)
    qc = lax.broadcasted_iota(jnp.int32, (nc, Q_BLOCK), 1)
    for i in range(nq):
        base = i * Q_BLOCK
        tcmp_ref[i] = _bias_select(base + qc - cend, base - ((nc - 1) * CMP_STRIDE + CMP_LEN - 1),
                                   base + Q_BLOCK - 1, value_of)


def prompt_bias_tables(rel_bias, nq, n_cmp):
    return pl.pallas_call(
        _bias_tables_kernel,
        grid=(A_HEADS,),
        in_specs=[pl.BlockSpec(memory_space=pltpu.SMEM)],
        out_specs=[
            pl.BlockSpec((None, nq + 1, LANE, Q_BLOCK), lambda h: (h // A_HPG, 0, 0, h % A_HPG)),
            pl.BlockSpec((None, nq, n_cmp, Q_BLOCK), lambda h: (h // A_HPG, 0, 0, h % A_HPG)),
        ],
        out_shape=[
            jax.ShapeDtypeStruct((A_KV, nq + 1, LANE, A_HPG * Q_BLOCK), F32),
            jax.ShapeDtypeStruct((A_KV, nq, n_cmp, A_HPG * Q_BLOCK), F32),
        ],
        compiler_params=_cparams(("arbitrary",)),
        name="prompt_bias_tables",
    )(rel_bias)


def _bias_rows_kernel(rb_ref, cmp_ref, sel_ref, win_ref, *, past_len, n_q):
    head = pl.program_id(0)
    value_of = lambda b: rb_ref[b, head]

    def fill(ref, key_pos, kmin, kmax, q0):
        q = lax.broadcasted_iota(jnp.int32, ref.shape, 0)
        k = lax.broadcasted_iota(jnp.int32, ref.shape, 1)
        ref[...] = _bias_select(q0 + q - key_pos(k), q0 - kmax, q0 + n_q - 1 - kmin, value_of)

    nc, nk, nw = cmp_ref.shape[1], sel_ref.shape[1], win_ref.shape[1]
    fill(cmp_ref, lambda k: k * CMP_STRIDE + (CMP_LEN - 1), CMP_LEN - 1, (nc - 1) * CMP_STRIDE + CMP_LEN - 1, past_len)
    fill(sel_ref, lambda k: k, 0, nk - 1, past_len)
    fill(win_ref, lambda k: k, 0, nw - 1, WINDOW)


def sample_bias_rows(rel_bias, past_len, n_q, n_cmp, n_keys, n_win):
    shapes = [(A_HEADS * n_q, n) for n in (n_cmp, n_keys, n_win)]
    return pl.pallas_call(
        functools.partial(_bias_rows_kernel, past_len=past_len, n_q=n_q),
        grid=(A_HEADS,),
        in_specs=[pl.BlockSpec(memory_space=pltpu.SMEM)],
        out_specs=[pl.BlockSpec((n_q, s[1]), lambda h: (h, 0)) for s in shapes],
        out_shape=[jax.ShapeDtypeStruct(s, F32) for s in shapes],
        compiler_params=_cparams(("arbitrary",)),
        name="sample_bias_rows",
    )(rel_bias)


def _nsa_prompt_kernel(q_ref, kc_ref, vc_ref, ks_ref, vs_ref, kw_ref, vw_ref, zc_ref, zs_ref, zw_ref,
                       gt_ref, tsel_ref, tcmp_ref, msel_ref, exp_ref, o_ref,
                       qa_ref, acc_ref, m_ref, l_ref, madd_ref, mix_ref, *, n_sel_pad, n_sel, k_top):
    i = pl.program_id(2)
    qb = Q_BLOCK
    kt = 2 * LANE
    n_kt = madd_ref.shape[0]
    nc = kc_ref.shape[0]
    gates = _sigmoid(gt_ref[...])
    hs = [slice(h * HD, (h + 1) * HD) for h in range(A_HPG)]
    for h in range(A_HPG):
        qa_ref[hs[h], :] = q_ref[:, hs[h]]
    qa = qa_ref[...]

    def heads(x):
        return jnp.concatenate([x] * A_HPG, axis=1)

    def gate_row(branch):
        return jnp.concatenate([gates[branch * A_HPG + h: branch * A_HPG + h + 1, :] for h in range(A_HPG)], axis=1)

    def emit(branch, o_t, first):
        z_ref = (zc_ref, zs_ref, zw_ref)[branch]
        for h in range(A_HPG):
            val = o_t[:, hs[h]] * z_ref[hs[h], :]
            if first:
                mix_ref[hs[h], :] = val
            else:
                mix_ref[hs[h], :] = mix_ref[hs[h], :] + val

    kc = kc_ref[...].astype(BF16)
    vc = vc_ref[...].astype(BF16)
    cend = lax.broadcasted_iota(jnp.int32, (nc, qb), 0) * CMP_STRIDE + (CMP_LEN - 1)
    qpos_c = i * qb + lax.broadcasted_iota(jnp.int32, (nc, qb), 1)
    cvalid = heads(cend <= qpos_c)
    s = jnp.where(cvalid, _dot_nt(kc, qa) + tcmp_ref[...], NEG)
    mx = jnp.max(s, axis=0, keepdims=True)
    e = jnp.where(cvalid, jnp.exp2(s - mx), 0.0)
    p = e * (1.0 / jnp.maximum(jnp.sum(e, axis=0, keepdims=True), TINY))
    imp = p[:, hs[0]]
    for h in range(1, A_HPG):
        imp = imp + p[:, hs[h]]
    emit(0, _dot_tn(vc, p.astype(BF16)) * gate_row(0), True)

    hi = imp.astype(BF16)
    r1 = imp - hi.astype(F32)
    mid = r1.astype(BF16)
    lo = (r1 - mid.astype(F32)).astype(BF16)
    msel = msel_ref[...]
    imp_sel = _dot(msel, hi) + _dot(msel, mid) + _dot(msel, lo)
    blk = lax.broadcasted_iota(jnp.int32, (n_sel_pad, qb), 0)
    qpos = i * qb + lax.broadcasted_iota(jnp.int32, (n_sel_pad, qb), 1)
    cur = jnp.right_shift(qpos, int(math.log2(SEL_BLOCK)))
    forced = (blk == 0) | (blk == cur) | (blk == cur - 1)
    valid = blk * SEL_BLOCK <= qpos
    score = jnp.where(valid, jnp.where(forced, FORCE_SCORE, imp_sel), -1.0)
    score = jnp.where(blk < n_sel, score, -jnp.inf)
    sel = _select_topk_t(score, blk, k_top).astype(BF16)
    kpos2 = lax.broadcasted_iota(jnp.int32, (kt, qb), 0)
    qpos2 = i * qb + lax.broadcasted_iota(jnp.int32, (kt, qb), 1)
    for jj in range(n_kt):
        hit = _dot(exp_ref[jj * kt:(jj + 1) * kt, :], sel)
        ok = (hit > 0.5) & ((jj * kt + kpos2) <= qpos2)
        madd_ref[jj] = jnp.where(ok, 0.0, NEG)

    def reset():
        m_ref[...] = jnp.full(m_ref.shape, NEG, F32)
        l_ref[...] = jnp.zeros(l_ref.shape, F32)
        acc_ref[...] = jnp.zeros(acc_ref.shape, F32)

    def tile_update(k, v, bias, madd):
        s = _dot_nt(k, qa) + bias + heads(madd)
        m_old = m_ref[...]
        m_new = jnp.maximum(m_old, jnp.max(s, axis=0, keepdims=True))
        alpha = jnp.exp2(m_old - m_new)
        p = jnp.exp2(s - m_new)
        l_ref[...] = alpha * l_ref[...] + jnp.sum(p, axis=0, keepdims=True)
        m_ref[...] = m_new
        acc_ref[...] = acc_ref[...] * alpha + _dot_tn(v, p.astype(BF16))

    def finish(branch):
        emit(branch, acc_ref[...] * (gate_row(branch) / l_ref[...]), False)

    reset()

    def sel_body(jj, carry):
        k0 = pl.multiple_of(jj * kt, kt)
        k = ks_ref[pl.ds(k0, kt), :].astype(BF16)
        v = vs_ref[pl.ds(k0, kt), :].astype(BF16)
        off = i - 2 * jj
        bias = jnp.concatenate([tsel_ref[off + 1], tsel_ref[off]], axis=0)
        tile_update(k, v, bias, madd_ref[jj])
        return carry

    lax.fori_loop(0, i // 2 + 1, sel_body, 0)
    finish(1)

    reset()
    kl = lax.broadcasted_iota(jnp.int32, (qb, qb), 0)
    ql = lax.broadcasted_iota(jnp.int32, (qb, qb), 1)

    def win_body(t, carry):
        k0 = pl.multiple_of((i - t) * qb, qb)
        k = kw_ref[pl.ds(k0, qb), :].astype(BF16)
        v = vw_ref[pl.ds(k0, qb), :].astype(BF16)
        dist = (t * qb + ql) - kl
        madd = jnp.where((dist >= 0) & (dist < WINDOW), 0.0, NEG)
        tile_update(k, v, tsel_ref[t + 1], madd)
        return carry

    lax.fori_loop(0, jnp.minimum(i, WINDOW // qb) + 1, win_body, 0)
    finish(2)
    for h in range(A_HPG):
        o_ref[:, hs[h]] = mix_ref[hs[h], :].T.astype(o_ref.dtype)


def nsa_prompt_attention(qn, cmp_kv, rows, win, z_t, gates_g, tsel, tcmp, msel, expand, nb, t):
    nq = t // Q_BLOCK
    qw = A_HPG * HD
    n_sel = -(-t // SEL_BLOCK)
    k_top = min(SEL_TOPK, n_sel)
    n_kt = t // (2 * LANE)
    n_cmp = cmp_kv.shape[1]
    n_sel_pad = msel.shape[0]
    kernel = functools.partial(_nsa_prompt_kernel, n_sel_pad=n_sel_pad, n_sel=n_sel, k_top=k_top)
    return pl.pallas_call(
        kernel,
        grid=(A_KV, nb, nq),
        in_specs=[
            pl.BlockSpec((Q_BLOCK, qw), lambda g, b, i: (b * nq + i, g)),
            pl.BlockSpec((None, n_cmp, HD), lambda g, b, i: (b, 0, g)),
            pl.BlockSpec((None, n_cmp, HD), lambda g, b, i: (b, 0, A_KV + g)),
            pl.BlockSpec((t, HD), lambda g, b, i: (b, 2 * A_KV + g)),
            pl.BlockSpec((t, HD), lambda g, b, i: (b, 3 * A_KV + g)),
            pl.BlockSpec((t, HD), lambda g, b, i: (b, g)),
            pl.BlockSpec((t, HD), lambda g, b, i: (b, A_KV + g)),
            pl.BlockSpec((qw, Q_BLOCK), lambda g, b, i: (g, b * nq + i)),
            pl.BlockSpec((qw, Q_BLOCK), lambda g, b, i: (A_KV + g, b * nq + i)),
            pl.BlockSpec((qw, Q_BLOCK), lambda g, b, i: (2 * A_KV + g, b * nq + i)),
            pl.BlockSpec((None, 3 * A_HPG, Q_BLOCK), lambda g, b, i: (g, 0, b * nq + i)),
            pl.BlockSpec((None, nq + 1, LANE, qw), lambda g, b, i: (g, 0, 0, 0)),
            pl.BlockSpec((None, None, n_cmp, qw), lambda g, b, i: (g, i, 0, 0)),
            pl.BlockSpec((n_sel_pad, n_cmp), lambda g, b, i: (0, 0)),
            pl.BlockSpec((t, n_sel_pad), lambda g, b, i: (0, 0)),
        ],
        out_specs=pl.BlockSpec((Q_BLOCK, qw), lambda g, b, i: (b * nq + i, g)),
        out_shape=jax.ShapeDtypeStruct((nb * t, A_HEADS * HD), BF16),
        scratch_shapes=[
            pltpu.VMEM((qw, HD), BF16),
            pltpu.VMEM((HD, qw), F32),
            pltpu.VMEM((1, qw), F32),
            pltpu.VMEM((1, qw), F32),
            pltpu.VMEM((n_kt, 2 * LANE, Q_BLOCK), F32),
            pltpu.VMEM((qw, Q_BLOCK), F32),
        ],
        compiler_params=_cparams(("arbitrary", "arbitrary", "arbitrary")),
        name="nsa_prompt_attention",
    )(qn, cmp_kv, cmp_kv, rows, rows, win, win, z_t, z_t, z_t, gates_g, tsel, tcmp, msel, expand)


def _diag_blocks(o_full, rows_per_group):
    return jnp.concatenate(
        [o_full[g * rows_per_group:(g + 1) * rows_per_group, g * HD:(g + 1) * HD] for g in range(A_KV)], axis=0)


def _nsa_sample_cmp_kernel(q_ref, kc_ref, vc_ref, bias_ref, msel_ref, oc_ref, sel_ref,
                           *, past_len, n_sel, k_top, n_q, blocks_per_step):
    n_cmp = kc_ref.shape[0]
    rows = q_ref.shape[0]
    rpg = A_HPG * n_q
    q = q_ref[...]
    s = _dot_nt(q, kc_ref[...].astype(BF16)) + bias_ref[...]
    rowi = lax.broadcasted_iota(jnp.int32, (rows, n_cmp), 0)
    ci = lax.broadcasted_iota(jnp.int32, (rows, n_cmp), 1)
    qpos = past_len + (rowi & (n_q - 1))
    valid = (ci * CMP_STRIDE + (CMP_LEN - 1)) <= qpos
    s = jnp.where(valid, s, NEG)
    mx = jnp.max(s, axis=-1, keepdims=True)
    e = jnp.where(valid, jnp.exp2(s - mx), 0.0)
    p = e / jnp.maximum(jnp.sum(e, axis=-1, keepdims=True), TINY)
    oc_ref[...] = _diag_blocks(_dot(p.astype(BF16), vc_ref[...].astype(BF16)), rpg)
    imps = []
    for g in range(A_KV):
        acc = p[g * rpg: g * rpg + n_q]
        for h in range(1, A_HPG):
            acc = acc + p[g * rpg + h * n_q: g * rpg + (h + 1) * n_q]
        imps.append(acc)
    imp = jnp.concatenate(imps, axis=0)
    imp_sel = _split3_dot(imp, msel_ref[...])
    shp = imp_sel.shape
    blk = lax.broadcasted_iota(jnp.int32, shp, 1)
    qp = past_len + (lax.broadcasted_iota(jnp.int32, shp, 0) & (n_q - 1))
    cur = jnp.right_shift(qp, int(math.log2(SEL_BLOCK)))
    forced = (blk == 0) | (blk == cur) | (blk == cur - 1)
    ok = blk * SEL_BLOCK <= qp
    score = jnp.where(ok, jnp.where(forced, FORCE_SCORE, imp_sel), -1.0)
    score = jnp.where(blk < n_sel, score, -jnp.inf)
    sel = _select_topk(score, blk, k_top)
    for st in range(sel_ref.shape[0]):
        sel_ref[st] = sel[:, st * blocks_per_step:(st + 1) * blocks_per_step]


def _nsa_sample_sel_kernel(pt_ref, *refs, n_pages, n_steps, past_len, n_q):
    del pt_ref
    pages = refs[:n_pages]
    q_ref, tail_ref, bias_ref, sel_ref, exp_ref, o_ref, m_ref, l_ref, acc_ref = refs[n_pages:]
    s_id = pl.program_id(1)
    kvw = A_KV * HD
    rows = q_ref.shape[0]
    rpg = A_HPG * n_q

    @pl.when(s_id == 0)
    def _():
        m_ref[...] = jnp.full(m_ref.shape, NEG, F32)
        l_ref[...] = jnp.zeros(l_ref.shape, F32)
        acc_ref[...] = jnp.zeros(acc_ref.shape, F32)

    def update(k, v, nk, key0):
        hit = _dot(sel_ref[...].astype(BF16), exp_ref[:, :nk])
        kpos = key0 + lax.broadcasted_iota(jnp.int32, hit.shape, 1)
        qpos = past_len + (lax.broadcasted_iota(jnp.int32, hit.shape, 0) & (n_q - 1))
        madd = jnp.where((hit > 0.5) & (kpos <= qpos), 0.0, NEG)
        madd = jnp.concatenate(
            [madd[g * n_q:(g + 1) * n_q] for g in range(A_KV) for _ in range(A_HPG)], axis=0)
        s = _dot_nt(q_ref[...], k) + bias_ref[:, :nk] + madd
        m_old = m_ref[...]
        m_new = jnp.maximum(m_old, jnp.max(s, axis=-1, keepdims=True))
        alpha = jnp.exp2(m_old - m_new)
        p = jnp.exp2(s - m_new)
        l_ref[...] = alpha * l_ref[...] + jnp.sum(p, axis=-1, keepdims=True)
        acc_ref[...] = alpha * acc_ref[...] + _dot(p.astype(BF16), v)
        m_ref[...] = m_new

    @pl.when(s_id < n_steps)
    def _():
        by_slot = [jnp.swapaxes(pg[...], 0, 1) for pg in pages]

        def gather(c0):
            cols = [jnp.concatenate([x[c0 + g] for x in by_slot], axis=0) for g in range(A_KV)]
            return jnp.concatenate(cols, axis=1).astype(BF16)

        update(gather(0), gather(A_KV), n_pages * PAGE, s_id * (n_pages * PAGE))

    @pl.when(s_id == n_steps)
    def _():
        update(tail_ref[:, :kvw].astype(BF16), tail_ref[:, kvw:].astype(BF16), PAGE, past_len)
        o_ref[...] = _diag_blocks(acc_ref[...] / l_ref[...], rpg)


def _nsa_sample_mix_kernel(q_ref, win_ref, bias_ref, oc_ref, os_ref, z_ref, gt_ref, o_ref, *, n_q):
    kvw = A_KV * HD
    rows = q_ref.shape[0]
    rpg = A_HPG * n_q
    nk = win_ref.shape[0]
    s = _dot_nt(q_ref[...], win_ref[:, :kvw].astype(BF16)) + bias_ref[...]
    qi = lax.broadcasted_iota(jnp.int32, (rows, nk), 0) & (n_q - 1)
    ki = lax.broadcasted_iota(jnp.int32, (rows, nk), 1)
    dist = qi + WINDOW - ki
    s = s + jnp.where((dist >= 0) & (dist < WINDOW), 0.0, NEG)
    mx = jnp.max(s, axis=-1, keepdims=True)
    e = jnp.exp2(s - mx)
    p = e / jnp.sum(e, axis=-1, keepdims=True)
    ow = _diag_blocks(_dot(p.astype(BF16), win_ref[:, kvw:].astype(BF16)), rpg)
    gates = _sigmoid(gt_ref[...])
    width = A_HEADS * HD
    for hh in range(A_HEADS):
        r = slice(hh * n_q, (hh + 1) * n_q)
        c = slice(hh * HD, (hh + 1) * HD)
        val = None
        for br, o in enumerate((oc_ref, os_ref, ow)):
            zc = slice(br * width + hh * HD, br * width + (hh + 1) * HD)
            gcol = gates[:, br * A_HEADS + hh: br * A_HEADS + hh + 1]
            term = gcol * o[r, :] * z_ref[:, zc]
            val = term if val is None else val + term
        o_ref[:, c] = val.astype(o_ref.dtype)


def nsa_sample_attention(qbd, cmp_kv, cache4d, page_table, tail_sel, win_all, z_s, gl_s,
                         b_cmp, b_sel, b_win, msel, expand, past_len, n_q):
    nb, rows, kvw = qbd.shape
    n_cmp = cmp_kv.shape[1]
    n_tab = page_table.shape[1]
    n_pages = min(PAGES_PER_STEP, n_tab)
    n_steps = n_tab // n_pages
    bps = n_pages * PAGE // SEL_BLOCK
    n_sel = -(-(past_len + n_q) // SEL_BLOCK)
    k_top = min(SEL_TOPK, n_sel)
    gq = A_KV * n_q

    oc, selmask = pl.pallas_call(
        functools.partial(_nsa_sample_cmp_kernel, past_len=past_len, n_sel=n_sel, k_top=k_top,
                          n_q=n_q, blocks_per_step=bps),
        grid=(nb,),
        in_specs=[
            pl.BlockSpec((None, rows, kvw), lambda b: (b, 0, 0)),
            pl.BlockSpec((None, n_cmp, kvw), lambda b: (b, 0, 0)),
            pl.BlockSpec((None, n_cmp, kvw), lambda b: (b, 0, 1)),
            pl.BlockSpec(b_cmp.shape, lambda b: (0, 0)),
            pl.BlockSpec(msel.shape, lambda b: (0, 0)),
        ],
        out_specs=[
            pl.BlockSpec((None, rows, HD), lambda b: (b, 0, 0)),
            pl.BlockSpec((None, n_steps + 1, gq, bps), lambda b: (b, 0, 0, 0)),
        ],
        out_shape=[
            jax.ShapeDtypeStruct((nb, rows, HD), F32),
            jax.ShapeDtypeStruct((nb, n_steps + 1, gq, bps), F32),
        ],
        compiler_params=_cparams(("arbitrary",)),
        name="nsa_sample_cmp",
    )(qbd, cmp_kv, cmp_kv, b_cmp, msel)

    half = 2 * kvw
    last = n_steps - 1

    def page_spec(k):
        return pl.BlockSpec(
            (None, PAGE, NCK, HD),
            lambda b, s, pt: (pt[b, jnp.minimum(s, last) * n_pages + k], 0, 1, 0))

    nkeys = n_pages * PAGE
    grid_spec = pltpu.PrefetchScalarGridSpec(
        num_scalar_prefetch=1,
        grid=(nb, n_steps + 1),
        in_specs=[page_spec(k) for k in range(n_pages)] + [
            pl.BlockSpec((None, rows, kvw), lambda b, s, pt: (b, 0, 0)),
            pl.BlockSpec((None, PAGE, half), lambda b, s, pt: (b, 0, 0)),
            pl.BlockSpec((rows, nkeys), lambda b, s, pt: (0, s)),
            pl.BlockSpec((None, None, gq, bps), lambda b, s, pt: (b, s, 0, 0)),
            pl.BlockSpec(expand.shape, lambda b, s, pt: (0, 0)),
        ],
        out_specs=pl.BlockSpec((None, rows, HD), lambda b, s, pt: (b, 0, 0)),
        scratch_shapes=[
            pltpu.VMEM((rows, 1), F32),
            pltpu.VMEM((rows, 1), F32),
            pltpu.VMEM((rows, kvw), F32),
        ],
    )
    osel = pl.pallas_call(
        functools.partial(_nsa_sample_sel_kernel, n_pages=n_pages, n_steps=n_steps,
                          past_len=past_len, n_q=n_q),
        grid_spec=grid_spec,
        out_shape=jax.ShapeDtypeStruct((nb, rows, HD), F32),
        compiler_params=_cparams(("arbitrary", "arbitrary")),
        name="nsa_sample_sel",
    )(page_table, *([cache4d] * n_pages), qbd, tail_sel, b_sel, selmask, expand)

    nkw = win_all.shape[1]
    zw = z_s.shape[1]
    return pl.pallas_call(
        functools.partial(_nsa_sample_mix_kernel, n_q=n_q),
        grid=(nb,),
        in_specs=[
            pl.BlockSpec((None, rows, kvw), lambda b: (b, 0, 0)),
            pl.BlockSpec((None, nkw, 2 * kvw), lambda b: (b, 0, 0)),
            pl.BlockSpec(b_win.shape, lambda b: (0, 0)),
            pl.BlockSpec((None, rows, HD), lambda b: (b, 0, 0)),
            pl.BlockSpec((None, rows, HD), lambda b: (b, 0, 0)),
            pl.BlockSpec((n_q, zw), lambda b: (b, 0)),
            pl.BlockSpec((n_q, 3 * A_HEADS), lambda b: (b, 0)),
        ],
        out_specs=pl.BlockSpec((n_q, A_HEADS * HD), lambda b: (b, 0)),
        out_shape=jax.ShapeDtypeStruct((nb * n_q, A_HEADS * HD), F32),
        compiler_params=_cparams(("arbitrary",)),
        name="nsa_sample_mix",
    )(qbd, win_all, b_win, oc, osel, z_s, gl_s)


def _retention_kernel(q_ref, k_ref, v_ref, g_ref, cos_ref, sin_ref, dm_ref, qd_ref, kd_ref, cd_ref,
                      s0_ref, o_ref, s_ref, *, dk, dv, hps):
    c = pl.program_id(2)
    half = dk // 2

    @pl.when(c == 0)
    def _():
        s_ref[...] = s0_ref[...]

    cos = cos_ref[...]
    sin = sin_ref[...]

    def rot(x):
        x1, x2 = x[:, :half], x[:, half:]
        return jnp.concatenate([x1 * cos - x2 * sin, x1 * sin + x2 * cos], axis=1)

    for j in range(hps):
        ks, vs = slice(j * dk, (j + 1) * dk), slice(j * dv, (j + 1) * dv)
        q = rot(q_ref[:, ks])
        k = rot(k_ref[:, ks]) * (dk ** -0.5)
        v = v_ref[:, vs].astype(BF16)
        qb = q.astype(BF16)
        att = _dot_nt(qb, k.astype(BF16)) * dm_ref[j]
        state = s_ref[j]
        o = _dot(att.astype(BF16), v) + _dot(qb, state.astype(BF16)) * qd_ref[j]
        s_ref[j] = state * cd_ref[j] + _dot_tn((k * kd_ref[j]).astype(BF16), v)
        o = o * lax.rsqrt(jnp.mean(o * o, axis=-1, keepdims=True) + EPS)
        o_ref[:, vs] = (o * _silu(g_ref[:, vs])).astype(o_ref.dtype)


def retention(proj, state0, q0, nb, t):
    _, nh, dk, dv = state0.shape
    cs = R_CHUNK if t % R_CHUNK == 0 else t
    n = t // cs
    half = dk // 2
    pos = (q0 + jnp.arange(t)).astype(F32)
    inv = jnp.power(ROPE_BASE, -jnp.arange(half, dtype=F32) / half)
    ang = pos[:, None] * inv[None, :]
    cos, sin = jnp.cos(ang), jnp.sin(ang)
    log_g = jnp.log1p(-jnp.exp2(-5.0 - jnp.arange(nh, dtype=F32)))
    ii = jnp.arange(cs, dtype=F32)
    rel = ii[:, None] - ii[None, :]
    causal = rel >= 0
    dmat = jnp.where(causal, jnp.exp(jnp.where(causal, rel, 0.0) * log_g[:, None, None]), 0.0)
    q_dec = jnp.exp((ii + 1.0) * log_g[:, None])[:, :, None]
    k_dec = jnp.exp((cs - 1.0 - ii) * log_g[:, None])[:, :, None]
    c_dec = jnp.exp(cs * log_g)[:, None, None]
    hps = RET_HEADS_PER_STEP
    assert nh % hps == 0
    ng = nh // hps
    vb0 = 2 * nh * dk // (hps * dv)
    return pl.pallas_call(
        functools.partial(_retention_kernel, dk=dk, dv=dv, hps=hps),
        grid=(nb, ng, n),
        in_specs=[
            pl.BlockSpec((cs, hps * dk), lambda b, h, c: (b * n + c, h)),
            pl.BlockSpec((cs, hps * dk), lambda b, h, c: (b * n + c, ng + h)),
            pl.BlockSpec((cs, hps * dv), lambda b, h, c: (b * n + c, vb0 + h)),
            pl.BlockSpec((cs, hps * dv), lambda b, h, c: (b * n + c, vb0 + ng + h)),
            pl.BlockSpec((cs, half), lambda b, h, c: (c, 0)),
            pl.BlockSpec((cs, half), lambda b, h, c: (c, 0)),
            pl.BlockSpec((hps, cs, cs), lambda b, h, c: (h, 0, 0)),
            pl.BlockSpec((hps, cs, 1), lambda b, h, c: (h, 0, 0)),
            pl.BlockSpec((hps, cs, 1), lambda b, h, c: (h, 0, 0)),
            pl.BlockSpec((hps, 1, 1), lambda b, h, c: (h, 0, 0)),
            pl.BlockSpec((None, hps, dk, dv), lambda b, h, c: (b, h, 0, 0)),
        ],
        out_specs=[
            pl.BlockSpec((cs, hps * dv), lambda b, h, c: (b * n + c, h)),
            pl.BlockSpec((None, hps, dk, dv), lambda b, h, c: (b, h, 0, 0)),
        ],
        out_shape=[
            jax.ShapeDtypeStruct((nb * t, nh * dv), BF16 if cs % 16 == 0 else F32),
            jax.ShapeDtypeStruct(state0.shape, F32),
        ],
        compiler_params=_cparams(("arbitrary", "arbitrary", "arbitrary")),
        name="retention",
    )(proj, proj, proj, proj, cos, sin, dmat, q_dec, k_dec, c_dec, state0)


def _t5_bucket_np(dist):
    n = np.maximum(dist, 0)
    exact = N_BUCKETS // 2
    logv = np.log(np.maximum(n, 1).astype(np.float32) / np.float32(exact)) / np.float32(
        math.log(MAX_DISTANCE / exact))
    large = np.minimum(exact + (logv * np.float32(N_BUCKETS - exact)).astype(np.int32), N_BUCKETS - 1)
    return np.where(n < exact, n, large).astype(np.int32)


def _bucket_starts():
    bk = _t5_bucket_np(np.arange(4 * MAX_DISTANCE))
    assert np.all(np.diff(bk) >= 0) and bk[-1] == N_BUCKETS - 1
    return [int(np.argmax(bk >= b)) for b in range(N_BUCKETS)]


_BUCKET_START = _bucket_starts()


def _sel_matrix(n_cmp, n_cmp_pad, n_sel_pad):
    c = np.arange(n_cmp_pad)[:, None]
    j = np.arange(n_sel_pad)[None, :]
    a = (c >= SEL_RATIO * j) & (c < SEL_RATIO * j + SEL_RATIO)
    b = (c >= SEL_RATIO * j - 1) & (c < SEL_RATIO * j + SEL_RATIO - 1)
    m = (a.astype(np.float32) + b.astype(np.float32)) * (c < n_cmp)
    return jnp.asarray(m, BF16)


def _expand_matrix(n_blk_pad, n_keys):
    blk = np.arange(n_blk_pad)[:, None]
    key = np.arange(n_keys)[None, :]
    return jnp.asarray((key // SEL_BLOCK == blk).astype(np.float32), BF16)


def _w1_pairs(w1):
    w = w1.reshape(2, 2, CMP_STRIDE // 2, 2, HD, HD)
    w = jnp.transpose(w, (2, 3, 4, 0, 1, 5))
    return w.reshape(CMP_STRIDE // 2, 2 * HD, 4 * HD).astype(BF16)


def kernel(x_prompt, x_sample, c_prompt, c_sample, cache_nsa_kv, cache_nsa_win, state_ret, page_table,
           norm_g, ada_w, ada_b, rel_bias, a_w_in, a_w_out, a_cmp_pe, a_cmp_w1, a_cmp_w2, a_qk_g,
           r_w_in, r_w_out):
    nbp, t, d = x_prompt.shape
    nbs, ts, _ = x_sample.shape
    n_tab = page_table.shape[1]
    past_len = n_tab * PAGE
    kvw = A_KV * HD
    qw = A_HEADS * HD
    assert t % (2 * LANE) == 0 and t >= WINDOW and ts == 8 and past_len >= WINDOW
    assert cache_nsa_win.shape[2] == WINDOW

    n_c = nbp + nbs
    pad_c = (-n_c) % 8
    c_all = jnp.concatenate([c_prompt, c_sample, jnp.zeros((pad_c, d), F32)], axis=0)
    mod = ada_modulation(c_all, ada_w, ada_b).reshape(ada_w.shape[0], n_c + pad_c, 3, d)

    def mods(layer):
        m = mod[layer]
        return (m[:nbp, 0], m[:nbp, 1], m[:nbp, 2]), (m[nbp:n_c, 0], m[nbp:n_c, 1], m[nbp:n_c, 2])

    (sh_p, sc_p, gt_p), (sh_s, sc_s, gt_s) = mods(0)
    hp = norm_modulate(x_prompt, norm_g[0], sc_p, sh_p)
    hs = norm_modulate(x_sample, norm_g[0], sc_s, sh_s)
    w_in_t = jnp.swapaxes(a_w_in[0], 0, 1)
    n_main = w_in_t.shape[0] - 3 * A_HEADS
    n_qkv = qw + 6 * kvw
    proj_p, proj_s = project(hp, hs, w_in_t, n_cols=n_qkv, w_is_nk=True)
    zt_p, z_s = project_gate_t(hp, hs, w_in_t, n_qkv, n_main - n_qkv)
    w_gate = jnp.pad(w_in_t[n_main:], ((0, LANE - 3 * A_HEADS), (0, 0)))
    gl_p, gl_s = project(hp, hs, w_gate, w_is_nk=True)
    gl_p, gl_s = gl_p[:, :3 * A_HEADS], gl_s[:, :3 * A_HEADS]
    qk_g = a_qk_g[0]

    qn_p, rows_p, win_p = nsa_prep(proj_p, qk_g)
    qn_s, rows_s, win_s = nsa_prep(proj_s, qk_g)

    w_pairs = _w1_pairs(a_cmp_w1[0])
    pe = a_cmp_pe[0]
    w1 = a_cmp_w1[0].reshape(2, CMP_LEN, HD, HD)
    w2 = a_cmp_w2[0]
    cpp = PAGE // CMP_STRIDE

    pages_p = t // PAGE
    ident = jnp.arange(nbp * pages_p, dtype=jnp.int32).reshape(nbp, pages_p)
    ab_p = compress_stage_a(rows_p.reshape(nbp * pages_p, PAGE, 4 * A_KV, HD), ident, w_pairs,
                            min(PAGES_PER_STEP, pages_p))
    zero_tail = jnp.zeros((nbp, cpp, NCK, 2 * HD), F32)
    cmp_p = compress_stage_b(ab_p, zero_tail, pe, w1, w2, qk_g)

    nq = t // Q_BLOCK
    n_ch_p = t // CMP_STRIDE
    tsel, tcmp = prompt_bias_tables(rel_bias, nq, n_ch_p)
    n_cmp_p = t // CMP_STRIDE - 1
    n_sel_p = -(-t // SEL_BLOCK)
    n_sel_pad_p = -(-n_sel_p // 16) * 16
    msel_p = _sel_matrix(n_cmp_p, n_ch_p, n_sel_pad_p).T
    expand_p = _expand_matrix(n_sel_pad_p, t).T
    gates_g = jnp.transpose(gl_p.reshape(nbp * t, 3, A_KV, A_HPG), (2, 1, 3, 0)).reshape(A_KV, 3 * A_HPG, nbp * t)
    mixed_p = nsa_prompt_attention(qn_p, cmp_p, rows_p, win_p, zt_p, gates_g,
                                   tsel, tcmp, msel_p, expand_p, nbp, t)

    n_pool = cache_nsa_kv.shape[1]
    cache4d = cache_nsa_kv[0].reshape(n_pool, PAGE, 4 * A_KV, HD)
    n_pages = min(PAGES_PER_STEP, n_tab)
    ab_s = compress_stage_a(cache4d, page_table, w_pairs, n_pages)
    tail_rows = jnp.pad(rows_s.reshape(nbs, ts, 4 * kvw), ((0, 0), (0, PAGE - ts), (0, 0)))
    ident_s = jnp.arange(nbs, dtype=jnp.int32).reshape(nbs, 1)
    ab_tail = compress_stage_a(tail_rows.reshape(nbs, PAGE, 4 * A_KV, HD), ident_s, w_pairs, 1)
    cmp_s = compress_stage_b(ab_s, ab_tail, pe, w1, w2, qk_g)

    n_cmp_s = cmp_s.shape[1]
    n_sel_s = -(-(past_len + ts) // SEL_BLOCK)
    n_steps = n_tab // n_pages
    bps = n_pages * PAGE // SEL_BLOCK
    n_sel_pad = (n_steps + 1) * bps
    nkeys = (n_steps + 1) * n_pages * PAGE
    nkw = WINDOW + LANE
    b_cmp, b_sel, b_win = sample_bias_rows(rel_bias, past_len, ts, n_cmp_s, nkeys, nkw)
    msel_s = _sel_matrix(n_cmp_s, n_cmp_s, n_sel_pad)
    expand_s = _expand_matrix(bps, n_pages * PAGE)
    q5 = qn_s.reshape(nbs, ts, A_KV, A_HPG, HD)
    eye = jnp.eye(A_KV, dtype=BF16)
    qbd = jnp.einsum('bqghd,ge->bghqed', q5, eye).reshape(nbs, A_HEADS * ts, kvw)
    win_new = win_s.reshape(nbs, ts, 2 * kvw)
    win_cache = cache_nsa_win[0].reshape(nbs, WINDOW, 2 * kvw)
    win_all = jnp.concatenate([win_cache, win_new], axis=1)
    win_pad = jnp.pad(win_all, ((0, 0), (0, nkw - WINDOW - ts), (0, 0)))
    tail_sel = tail_rows[:, :, 2 * kvw:]
    mixed_s = nsa_sample_attention(qbd, cmp_s, cache4d, page_table, tail_sel, win_pad, z_s, gl_s,
                                   b_cmp, b_sel, b_win, msel_s, expand_s, past_len, ts).astype(BF16)

    xs_flat = x_sample.reshape(nbs * ts, d)
    gate_s_rows = jnp.repeat(gt_s, ts, axis=0)
    x1p, x1s = project_residual(mixed_p, mixed_s, a_w_out[0], x_prompt.reshape(nbp * t, d), gt_p,
                                xs_flat, gate_s_rows, t)

    (sh_p, sc_p, gt_p), (sh_s, sc_s, gt_s) = mods(1)
    hp = norm_modulate(x1p.reshape(nbp, t, d), norm_g[1], sc_p, sh_p)
    hs = norm_modulate(x1s.reshape(nbs, ts, d), norm_g[1], sc_s, sh_s)
    rp, rs = project(hp, hs, r_w_in[0])
    s0 = jnp.zeros((nbp,) + state_ret.shape[2:], F32)
    op, ret_p = retention(rp, s0, 0, nbp, t)
    os_, ret_s = retention(rs, state_ret[0], past_len, nbs, ts)
    os_ = os_.astype(BF16)
    gate_s_rows = jnp.repeat(gt_s, ts, axis=0)
    x2p, x2s = project_residual(op, os_, r_w_out[0], x1p, gt_p, x1s, gate_s_rows, t)

    kv_p = rows_p.reshape(1, nbp, t, 4, A_KV, HD)
    kv_s = rows_s.reshape(1, nbs, ts, 4, A_KV, HD)
    wst_p = win_p.reshape(nbp, t, 2, A_KV, HD)[None, :, t - WINDOW:]
    wst_s = win_all[:, ts:].reshape(1, nbs, WINDOW, 2, A_KV, HD)
    return (x2p.reshape(nbp, t, d), x2s.reshape(nbs, ts, d), kv_p, kv_s, wst_p, wst_s,
            ret_p[None], ret_s[None])
```

```python
import functools
import math

import numpy as np
import jax
import jax.numpy as jnp
from jax import lax
from jax.experimental import pallas as pl
from jax.experimental.pallas import tpu as pltpu

EPS = 1e-6
HD = 128
A_KV = 4
A_HPG = 8
A_HEADS = A_KV * A_HPG
CMP_STRIDE = 16
CMP_LEN = 32
SEL_BLOCK = 64
SEL_RATIO = SEL_BLOCK // CMP_STRIDE
SEL_TOPK = 16
WINDOW = 512
Q_BLOCK = 128
FORCE_SCORE = 1e4
NEG = -1e30
TINY = 1e-30
N_BUCKETS = 32
MAX_DISTANCE = 1024
R_HEADS = 16
R_CHUNK = 128
ROPE_BASE = 10000.0
LOG2E = math.log2(math.e)
Q_SCALE = HD ** -0.5 * LOG2E
PAGE = 128
PAGES_PER_STEP = 16
RET_HEADS_PER_STEP = 4
SEL_KEY_TILE = 512

LANE = 128
VMEM_LIMIT = 60 * 1024 * 1024

F32 = jnp.float32
BF16 = jnp.bfloat16


def _cparams(sem):
    return pltpu.CompilerParams(dimension_semantics=sem, vmem_limit_bytes=VMEM_LIMIT)


def _dot(a, b):
    return jnp.dot(a, b, preferred_element_type=F32)


def _dot_nt(a, b):
    return lax.dot_general(a, b, (((1,), (1,)), ((), ())), preferred_element_type=F32)


def _dot_tn(a, b):
    return lax.dot_general(a, b, (((0,), (0,)), ((), ())), preferred_element_type=F32)


def _silu(x):
    return x / (1.0 + jnp.exp(-x))


def _sigmoid(x):
    return 1.0 / (1.0 + jnp.exp(-x))


def _split3_dot(x, m_bf16):
    hi = x.astype(BF16)
    r1 = x - hi.astype(F32)
    mid = r1.astype(BF16)
    lo = (r1 - mid.astype(F32)).astype(BF16)
    return _dot(hi, m_bf16) + _dot(mid, m_bf16) + _dot(lo, m_bf16)


def _ada_kernel(c_ref, w_ref, b_ref, o_ref):
    a = _silu(c_ref[...]).astype(BF16)
    o_ref[...] = _dot(a, w_ref[...].astype(BF16)) + b_ref[...]


def ada_modulation(c_all, ada_w, ada_b):
    depth, d, n = ada_w.shape
    rows = c_all.shape[0]
    tn = 512
    return pl.pallas_call(
        _ada_kernel,
        grid=(depth, n // tn),
        in_specs=[
            pl.BlockSpec((rows, d), lambda l, j: (0, 0)),
            pl.BlockSpec((None, d, tn), lambda l, j: (l, 0, j)),
            pl.BlockSpec((None, 1, tn), lambda l, j: (l, 0, j)),
        ],
        out_specs=pl.BlockSpec((None, rows, tn), lambda l, j: (l, 0, j)),
        out_shape=jax.ShapeDtypeStruct((depth, rows, n), F32),
        compiler_params=_cparams(("arbitrary", "arbitrary")),
        name="ada_modulation",
    )(c_all, ada_w, ada_b.reshape(depth, 1, n))


def _norm_mod_kernel(x_ref, g_ref, sc_ref, sh_ref, o_ref):
    x = x_ref[...]
    y = x * lax.rsqrt(jnp.mean(x * x, axis=-1, keepdims=True) + EPS)
    y = y * g_ref[...]
    o_ref[...] = (y * (1.0 + sc_ref[...]) + sh_ref[...]).astype(o_ref.dtype)


def norm_modulate(x, g, scale, shift):
    b, t, d = x.shape
    tt = min(t, 256)
    out = pl.pallas_call(
        _norm_mod_kernel,
        grid=(b, t // tt),
        in_specs=[
            pl.BlockSpec((None, tt, d), lambda i, j: (i, j, 0)),
            pl.BlockSpec((1, d), lambda i, j: (0, 0)),
            pl.BlockSpec((None, 1, d), lambda i, j: (i, 0, 0)),
            pl.BlockSpec((None, 1, d), lambda i, j: (i, 0, 0)),
        ],
        out_specs=pl.BlockSpec((None, tt, d), lambda i, j: (i, j, 0)),
        out_shape=jax.ShapeDtypeStruct((b, t, d), BF16),
        compiler_params=_cparams(("arbitrary", "arbitrary")),
        name="norm_modulate",
    )(x, g.reshape(1, d), scale.reshape(b, 1, d), shift.reshape(b, 1, d))
    return out.reshape(b * t, d)


def _proj_kernel(xp_ref, xs_ref, w_ref, op_ref, os_ref, wb_ref):
    @pl.when(pl.program_id(1) == 0)
    def _():
        wb_ref[...] = w_ref[...].astype(BF16)
        os_ref[...] = _dot(xs_ref[...], wb_ref[...])

    op_ref[...] = _dot(xp_ref[...], wb_ref[...])


def _proj_nk_kernel(xp_ref, xs_ref, w_ref, op_ref, os_ref, wb_ref):
    @pl.when(pl.program_id(1) == 0)
    def _():
        wb_ref[...] = w_ref[...].astype(BF16)
        os_ref[...] = _dot_nt(xs_ref[...], wb_ref[...])

    op_ref[...] = _dot_nt(xp_ref[...], wb_ref[...])


def _proj_gate_t_kernel(xp_ref, xs_ref, w_ref, opt_ref, os_ref, wb_ref):
    @pl.when(pl.program_id(1) == 0)
    def _():
        wb_ref[...] = w_ref[...].astype(BF16)
        os_ref[...] = _silu(_dot_nt(xs_ref[...], wb_ref[...]))

    opt_ref[...] = _silu(_dot_nt(wb_ref[...], xp_ref[...]))


def project_gate_t(xp, xs, w_nk, row0, n_rows):
    mp, k = xp.shape
    ms = xs.shape[0]
    tm, tn = _proj_tiles(mp, k, n_rows)
    assert row0 % tn == 0
    j0 = row0 // tn
    return pl.pallas_call(
        _proj_gate_t_kernel,
        grid=(n_rows // tn, mp // tm),
        in_specs=[
            pl.BlockSpec((tm, k), lambda j, i: (i, 0)),
            pl.BlockSpec((ms, k), lambda j, i: (0, 0)),
            pl.BlockSpec((tn, k), lambda j, i: (j0 + j, 0)),
        ],
        out_specs=[
            pl.BlockSpec((tn, tm), lambda j, i: (j, i)),
            pl.BlockSpec((ms, tn), lambda j, i: (0, j)),
        ],
        out_shape=[jax.ShapeDtypeStruct((n_rows, mp), F32), jax.ShapeDtypeStruct((ms, n_rows), F32)],
        scratch_shapes=[pltpu.VMEM((tn, k), BF16)],
        compiler_params=_cparams(("arbitrary", "arbitrary")),
        name="project_gate_t",
    )(xp, xs, w_nk)


def _proj_res_kernel(xp_ref, xs_ref, w_ref, rp_ref, gp_ref, rs_ref, gs_ref, op_ref, os_ref, wb_ref):
    @pl.when(pl.program_id(1) == 0)
    def _():
        wb_ref[...] = w_ref[...].astype(BF16)
        os_ref[...] = rs_ref[...] + gs_ref[...] * _dot(xs_ref[...], wb_ref[...])

    op_ref[...] = rp_ref[...] + gp_ref[...] * _dot(xp_ref[...], wb_ref[...])


def _proj_tiles(mp, k, n, residual=False):
    if k <= 4096:
        tm, tn = (1024, 512) if residual else (512, 1024)
    else:
        tm, tn = 256, 512
    tm = min(tm, mp)
    while mp % tm:
        tm //= 2
    tn = min(tn, n)
    while n % tn:
        tn //= 2
    return tm, tn


def project(xp, xs, w, n_cols=None, w_is_nk=False):
    mp, k = xp.shape
    ms = xs.shape[0]
    n = w.shape[0 if w_is_nk else 1] if n_cols is None else n_cols
    tm, tn = _proj_tiles(mp, k, n)
    w_spec = pl.BlockSpec((tn, k), lambda j, i: (j, 0)) if w_is_nk else pl.BlockSpec((k, tn), lambda j, i: (0, j))
    return pl.pallas_call(
        _proj_nk_kernel if w_is_nk else _proj_kernel,
        grid=(n // tn, mp // tm),
        in_specs=[
            pl.BlockSpec((tm, k), lambda j, i: (i, 0)),
            pl.BlockSpec((ms, k), lambda j, i: (0, 0)),
            w_spec,
        ],
        out_specs=[
            pl.BlockSpec((tm, tn), lambda j, i: (i, j)),
            pl.BlockSpec((ms, tn), lambda j, i: (0, j)),
        ],
        out_shape=[jax.ShapeDtypeStruct((mp, n), F32), jax.ShapeDtypeStruct((ms, n), F32)],
        scratch_shapes=[pltpu.VMEM((tn, k) if w_is_nk else (k, tn), BF16)],
        compiler_params=_cparams(("arbitrary", "arbitrary")),
        name="project",
    )(xp, xs, w)


def project_residual(xp, xs, w, res_p, gate_p, res_s, gate_s, rows_per_batch):
    mp, k = xp.shape
    ms = xs.shape[0]
    n = w.shape[1]
    tm, tn = _proj_tiles(mp, k, n, residual=True)
    tm = min(tm, rows_per_batch)
    tpb = rows_per_batch // tm
    nb = gate_p.shape[0]
    return pl.pallas_call(
        _proj_res_kernel,
        grid=(n // tn, mp // tm),
        in_specs=[
            pl.BlockSpec((tm, k), lambda j, i: (i, 0)),
            pl.BlockSpec((ms, k), lambda j, i: (0, 0)),
            pl.BlockSpec((k, tn), lambda j, i: (0, j)),
            pl.BlockSpec((tm, tn), lambda j, i: (i, j)),
            pl.BlockSpec((None, 1, tn), lambda j, i: (i // tpb, 0, j)),
            pl.BlockSpec((ms, tn), lambda j, i: (0, j)),
            pl.BlockSpec((ms, tn), lambda j, i: (0, j)),
        ],
        out_specs=[
            pl.BlockSpec((tm, tn), lambda j, i: (i, j)),
            pl.BlockSpec((ms, tn), lambda j, i: (0, j)),
        ],
        out_shape=[jax.ShapeDtypeStruct((mp, n), F32), jax.ShapeDtypeStruct((ms, n), F32)],
        scratch_shapes=[pltpu.VMEM((k, tn), BF16)],
        compiler_params=_cparams(("arbitrary", "arbitrary")),
        name="project_residual",
    )(xp, xs, w, res_p, gate_p.reshape(nb, 1, n), res_s, gate_s)


def _nsa_prep_kernel(q_ref, a_ref, b_ref, c_ref, g_ref, qn_ref, rows_ref, win_ref):
    def hnorm(x, g):
        return x * lax.rsqrt(jnp.mean(x * x, axis=-1, keepdims=True) + EPS) * g

    g = g_ref[...]
    for j in range(A_HEADS):
        sl = slice(j * HD, (j + 1) * HD)
        qn_ref[:, sl] = (hnorm(q_ref[:, sl], g[0:1]) * Q_SCALE).astype(qn_ref.dtype)
    kvw = A_KV * HD
    rows_ref[:, : 2 * kvw] = a_ref[...]
    rows_ref[:, 3 * kvw:] = b_ref[:, kvw:]
    win_ref[:, kvw:] = c_ref[:, kvw:]
    for j in range(A_KV):
        sl = slice(j * HD, (j + 1) * HD)
        rows_ref[:, 2 * kvw + j * HD: 2 * kvw + (j + 1) * HD] = hnorm(b_ref[:, sl], g[2:3])
        win_ref[:, sl] = hnorm(c_ref[:, sl], g[3:4])


def nsa_prep(proj, qk_g):
    m = proj.shape[0]
    tm = min(m, 256)
    qw = A_HEADS * HD
    kv2 = 2 * A_KV * HD
    base = qw // kv2
    return pl.pallas_call(
        _nsa_prep_kernel,
        grid=(m // tm,),
        in_specs=[
            pl.BlockSpec((tm, qw), lambda i: (i, 0)),
            pl.BlockSpec((tm, kv2), lambda i: (i, base)),
            pl.BlockSpec((tm, kv2), lambda i: (i, base + 1)),
            pl.BlockSpec((tm, kv2), lambda i: (i, base + 2)),
            pl.BlockSpec((4, HD), lambda i: (0, 0)),
        ],
        out_specs=[
            pl.BlockSpec((tm, qw), lambda i: (i, 0)),
            pl.BlockSpec((tm, 2 * kv2), lambda i: (i, 0)),
            pl.BlockSpec((tm, kv2), lambda i: (i, 0)),
        ],
        out_shape=[
            jax.ShapeDtypeStruct((m, qw), BF16),
            jax.ShapeDtypeStruct((m, 2 * kv2), F32),
            jax.ShapeDtypeStruct((m, kv2), F32),
        ],
        compiler_params=_cparams(("arbitrary",)),
        name="nsa_prep",
    )(proj, proj, proj, proj, qk_g)


NCK = 2 * A_KV


def _cmp_a_kernel(pt_ref, *refs, n_pages):
    del pt_ref
    pages = refs[:n_pages]
    w_ref = refs[n_pages]
    o_ref = refs[n_pages + 1]
    cpp = PAGE // CMP_STRIDE
    rows = n_pages * cpp * NCK
    acc = None
    for tp in range(CMP_STRIDE // 2):
        pieces = []
        for pg in pages:
            for ch in range(cpp):
                t0 = ch * CMP_STRIDE + 2 * tp
                pieces.append(jnp.concatenate([pg[t0], pg[t0 + 1]], axis=1))
        lhs = jnp.concatenate(pieces, axis=0).astype(BF16)
        d = _dot(lhs, w_ref[tp])
        acc = d if acc is None else acc + d
    is_k = (lax.broadcasted_iota(jnp.int32, (rows, 2 * HD), 0) & (NCK - 1)) < A_KV
    out = jnp.where(is_k, acc[:, :2 * HD], acc[:, 2 * HD:])
    o_ref[...] = out.reshape(n_pages * cpp, NCK, 2 * HD)


def compress_stage_a(store, page_table, w_pairs, n_pages):
    nb, n_tab = page_table.shape
    steps = n_tab // n_pages
    cpp = PAGE // CMP_STRIDE

    def page_spec(k):
        return pl.BlockSpec((None, PAGE, NCK, HD), lambda b, s, pt: (pt[b, s * n_pages + k], 0, 0, 0))

    grid_spec = pltpu.PrefetchScalarGridSpec(
        num_scalar_prefetch=1,
        grid=(nb, steps),
        in_specs=[page_spec(k) for k in range(n_pages)]
        + [pl.BlockSpec(w_pairs.shape, lambda b, s, pt: (0, 0, 0))],
        out_specs=pl.BlockSpec((None, n_pages * cpp, NCK, 2 * HD), lambda b, s, pt: (b, s, 0, 0)),
    )
    return pl.pallas_call(
        functools.partial(_cmp_a_kernel, n_pages=n_pages),
        grid_spec=grid_spec,
        out_shape=jax.ShapeDtypeStruct((nb, n_tab * cpp, NCK, 2 * HD), F32),
        compiler_params=_cparams(("arbitrary", "arbitrary")),
        name="compress_stage_a",
    )(page_table, *([store] * n_pages), w_pairs)


def _cmp_b_kernel(ab_ref, nx_ref, tail_ref, pe_ref, w1_ref, w2_ref, g_ref, o_ref, const_ref):
    j = pl.program_id(1)
    nch = ab_ref.shape[0]
    rows = nch * NCK

    @pl.when((pl.program_id(0) == 0) & (j == 0))
    def _():
        for kind in range(2):
            const = jnp.zeros((8, HD), F32)
            for t in range(CMP_LEN):
                row = jnp.broadcast_to(pe_ref[kind, t:t + 1, :], (8, HD)).astype(BF16)
                const = const + _dot(row, w1_ref[kind, t].astype(BF16))
            const_ref[kind * A_KV:(kind + 1) * A_KV, :] = const[:A_KV]

    ab = ab_ref[...].reshape(rows, 2 * HD)
    nxt = pltpu.roll(ab[:, HD:], rows - NCK, 0).reshape(nch, NCK, HD)
    first_next = jnp.where(j == pl.num_programs(1) - 1, tail_ref[0, :, HD:], nx_ref[0, :, HD:])
    cid = lax.broadcasted_iota(jnp.int32, (nch, NCK, HD), 0)
    nxt = jnp.where(cid == nch - 1, first_next[None], nxt)
    pre = ab[:, :HD].reshape(nch, NCK, HD) + nxt + const_ref[...][None]
    act = _silu(pre).reshape(rows, HD).astype(BF16)
    w2 = jnp.concatenate([w2_ref[0], w2_ref[1]], axis=1).astype(BF16)
    y2 = _dot(act, w2)
    is_k = (lax.broadcasted_iota(jnp.int32, (rows, HD), 0) & (NCK - 1)) < A_KV
    y = jnp.where(is_k, y2[:, :HD], y2[:, HD:])
    yn = y * lax.rsqrt(jnp.mean(y * y, axis=-1, keepdims=True) + EPS) * g_ref[1:2]
    o_ref[...] = jnp.where(is_k, yn, y).reshape(nch, NCK, HD)


def compress_stage_b(ab, tail, pe, w1, w2, qk_g):
    nb, n_ch = ab.shape[:2]
    tc = min(256, n_ch)
    assert n_ch % tc == 0
    nt = n_ch // tc
    out = pl.pallas_call(
        _cmp_b_kernel,
        grid=(nb, nt),
        in_specs=[
            pl.BlockSpec((None, tc, NCK, 2 * HD), lambda b, j: (b, j, 0, 0)),
            pl.BlockSpec((None, 1, NCK, 2 * HD), lambda b, j: (b, jnp.minimum((j + 1) * tc, n_ch - 1), 0, 0)),
            pl.BlockSpec((None, 1, NCK, 2 * HD), lambda b, j: (b, 0, 0, 0)),
            pl.BlockSpec(pe.shape, lambda b, j: (0, 0, 0)),
            pl.BlockSpec(w1.shape, lambda b, j: (0, 0, 0, 0)),
            pl.BlockSpec(w2.shape, lambda b, j: (0, 0, 0)),
            pl.BlockSpec((4, HD), lambda b, j: (0, 0)),
        ],
        out_specs=pl.BlockSpec((None, tc, NCK, HD), lambda b, j: (b, j, 0, 0)),
        out_shape=jax.ShapeDtypeStruct((nb, n_ch, NCK, HD), F32),
        scratch_shapes=[pltpu.VMEM((NCK, HD), F32)],
        compiler_params=_cparams(("arbitrary", "arbitrary")),
        name="compress_stage_b",
    )(ab, ab, tail, pe, w1, w2, qk_g)
    return out.reshape(nb, n_ch * NCK, HD)


def _cmp_rows(cmp_ref, slot, n):
    return cmp_ref[pl.ds(slot, n, stride=NCK), :]


def _select_topk(score, blk, k_top):
    sel = jnp.zeros(score.shape, F32)
    for _ in range(k_top):
        m = jnp.max(score, axis=-1, keepdims=True)
        cand = jnp.where(score == m, blk, jnp.int32(1 << 30))
        first = jnp.min(cand, axis=-1, keepdims=True)
        pick = blk == first
        sel = jnp.where(pick, 1.0, sel)
        score = jnp.where(pick, -jnp.inf, score)
    return sel


def _select_topk_t(score, blk, k_top):
    sel = jnp.zeros(score.shape, F32)
    for _ in range(k_top):
        m = jnp.max(score, axis=0, keepdims=True)
        cand = jnp.where(score == m, blk, jnp.int32(1 << 30))
        first = jnp.min(cand, axis=0, keepdims=True)
        pick = blk == first
        sel = jnp.where(pick, 1.0, sel)
        score = jnp.where(pick, -jnp.inf, score)
    return sel


def _bias_select(dist, dmin, dmax, value_of):
    lo = int(_t5_bucket_np(np.array([max(dmin, 0)]))[0])
    hi = int(_t5_bucket_np(np.array([max(dmax, 0)]))[0])
    out = jnp.full(dist.shape, value_of(lo), F32)
    for b in range(lo + 1, hi + 1):
        out = jnp.where(dist >= _BUCKET_START[b], value_of(b), out)
    return out * LOG2E


def _bias_tables_kernel(rb_ref, tsel_ref, twin_ref, tcmp_ref):
    head = pl.program_id(0)
    value_of = lambda b: rb_ref[b, head]
    n_off, nq, nc = tsel_ref.shape[0], tcmp_ref.shape[0], tcmp_ref.shape[1]
    kl = lax.broadcasted_iota(jnp.int32, (LANE, Q_BLOCK), 0)
    ql = lax.broadcasted_iota(jnp.int32, (LANE, Q_BLOCK), 1)
    for o in range(n_off):
        base = (o - 1) * Q_BLOCK
        dist = base + ql - kl
        tile = _bias_select(dist, base - (LANE - 1), base + Q_BLOCK - 1, value_of)
        tsel_ref[o] = tile
        if 1 <= o <= twin_ref.shape[0]:
            twin_ref[o - 1] = tile + jnp.where((dist >= 0) & (dist < WINDOW), 0.0, NEG)
    cend = lax.broadcasted_iota(jnp.int32, (nc, Q_BLOCK), 0) * CMP_STRIDE + (CMP_LEN - 1)
    qc = lax.broadcasted_iota(jnp.int32, (nc, Q_BLOCK), 1)
    for i in range(nq):
        base = i * Q_BLOCK
        tcmp_ref[i] = _bias_select(base + qc - cend, base - ((nc - 1) * CMP_STRIDE + CMP_LEN - 1),
                                   base + Q_BLOCK - 1, value_of)


def prompt_bias_tables(rel_bias, nq, n_cmp):
    n_win = min(WINDOW // Q_BLOCK + 1, nq)
    return pl.pallas_call(
        _bias_tables_kernel,
        grid=(A_HEADS,),
        in_specs=[pl.BlockSpec(memory_space=pltpu.SMEM)],
        out_specs=[
            pl.BlockSpec((None, nq + 1, LANE, Q_BLOCK), lambda h: (h // A_HPG, 0, 0, h % A_HPG)),
            pl.BlockSpec((None, n_win, LANE, Q_BLOCK), lambda h: (h // A_HPG, 0, 0, h % A_HPG)),
            pl.BlockSpec((None, nq, n_cmp, Q_BLOCK), lambda h: (h // A_HPG, 0, 0, h % A_HPG)),
        ],
        out_shape=[
            jax.ShapeDtypeStruct((A_KV, nq + 1, LANE, A_HPG * Q_BLOCK), F32),
            jax.ShapeDtypeStruct((A_KV, n_win, LANE, A_HPG * Q_BLOCK), F32),
            jax.ShapeDtypeStruct((A_KV, nq, n_cmp, A_HPG * Q_BLOCK), F32),
        ],
        compiler_params=_cparams(("arbitrary",)),
        name="prompt_bias_tables",
    )(rel_bias)


def _bias_rows_kernel(rb_ref, cmp_ref, sel_ref, win_ref, *, past_len, n_q):
    head = pl.program_id(0)
    value_of = lambda b: rb_ref[b, head]

    def fill(ref, key_pos, kmin, kmax, q0):
        q = lax.broadcasted_iota(jnp.int32, ref.shape, 0)
        k = lax.broadcasted_iota(jnp.int32, ref.shape, 1)
        ref[...] = _bias_select(q0 + q - key_pos(k), q0 - kmax, q0 + n_q - 1 - kmin, value_of)

    nc, nk, nw = cmp_ref.shape[1], sel_ref.shape[1], win_ref.shape[1]
    fill(cmp_ref, lambda k: k * CMP_STRIDE + (CMP_LEN - 1), CMP_LEN - 1, (nc - 1) * CMP_STRIDE + CMP_LEN - 1, past_len)
    fill(sel_ref, lambda k: k, 0, nk - 1, past_len)
    fill(win_ref, lambda k: k, 0, nw - 1, WINDOW)


def sample_bias_rows(rel_bias, past_len, n_q, n_cmp, n_keys, n_win):
    shapes = [(A_HEADS * n_q, n) for n in (n_cmp, n_keys, n_win)]
    return pl.pallas_call(
        functools.partial(_bias_rows_kernel, past_len=past_len, n_q=n_q),
        grid=(A_HEADS,),
        in_specs=[pl.BlockSpec(memory_space=pltpu.SMEM)],
        out_specs=[pl.BlockSpec((n_q, s[1]), lambda h: (h, 0)) for s in shapes],
        out_shape=[jax.ShapeDtypeStruct(s, F32) for s in shapes],
        compiler_params=_cparams(("arbitrary",)),
        name="sample_bias_rows",
    )(rel_bias)


def _nsa_prompt_kernel(q_ref, cmp_ref, ks_ref, vs_ref, kw_ref, vw_ref, zc_ref, zs_ref, zw_ref,
                       gt_ref, tsel_ref, twin_ref, tcmp_ref, msel_ref, exp_ref, o_ref,
                       qa_ref, acc_ref, m_ref, l_ref, madd_ref, mix_ref, *, n_sel_pad, n_sel, k_top):
    i = pl.program_id(2)
    qb = Q_BLOCK
    n_kt, kt = madd_ref.shape[:2]
    tpk = kt // LANE
    nc = cmp_ref.shape[0] // NCK
    gates = _sigmoid(gt_ref[...])
    hs = [slice(h * HD, (h + 1) * HD) for h in range(A_HPG)]
    for h in range(A_HPG):
        qa_ref[hs[h], :] = q_ref[:, hs[h]]
    qa = qa_ref[...]

    def heads(x):
        return jnp.concatenate([x] * A_HPG, axis=1)

    def gate_row(branch):
        return jnp.concatenate([gates[branch * A_HPG + h: branch * A_HPG + h + 1, :] for h in range(A_HPG)], axis=1)

    def emit(branch, o_t, first):
        z_ref = (zc_ref, zs_ref, zw_ref)[branch]
        for h in range(A_HPG):
            val = o_t[:, hs[h]] * z_ref[hs[h], :]
            if first:
                mix_ref[hs[h], :] = val
            else:
                mix_ref[hs[h], :] = mix_ref[hs[h], :] + val

    grp = pl.program_id(0)
    kc = _cmp_rows(cmp_ref, grp, nc).astype(BF16)
    vc = _cmp_rows(cmp_ref, A_KV + grp, nc).astype(BF16)
    cend = lax.broadcasted_iota(jnp.int32, (nc, qb), 0) * CMP_STRIDE + (CMP_LEN - 1)
    qpos_c = i * qb + lax.broadcasted_iota(jnp.int32, (nc, qb), 1)
    cvalid = heads(cend <= qpos_c)
    s = jnp.where(cvalid, _dot_nt(kc, qa) + tcmp_ref[...], NEG)
    mx = jnp.max(s, axis=0, keepdims=True)
    e = jnp.where(cvalid, jnp.exp2(s - mx), 0.0)
    p = e * (1.0 / jnp.maximum(jnp.sum(e, axis=0, keepdims=True), TINY))
    imp = p[:, hs[0]]
    for h in range(1, A_HPG):
        imp = imp + p[:, hs[h]]
    emit(0, _dot_tn(vc, p.astype(BF16)) * gate_row(0), True)

    hi = imp.astype(BF16)
    r1 = imp - hi.astype(F32)
    mid = r1.astype(BF16)
    lo = (r1 - mid.astype(F32)).astype(BF16)
    msel = msel_ref[...]
    imp_sel = _dot(msel, hi) + _dot(msel, mid) + _dot(msel, lo)
    blk = lax.broadcasted_iota(jnp.int32, (n_sel_pad, qb), 0)
    qpos = i * qb + lax.broadcasted_iota(jnp.int32, (n_sel_pad, qb), 1)
    cur = jnp.right_shift(qpos, int(math.log2(SEL_BLOCK)))
    forced = (blk == 0) | (blk == cur) | (blk == cur - 1)
    valid = blk * SEL_BLOCK <= qpos
    score = jnp.where(valid, jnp.where(forced, FORCE_SCORE, imp_sel), -1.0)
    score = jnp.where(blk < n_sel, score, -jnp.inf)
    sel = _select_topk_t(score, blk, k_top).astype(BF16)
    kpos2 = lax.broadcasted_iota(jnp.int32, (kt, qb), 0)
    qpos2 = i * qb + lax.broadcasted_iota(jnp.int32, (kt, qb), 1)
    for jj in range(n_kt):
        hit = _dot(exp_ref[jj * kt:(jj + 1) * kt, :], sel)
        ok = (hit > 0.5) & ((jj * kt + kpos2) <= qpos2)
        madd_ref[jj] = jnp.where(ok, 0.0, NEG)

    def reset():
        m_ref[...] = jnp.full(m_ref.shape, NEG, F32)
        l_ref[...] = jnp.zeros(l_ref.shape, F32)
        acc_ref[...] = jnp.zeros(acc_ref.shape, F32)

    def tile_update(k, v, bias, madd):
        s = _dot_nt(k, qa) + bias
        if madd is not None:
            s = s + heads(madd)
        m_old = m_ref[...]
        m_new = jnp.maximum(m_old, jnp.max(s, axis=0, keepdims=True))
        alpha = jnp.exp2(m_old - m_new)
        p = jnp.exp2(s - m_new)
        l_ref[...] = alpha * l_ref[...] + jnp.sum(p, axis=0, keepdims=True)
        m_ref[...] = m_new
        acc_ref[...] = acc_ref[...] * alpha + _dot_tn(v, p.astype(BF16))

    def finish(branch):
        emit(branch, acc_ref[...] * (gate_row(branch) / l_ref[...]), False)

    reset()

    def sel_body(jj, carry):
        k0 = pl.multiple_of(jj * kt, kt)
        k = ks_ref[pl.ds(k0, kt), :].astype(BF16)
        v = vs_ref[pl.ds(k0, kt), :].astype(BF16)
        off = i - tpk * jj
        bias = jnp.concatenate([tsel_ref[jnp.maximum(off + 1 - r, 0)] for r in range(tpk)], axis=0)
        tile_update(k, v, bias, madd_ref[jj])
        return carry

    lax.fori_loop(0, i // tpk + 1, sel_body, 0)
    finish(1)

    reset()

    def win_body(t, carry):
        k0 = pl.multiple_of((i - t) * qb, qb)
        k = kw_ref[pl.ds(k0, qb), :].astype(BF16)
        v = vw_ref[pl.ds(k0, qb), :].astype(BF16)
        tile_update(k, v, twin_ref[t], None)
        return carry

    lax.fori_loop(0, jnp.minimum(i, WINDOW // qb) + 1, win_body, 0)
    finish(2)
    for h in range(A_HPG):
        o_ref[:, hs[h]] = mix_ref[hs[h], :].T.astype(o_ref.dtype)


def nsa_prompt_attention(qn, cmp_kv, rows, win, z_t, gates_g, tsel, twin, tcmp, msel, expand, nb, t):
    nq = t // Q_BLOCK
    qw = A_HPG * HD
    n_sel = -(-t // SEL_BLOCK)
    k_top = min(SEL_TOPK, n_sel)
    kt = SEL_KEY_TILE if t % SEL_KEY_TILE == 0 else 2 * LANE
    n_kt = t // kt
    n_cmp = cmp_kv.shape[1] // NCK
    n_sel_pad = msel.shape[0]
    kernel = functools.partial(_nsa_prompt_kernel, n_sel_pad=n_sel_pad, n_sel=n_sel, k_top=k_top)
    return pl.pallas_call(
        kernel,
        grid=(A_KV, nb, nq),
        in_specs=[
            pl.BlockSpec((Q_BLOCK, qw), lambda g, b, i: (b * nq + i, g)),
            pl.BlockSpec((None, n_cmp * NCK, HD), lambda g, b, i: (b, 0, 0)),
            pl.BlockSpec((t, HD), lambda g, b, i: (b, 2 * A_KV + g)),
            pl.BlockSpec((t, HD), lambda g, b, i: (b, 3 * A_KV + g)),
            pl.BlockSpec((t, HD), lambda g, b, i: (b, g)),
            pl.BlockSpec((t, HD), lambda g, b, i: (b, A_KV + g)),
            pl.BlockSpec((qw, Q_BLOCK), lambda g, b, i: (g, b * nq + i)),
            pl.BlockSpec((qw, Q_BLOCK), lambda g, b, i: (A_KV + g, b * nq + i)),
            pl.BlockSpec((qw, Q_BLOCK), lambda g, b, i: (2 * A_KV + g, b * nq + i)),
            pl.BlockSpec((None, 3 * A_HPG, Q_BLOCK), lambda g, b, i: (g, 0, b * nq + i)),
            pl.BlockSpec((None, nq + 1, LANE, qw), lambda g, b, i: (g, 0, 0, 0)),
            pl.BlockSpec((None, twin.shape[1], LANE, qw), lambda g, b, i: (g, 0, 0, 0)),
            pl.BlockSpec((None, None, n_cmp, qw), lambda g, b, i: (g, i, 0, 0)),
            pl.BlockSpec((n_sel_pad, n_cmp), lambda g, b, i: (0, 0)),
            pl.BlockSpec((t, n_sel_pad), lambda g, b, i: (0, 0)),
        ],
        out_specs=pl.BlockSpec((Q_BLOCK, qw), lambda g, b, i: (b * nq + i, g)),
        out_shape=jax.ShapeDtypeStruct((nb * t, A_HEADS * HD), BF16),
        scratch_shapes=[
            pltpu.VMEM((qw, HD), BF16),
            pltpu.VMEM((HD, qw), F32),
            pltpu.VMEM((1, qw), F32),
            pltpu.VMEM((1, qw), F32),
            pltpu.VMEM((n_kt, kt, Q_BLOCK), F32),
            pltpu.VMEM((qw, Q_BLOCK), F32),
        ],
        compiler_params=_cparams(("arbitrary", "arbitrary", "arbitrary")),
        name="nsa_prompt_attention",
    )(qn, cmp_kv, rows, rows, win, win, z_t, z_t, z_t, gates_g, tsel, twin, tcmp, msel, expand)


def _diag_blocks(o_full, rows_per_group):
    return jnp.concatenate(
        [o_full[g * rows_per_group:(g + 1) * rows_per_group, g * HD:(g + 1) * HD] for g in range(A_KV)], axis=0)


def _nsa_sample_cmp_kernel(q_ref, cmp_ref, bias_ref, msel_ref, oc_ref, sel_ref,
                           *, past_len, n_sel, k_top, n_q, blocks_per_step):
    n_cmp = cmp_ref.shape[0] // NCK
    kc = jnp.concatenate([_cmp_rows(cmp_ref, g, n_cmp) for g in range(A_KV)], axis=1).astype(BF16)
    vc = jnp.concatenate([_cmp_rows(cmp_ref, A_KV + g, n_cmp) for g in range(A_KV)], axis=1).astype(BF16)
    rows = q_ref.shape[0]
    rpg = A_HPG * n_q
    q = q_ref[...]
    s = _dot_nt(q, kc) + bias_ref[...]
    rowi = lax.broadcasted_iota(jnp.int32, (rows, n_cmp), 0)
    ci = lax.broadcasted_iota(jnp.int32, (rows, n_cmp), 1)
    qpos = past_len + (rowi & (n_q - 1))
    valid = (ci * CMP_STRIDE + (CMP_LEN - 1)) <= qpos
    s = jnp.where(valid, s, NEG)
    mx = jnp.max(s, axis=-1, keepdims=True)
    e = jnp.where(valid, jnp.exp2(s - mx), 0.0)
    p = e / jnp.maximum(jnp.sum(e, axis=-1, keepdims=True), TINY)
    oc_ref[...] = _diag_blocks(_dot(p.astype(BF16), vc), rpg)
    imps = []
    for g in range(A_KV):
        acc = p[g * rpg: g * rpg + n_q]
        for h in range(1, A_HPG):
            acc = acc + p[g * rpg + h * n_q: g * rpg + (h + 1) * n_q]
        imps.append(acc)
    imp = jnp.concatenate(imps, axis=0)
    imp_sel = _split3_dot(imp, msel_ref[...])
    shp = imp_sel.shape
    blk = lax.broadcasted_iota(jnp.int32, shp, 1)
    qp = past_len + (lax.broadcasted_iota(jnp.int32, shp, 0) & (n_q - 1))
    cur = jnp.right_shift(qp, int(math.log2(SEL_BLOCK)))
    forced = (blk == 0) | (blk == cur) | (blk == cur - 1)
    ok = blk * SEL_BLOCK <= qp
    score = jnp.where(ok, jnp.where(forced, FORCE_SCORE, imp_sel), -1.0)
    score = jnp.where(blk < n_sel, score, -jnp.inf)
    sel = _select_topk(score, blk, k_top)
    for st in range(sel_ref.shape[0]):
        sel_ref[st] = sel[:, st * blocks_per_step:(st + 1) * blocks_per_step]


def _nsa_sample_sel_kernel(pt_ref, *refs, n_pages, n_steps, past_len, n_q):
    del pt_ref
    pages = refs[:n_pages]
    q_ref, tail_ref, bias_ref, sel_ref, exp_ref, o_ref, m_ref, l_ref, acc_ref = refs[n_pages:]
    s_id = pl.program_id(1)
    kvw = A_KV * HD
    rows = q_ref.shape[0]
    rpg = A_HPG * n_q

    @pl.when(s_id == 0)
    def _():
        m_ref[...] = jnp.full(m_ref.shape, NEG, F32)
        l_ref[...] = jnp.zeros(l_ref.shape, F32)
        acc_ref[...] = jnp.zeros(acc_ref.shape, F32)

    def update(k, v, nk, key0):
        hit = _dot(sel_ref[...].astype(BF16), exp_ref[:, :nk])
        kpos = key0 + lax.broadcasted_iota(jnp.int32, hit.shape, 1)
        qpos = past_len + (lax.broadcasted_iota(jnp.int32, hit.shape, 0) & (n_q - 1))
        madd = jnp.where((hit > 0.5) & (kpos <= qpos), 0.0, NEG)
        madd = jnp.concatenate(
            [madd[g * n_q:(g + 1) * n_q] for g in range(A_KV) for _ in range(A_HPG)], axis=0)
        s = _dot_nt(q_ref[...], k) + bias_ref[:, :nk] + madd
        m_old = m_ref[...]
        m_new = jnp.maximum(m_old, jnp.max(s, axis=-1, keepdims=True))
        alpha = jnp.exp2(m_old - m_new)
        p = jnp.exp2(s - m_new)
        l_ref[...] = alpha * l_ref[...] + jnp.sum(p, axis=-1, keepdims=True)
        acc_ref[...] = alpha * acc_ref[...] + _dot(p.astype(BF16), v)
        m_ref[...] = m_new

    @pl.when(s_id < n_steps)
    def _():
        by_slot = [jnp.swapaxes(pg[...], 0, 1) for pg in pages]

        def gather(c0):
            cols = [jnp.concatenate([x[c0 + g] for x in by_slot], axis=0) for g in range(A_KV)]
            return jnp.concatenate(cols, axis=1).astype(BF16)

        update(gather(0), gather(A_KV), n_pages * PAGE, s_id * (n_pages * PAGE))

    @pl.when(s_id == n_steps)
    def _():
        update(tail_ref[:, :kvw].astype(BF16), tail_ref[:, kvw:].astype(BF16), PAGE, past_len)
        o_ref[...] = _diag_blocks(acc_ref[...] / l_ref[...], rpg)


def _nsa_sample_mix_kernel(q_ref, win_ref, bias_ref, oc_ref, os_ref, z_ref, gt_ref, o_ref, *, n_q):
    kvw = A_KV * HD
    rows = q_ref.shape[0]
    rpg = A_HPG * n_q
    nk = win_ref.shape[0]
    s = _dot_nt(q_ref[...], win_ref[:, :kvw].astype(BF16)) + bias_ref[...]
    qi = lax.broadcasted_iota(jnp.int32, (rows, nk), 0) & (n_q - 1)
    ki = lax.broadcasted_iota(jnp.int32, (rows, nk), 1)
    dist = qi + WINDOW - ki
    s = s + jnp.where((dist >= 0) & (dist < WINDOW), 0.0, NEG)
    mx = jnp.max(s, axis=-1, keepdims=True)
    e = jnp.exp2(s - mx)
    p = e / jnp.sum(e, axis=-1, keepdims=True)
    ow = _diag_blocks(_dot(p.astype(BF16), win_ref[:, kvw:].astype(BF16)), rpg)
    gates = _sigmoid(gt_ref[...])
    width = A_HEADS * HD
    for hh in range(A_HEADS):
        r = slice(hh * n_q, (hh + 1) * n_q)
        c = slice(hh * HD, (hh + 1) * HD)
        val = None
        for br, o in enumerate((oc_ref, os_ref, ow)):
            zc = slice(br * width + hh * HD, br * width + (hh + 1) * HD)
            gcol = gates[:, br * A_HEADS + hh: br * A_HEADS + hh + 1]
            term = gcol * o[r, :] * z_ref[:, zc]
            val = term if val is None else val + term
        o_ref[:, c] = val.astype(o_ref.dtype)


def nsa_sample_attention(qbd, cmp_kv, cache4d, page_table, tail_sel, win_all, z_s, gl_s,
                         b_cmp, b_sel, b_win, msel, expand, past_len, n_q):
    nb, rows, kvw = qbd.shape
    n_cmp = cmp_kv.shape[1] // NCK
    n_tab = page_table.shape[1]
    n_pages = min(PAGES_PER_STEP, n_tab)
    n_steps = n_tab // n_pages
    bps = n_pages * PAGE // SEL_BLOCK
    n_sel = -(-(past_len + n_q) // SEL_BLOCK)
    k_top = min(SEL_TOPK, n_sel)
    gq = A_KV * n_q

    oc, selmask = pl.pallas_call(
        functools.partial(_nsa_sample_cmp_kernel, past_len=past_len, n_sel=n_sel, k_top=k_top,
                          n_q=n_q, blocks_per_step=bps),
        grid=(nb,),
        in_specs=[
            pl.BlockSpec((None, rows, kvw), lambda b: (b, 0, 0)),
            pl.BlockSpec((None, n_cmp * NCK, HD), lambda b: (b, 0, 0)),
            pl.BlockSpec(b_cmp.shape, lambda b: (0, 0)),
            pl.BlockSpec(msel.shape, lambda b: (0, 0)),
        ],
        out_specs=[
            pl.BlockSpec((None, rows, HD), lambda b: (b, 0, 0)),
            pl.BlockSpec((None, n_steps + 1, gq, bps), lambda b: (b, 0, 0, 0)),
        ],
        out_shape=[
            jax.ShapeDtypeStruct((nb, rows, HD), F32),
            jax.ShapeDtypeStruct((nb, n_steps + 1, gq, bps), F32),
        ],
        compiler_params=_cparams(("arbitrary",)),
        name="nsa_sample_cmp",
    )(qbd, cmp_kv, b_cmp, msel)

    half = 2 * kvw
    last = n_steps - 1

    def page_spec(k):
        return pl.BlockSpec(
            (None, PAGE, NCK, HD),
            lambda b, s, pt: (pt[b, jnp.minimum(s, last) * n_pages + k], 0, 1, 0))

    nkeys = n_pages * PAGE
    grid_spec = pltpu.PrefetchScalarGridSpec(
        num_scalar_prefetch=1,
        grid=(nb, n_steps + 1),
        in_specs=[page_spec(k) for k in range(n_pages)] + [
            pl.BlockSpec((None, rows, kvw), lambda b, s, pt: (b, 0, 0)),
            pl.BlockSpec((None, PAGE, half), lambda b, s, pt: (b, 0, 0)),
            pl.BlockSpec((rows, nkeys), lambda b, s, pt: (0, s)),
            pl.BlockSpec((None, None, gq, bps), lambda b, s, pt: (b, s, 0, 0)),
            pl.BlockSpec(expand.shape, lambda b, s, pt: (0, 0)),
        ],
        out_specs=pl.BlockSpec((None, rows, HD), lambda b, s, pt: (b, 0, 0)),
        scratch_shapes=[
            pltpu.VMEM((rows, 1), F32),
            pltpu.VMEM((rows, 1), F32),
            pltpu.VMEM((rows, kvw), F32),
        ],
    )
    osel = pl.pallas_call(
        functools.partial(_nsa_sample_sel_kernel, n_pages=n_pages, n_steps=n_steps,
                          past_len=past_len, n_q=n_q),
        grid_spec=grid_spec,
        out_shape=jax.ShapeDtypeStruct((nb, rows, HD), F32),
        compiler_params=_cparams(("arbitrary", "arbitrary")),
        name="nsa_sample_sel",
    )(page_table, *([cache4d] * n_pages), qbd, tail_sel, b_sel, selmask, expand)

    nkw = win_all.shape[1]
    zw = z_s.shape[1]
    return pl.pallas_call(
        functools.partial(_nsa_sample_mix_kernel, n_q=n_q),
        grid=(nb,),
        in_specs=[
            pl.BlockSpec((None, rows, kvw), lambda b: (b, 0, 0)),
            pl.BlockSpec((None, nkw, 2 * kvw), lambda b: (b, 0, 0)),
            pl.BlockSpec(b_win.shape, lambda b: (0, 0)),
            pl.BlockSpec((None, rows, HD), lambda b: (b, 0, 0)),
            pl.BlockSpec((None, rows, HD), lambda b: (b, 0, 0)),
            pl.BlockSpec((n_q, zw), lambda b: (b, 0)),
            pl.BlockSpec((n_q, 3 * A_HEADS), lambda b: (b, 0)),
        ],
        out_specs=pl.BlockSpec((n_q, A_HEADS * HD), lambda b: (b, 0)),
        out_shape=jax.ShapeDtypeStruct((nb * n_q, A_HEADS * HD), F32),
        compiler_params=_cparams(("arbitrary",)),
        name="nsa_sample_mix",
    )(qbd, win_all, b_win, oc, osel, z_s, gl_s)


def _retention_kernel(q_ref, k_ref, v_ref, g_ref, cos_ref, sin_ref, dm_ref, qd_ref, kd_ref, cd_ref,
                      s0_ref, o_ref, s_ref, *, dk, dv, hps):
    c = pl.program_id(2)
    half = dk // 2

    @pl.when(c == 0)
    def _():
        s_ref[...] = s0_ref[...]

    cos = cos_ref[...]
    sin = sin_ref[...]

    def rot(x):
        x1, x2 = x[:, :half], x[:, half:]
        return jnp.concatenate([x1 * cos - x2 * sin, x1 * sin + x2 * cos], axis=1)

    for j in range(hps):
        ks, vs = slice(j * dk, (j + 1) * dk), slice(j * dv, (j + 1) * dv)
        q = rot(q_ref[:, ks])
        k = rot(k_ref[:, ks]) * (dk ** -0.5)
        v = v_ref[:, vs].astype(BF16)
        qb = q.astype(BF16)
        att = _dot_nt(qb, k.astype(BF16)) * dm_ref[j]
        state = s_ref[j]
        o = _dot(att.astype(BF16), v) + _dot(qb, state.astype(BF16)) * qd_ref[j]
        s_ref[j] = state * cd_ref[j] + _dot_tn((k * kd_ref[j]).astype(BF16), v)
        o = o * lax.rsqrt(jnp.mean(o * o, axis=-1, keepdims=True) + EPS)
        o_ref[:, vs] = (o * _silu(g_ref[:, vs])).astype(o_ref.dtype)


def retention(proj, state0, q0, nb, t):
    _, nh, dk, dv = state0.shape
    cs = R_CHUNK if t % R_CHUNK == 0 else t
    n = t // cs
    half = dk // 2
    pos = (q0 + jnp.arange(t)).astype(F32)
    inv = jnp.power(ROPE_BASE, -jnp.arange(half, dtype=F32) / half)
    ang = pos[:, None] * inv[None, :]
    cos, sin = jnp.cos(ang), jnp.sin(ang)
    log_g = jnp.log1p(-jnp.exp2(-5.0 - jnp.arange(nh, dtype=F32)))
    ii = jnp.arange(cs, dtype=F32)
    rel = ii[:, None] - ii[None, :]
    causal = rel >= 0
    dmat = jnp.where(causal, jnp.exp(jnp.where(causal, rel, 0.0) * log_g[:, None, None]), 0.0)
    q_dec = jnp.exp((ii + 1.0) * log_g[:, None])[:, :, None]
    k_dec = jnp.exp((cs - 1.0 - ii) * log_g[:, None])[:, :, None]
    c_dec = jnp.exp(cs * log_g)[:, None, None]
    hps = RET_HEADS_PER_STEP
    assert nh % hps == 0
    ng = nh // hps
    vb0 = 2 * nh * dk // (hps * dv)
    return pl.pallas_call(
        functools.partial(_retention_kernel, dk=dk, dv=dv, hps=hps),
        grid=(nb, ng, n),
        in_specs=[
            pl.BlockSpec((cs, hps * dk), lambda b, h, c: (b * n + c, h)),
            pl.BlockSpec((cs, hps * dk), lambda b, h, c: (b * n + c, ng + h)),
            pl.BlockSpec((cs, hps * dv), lambda b, h, c: (b * n + c, vb0 + h)),
            pl.BlockSpec((cs, hps * dv), lambda b, h, c: (b * n + c, vb0 + ng + h)),
            pl.BlockSpec((cs, half), lambda b, h, c: (c, 0)),
            pl.BlockSpec((cs, half), lambda b, h, c: (c, 0)),
            pl.BlockSpec((hps, cs, cs), lambda b, h, c: (h, 0, 0)),
            pl.BlockSpec((hps, cs, 1), lambda b, h, c: (h, 0, 0)),
            pl.BlockSpec((hps, cs, 1), lambda b, h, c: (h, 0, 0)),
            pl.BlockSpec((hps, 1, 1), lambda b, h, c: (h, 0, 0)),
            pl.BlockSpec((None, hps, dk, dv), lambda b, h, c: (b, h, 0, 0)),
        ],
        out_specs=[
            pl.BlockSpec((cs, hps * dv), lambda b, h, c: (b * n + c, h)),
            pl.BlockSpec((None, hps, dk, dv), lambda b, h, c: (b, h, 0, 0)),
        ],
        out_shape=[
            jax.ShapeDtypeStruct((nb * t, nh * dv), BF16 if cs % 16 == 0 else F32),
            jax.ShapeDtypeStruct(state0.shape, F32),
        ],
        compiler_params=_cparams(("arbitrary", "arbitrary", "arbitrary")),
        name="retention",
    )(proj, proj, proj, proj, cos, sin, dmat, q_dec, k_dec, c_dec, state0)


def _t5_bucket_np(dist):
    n = np.maximum(dist, 0)
    exact = N_BUCKETS // 2
    logv = np.log(np.maximum(n, 1).astype(np.float32) / np.float32(exact)) / np.float32(
        math.log(MAX_DISTANCE / exact))
    large = np.minimum(exact + (logv * np.float32(N_BUCKETS - exact)).astype(np.int32), N_BUCKETS - 1)
    return np.where(n < exact, n, large).astype(np.int32)


def _bucket_starts():
    bk = _t5_bucket_np(np.arange(4 * MAX_DISTANCE))
    assert np.all(np.diff(bk) >= 0) and bk[-1] == N_BUCKETS - 1
    return [int(np.argmax(bk >= b)) for b in range(N_BUCKETS)]


_BUCKET_START = _bucket_starts()


def _sel_matrix(n_cmp, n_cmp_pad, n_sel_pad):
    c = np.arange(n_cmp_pad)[:, None]
    j = np.arange(n_sel_pad)[None, :]
    a = (c >= SEL_RATIO * j) & (c < SEL_RATIO * j + SEL_RATIO)
    b = (c >= SEL_RATIO * j - 1) & (c < SEL_RATIO * j + SEL_RATIO - 1)
    m = (a.astype(np.float32) + b.astype(np.float32)) * (c < n_cmp)
    return jnp.asarray(m, BF16)


def _expand_matrix(n_blk_pad, n_keys):
    blk = np.arange(n_blk_pad)[:, None]
    key = np.arange(n_keys)[None, :]
    return jnp.asarray((key // SEL_BLOCK == blk).astype(np.float32), BF16)


def _w1_pairs(w1):
    w = w1.reshape(2, 2, CMP_STRIDE // 2, 2, HD, HD)
    w = jnp.transpose(w, (2, 3, 4, 0, 1, 5))
    return w.reshape(CMP_STRIDE // 2, 2 * HD, 4 * HD).astype(BF16)


def kernel(x_prompt, x_sample, c_prompt, c_sample, cache_nsa_kv, cache_nsa_win, state_ret, page_table,
           norm_g, ada_w, ada_b, rel_bias, a_w_in, a_w_out, a_cmp_pe, a_cmp_w1, a_cmp_w2, a_qk_g,
           r_w_in, r_w_out):
    nbp, t, d = x_prompt.shape
    nbs, ts, _ = x_sample.shape
    n_tab = page_table.shape[1]
    past_len = n_tab * PAGE
    kvw = A_KV * HD
    qw = A_HEADS * HD
    assert t % (2 * LANE) == 0 and t >= WINDOW and ts == 8 and past_len >= WINDOW
    assert cache_nsa_win.shape[2] == WINDOW

    n_c = nbp + nbs
    pad_c = (-n_c) % 8
    c_all = jnp.concatenate([c_prompt, c_sample, jnp.zeros((pad_c, d), F32)], axis=0)
    mod = ada_modulation(c_all, ada_w, ada_b).reshape(ada_w.shape[0], n_c + pad_c, 3, d)

    def mods(layer):
        m = mod[layer]
        return (m[:nbp, 0], m[:nbp, 1], m[:nbp, 2]), (m[nbp:n_c, 0], m[nbp:n_c, 1], m[nbp:n_c, 2])

    (sh_p, sc_p, gt_p), (sh_s, sc_s, gt_s) = mods(0)
    hp = norm_modulate(x_prompt, norm_g[0], sc_p, sh_p)
    hs = norm_modulate(x_sample, norm_g[0], sc_s, sh_s)
    w_in_t = jnp.swapaxes(a_w_in[0], 0, 1)
    n_main = w_in_t.shape[0] - 3 * A_HEADS
    n_qkv = qw + 6 * kvw
    proj_p, proj_s = project(hp, hs, w_in_t, n_cols=n_qkv, w_is_nk=True)
    zt_p, z_s = project_gate_t(hp, hs, w_in_t, n_qkv, n_main - n_qkv)
    w_gate = jnp.pad(w_in_t[n_main:], ((0, LANE - 3 * A_HEADS), (0, 0)))
    gl_p, gl_s = project(hp, hs, w_gate, w_is_nk=True)
    gl_p, gl_s = gl_p[:, :3 * A_HEADS], gl_s[:, :3 * A_HEADS]
    qk_g = a_qk_g[0]

    qn_p, rows_p, win_p = nsa_prep(proj_p, qk_g)
    qn_s, rows_s, win_s = nsa_prep(proj_s, qk_g)

    w_pairs = _w1_pairs(a_cmp_w1[0])
    pe = a_cmp_pe[0]
    w1 = a_cmp_w1[0].reshape(2, CMP_LEN, HD, HD)
    w2 = a_cmp_w2[0]
    cpp = PAGE // CMP_STRIDE

    pages_p = t // PAGE
    ident = jnp.arange(nbp * pages_p, dtype=jnp.int32).reshape(nbp, pages_p)
    ab_p = compress_stage_a(rows_p.reshape(nbp * pages_p, PAGE, 4 * A_KV, HD), ident, w_pairs,
                            min(PAGES_PER_STEP, pages_p))
    zero_tail = jnp.zeros((nbp, cpp, NCK, 2 * HD), F32)
    cmp_p = compress_stage_b(ab_p, zero_tail, pe, w1, w2, qk_g)

    nq = t // Q_BLOCK
    n_ch_p = t // CMP_STRIDE
    tsel, twin, tcmp = prompt_bias_tables(rel_bias, nq, n_ch_p)
    n_cmp_p = t // CMP_STRIDE - 1
    n_sel_p = -(-t // SEL_BLOCK)
    n_sel_pad_p = -(-n_sel_p // 16) * 16
    msel_p = _sel_matrix(n_cmp_p, n_ch_p, n_sel_pad_p).T
    expand_p = _expand_matrix(n_sel_pad_p, t).T
    gates_g = jnp.transpose(gl_p.reshape(nbp * t, 3, A_KV, A_HPG), (2, 1, 3, 0)).reshape(A_KV, 3 * A_HPG, nbp * t)
    mixed_p = nsa_prompt_attention(qn_p, cmp_p, rows_p, win_p, zt_p, gates_g,
                                   tsel, twin, tcmp, msel_p, expand_p, nbp, t)

    n_pool = cache_nsa_kv.shape[1]
    cache4d = cache_nsa_kv[0].reshape(n_pool, PAGE, 4 * A_KV, HD)
    n_pages = min(PAGES_PER_STEP, n_tab)
    ab_s = compress_stage_a(cache4d, page_table, w_pairs, n_pages)
    tail_rows = jnp.pad(rows_s.reshape(nbs, ts, 4 * kvw), ((0, 0), (0, PAGE - ts), (0, 0)))
    ident_s = jnp.arange(nbs, dtype=jnp.int32).reshape(nbs, 1)
    ab_tail = compress_stage_a(tail_rows.reshape(nbs, PAGE, 4 * A_KV, HD), ident_s, w_pairs, 1)
    cmp_s = compress_stage_b(ab_s, ab_tail, pe, w1, w2, qk_g)

    n_cmp_s = cmp_s.shape[1] // NCK
    n_sel_s = -(-(past_len + ts) // SEL_BLOCK)
    n_steps = n_tab // n_pages
    bps = n_pages * PAGE // SEL_BLOCK
    n_sel_pad = (n_steps + 1) * bps
    nkeys = (n_steps + 1) * n_pages * PAGE
    nkw = WINDOW + LANE
    b_cmp, b_sel, b_win = sample_bias_rows(rel_bias, past_len, ts, n_cmp_s, nkeys, nkw)
    msel_s = _sel_matrix(n_cmp_s, n_cmp_s, n_sel_pad)
    expand_s = _expand_matrix(bps, n_pages * PAGE)
    q5 = qn_s.reshape(nbs, ts, A_KV, A_HPG, HD)
    eye = jnp.eye(A_KV, dtype=BF16)
    qbd = jnp.einsum('bqghd,ge->bghqed', q5, eye).reshape(nbs, A_HEADS * ts, kvw)
    win_new = win_s.reshape(nbs, ts, 2 * kvw)
    win_cache = cache_nsa_win[0].reshape(nbs, WINDOW, 2 * kvw)
    win_all = jnp.concatenate([win_cache, win_new], axis=1)
    win_pad = jnp.pad(win_all, ((0, 0), (0, nkw - WINDOW - ts), (0, 0)))
    tail_sel = tail_rows[:, :, 2 * kvw:]
    mixed_s = nsa_sample_attention(qbd, cmp_s, cache4d, page_table, tail_sel, win_pad, z_s, gl_s,
                                   b_cmp, b_sel, b_win, msel_s, expand_s, past_len, ts).astype(BF16)

    xs_flat = x_sample.reshape(nbs * ts, d)
    gate_s_rows = jnp.repeat(gt_s, ts, axis=0)
    x1p, x1s = project_residual(mixed_p, mixed_s, a_w_out[0], x_prompt.reshape(nbp * t, d), gt_p,
                                xs_flat, gate_s_rows, t)

    (sh_p, sc_p, gt_p), (sh_s, sc_s, gt_s) = mods(1)
    hp = norm_modulate(x1p.reshape(nbp, t, d), norm_g[1], sc_p, sh_p)
    hs = norm_modulate(x1s.reshape(nbs, ts, d), norm_g[1], sc_s, sh_s)
    rp, rs = project(hp, hs, r_w_in[0])
    s0 = jnp.zeros((nbp,) + state_ret.shape[2:], F32)
    op, ret_p = retention(rp, s0, 0, nbp, t)
    os_, ret_s = retention(rs, state_ret[0], past_len, nbs, ts)
    os_ = os_.astype(BF16)
    gate_s_rows = jnp.repeat(gt_s, ts, axis=0)
    x2p, x2s = project_residual(op, os_, r_w_out[0], x1p, gt_p, x1s, gate_s_rows, t)

    kv_p = rows_p.reshape(1, nbp, t, 4, A_KV, HD)
    kv_s = rows_s.reshape(1, nbs, ts, 4, A_KV, HD)
    wst_p = win_p.reshape(nbp, t, 2, A_KV, HD)[None, :, t - WINDOW:]
    wst_s = win_all[:, ts:].reshape(1, nbs, WINDOW, 2, A_KV, HD)
    return (x2p.reshape(nbp, t, d), x2s.reshape(nbs, ts, d), kv_p, kv_s, wst_p, wst_s,
            ret_p[None], ret_s[None])
```

```python
import functools
import math

import numpy as np
import jax
import jax.numpy as jnp
from jax import lax
from jax.experimental import pallas as pl
from jax.experimental.pallas import tpu as pltpu

EPS = 1e-6
HD = 128
A_KV = 4
A_HPG = 8
A_HEADS = A_KV * A_HPG
CMP_STRIDE = 16
CMP_LEN = 32
SEL_BLOCK = 64
SEL_RATIO = SEL_BLOCK // CMP_STRIDE
SEL_TOPK = 16
WINDOW = 512
Q_BLOCK = 128
FORCE_SCORE = 1e4
NEG = -1e30
TINY = 1e-30
N_BUCKETS = 32
MAX_DISTANCE = 1024
R_CHUNK = 128
ROPE_BASE = 10000.0
LOG2E = math.log2(math.e)
Q_SCALE = HD ** -0.5 * LOG2E
PAGE = 128
PAGES_PER_STEP = 16
RET_HEADS_PER_STEP = 4
SEL_KEY_TILE = 512

LANE = 128
VMEM_LIMIT = 60 * 1024 * 1024

F32 = jnp.float32
BF16 = jnp.bfloat16


def _cparams(sem):
    return pltpu.CompilerParams(dimension_semantics=sem, vmem_limit_bytes=VMEM_LIMIT)


def _dot(a, b):
    return jnp.dot(a, b, preferred_element_type=F32)


def _dot_nt(a, b):
    return lax.dot_general(a, b, (((1,), (1,)), ((), ())), preferred_element_type=F32)


def _dot_tn(a, b):
    return lax.dot_general(a, b, (((0,), (0,)), ((), ())), preferred_element_type=F32)


def _silu(x):
    return x / (1.0 + jnp.exp(-x))


def _sigmoid(x):
    return 1.0 / (1.0 + jnp.exp(-x))


def _split3_dot(x, m_bf16):
    hi = x.astype(BF16)
    r1 = x - hi.astype(F32)
    mid = r1.astype(BF16)
    lo = (r1 - mid.astype(F32)).astype(BF16)
    return _dot(hi, m_bf16) + _dot(mid, m_bf16) + _dot(lo, m_bf16)


def _ada_kernel(c_ref, w_ref, b_ref, o_ref):
    a = _silu(c_ref[...]).astype(BF16)
    o_ref[...] = _dot(a, w_ref[...].astype(BF16)) + b_ref[...]


def ada_modulation(c_all, ada_w, ada_b):
    depth, d, n = ada_w.shape
    rows = c_all.shape[0]
    tn = 512
    return pl.pallas_call(
        _ada_kernel,
        grid=(depth, n // tn),
        in_specs=[
            pl.BlockSpec((rows, d), lambda l, j: (0, 0)),
            pl.BlockSpec((None, d, tn), lambda l, j: (l, 0, j)),
            pl.BlockSpec((None, 1, tn), lambda l, j: (l, 0, j)),
        ],
        out_specs=pl.BlockSpec((None, rows, tn), lambda l, j: (l, 0, j)),
        out_shape=jax.ShapeDtypeStruct((depth, rows, n), F32),
        compiler_params=_cparams(("arbitrary", "arbitrary")),
        name="ada_modulation",
    )(c_all, ada_w, ada_b.reshape(depth, 1, n))


def _norm_mod_kernel(x_ref, g_ref, sc_ref, sh_ref, o_ref):
    x = x_ref[...]
    y = x * lax.rsqrt(jnp.mean(x * x, axis=-1, keepdims=True) + EPS)
    y = y * g_ref[...]
    o_ref[...] = (y * (1.0 + sc_ref[...]) + sh_ref[...]).astype(o_ref.dtype)


def norm_modulate(x, g, scale, shift):
    b, t, d = x.shape
    tt = min(t, 256)
    out = pl.pallas_call(
        _norm_mod_kernel,
        grid=(b, t // tt),
        in_specs=[
            pl.BlockSpec((None, tt, d), lambda i, j: (i, j, 0)),
            pl.BlockSpec((1, d), lambda i, j: (0, 0)),
            pl.BlockSpec((None, 1, d), lambda i, j: (i, 0, 0)),
            pl.BlockSpec((None, 1, d), lambda i, j: (i, 0, 0)),
        ],
        out_specs=pl.BlockSpec((None, tt, d), lambda i, j: (i, j, 0)),
        out_shape=jax.ShapeDtypeStruct((b, t, d), BF16),
        compiler_params=_cparams(("arbitrary", "arbitrary")),
        name="norm_modulate",
    )(x, g.reshape(1, d), scale.reshape(b, 1, d), shift.reshape(b, 1, d))
    return out.reshape(b * t, d)


def _proj_kernel(xp_ref, xs_ref, w_ref, op_ref, os_ref, wb_ref):
    @pl.when(pl.program_id(1) == 0)
    def _():
        wb_ref[...] = w_ref[...].astype(BF16)
        os_ref[...] = _dot(xs_ref[...], wb_ref[...])

    op_ref[...] = _dot(xp_ref[...], wb_ref[...])


def _proj_nk_kernel(xp_ref, xs_ref, w_ref, op_ref, os_ref, wb_ref):
    @pl.when(pl.program_id(1) == 0)
    def _():
        wb_ref[...] = w_ref[...].astype(BF16)
        os_ref[...] = _dot_nt(xs_ref[...], wb_ref[...])

    op_ref[...] = _dot_nt(xp_ref[...], wb_ref[...])


def _proj_gate_t_kernel(xp_ref, xs_ref, w_ref, opt_ref, os_ref, wb_ref):
    @pl.when(pl.program_id(1) == 0)
    def _():
        wb_ref[...] = w_ref[...].astype(BF16)
        os_ref[...] = _silu(_dot_nt(xs_ref[...], wb_ref[...]))

    opt_ref[...] = _silu(_dot_nt(wb_ref[...], xp_ref[...]))


def project_gate_t(xp, xs, w_nk, row0, n_rows):
    mp, k = xp.shape
    ms = xs.shape[0]
    tm, tn = _proj_tiles(mp, k, n_rows)
    assert row0 % tn == 0
    j0 = row0 // tn
    return pl.pallas_call(
        _proj_gate_t_kernel,
        grid=(n_rows // tn, mp // tm),
        in_specs=[
            pl.BlockSpec((tm, k), lambda j, i: (i, 0)),
            pl.BlockSpec((ms, k), lambda j, i: (0, 0)),
            pl.BlockSpec((tn, k), lambda j, i: (j0 + j, 0)),
        ],
        out_specs=[
            pl.BlockSpec((tn, tm), lambda j, i: (j, i)),
            pl.BlockSpec((ms, tn), lambda j, i: (0, j)),
        ],
        out_shape=[jax.ShapeDtypeStruct((n_rows, mp), F32), jax.ShapeDtypeStruct((ms, n_rows), F32)],
        scratch_shapes=[pltpu.VMEM((tn, k), BF16)],
        compiler_params=_cparams(("arbitrary", "arbitrary")),
        name="project_gate_t",
    )(xp, xs, w_nk)


def _proj_res_kernel(xp_ref, xs_ref, w_ref, rp_ref, gp_ref, rs_ref, gs_ref, op_ref, os_ref, wb_ref):
    @pl.when(pl.program_id(1) == 0)
    def _():
        wb_ref[...] = w_ref[...].astype(BF16)
        os_ref[...] = rs_ref[...] + gs_ref[...] * _dot(xs_ref[...], wb_ref[...])

    op_ref[...] = rp_ref[...] + gp_ref[...] * _dot(xp_ref[...], wb_ref[...])


def _proj_tiles(mp, k, n, residual=False):
    if k <= 4096:
        tm, tn = (1024, 512) if residual else (512, 1024)
    else:
        tm, tn = 256, 512
    tm = min(tm, mp)
    while mp % tm:
        tm //= 2
    tn = min(tn, n)
    while n % tn:
        tn //= 2
    return tm, tn


def project(xp, xs, w, n_cols=None, w_is_nk=False):
    mp, k = xp.shape
    ms = xs.shape[0]
    n = w.shape[0 if w_is_nk else 1] if n_cols is None else n_cols
    tm, tn = _proj_tiles(mp, k, n)
    w_spec = pl.BlockSpec((tn, k), lambda j, i: (j, 0)) if w_is_nk else pl.BlockSpec((k, tn), lambda j, i: (0, j))
    return pl.pallas_call(
        _proj_nk_kernel if w_is_nk else _proj_kernel,
        grid=(n // tn, mp // tm),
        in_specs=[
            pl.BlockSpec((tm, k), lambda j, i: (i, 0)),
            pl.BlockSpec((ms, k), lambda j, i: (0, 0)),
            w_spec,
        ],
        out_specs=[
            pl.BlockSpec((tm, tn), lambda j, i: (i, j)),
            pl.BlockSpec((ms, tn), lambda j, i: (0, j)),
        ],
        out_shape=[jax.ShapeDtypeStruct((mp, n), F32), jax.ShapeDtypeStruct((ms, n), F32)],
        scratch_shapes=[pltpu.VMEM((tn, k) if w_is_nk else (k, tn), BF16)],
        compiler_params=_cparams(("arbitrary", "arbitrary")),
        name="project",
    )(xp, xs, w)


def project_residual(xp, xs, w, res_p, gate_p, res_s, gate_s, rows_per_batch):
    mp, k = xp.shape
    ms = xs.shape[0]
    n = w.shape[1]
    tm, tn = _proj_tiles(mp, k, n, residual=True)
    tm = min(tm, rows_per_batch)
    tpb = rows_per_batch // tm
    nb = gate_p.shape[0]
    return pl.pallas_call(
        _proj_res_kernel,
        grid=(n // tn, mp // tm),
        in_specs=[
            pl.BlockSpec((tm, k), lambda j, i: (i, 0)),
            pl.BlockSpec((ms, k), lambda j, i: (0, 0)),
            pl.BlockSpec((k, tn), lambda j, i: (0, j)),
            pl.BlockSpec((tm, tn), lambda j, i: (i, j)),
            pl.BlockSpec((None, 1, tn), lambda j, i: (i // tpb, 0, j)),
            pl.BlockSpec((ms, tn), lambda j, i: (0, j)),
            pl.BlockSpec((ms, tn), lambda j, i: (0, j)),
        ],
        out_specs=[
            pl.BlockSpec((tm, tn), lambda j, i: (i, j)),
            pl.BlockSpec((ms, tn), lambda j, i: (0, j)),
        ],
        out_shape=[jax.ShapeDtypeStruct((mp, n), F32), jax.ShapeDtypeStruct((ms, n), F32)],
        scratch_shapes=[pltpu.VMEM((k, tn), BF16)],
        compiler_params=_cparams(("arbitrary", "arbitrary")),
        name="project_residual",
    )(xp, xs, w, res_p, gate_p.reshape(nb, 1, n), res_s, gate_s)


def _nsa_prep_kernel(q_ref, a_ref, b_ref, c_ref, g_ref, qn_ref, rows_ref, win_ref):
    def hnorm(x, g):
        return x * lax.rsqrt(jnp.mean(x * x, axis=-1, keepdims=True) + EPS) * g

    g = g_ref[...]
    for j in range(A_HEADS):
        sl = slice(j * HD, (j + 1) * HD)
        qn_ref[:, sl] = (hnorm(q_ref[:, sl], g[0:1]) * Q_SCALE).astype(qn_ref.dtype)
    kvw = A_KV * HD
    rows_ref[:, : 2 * kvw] = a_ref[...]
    rows_ref[:, 3 * kvw:] = b_ref[:, kvw:]
    win_ref[:, kvw:] = c_ref[:, kvw:]
    for j in range(A_KV):
        sl = slice(j * HD, (j + 1) * HD)
        rows_ref[:, 2 * kvw + j * HD: 2 * kvw + (j + 1) * HD] = hnorm(b_ref[:, sl], g[2:3])
        win_ref[:, sl] = hnorm(c_ref[:, sl], g[3:4])


def nsa_prep(proj, qk_g):
    m = proj.shape[0]
    tm = min(m, 256)
    qw = A_HEADS * HD
    kv2 = 2 * A_KV * HD
    base = qw // kv2
    return pl.pallas_call(
        _nsa_prep_kernel,
        grid=(m // tm,),
        in_specs=[
            pl.BlockSpec((tm, qw), lambda i: (i, 0)),
            pl.BlockSpec((tm, kv2), lambda i: (i, base)),
            pl.BlockSpec((tm, kv2), lambda i: (i, base + 1)),
            pl.BlockSpec((tm, kv2), lambda i: (i, base + 2)),
            pl.BlockSpec((4, HD), lambda i: (0, 0)),
        ],
        out_specs=[
            pl.BlockSpec((tm, qw), lambda i: (i, 0)),
            pl.BlockSpec((tm, 2 * kv2), lambda i: (i, 0)),
            pl.BlockSpec((tm, kv2), lambda i: (i, 0)),
        ],
        out_shape=[
            jax.ShapeDtypeStruct((m, qw), BF16),
            jax.ShapeDtypeStruct((m, 2 * kv2), F32),
            jax.ShapeDtypeStruct((m, kv2), F32),
        ],
        compiler_params=_cparams(("arbitrary",)),
        name="nsa_prep",
    )(proj, proj, proj, proj, qk_g)


NCK = 2 * A_KV


def _cmp_a_kernel(pt_ref, *refs, n_pages):
    del pt_ref
    pages = refs[:n_pages]
    w_ref = refs[n_pages]
    o_ref = refs[n_pages + 1]
    cpp = PAGE // CMP_STRIDE
    n_ch = n_pages * cpp

    def halves(tiles, lo):
        return [jnp.concatenate([tiles[a][lo:lo + A_KV], tiles[a + 1][lo:lo + A_KV]], axis=0)
                for a in range(0, len(tiles), 2)]

    acc = [None, None]
    for tp in range(CMP_STRIDE // 2):
        pieces = []
        for pg in pages:
            for ch in range(cpp):
                t0 = ch * CMP_STRIDE + 2 * tp
                pieces.append(jnp.concatenate([pg[t0], pg[t0 + 1]], axis=1))
        for kind in range(2):
            lhs = jnp.concatenate(halves(pieces, kind * A_KV), axis=0).astype(BF16)
            d = _dot(lhs, w_ref[tp, :, kind * 2 * HD:(kind + 1) * 2 * HD])
            acc[kind] = d if acc[kind] is None else acc[kind] + d
    for a in range(0, n_ch, 2):
        r = slice(a * A_KV, (a + 2) * A_KV)
        kk, vv = acc[0][r], acc[1][r]
        o_ref[a] = jnp.concatenate([kk[:A_KV], vv[:A_KV]], axis=0)
        o_ref[a + 1] = jnp.concatenate([kk[A_KV:], vv[A_KV:]], axis=0)


def compress_stage_a(store, page_table, w_pairs, n_pages):
    nb, n_tab = page_table.shape
    steps = n_tab // n_pages
    cpp = PAGE // CMP_STRIDE

    def page_spec(k):
        return pl.BlockSpec((None, PAGE, NCK, HD), lambda b, s, pt: (pt[b, s * n_pages + k], 0, 0, 0))

    grid_spec = pltpu.PrefetchScalarGridSpec(
        num_scalar_prefetch=1,
        grid=(nb, steps),
        in_specs=[page_spec(k) for k in range(n_pages)]
        + [pl.BlockSpec(w_pairs.shape, lambda b, s, pt: (0, 0, 0))],
        out_specs=pl.BlockSpec((None, n_pages * cpp, NCK, 2 * HD), lambda b, s, pt: (b, s, 0, 0)),
    )
    return pl.pallas_call(
        functools.partial(_cmp_a_kernel, n_pages=n_pages),
        grid_spec=grid_spec,
        out_shape=jax.ShapeDtypeStruct((nb, n_tab * cpp, NCK, 2 * HD), F32),
        compiler_params=_cparams(("arbitrary", "arbitrary")),
        name="compress_stage_a",
    )(page_table, *([store] * n_pages), w_pairs)


def _cmp_b_kernel(ab_ref, nx_ref, tail_ref, pe_ref, w1_ref, w2_ref, g_ref, o_ref, const_ref):
    j = pl.program_id(1)
    nch = ab_ref.shape[0]
    rows = nch * NCK

    @pl.when((pl.program_id(0) == 0) & (j == 0))
    def _():
        for kind in range(2):
            const = jnp.zeros((8, HD), F32)
            for t in range(CMP_LEN):
                row = jnp.broadcast_to(pe_ref[kind, t:t + 1, :], (8, HD)).astype(BF16)
                const = const + _dot(row, w1_ref[kind, t].astype(BF16))
            const_ref[kind * A_KV:(kind + 1) * A_KV, :] = const[:A_KV]

    ab = ab_ref[...].reshape(rows, 2 * HD)
    nxt = pltpu.roll(ab[:, HD:], rows - NCK, 0).reshape(nch, NCK, HD)
    first_next = jnp.where(j == pl.num_programs(1) - 1, tail_ref[0, :, HD:], nx_ref[0, :, HD:])
    cid = lax.broadcasted_iota(jnp.int32, (nch, NCK, HD), 0)
    nxt = jnp.where(cid == nch - 1, first_next[None], nxt)
    pre = ab[:, :HD].reshape(nch, NCK, HD) + nxt + const_ref[...][None]
    act = _silu(pre).reshape(rows, HD).astype(BF16)
    w2 = jnp.concatenate([w2_ref[0], w2_ref[1]], axis=1).astype(BF16)
    y2 = _dot(act, w2)
    is_k = (lax.broadcasted_iota(jnp.int32, (rows, HD), 0) & (NCK - 1)) < A_KV
    y = jnp.where(is_k, y2[:, :HD], y2[:, HD:])
    yn = y * lax.rsqrt(jnp.mean(y * y, axis=-1, keepdims=True) + EPS) * g_ref[1:2]
    o_ref[...] = jnp.where(is_k, yn, y).reshape(nch, NCK, HD)


def compress_stage_b(ab, tail, pe, w1, w2, qk_g):
    nb, n_ch = ab.shape[:2]
    tc = min(256, n_ch)
    assert n_ch % tc == 0
    nt = n_ch // tc
    out = pl.pallas_call(
        _cmp_b_kernel,
        grid=(nb, nt),
        in_specs=[
            pl.BlockSpec((None, tc, NCK, 2 * HD), lambda b, j: (b, j, 0, 0)),
            pl.BlockSpec((None, 1, NCK, 2 * HD), lambda b, j: (b, jnp.minimum((j + 1) * tc, n_ch - 1), 0, 0)),
            pl.BlockSpec((None, 1, NCK, 2 * HD), lambda b, j: (b, 0, 0, 0)),
            pl.BlockSpec(pe.shape, lambda b, j: (0, 0, 0)),
            pl.BlockSpec(w1.shape, lambda b, j: (0, 0, 0, 0)),
            pl.BlockSpec(w2.shape, lambda b, j: (0, 0, 0)),
            pl.BlockSpec((4, HD), lambda b, j: (0, 0)),
        ],
        out_specs=pl.BlockSpec((None, tc, NCK, HD), lambda b, j: (b, j, 0, 0)),
        out_shape=jax.ShapeDtypeStruct((nb, n_ch, NCK, HD), F32),
        scratch_shapes=[pltpu.VMEM((NCK, HD), F32)],
        compiler_params=_cparams(("arbitrary", "arbitrary")),
        name="compress_stage_b",
    )(ab, ab, tail, pe, w1, w2, qk_g)
    return out.reshape(nb, n_ch * NCK, HD)


def _cmp_rows(cmp_ref, slot, n):
    return cmp_ref[pl.ds(slot, n, stride=NCK), :]


def _select_topk(score, blk, k_top):
    sel = jnp.zeros(score.shape, F32)
    for _ in range(k_top):
        m = jnp.max(score, axis=-1, keepdims=True)
        cand = jnp.where(score == m, blk, jnp.int32(1 << 30))
        first = jnp.min(cand, axis=-1, keepdims=True)
        pick = blk == first
        sel = jnp.where(pick, 1.0, sel)
        score = jnp.where(pick, -jnp.inf, score)
    return sel


def _select_topk_t(score, blk, k_top, n_blk):
    rank = jnp.zeros(score.shape, F32)
    for i in range(n_blk):
        row = score[i:i + 1, :]
        tie = jnp.where(blk > i, 1.0, 0.0)
        rank = rank + jnp.where(row > score, 1.0, jnp.where(row == score, tie, 0.0))
    return jnp.where(rank < k_top, 1.0, 0.0)


def _bias_select(dist, dmin, dmax, value_of):
    lo = int(_t5_bucket_np(np.array([max(dmin, 0)]))[0])
    hi = int(_t5_bucket_np(np.array([max(dmax, 0)]))[0])
    out = jnp.full(dist.shape, value_of(lo), F32)
    for b in range(lo + 1, hi + 1):
        out = jnp.where(dist >= _BUCKET_START[b], value_of(b), out)
    return out * LOG2E


def _bias_tables_kernel(rb_ref, tsel_ref, twin_ref, tcmp_ref):
    head = pl.program_id(0)
    value_of = lambda b: rb_ref[b, head]
    n_off, nq, nc = tsel_ref.shape[0], tcmp_ref.shape[0], tcmp_ref.shape[1]
    kl = lax.broadcasted_iota(jnp.int32, (LANE, Q_BLOCK), 0)
    ql = lax.broadcasted_iota(jnp.int32, (LANE, Q_BLOCK), 1)
    for o in range(n_off):
        base = (o - 1) * Q_BLOCK
        dist = base + ql - kl
        tile = _bias_select(dist, base - (LANE - 1), base + Q_BLOCK - 1, value_of)
        tsel_ref[o] = tile
        if 1 <= o <= twin_ref.shape[0]:
            twin_ref[o - 1] = tile + jnp.where((dist >= 0) & (dist < WINDOW), 0.0, NEG)
    cend = lax.broadcasted_iota(jnp.int32, (nc, Q_BLOCK), 0) * CMP_STRIDE + (CMP_LEN - 1)
    qc = lax.broadcasted_iota(jnp.int32, (nc, Q_BLOCK), 1)
    for i in range(nq):
        base = i * Q_BLOCK
        tcmp_ref[i] = _bias_select(base + qc - cend, base - ((nc - 1) * CMP_STRIDE + CMP_LEN - 1),
                                   base + Q_BLOCK - 1, value_of)


def prompt_bias_tables(rel_bias, nq, n_cmp):
    n_win = min(WINDOW // Q_BLOCK + 1, nq)
    return pl.pallas_call(
        _bias_tables_kernel,
        grid=(A_HEADS,),
        in_specs=[pl.BlockSpec(memory_space=pltpu.SMEM)],
        out_specs=[
            pl.BlockSpec((None, nq + 1, LANE, Q_BLOCK), lambda h: (h // A_HPG, 0, 0, h % A_HPG)),
            pl.BlockSpec((None, n_win, LANE, Q_BLOCK), lambda h: (h // A_HPG, 0, 0, h % A_HPG)),
            pl.BlockSpec((None, nq, n_cmp, Q_BLOCK), lambda h: (h // A_HPG, 0, 0, h % A_HPG)),
        ],
        out_shape=[
            jax.ShapeDtypeStruct((A_KV, nq + 1, LANE, A_HPG * Q_BLOCK), F32),
            jax.ShapeDtypeStruct((A_KV, n_win, LANE, A_HPG * Q_BLOCK), F32),
            jax.ShapeDtypeStruct((A_KV, nq, n_cmp, A_HPG * Q_BLOCK), F32),
        ],
        compiler_params=_cparams(("arbitrary",)),
        name="prompt_bias_tables",
    )(rel_bias)


def _bias_rows_kernel(rb_ref, cmp_ref, sel_ref, win_ref, *, past_len, n_q):
    head = pl.program_id(0)
    value_of = lambda b: rb_ref[b, head]

    def fill(ref, key_pos, kmin, kmax, q0):
        q = lax.broadcasted_iota(jnp.int32, ref.shape, 0)
        k = lax.broadcasted_iota(jnp.int32, ref.shape, 1)
        ref[...] = _bias_select(q0 + q - key_pos(k), q0 - kmax, q0 + n_q - 1 - kmin, value_of)

    nc, nk, nw = cmp_ref.shape[1], sel_ref.shape[1], win_ref.shape[1]
    fill(cmp_ref, lambda k: k * CMP_STRIDE + (CMP_LEN - 1), CMP_LEN - 1, (nc - 1) * CMP_STRIDE + CMP_LEN - 1, past_len)
    fill(sel_ref, lambda k: k, 0, nk - 1, past_len)
    fill(win_ref, lambda k: k, 0, nw - 1, WINDOW)


def sample_bias_rows(rel_bias, past_len, n_q, n_cmp, n_keys, n_win):
    shapes = [(A_HEADS * n_q, n) for n in (n_cmp, n_keys, n_win)]
    return pl.pallas_call(
        functools.partial(_bias_rows_kernel, past_len=past_len, n_q=n_q),
        grid=(A_HEADS,),
        in_specs=[pl.BlockSpec(memory_space=pltpu.SMEM)],
        out_specs=[pl.BlockSpec((n_q, s[1]), lambda h: (h, 0)) for s in shapes],
        out_shape=[jax.ShapeDtypeStruct(s, F32) for s in shapes],
        compiler_params=_cparams(("arbitrary",)),
        name="sample_bias_rows",
    )(rel_bias)


def _nsa_prompt_kernel(q_ref, cmp_ref, ks_ref, vs_ref, kw_ref, vw_ref, zc_ref, zs_ref, zw_ref,
                       gt_ref, tsel_ref, twin_ref, tcmp_ref, msel_ref, exp_ref, o_ref,
                       qa_ref, acc_ref, m_ref, l_ref, madd_ref, mix_ref, *, n_sel_pad, n_sel, k_top):
    i = pl.program_id(2)
    qb = Q_BLOCK
    n_kt, kt = madd_ref.shape[:2]
    tpk = kt // LANE
    nc = cmp_ref.shape[0] // NCK
    gates = _sigmoid(gt_ref[...])
    hs = [slice(h * HD, (h + 1) * HD) for h in range(A_HPG)]
    for h in range(A_HPG):
        qa_ref[hs[h], :] = q_ref[:, hs[h]]
    qa = qa_ref[...]

    def heads(x):
        return jnp.concatenate([x] * A_HPG, axis=1)

    def gate_row(branch):
        return jnp.concatenate([gates[branch * A_HPG + h: branch * A_HPG + h + 1, :] for h in range(A_HPG)], axis=1)

    def emit(branch, o_t, first):
        z_ref = (zc_ref, zs_ref, zw_ref)[branch]
        for h in range(A_HPG):
            val = o_t[:, hs[h]] * z_ref[hs[h], :]
            if first:
                mix_ref[hs[h], :] = val
            else:
                mix_ref[hs[h], :] = mix_ref[hs[h], :] + val

    grp = pl.program_id(0)
    kc = _cmp_rows(cmp_ref, grp, nc).astype(BF16)
    vc = _cmp_rows(cmp_ref, A_KV + grp, nc).astype(BF16)
    cend = lax.broadcasted_iota(jnp.int32, (nc, qb), 0) * CMP_STRIDE + (CMP_LEN - 1)
    qpos_c = i * qb + lax.broadcasted_iota(jnp.int32, (nc, qb), 1)
    cvalid = heads(cend <= qpos_c)
    s = jnp.where(cvalid, _dot_nt(kc, qa) + tcmp_ref[...], NEG)
    mx = jnp.max(s, axis=0, keepdims=True)
    e = jnp.where(cvalid, jnp.exp2(s - mx), 0.0)
    p = e * (1.0 / jnp.maximum(jnp.sum(e, axis=0, keepdims=True), TINY))
    imp = p[:, hs[0]]
    for h in range(1, A_HPG):
        imp = imp + p[:, hs[h]]
    emit(0, _dot_tn(vc, p.astype(BF16)) * gate_row(0), True)

    hi = imp.astype(BF16)
    r1 = imp - hi.astype(F32)
    mid = r1.astype(BF16)
    lo = (r1 - mid.astype(F32)).astype(BF16)
    msel = msel_ref[...]
    imp_sel = _dot(msel, hi) + _dot(msel, mid) + _dot(msel, lo)
    blk = lax.broadcasted_iota(jnp.int32, (n_sel_pad, qb), 0)
    qpos = i * qb + lax.broadcasted_iota(jnp.int32, (n_sel_pad, qb), 1)
    cur = jnp.right_shift(qpos, int(math.log2(SEL_BLOCK)))
    forced = (blk == 0) | (blk == cur) | (blk == cur - 1)
    valid = blk * SEL_BLOCK <= qpos
    score = jnp.where(valid, jnp.where(forced, FORCE_SCORE, imp_sel), -1.0)
    score = jnp.where(blk < n_sel, score, -jnp.inf)
    sel = _select_topk_t(score, blk, k_top, n_sel).astype(BF16)
    kpos2 = lax.broadcasted_iota(jnp.int32, (kt, qb), 0)
    qpos2 = i * qb + lax.broadcasted_iota(jnp.int32, (kt, qb), 1)
    for jj in range(n_kt):
        hit = _dot(exp_ref[jj * kt:(jj + 1) * kt, :], sel)
        ok = (hit > 0.5) & ((jj * kt + kpos2) <= qpos2)
        madd_ref[jj] = jnp.where(ok, 0.0, NEG)

    def reset():
        m_ref[...] = jnp.full(m_ref.shape, NEG, F32)
        l_ref[...] = jnp.zeros(l_ref.shape, F32)
        acc_ref[...] = jnp.zeros(acc_ref.shape, F32)

    def tile_update(k, v, bias, madd):
        s = _dot_nt(k, qa) + bias
        if madd is not None:
            s = s + heads(madd)
        m_old = m_ref[...]
        m_new = jnp.maximum(m_old, jnp.max(s, axis=0, keepdims=True))
        alpha = jnp.exp2(m_old - m_new)
        p = jnp.exp2(s - m_new)
        l_ref[...] = alpha * l_ref[...] + jnp.sum(p, axis=0, keepdims=True)
        m_ref[...] = m_new
        acc_ref[...] = acc_ref[...] * alpha + _dot_tn(v, p.astype(BF16))

    def finish(branch):
        emit(branch, acc_ref[...] * (gate_row(branch) / l_ref[...]), False)

    reset()

    def sel_body(jj, carry):
        k0 = pl.multiple_of(jj * kt, kt)
        k = ks_ref[pl.ds(k0, kt), :].astype(BF16)
        v = vs_ref[pl.ds(k0, kt), :].astype(BF16)
        off = i - tpk * jj
        bias = jnp.concatenate([tsel_ref[jnp.maximum(off + 1 - r, 0)] for r in range(tpk)], axis=0)
        tile_update(k, v, bias, madd_ref[jj])
        return carry

    lax.fori_loop(0, i // tpk + 1, sel_body, 0)
    finish(1)

    reset()

    def win_body(t, carry):
        k0 = pl.multiple_of((i - t) * qb, qb)
        k = kw_ref[pl.ds(k0, qb), :].astype(BF16)
        v = vw_ref[pl.ds(k0, qb), :].astype(BF16)
        tile_update(k, v, twin_ref[t], None)
        return carry

    lax.fori_loop(0, jnp.minimum(i, WINDOW // qb) + 1, win_body, 0)
    finish(2)
    for h in range(A_HPG):
        o_ref[:, hs[h]] = mix_ref[hs[h], :].T.astype(o_ref.dtype)


def nsa_prompt_attention(qn, cmp_kv, rows, win, z_t, gates_g, tsel, twin, tcmp, msel, expand, nb, t):
    nq = t // Q_BLOCK
    qw = A_HPG * HD
    n_sel = -(-t // SEL_BLOCK)
    k_top = min(SEL_TOPK, n_sel)
    kt = SEL_KEY_TILE if t % SEL_KEY_TILE == 0 else 2 * LANE
    n_kt = t // kt
    n_cmp = cmp_kv.shape[1] // NCK
    n_sel_pad = msel.shape[0]
    kernel = functools.partial(_nsa_prompt_kernel, n_sel_pad=n_sel_pad, n_sel=n_sel, k_top=k_top)
    return pl.pallas_call(
        kernel,
        grid=(A_KV, nb, nq),
        in_specs=[
            pl.BlockSpec((Q_BLOCK, qw), lambda g, b, i: (b * nq + i, g)),
            pl.BlockSpec((None, n_cmp * NCK, HD), lambda g, b, i: (b, 0, 0)),
            pl.BlockSpec((t, HD), lambda g, b, i: (b, 2 * A_KV + g)),
            pl.BlockSpec((t, HD), lambda g, b, i: (b, 3 * A_KV + g)),
            pl.BlockSpec((t, HD), lambda g, b, i: (b, g)),
            pl.BlockSpec((t, HD), lambda g, b, i: (b, A_KV + g)),
            pl.BlockSpec((qw, Q_BLOCK), lambda g, b, i: (g, b * nq + i)),
            pl.BlockSpec((qw, Q_BLOCK), lambda g, b, i: (A_KV + g, b * nq + i)),
            pl.BlockSpec((qw, Q_BLOCK), lambda g, b, i: (2 * A_KV + g, b * nq + i)),
            pl.BlockSpec((None, 3 * A_HPG, Q_BLOCK), lambda g, b, i: (g, 0, b * nq + i)),
            pl.BlockSpec((None, nq + 1, LANE, qw), lambda g, b, i: (g, 0, 0, 0)),
            pl.BlockSpec((None, twin.shape[1], LANE, qw), lambda g, b, i: (g, 0, 0, 0)),
            pl.BlockSpec((None, None, n_cmp, qw), lambda g, b, i: (g, i, 0, 0)),
            pl.BlockSpec((n_sel_pad, n_cmp), lambda g, b, i: (0, 0)),
            pl.BlockSpec((t, n_sel_pad), lambda g, b, i: (0, 0)),
        ],
        out_specs=pl.BlockSpec((Q_BLOCK, qw), lambda g, b, i: (b * nq + i, g)),
        out_shape=jax.ShapeDtypeStruct((nb * t, A_HEADS * HD), BF16),
        scratch_shapes=[
            pltpu.VMEM((qw, HD), BF16),
            pltpu.VMEM((HD, qw), F32),
            pltpu.VMEM((1, qw), F32),
            pltpu.VMEM((1, qw), F32),
            pltpu.VMEM((n_kt, kt, Q_BLOCK), F32),
            pltpu.VMEM((qw, Q_BLOCK), F32),
        ],
        compiler_params=_cparams(("arbitrary", "arbitrary", "arbitrary")),
        name="nsa_prompt_attention",
    )(qn, cmp_kv, rows, rows, win, win, z_t, z_t, z_t, gates_g, tsel, twin, tcmp, msel, expand)


def _diag_blocks(o_full, rows_per_group):
    return jnp.concatenate(
        [o_full[g * rows_per_group:(g + 1) * rows_per_group, g * HD:(g + 1) * HD] for g in range(A_KV)], axis=0)


def _nsa_sample_cmp_kernel(q_ref, cmp_ref, bias_ref, msel_ref, oc_ref, sel_ref,
                           *, past_len, n_sel, k_top, n_q, blocks_per_step):
    n_cmp = cmp_ref.shape[0] // NCK
    kc = jnp.concatenate([_cmp_rows(cmp_ref, g, n_cmp) for g in range(A_KV)], axis=1).astype(BF16)
    vc = jnp.concatenate([_cmp_rows(cmp_ref, A_KV + g, n_cmp) for g in range(A_KV)], axis=1).astype(BF16)
    rows = q_ref.shape[0]
    rpg = A_HPG * n_q
    q = q_ref[...]
    s = _dot_nt(q, kc) + bias_ref[...]
    rowi = lax.broadcasted_iota(jnp.int32, (rows, n_cmp), 0)
    ci = lax.broadcasted_iota(jnp.int32, (rows, n_cmp), 1)
    qpos = past_len + (rowi & (n_q - 1))
    valid = (ci * CMP_STRIDE + (CMP_LEN - 1)) <= qpos
    s = jnp.where(valid, s, NEG)
    mx = jnp.max(s, axis=-1, keepdims=True)
    e = jnp.where(valid, jnp.exp2(s - mx), 0.0)
    p = e / jnp.maximum(jnp.sum(e, axis=-1, keepdims=True), TINY)
    oc_ref[...] = _diag_blocks(_dot(p.astype(BF16), vc), rpg)
    imps = []
    for g in range(A_KV):
        acc = p[g * rpg: g * rpg + n_q]
        for h in range(1, A_HPG):
            acc = acc + p[g * rpg + h * n_q: g * rpg + (h + 1) * n_q]
        imps.append(acc)
    imp = jnp.concatenate(imps, axis=0)
    imp_sel = _split3_dot(imp, msel_ref[...])
    shp = imp_sel.shape
    blk = lax.broadcasted_iota(jnp.int32, shp, 1)
    qp = past_len + (lax.broadcasted_iota(jnp.int32, shp, 0) & (n_q - 1))
    cur = jnp.right_shift(qp, int(math.log2(SEL_BLOCK)))
    forced = (blk == 0) | (blk == cur) | (blk == cur - 1)
    ok = blk * SEL_BLOCK <= qp
    score = jnp.where(ok, jnp.where(forced, FORCE_SCORE, imp_sel), -1.0)
    score = jnp.where(blk < n_sel, score, -jnp.inf)
    sel = _select_topk(score, blk, k_top)
    for st in range(sel_ref.shape[0]):
        sel_ref[st] = sel[:, st * blocks_per_step:(st + 1) * blocks_per_step]


def _nsa_sample_sel_kernel(pt_ref, *refs, n_pages, n_steps, past_len, n_q):
    del pt_ref
    pages = refs[:n_pages]
    q_ref, tail_ref, bias_ref, sel_ref, exp_ref, o_ref, m_ref, l_ref, acc_ref = refs[n_pages:]
    s_id = pl.program_id(1)
    kvw = A_KV * HD
    rows = q_ref.shape[0]
    rpg = A_HPG * n_q

    @pl.when(s_id == 0)
    def _():
        m_ref[...] = jnp.full(m_ref.shape, NEG, F32)
        l_ref[...] = jnp.zeros(l_ref.shape, F32)
        acc_ref[...] = jnp.zeros(acc_ref.shape, F32)

    def update(k, v, nk, key0):
        hit = _dot(sel_ref[...].astype(BF16), exp_ref[:, :nk])
        kpos = key0 + lax.broadcasted_iota(jnp.int32, hit.shape, 1)
        qpos = past_len + (lax.broadcasted_iota(jnp.int32, hit.shape, 0) & (n_q - 1))
        madd = jnp.where((hit > 0.5) & (kpos <= qpos), 0.0, NEG)
        madd = jnp.concatenate(
            [madd[g * n_q:(g + 1) * n_q] for g in range(A_KV) for _ in range(A_HPG)], axis=0)
        s = _dot_nt(q_ref[...], k) + bias_ref[:, :nk] + madd
        m_old = m_ref[...]
        m_new = jnp.maximum(m_old, jnp.max(s, axis=-1, keepdims=True))
        alpha = jnp.exp2(m_old - m_new)
        p = jnp.exp2(s - m_new)
        l_ref[...] = alpha * l_ref[...] + jnp.sum(p, axis=-1, keepdims=True)
        acc_ref[...] = alpha * acc_ref[...] + _dot(p.astype(BF16), v)
        m_ref[...] = m_new

    @pl.when(s_id < n_steps)
    def _():
        by_slot = [jnp.swapaxes(pg[...], 0, 1) for pg in pages]

        def gather(c0):
            cols = [jnp.concatenate([x[c0 + g] for x in by_slot], axis=0) for g in range(A_KV)]
            return jnp.concatenate(cols, axis=1).astype(BF16)

        update(gather(0), gather(A_KV), n_pages * PAGE, s_id * (n_pages * PAGE))

    @pl.when(s_id == n_steps)
    def _():
        update(tail_ref[:, :kvw].astype(BF16), tail_ref[:, kvw:].astype(BF16), PAGE, past_len)
        o_ref[...] = _diag_blocks(acc_ref[...] / l_ref[...], rpg)


def _nsa_sample_mix_kernel(q_ref, win_ref, bias_ref, oc_ref, os_ref, z_ref, gt_ref, o_ref, *, n_q):
    kvw = A_KV * HD
    rows = q_ref.shape[0]
    rpg = A_HPG * n_q
    nk = win_ref.shape[0]
    s = _dot_nt(q_ref[...], win_ref[:, :kvw].astype(BF16)) + bias_ref[...]
    qi = lax.broadcasted_iota(jnp.int32, (rows, nk), 0) & (n_q - 1)
    ki = lax.broadcasted_iota(jnp.int32, (rows, nk), 1)
    dist = qi + WINDOW - ki
    s = s + jnp.where((dist >= 0) & (dist < WINDOW), 0.0, NEG)
    mx = jnp.max(s, axis=-1, keepdims=True)
    e = jnp.exp2(s - mx)
    p = e / jnp.sum(e, axis=-1, keepdims=True)
    ow = _diag_blocks(_dot(p.astype(BF16), win_ref[:, kvw:].astype(BF16)), rpg)
    gates = _sigmoid(gt_ref[...])
    width = A_HEADS * HD
    for hh in range(A_HEADS):
        r = slice(hh * n_q, (hh + 1) * n_q)
        c = slice(hh * HD, (hh + 1) * HD)
        val = None
        for br, o in enumerate((oc_ref, os_ref, ow)):
            zc = slice(br * width + hh * HD, br * width + (hh + 1) * HD)
            gcol = gates[:, br * A_HEADS + hh: br * A_HEADS + hh + 1]
            term = gcol * o[r, :] * z_ref[:, zc]
            val = term if val is None else val + term
        o_ref[:, c] = val.astype(o_ref.dtype)


def nsa_sample_attention(qbd, cmp_kv, cache4d, page_table, tail_sel, win_all, z_s, gl_s,
                         b_cmp, b_sel, b_win, msel, expand, past_len, n_q):
    nb, rows, kvw = qbd.shape
    n_cmp = cmp_kv.shape[1] // NCK
    n_tab = page_table.shape[1]
    n_pages = min(PAGES_PER_STEP, n_tab)
    n_steps = n_tab // n_pages
    bps = n_pages * PAGE // SEL_BLOCK
    n_sel = -(-(past_len + n_q) // SEL_BLOCK)
    k_top = min(SEL_TOPK, n_sel)
    gq = A_KV * n_q

    oc, selmask = pl.pallas_call(
        functools.partial(_nsa_sample_cmp_kernel, past_len=past_len, n_sel=n_sel, k_top=k_top,
                          n_q=n_q, blocks_per_step=bps),
        grid=(nb,),
        in_specs=[
            pl.BlockSpec((None, rows, kvw), lambda b: (b, 0, 0)),
            pl.BlockSpec((None, n_cmp * NCK, HD), lambda b: (b, 0, 0)),
            pl.BlockSpec(b_cmp.shape, lambda b: (0, 0)),
            pl.BlockSpec(msel.shape, lambda b: (0, 0)),
        ],
        out_specs=[
            pl.BlockSpec((None, rows, HD), lambda b: (b, 0, 0)),
            pl.BlockSpec((None, n_steps + 1, gq, bps), lambda b: (b, 0, 0, 0)),
        ],
        out_shape=[
            jax.ShapeDtypeStruct((nb, rows, HD), F32),
            jax.ShapeDtypeStruct((nb, n_steps + 1, gq, bps), F32),
        ],
        compiler_params=_cparams(("arbitrary",)),
        name="nsa_sample_cmp",
    )(qbd, cmp_kv, b_cmp, msel)

    half = 2 * kvw
    last = n_steps - 1

    def page_spec(k):
        return pl.BlockSpec(
            (None, PAGE, NCK, HD),
            lambda b, s, pt: (pt[b, jnp.minimum(s, last) * n_pages + k], 0, 1, 0))

    nkeys = n_pages * PAGE
    grid_spec = pltpu.PrefetchScalarGridSpec(
        num_scalar_prefetch=1,
        grid=(nb, n_steps + 1),
        in_specs=[page_spec(k) for k in range(n_pages)] + [
            pl.BlockSpec((None, rows, kvw), lambda b, s, pt: (b, 0, 0)),
            pl.BlockSpec((None, PAGE, half), lambda b, s, pt: (b, 0, 0)),
            pl.BlockSpec((rows, nkeys), lambda b, s, pt: (0, s)),
            pl.BlockSpec((None, None, gq, bps), lambda b, s, pt: (b, s, 0, 0)),
            pl.BlockSpec(expand.shape, lambda b, s, pt: (0, 0)),
        ],
        out_specs=pl.BlockSpec((None, rows, HD), lambda b, s, pt: (b, 0, 0)),
        scratch_shapes=[
            pltpu.VMEM((rows, 1), F32),
            pltpu.VMEM((rows, 1), F32),
            pltpu.VMEM((rows, kvw), F32),
        ],
    )
    osel = pl.pallas_call(
        functools.partial(_nsa_sample_sel_kernel, n_pages=n_pages, n_steps=n_steps,
                          past_len=past_len, n_q=n_q),
        grid_spec=grid_spec,
        out_shape=jax.ShapeDtypeStruct((nb, rows, HD), F32),
        compiler_params=_cparams(("arbitrary", "arbitrary")),
        name="nsa_sample_sel",
    )(page_table, *([cache4d] * n_pages), qbd, tail_sel, b_sel, selmask, expand)

    nkw = win_all.shape[1]
    zw = z_s.shape[1]
    return pl.pallas_call(
        functools.partial(_nsa_sample_mix_kernel, n_q=n_q),
        grid=(nb,),
        in_specs=[
            pl.BlockSpec((None, rows, kvw), lambda b: (b, 0, 0)),
            pl.BlockSpec((None, nkw, 2 * kvw), lambda b: (b, 0, 0)),
            pl.BlockSpec(b_win.shape, lambda b: (0, 0)),
            pl.BlockSpec((None, rows, HD), lambda b: (b, 0, 0)),
            pl.BlockSpec((None, rows, HD), lambda b: (b, 0, 0)),
            pl.BlockSpec((n_q, zw), lambda b: (b, 0)),
            pl.BlockSpec((n_q, 3 * A_HEADS), lambda b: (b, 0)),
        ],
        out_specs=pl.BlockSpec((n_q, A_HEADS * HD), lambda b: (b, 0)),
        out_shape=jax.ShapeDtypeStruct((nb * n_q, A_HEADS * HD), F32),
        compiler_params=_cparams(("arbitrary",)),
        name="nsa_sample_mix",
    )(qbd, win_all, b_win, oc, osel, z_s, gl_s)


def _retention_kernel(q_ref, k_ref, v_ref, g_ref, cos_ref, sin_ref, dm_ref, qd_ref, kd_ref, cd_ref,
                      s0_ref, o_ref, s_ref, *, dk, dv, hps):
    c = pl.program_id(2)
    half = dk // 2

    @pl.when(c == 0)
    def _():
        s_ref[...] = s0_ref[...]

    cos = cos_ref[...]
    sin = sin_ref[...]

    def rot(x):
        x1, x2 = x[:, :half], x[:, half:]
        return jnp.concatenate([x1 * cos - x2 * sin, x1 * sin + x2 * cos], axis=1)

    for j in range(hps):
        ks, vs = slice(j * dk, (j + 1) * dk), slice(j * dv, (j + 1) * dv)
        q = rot(q_ref[:, ks])
        k = rot(k_ref[:, ks]) * (dk ** -0.5)
        v = v_ref[:, vs].astype(BF16)
        qb = q.astype(BF16)
        att = _dot_nt(qb, k.astype(BF16)) * dm_ref[j]
        state = s_ref[j]
        o = _dot(att.astype(BF16), v) + _dot(qb, state.astype(BF16)) * qd_ref[j]
        s_ref[j] = state * cd_ref[j] + _dot_tn((k * kd_ref[j]).astype(BF16), v)
        o = o * lax.rsqrt(jnp.mean(o * o, axis=-1, keepdims=True) + EPS)
        o_ref[:, vs] = (o * _silu(g_ref[:, vs])).astype(o_ref.dtype)


def retention(proj, state0, q0, nb, t):
    _, nh, dk, dv = state0.shape
    cs = R_CHUNK if t % R_CHUNK == 0 else t
    n = t // cs
    half = dk // 2
    pos = (q0 + jnp.arange(t)).astype(F32)
    inv = jnp.power(ROPE_BASE, -jnp.arange(half, dtype=F32) / half)
    ang = pos[:, None] * inv[None, :]
    cos, sin = jnp.cos(ang), jnp.sin(ang)
    log_g = jnp.log1p(-jnp.exp2(-5.0 - jnp.arange(nh, dtype=F32)))
    ii = jnp.arange(cs, dtype=F32)
    rel = ii[:, None] - ii[None, :]
    causal = rel >= 0
    dmat = jnp.where(causal, jnp.exp(jnp.where(causal, rel, 0.0) * log_g[:, None, None]), 0.0)
    q_dec = jnp.exp((ii + 1.0) * log_g[:, None])[:, :, None]
    k_dec = jnp.exp((cs - 1.0 - ii) * log_g[:, None])[:, :, None]
    c_dec = jnp.exp(cs * log_g)[:, None, None]
    hps = RET_HEADS_PER_STEP
    assert nh % hps == 0
    ng = nh // hps
    vb0 = 2 * nh * dk // (hps * dv)
    return pl.pallas_call(
        functools.partial(_retention_kernel, dk=dk, dv=dv, hps=hps),
        grid=(nb, ng, n),
        in_specs=[
            pl.BlockSpec((cs, hps * dk), lambda b, h, c: (b * n + c, h)),
            pl.BlockSpec((cs, hps * dk), lambda b, h, c: (b * n + c, ng + h)),
            pl.BlockSpec((cs, hps * dv), lambda b, h, c: (b * n + c, vb0 + h)),
            pl.BlockSpec((cs, hps * dv), lambda b, h, c: (b * n + c, vb0 + ng + h)),
            pl.BlockSpec((cs, half), lambda b, h, c: (c, 0)),
            pl.BlockSpec((cs, half), lambda b, h, c: (c, 0)),
            pl.BlockSpec((hps, cs, cs), lambda b, h, c: (h, 0, 0)),
            pl.BlockSpec((hps, cs, 1), lambda b, h, c: (h, 0, 0)),
            pl.BlockSpec((hps, cs, 1), lambda b, h, c: (h, 0, 0)),
            pl.BlockSpec((hps, 1, 1), lambda b, h, c: (h, 0, 0)),
            pl.BlockSpec((None, hps, dk, dv), lambda b, h, c: (b, h, 0, 0)),
        ],
        out_specs=[
            pl.BlockSpec((cs, hps * dv), lambda b, h, c: (b * n + c, h)),
            pl.BlockSpec((None, hps, dk, dv), lambda b, h, c: (b, h, 0, 0)),
        ],
        out_shape=[
            jax.ShapeDtypeStruct((nb * t, nh * dv), BF16 if cs % 16 == 0 else F32),
            jax.ShapeDtypeStruct(state0.shape, F32),
        ],
        compiler_params=_cparams(("arbitrary", "arbitrary", "arbitrary")),
        name="retention",
    )(proj, proj, proj, proj, cos, sin, dmat, q_dec, k_dec, c_dec, state0)


def _t5_bucket_np(dist):
    n = np.maximum(dist, 0)
    exact = N_BUCKETS // 2
    logv = np.log(np.maximum(n, 1).astype(np.float32) / np.float32(exact)) / np.float32(
        math.log(MAX_DISTANCE / exact))
    large = np.minimum(exact + (logv * np.float32(N_BUCKETS - exact)).astype(np.int32), N_BUCKETS - 1)
    return np.where(n < exact, n, large).astype(np.int32)


def _bucket_starts():
    bk = _t5_bucket_np(np.arange(4 * MAX_DISTANCE))
    assert np.all(np.diff(bk) >= 0) and bk[-1] == N_BUCKETS - 1
    return [int(np.argmax(bk >= b)) for b in range(N_BUCKETS)]


_BUCKET_START = _bucket_starts()


def _sel_matrix(n_cmp, n_cmp_pad, n_sel_pad):
    c = np.arange(n_cmp_pad)[:, None]
    j = np.arange(n_sel_pad)[None, :]
    a = (c >= SEL_RATIO * j) & (c < SEL_RATIO * j + SEL_RATIO)
    b = (c >= SEL_RATIO * j - 1) & (c < SEL_RATIO * j + SEL_RATIO - 1)
    m = (a.astype(np.float32) + b.astype(np.float32)) * (c < n_cmp)
    return jnp.asarray(m, BF16)


def _expand_matrix(n_blk_pad, n_keys):
    blk = np.arange(n_blk_pad)[:, None]
    key = np.arange(n_keys)[None, :]
    return jnp.asarray((key // SEL_BLOCK == blk).astype(np.float32), BF16)


def _w1_pairs(w1):
    w = w1.reshape(2, 2, CMP_STRIDE // 2, 2, HD, HD)
    w = jnp.transpose(w, (2, 3, 4, 0, 1, 5))
    return w.reshape(CMP_STRIDE // 2, 2 * HD, 4 * HD).astype(BF16)


def kernel(x_prompt, x_sample, c_prompt, c_sample, cache_nsa_kv, cache_nsa_win, state_ret, page_table,
           norm_g, ada_w, ada_b, rel_bias, a_w_in, a_w_out, a_cmp_pe, a_cmp_w1, a_cmp_w2, a_qk_g,
           r_w_in, r_w_out):
    nbp, t, d = x_prompt.shape
    nbs, ts, _ = x_sample.shape
    n_tab = page_table.shape[1]
    past_len = n_tab * PAGE
    kvw = A_KV * HD
    qw = A_HEADS * HD
    assert t % (2 * LANE) == 0 and t >= WINDOW and ts == 8 and past_len >= WINDOW
    assert cache_nsa_win.shape[2] == WINDOW

    n_c = nbp + nbs
    pad_c = (-n_c) % 8
    c_all = jnp.concatenate([c_prompt, c_sample, jnp.zeros((pad_c, d), F32)], axis=0)
    mod = ada_modulation(c_all, ada_w, ada_b).reshape(ada_w.shape[0], n_c + pad_c, 3, d)

    def mods(layer):
        m = mod[layer]
        return (m[:nbp, 0], m[:nbp, 1], m[:nbp, 2]), (m[nbp:n_c, 0], m[nbp:n_c, 1], m[nbp:n_c, 2])

    (sh_p, sc_p, gt_p), (sh_s, sc_s, gt_s) = mods(0)
    hp = norm_modulate(x_prompt, norm_g[0], sc_p, sh_p)
    hs = norm_modulate(x_sample, norm_g[0], sc_s, sh_s)
    w_in_t = jnp.swapaxes(a_w_in[0], 0, 1)
    n_main = w_in_t.shape[0] - 3 * A_HEADS
    n_qkv = qw + 6 * kvw
    proj_p, proj_s = project(hp, hs, w_in_t, n_cols=n_qkv, w_is_nk=True)
    zt_p, z_s = project_gate_t(hp, hs, w_in_t, n_qkv, n_main - n_qkv)
    w_gate = jnp.pad(w_in_t[n_main:], ((0, LANE - 3 * A_HEADS), (0, 0)))
    gl_p, gl_s = project(hp, hs, w_gate, w_is_nk=True)
    gl_p, gl_s = gl_p[:, :3 * A_HEADS], gl_s[:, :3 * A_HEADS]
    qk_g = a_qk_g[0]

    qn_p, rows_p, win_p = nsa_prep(proj_p, qk_g)
    qn_s, rows_s, win_s = nsa_prep(proj_s, qk_g)

    w_pairs = _w1_pairs(a_cmp_w1[0])
    pe = a_cmp_pe[0]
    w1 = a_cmp_w1[0].reshape(2, CMP_LEN, HD, HD)
    w2 = a_cmp_w2[0]
    cpp = PAGE // CMP_STRIDE

    pages_p = t // PAGE
    ident = jnp.arange(nbp * pages_p, dtype=jnp.int32).reshape(nbp, pages_p)
    ab_p = compress_stage_a(rows_p.reshape(nbp * pages_p, PAGE, 4 * A_KV, HD), ident, w_pairs,
                            min(PAGES_PER_STEP, pages_p))
    zero_tail = jnp.zeros((nbp, cpp, NCK, 2 * HD), F32)
    cmp_p = compress_stage_b(ab_p, zero_tail, pe, w1, w2, qk_g)

    nq = t // Q_BLOCK
    n_ch_p = t // CMP_STRIDE
    tsel, twin, tcmp = prompt_bias_tables(rel_bias, nq, n_ch_p)
    n_cmp_p = t // CMP_STRIDE - 1
    n_sel_p = -(-t // SEL_BLOCK)
    n_sel_pad_p = -(-n_sel_p // 16) * 16
    msel_p = _sel_matrix(n_cmp_p, n_ch_p, n_sel_pad_p).T
    expand_p = _expand_matrix(n_sel_pad_p, t).T
    gates_g = jnp.transpose(gl_p.reshape(nbp * t, 3, A_KV, A_HPG), (2, 1, 3, 0)).reshape(A_KV, 3 * A_HPG, nbp * t)
    mixed_p = nsa_prompt_attention(qn_p, cmp_p, rows_p, win_p, zt_p, gates_g,
                                   tsel, twin, tcmp, msel_p, expand_p, nbp, t)

    n_pool = cache_nsa_kv.shape[1]
    cache4d = cache_nsa_kv[0].reshape(n_pool, PAGE, 4 * A_KV, HD)
    n_pages = min(PAGES_PER_STEP, n_tab)
    ab_s = compress_stage_a(cache4d, page_table, w_pairs, n_pages)
    tail_rows = jnp.pad(rows_s.reshape(nbs, ts, 4 * kvw), ((0, 0), (0, PAGE - ts), (0, 0)))
    ident_s = jnp.arange(nbs, dtype=jnp.int32).reshape(nbs, 1)
    ab_tail = compress_stage_a(tail_rows.reshape(nbs, PAGE, 4 * A_KV, HD), ident_s, w_pairs, 1)
    cmp_s = compress_stage_b(ab_s, ab_tail, pe, w1, w2, qk_g)

    n_cmp_s = cmp_s.shape[1] // NCK
    n_steps = n_tab // n_pages
    bps = n_pages * PAGE // SEL_BLOCK
    n_sel_pad = (n_steps + 1) * bps
    nkeys = (n_steps + 1) * n_pages * PAGE
    nkw = WINDOW + LANE
    b_cmp, b_sel, b_win = sample_bias_rows(rel_bias, past_len, ts, n_cmp_s, nkeys, nkw)
    msel_s = _sel_matrix(n_cmp_s, n_cmp_s, n_sel_pad)
    expand_s = _expand_matrix(bps, n_pages * PAGE)
    q5 = qn_s.reshape(nbs, ts, A_KV, A_HPG, HD)
    eye = jnp.eye(A_KV, dtype=BF16)
    qbd = jnp.einsum('bqghd,ge->bghqed', q5, eye).reshape(nbs, A_HEADS * ts, kvw)
    win_new = win_s.reshape(nbs, ts, 2 * kvw)
    win_cache = cache_nsa_win[0].reshape(nbs, WINDOW, 2 * kvw)
    win_all = jnp.concatenate([win_cache, win_new], axis=1)
    win_pad = jnp.pad(win_all, ((0, 0), (0, nkw - WINDOW - ts), (0, 0)))
    tail_sel = tail_rows[:, :, 2 * kvw:]
    mixed_s = nsa_sample_attention(qbd, cmp_s, cache4d, page_table, tail_sel, win_pad, z_s, gl_s,
                                   b_cmp, b_sel, b_win, msel_s, expand_s, past_len, ts).astype(BF16)

    xs_flat = x_sample.reshape(nbs * ts, d)
    gate_s_rows = jnp.repeat(gt_s, ts, axis=0)
    x1p, x1s = project_residual(mixed_p, mixed_s, a_w_out[0], x_prompt.reshape(nbp * t, d), gt_p,
                                xs_flat, gate_s_rows, t)

    (sh_p, sc_p, gt_p), (sh_s, sc_s, gt_s) = mods(1)
    hp = norm_modulate(x1p.reshape(nbp, t, d), norm_g[1], sc_p, sh_p)
    hs = norm_modulate(x1s.reshape(nbs, ts, d), norm_g[1], sc_s, sh_s)
    rp, rs = project(hp, hs, r_w_in[0])
    s0 = jnp.zeros((nbp,) + state_ret.shape[2:], F32)
    op, ret_p = retention(rp, s0, 0, nbp, t)
    os_, ret_s = retention(rs, state_ret[0], past_len, nbs, ts)
    os_ = os_.astype(BF16)
    gate_s_rows = jnp.repeat(gt_s, ts, axis=0)
    x2p, x2s = project_residual(op, os_, r_w_out[0], x1p, gt_p, x1s, gate_s_rows, t)

    kv_p = rows_p.reshape(1, nbp, t, 4, A_KV, HD)
    kv_s = rows_s.reshape(1, nbs, ts, 4, A_KV, HD)
    wst_p = win_p.reshape(nbp, t, 2, A_KV, HD)[None, :, t - WINDOW:]
    wst_s = win_all[:, ts:].reshape(1, nbs, WINDOW, 2, A_KV, HD)
    return (x2p.reshape(nbp, t, d), x2s.reshape(nbs, ts, d), kv_p, kv_s, wst_p, wst_s,
            ret_p[None], ret_s[None])
```

```python
import functools
import math

import numpy as np
import jax
import jax.numpy as jnp
from jax import lax
from jax.experimental import pallas as pl
from jax.experimental.pallas import tpu as pltpu

EPS = 1e-6
HD = 128
A_KV = 4
A_HPG = 8
A_HEADS = A_KV * A_HPG
CMP_STRIDE = 16
CMP_LEN = 32
SEL_BLOCK = 64
SEL_RATIO = SEL_BLOCK // CMP_STRIDE
SEL_TOPK = 16
WINDOW = 512
Q_BLOCK = 128
FORCE_SCORE = 1e4
NEG = -1e30
TINY = 1e-30
N_BUCKETS = 32
MAX_DISTANCE = 1024
R_CHUNK = 128
ROPE_BASE = 10000.0
LOG2E = math.log2(math.e)
Q_SCALE = HD ** -0.5 * LOG2E
PAGE = 128
PAGES_PER_STEP = 16
RET_HEADS_PER_STEP = 8
SEL_KEY_TILE = 512

LANE = 128
VMEM_LIMIT = 60 * 1024 * 1024

F32 = jnp.float32
BF16 = jnp.bfloat16


def _cparams(sem):
    return pltpu.CompilerParams(dimension_semantics=sem, vmem_limit_bytes=VMEM_LIMIT)


def _dot(a, b):
    return jnp.dot(a, b, preferred_element_type=F32)


def _dot_nt(a, b):
    return lax.dot_general(a, b, (((1,), (1,)), ((), ())), preferred_element_type=F32)


def _dot_tn(a, b):
    return lax.dot_general(a, b, (((0,), (0,)), ((), ())), preferred_element_type=F32)


def _silu(x):
    h = 0.5 * x
    return h + h * jnp.tanh(h)


def _sigmoid(x):
    return 0.5 + 0.5 * jnp.tanh(0.5 * x)


def _split3_dot(x, m_bf16):
    hi = x.astype(BF16)
    r1 = x - hi.astype(F32)
    mid = r1.astype(BF16)
    lo = (r1 - mid.astype(F32)).astype(BF16)
    return _dot(hi, m_bf16) + _dot(mid, m_bf16) + _dot(lo, m_bf16)


def _ada_kernel(c_ref, w_ref, b_ref, o_ref):
    a = _silu(c_ref[...]).astype(BF16)
    o_ref[...] = _dot(a, w_ref[...].astype(BF16)) + b_ref[...]


def ada_modulation(c_all, ada_w, ada_b):
    depth, d, n = ada_w.shape
    rows = c_all.shape[0]
    tn = 512
    return pl.pallas_call(
        _ada_kernel,
        grid=(depth, n // tn),
        in_specs=[
            pl.BlockSpec((rows, d), lambda l, j: (0, 0)),
            pl.BlockSpec((None, d, tn), lambda l, j: (l, 0, j)),
            pl.BlockSpec((None, 1, tn), lambda l, j: (l, 0, j)),
        ],
        out_specs=pl.BlockSpec((None, rows, tn), lambda l, j: (l, 0, j)),
        out_shape=jax.ShapeDtypeStruct((depth, rows, n), F32),
        compiler_params=_cparams(("arbitrary", "arbitrary")),
        name="ada_modulation",
    )(c_all, ada_w, ada_b.reshape(depth, 1, n))


def _norm_mod_kernel(x_ref, g_ref, sc_ref, sh_ref, o_ref):
    x = x_ref[...]
    y = x * lax.rsqrt(jnp.mean(x * x, axis=-1, keepdims=True) + EPS)
    y = y * g_ref[...]
    o_ref[...] = (y * (1.0 + sc_ref[...]) + sh_ref[...]).astype(o_ref.dtype)


def norm_modulate(x, g, scale, shift):
    b, t, d = x.shape
    tt = min(t, 256)
    out = pl.pallas_call(
        _norm_mod_kernel,
        grid=(b, t // tt),
        in_specs=[
            pl.BlockSpec((None, tt, d), lambda i, j: (i, j, 0)),
            pl.BlockSpec((1, d), lambda i, j: (0, 0)),
            pl.BlockSpec((None, 1, d), lambda i, j: (i, 0, 0)),
            pl.BlockSpec((None, 1, d), lambda i, j: (i, 0, 0)),
        ],
        out_specs=pl.BlockSpec((None, tt, d), lambda i, j: (i, j, 0)),
        out_shape=jax.ShapeDtypeStruct((b, t, d), BF16),
        compiler_params=_cparams(("arbitrary", "arbitrary")),
        name="norm_modulate",
    )(x, g.reshape(1, d), scale.reshape(b, 1, d), shift.reshape(b, 1, d))
    return out.reshape(b * t, d)


def _proj_kernel(xp_ref, xs_ref, w_ref, op_ref, os_ref, wb_ref):
    @pl.when(pl.program_id(1) == 0)
    def _():
        wb_ref[...] = w_ref[...].astype(BF16)
        os_ref[...] = _dot(xs_ref[...], wb_ref[...])

    op_ref[...] = _dot(xp_ref[...], wb_ref[...])


def _proj_nk_kernel(xp_ref, xs_ref, w_ref, op_ref, os_ref, wb_ref):
    @pl.when(pl.program_id(1) == 0)
    def _():
        wb_ref[...] = w_ref[...].astype(BF16)
        os_ref[...] = _dot_nt(xs_ref[...], wb_ref[...])

    op_ref[...] = _dot_nt(xp_ref[...], wb_ref[...])


def _proj_gate_t_kernel(xp_ref, xs_ref, w_ref, opt_ref, os_ref, wb_ref):
    @pl.when(pl.program_id(1) == 0)
    def _():
        wb_ref[...] = w_ref[...].astype(BF16)
        os_ref[...] = _silu(_dot_nt(xs_ref[...], wb_ref[...]))

    opt_ref[...] = _silu(_dot_nt(wb_ref[...], xp_ref[...]))


def project_gate_t(xp, xs, w_nk, row0, n_rows):
    mp, k = xp.shape
    ms = xs.shape[0]
    tm, tn = _proj_tiles(mp, k, n_rows)
    assert row0 % tn == 0
    j0 = row0 // tn
    return pl.pallas_call(
        _proj_gate_t_kernel,
        grid=(n_rows // tn, mp // tm),
        in_specs=[
            pl.BlockSpec((tm, k), lambda j, i: (i, 0)),
            pl.BlockSpec((ms, k), lambda j, i: (0, 0)),
            pl.BlockSpec((tn, k), lambda j, i: (j0 + j, 0)),
        ],
        out_specs=[
            pl.BlockSpec((tn, tm), lambda j, i: (j, i)),
            pl.BlockSpec((ms, tn), lambda j, i: (0, j)),
        ],
        out_shape=[jax.ShapeDtypeStruct((n_rows, mp), F32), jax.ShapeDtypeStruct((ms, n_rows), F32)],
        scratch_shapes=[pltpu.VMEM((tn, k), BF16)],
        compiler_params=_cparams(("arbitrary", "arbitrary")),
        name="project_gate_t",
    )(xp, xs, w_nk)


def _proj_res_kernel(xp_ref, xs_ref, w_ref, rp_ref, gp_ref, rs_ref, gs_ref, op_ref, os_ref, wb_ref):
    @pl.when(pl.program_id(1) == 0)
    def _():
        wb_ref[...] = w_ref[...].astype(BF16)
        os_ref[...] = rs_ref[...] + gs_ref[...] * _dot(xs_ref[...], wb_ref[...])

    op_ref[...] = rp_ref[...] + gp_ref[...] * _dot(xp_ref[...], wb_ref[...])


def _proj_tiles(mp, k, n, residual=False):
    if k <= 4096:
        tm, tn = (1024, 512) if residual else (512, 1024)
    else:
        tm, tn = 256, 512
    tm = min(tm, mp)
    while mp % tm:
        tm //= 2
    tn = min(tn, n)
    while n % tn:
        tn //= 2
    return tm, tn


def project(xp, xs, w, n_cols=None, w_is_nk=False):
    mp, k = xp.shape
    ms = xs.shape[0]
    n = w.shape[0 if w_is_nk else 1] if n_cols is None else n_cols
    tm, tn = _proj_tiles(mp, k, n)
    w_spec = pl.BlockSpec((tn, k), lambda j, i: (j, 0)) if w_is_nk else pl.BlockSpec((k, tn), lambda j, i: (0, j))
    return pl.pallas_call(
        _proj_nk_kernel if w_is_nk else _proj_kernel,
        grid=(n // tn, mp // tm),
        in_specs=[
            pl.BlockSpec((tm, k), lambda j, i: (i, 0)),
            pl.BlockSpec((ms, k), lambda j, i: (0, 0)),
            w_spec,
        ],
        out_specs=[
            pl.BlockSpec((tm, tn), lambda j, i: (i, j)),
            pl.BlockSpec((ms, tn), lambda j, i: (0, j)),
        ],
        out_shape=[jax.ShapeDtypeStruct((mp, n), F32), jax.ShapeDtypeStruct((ms, n), F32)],
        scratch_shapes=[pltpu.VMEM((tn, k) if w_is_nk else (k, tn), BF16)],
        compiler_params=_cparams(("arbitrary", "arbitrary")),
        name="project",
    )(xp, xs, w)


def project_residual(xp, xs, w, res_p, gate_p, res_s, gate_s, rows_per_batch):
    mp, k = xp.shape
    ms = xs.shape[0]
    n = w.shape[1]
    tm, tn = _proj_tiles(mp, k, n, residual=True)
    tm = min(tm, rows_per_batch)
    tpb = rows_per_batch // tm
    nb = gate_p.shape[0]
    return pl.pallas_call(
        _proj_res_kernel,
        grid=(n // tn, mp // tm),
        in_specs=[
            pl.BlockSpec((tm, k), lambda j, i: (i, 0)),
            pl.BlockSpec((ms, k), lambda j, i: (0, 0)),
            pl.BlockSpec((k, tn), lambda j, i: (0, j)),
            pl.BlockSpec((tm, tn), lambda j, i: (i, j)),
            pl.BlockSpec((None, 1, tn), lambda j, i: (i // tpb, 0, j)),
            pl.BlockSpec((ms, tn), lambda j, i: (0, j)),
            pl.BlockSpec((ms, tn), lambda j, i: (0, j)),
        ],
        out_specs=[
            pl.BlockSpec((tm, tn), lambda j, i: (i, j)),
            pl.BlockSpec((ms, tn), lambda j, i: (0, j)),
        ],
        out_shape=[jax.ShapeDtypeStruct((mp, n), F32), jax.ShapeDtypeStruct((ms, n), F32)],
        scratch_shapes=[pltpu.VMEM((k, tn), BF16)],
        compiler_params=_cparams(("arbitrary", "arbitrary")),
        name="project_residual",
    )(xp, xs, w, res_p, gate_p.reshape(nb, 1, n), res_s, gate_s)


def _nsa_prep_kernel(q_ref, a_ref, b_ref, c_ref, g_ref, qn_ref, rows_ref, win_ref, rows_t_ref, win_t_ref):
    def hnorm(x, g):
        return x * lax.rsqrt(jnp.mean(x * x, axis=-1, keepdims=True) + EPS) * g

    g = g_ref[...]
    for j in range(A_HEADS):
        sl = slice(j * HD, (j + 1) * HD)
        qn_ref[:, sl] = (hnorm(q_ref[:, sl], g[0:1]) * Q_SCALE).astype(qn_ref.dtype)
    kvw = A_KV * HD
    sls = [slice(j * HD, (j + 1) * HD) for j in range(2 * A_KV)]
    row_slots = ([a_ref[:, s] for s in sls] + [hnorm(b_ref[:, s], g[2:3]) for s in sls[:A_KV]]
                 + [b_ref[:, s] for s in sls[A_KV:]])
    win_slots = [hnorm(c_ref[:, s], g[3:4]) for s in sls[:A_KV]] + [c_ref[:, s] for s in sls[A_KV:]]
    for j, x in enumerate(row_slots):
        rows_ref[:, j * HD:(j + 1) * HD] = x
    for j, x in enumerate(win_slots):
        win_ref[:, j * HD:(j + 1) * HD] = x
    rows_t_ref[...] = jnp.swapaxes(jnp.stack(row_slots, axis=0), 0, 1)
    win_t_ref[...] = jnp.swapaxes(jnp.stack(win_slots, axis=0), 0, 1)


def nsa_prep(proj, qk_g):
    m = proj.shape[0]
    tm = min(m, 256)
    qw = A_HEADS * HD
    kv2 = 2 * A_KV * HD
    base = qw // kv2
    return pl.pallas_call(
        _nsa_prep_kernel,
        grid=(m // tm,),
        in_specs=[
            pl.BlockSpec((tm, qw), lambda i: (i, 0)),
            pl.BlockSpec((tm, kv2), lambda i: (i, base)),
            pl.BlockSpec((tm, kv2), lambda i: (i, base + 1)),
            pl.BlockSpec((tm, kv2), lambda i: (i, base + 2)),
            pl.BlockSpec((4, HD), lambda i: (0, 0)),
        ],
        out_specs=[
            pl.BlockSpec((tm, qw), lambda i: (i, 0)),
            pl.BlockSpec((tm, 2 * kv2), lambda i: (i, 0)),
            pl.BlockSpec((tm, kv2), lambda i: (i, 0)),
            pl.BlockSpec((tm, 4 * A_KV, HD), lambda i: (i, 0, 0)),
            pl.BlockSpec((tm, 2 * A_KV, HD), lambda i: (i, 0, 0)),
        ],
        out_shape=[
            jax.ShapeDtypeStruct((m, qw), BF16),
            jax.ShapeDtypeStruct((m, 2 * kv2), F32),
            jax.ShapeDtypeStruct((m, kv2), F32),
            jax.ShapeDtypeStruct((m, 4 * A_KV, HD), F32),
            jax.ShapeDtypeStruct((m, 2 * A_KV, HD), F32),
        ],
        compiler_params=_cparams(("arbitrary",)),
        name="nsa_prep",
    )(proj, proj, proj, proj, qk_g)


NCK = 2 * A_KV


def _cmp_a_kernel(pt_ref, *refs, n_pages):
    del pt_ref
    pages = refs[:n_pages]
    w_ref = refs[n_pages]
    o_ref = refs[n_pages + 1]
    cpp = PAGE // CMP_STRIDE
    n_ch = n_pages * cpp

    def halves(tiles, lo):
        return [jnp.concatenate([tiles[a][lo:lo + A_KV], tiles[a + 1][lo:lo + A_KV]], axis=0)
                for a in range(0, len(tiles), 2)]

    acc = [None, None]
    for tp in range(CMP_STRIDE // 2):
        pieces = []
        for pg in pages:
            for ch in range(cpp):
                t0 = ch * CMP_STRIDE + 2 * tp
                pieces.append(jnp.concatenate([pg[t0], pg[t0 + 1]], axis=1))
        for kind in range(2):
            lhs = jnp.concatenate(halves(pieces, kind * A_KV), axis=0).astype(BF16)
            d = _dot(lhs, w_ref[tp, :, kind * 2 * HD:(kind + 1) * 2 * HD])
            acc[kind] = d if acc[kind] is None else acc[kind] + d
    for a in range(0, n_ch, 2):
        r = slice(a * A_KV, (a + 2) * A_KV)
        kk, vv = acc[0][r], acc[1][r]
        o_ref[a] = jnp.concatenate([kk[:A_KV], vv[:A_KV]], axis=0)
        o_ref[a + 1] = jnp.concatenate([kk[A_KV:], vv[A_KV:]], axis=0)


def compress_stage_a(store, page_table, w_pairs, n_pages):
    nb, n_tab = page_table.shape
    steps = n_tab // n_pages
    cpp = PAGE // CMP_STRIDE

    def page_spec(k):
        return pl.BlockSpec((None, PAGE, NCK, HD), lambda b, s, pt: (pt[b, s * n_pages + k], 0, 0, 0))

    grid_spec = pltpu.PrefetchScalarGridSpec(
        num_scalar_prefetch=1,
        grid=(nb, steps),
        in_specs=[page_spec(k) for k in range(n_pages)]
        + [pl.BlockSpec(w_pairs.shape, lambda b, s, pt: (0, 0, 0))],
        out_specs=pl.BlockSpec((None, n_pages * cpp, NCK, 2 * HD), lambda b, s, pt: (b, s, 0, 0)),
    )
    return pl.pallas_call(
        functools.partial(_cmp_a_kernel, n_pages=n_pages),
        grid_spec=grid_spec,
        out_shape=jax.ShapeDtypeStruct((nb, n_tab * cpp, NCK, 2 * HD), F32),
        compiler_params=_cparams(("arbitrary", "arbitrary")),
        name="compress_stage_a",
    )(page_table, *([store] * n_pages), w_pairs)


def _cmp_b_kernel(ab_ref, nx_ref, tail_ref, pe_ref, w1_ref, w2_ref, g_ref, o_ref, const_ref):
    j = pl.program_id(1)
    nch = ab_ref.shape[0]
    rows = nch * NCK

    @pl.when((pl.program_id(0) == 0) & (j == 0))
    def _():
        for kind in range(2):
            const = jnp.zeros((8, HD), F32)
            for t in range(CMP_LEN):
                row = jnp.broadcast_to(pe_ref[kind, t:t + 1, :], (8, HD)).astype(BF16)
                const = const + _dot(row, w1_ref[kind, t].astype(BF16))
            const_ref[kind * A_KV:(kind + 1) * A_KV, :] = const[:A_KV]

    ab = ab_ref[...].reshape(rows, 2 * HD)
    nxt = pltpu.roll(ab[:, HD:], rows - NCK, 0).reshape(nch, NCK, HD)
    first_next = jnp.where(j == pl.num_programs(1) - 1, tail_ref[0, :, HD:], nx_ref[0, :, HD:])
    cid = lax.broadcasted_iota(jnp.int32, (nch, NCK, HD), 0)
    nxt = jnp.where(cid == nch - 1, first_next[None], nxt)
    pre = ab[:, :HD].reshape(nch, NCK, HD) + nxt + const_ref[...][None]
    act = _silu(pre).reshape(rows, HD).astype(BF16)
    w2 = jnp.concatenate([w2_ref[0], w2_ref[1]], axis=1).astype(BF16)
    y2 = _dot(act, w2)
    is_k = (lax.broadcasted_iota(jnp.int32, (rows, HD), 0) & (NCK - 1)) < A_KV
    y = jnp.where(is_k, y2[:, :HD], y2[:, HD:])
    yn = y * lax.rsqrt(jnp.mean(y * y, axis=-1, keepdims=True) + EPS) * g_ref[1:2]
    o_ref[...] = jnp.where(is_k, yn, y).reshape(nch, NCK, HD)


def compress_stage_b(ab, tail, pe, w1, w2, qk_g):
    nb, n_ch = ab.shape[:2]
    tc = min(256, n_ch)
    assert n_ch % tc == 0
    nt = n_ch // tc
    out = pl.pallas_call(
        _cmp_b_kernel,
        grid=(nb, nt),
        in_specs=[
            pl.BlockSpec((None, tc, NCK, 2 * HD), lambda b, j: (b, j, 0, 0)),
            pl.BlockSpec((None, 1, NCK, 2 * HD), lambda b, j: (b, jnp.minimum((j + 1) * tc, n_ch - 1), 0, 0)),
            pl.BlockSpec((None, 1, NCK, 2 * HD), lambda b, j: (b, 0, 0, 0)),
            pl.BlockSpec(pe.shape, lambda b, j: (0, 0, 0)),
            pl.BlockSpec(w1.shape, lambda b, j: (0, 0, 0, 0)),
            pl.BlockSpec(w2.shape, lambda b, j: (0, 0, 0)),
            pl.BlockSpec((4, HD), lambda b, j: (0, 0)),
        ],
        out_specs=pl.BlockSpec((None, tc, NCK, HD), lambda b, j: (b, j, 0, 0)),
        out_shape=jax.ShapeDtypeStruct((nb, n_ch, NCK, HD), F32),
        scratch_shapes=[pltpu.VMEM((NCK, HD), F32)],
        compiler_params=_cparams(("arbitrary", "arbitrary")),
        name="compress_stage_b",
    )(ab, ab, tail, pe, w1, w2, qk_g)
    return out.reshape(nb, n_ch * NCK, HD)


def _cmp_rows(cmp_ref, slot, n):
    return cmp_ref[pl.ds(slot, n, stride=NCK), :]


def _select_topk(score, blk, k_top):
    sel = jnp.zeros(score.shape, F32)
    for _ in range(k_top):
        m = jnp.max(score, axis=-1, keepdims=True)
        cand = jnp.where(score == m, blk, jnp.int32(1 << 30))
        first = jnp.min(cand, axis=-1, keepdims=True)
        pick = blk == first
        sel = jnp.where(pick, 1.0, sel)
        score = jnp.where(pick, -jnp.inf, score)
    return sel


def _select_topk_t(score, blk, k_top, n_blk):
    rank = jnp.zeros(score.shape, F32)
    for i in range(n_blk):
        row = score[i:i + 1, :]
        tie = jnp.where(blk > i, 1.0, 0.0)
        rank = rank + jnp.where(row > score, 1.0, jnp.where(row == score, tie, 0.0))
    return jnp.where(rank < k_top, 1.0, 0.0)


def _bias_select(dist, dmin, dmax, value_of):
    lo = int(_t5_bucket_np(np.array([max(dmin, 0)]))[0])
    hi = int(_t5_bucket_np(np.array([max(dmax, 0)]))[0])
    out = jnp.full(dist.shape, value_of(lo), F32)
    for b in range(lo + 1, hi + 1):
        out = jnp.where(dist >= _BUCKET_START[b], value_of(b), out)
    return out * LOG2E


def _bias_tables_kernel(rb_ref, tsel_ref, twin_ref, tcmp_ref):
    head = pl.program_id(0)
    value_of = lambda b: rb_ref[b, head]
    n_off, nq, nc = tsel_ref.shape[0], tcmp_ref.shape[0], tcmp_ref.shape[1]
    kl = lax.broadcasted_iota(jnp.int32, (LANE, Q_BLOCK), 0)
    ql = lax.broadcasted_iota(jnp.int32, (LANE, Q_BLOCK), 1)
    for o in range(n_off):
        base = (o - 1) * Q_BLOCK
        dist = base + ql - kl
        tile = _bias_select(dist, base - (LANE - 1), base + Q_BLOCK - 1, value_of)
        tsel_ref[o] = tile
        if 1 <= o <= twin_ref.shape[0]:
            twin_ref[o - 1] = tile + jnp.where((dist >= 0) & (dist < WINDOW), 0.0, NEG)
    cend = lax.broadcasted_iota(jnp.int32, (nc, Q_BLOCK), 0) * CMP_STRIDE + (CMP_LEN - 1)
    qc = lax.broadcasted_iota(jnp.int32, (nc, Q_BLOCK), 1)
    for i in range(nq):
        base = i * Q_BLOCK
        tcmp_ref[i] = _bias_select(base + qc - cend, base - ((nc - 1) * CMP_STRIDE + CMP_LEN - 1),
                                   base + Q_BLOCK - 1, value_of)


def prompt_bias_tables(rel_bias, nq, n_cmp):
    n_win = min(WINDOW // Q_BLOCK + 1, nq)
    return pl.pallas_call(
        _bias_tables_kernel,
        grid=(A_HEADS,),
        in_specs=[pl.BlockSpec(memory_space=pltpu.SMEM)],
        out_specs=[
            pl.BlockSpec((None, nq + 1, LANE, Q_BLOCK), lambda h: (h // A_HPG, 0, 0, h % A_HPG)),
            pl.BlockSpec((None, n_win, LANE, Q_BLOCK), lambda h: (h // A_HPG, 0, 0, h % A_HPG)),
            pl.BlockSpec((None, nq, n_cmp, Q_BLOCK), lambda h: (h // A_HPG, 0, 0, h % A_HPG)),
        ],
        out_shape=[
            jax.ShapeDtypeStruct((A_KV, nq + 1, LANE, A_HPG * Q_BLOCK), F32),
            jax.ShapeDtypeStruct((A_KV, n_win, LANE, A_HPG * Q_BLOCK), F32),
            jax.ShapeDtypeStruct((A_KV, nq, n_cmp, A_HPG * Q_BLOCK), F32),
        ],
        compiler_params=_cparams(("arbitrary",)),
        name="prompt_bias_tables",
    )(rel_bias)


def _bias_rows_kernel(rb_ref, cmp_ref, sel_ref, win_ref, *, past_len, n_q):
    head = pl.program_id(0)
    value_of = lambda b: rb_ref[b, head]

    def fill(ref, key_pos, kmin, kmax, q0):
        q = lax.broadcasted_iota(jnp.int32, ref.shape, 0)
        k = lax.broadcasted_iota(jnp.int32, ref.shape, 1)
        ref[...] = _bias_select(q0 + q - key_pos(k), q0 - kmax, q0 + n_q - 1 - kmin, value_of)

    nc, nk, nw = cmp_ref.shape[1], sel_ref.shape[1], win_ref.shape[1]
    fill(cmp_ref, lambda k: k * CMP_STRIDE + (CMP_LEN - 1), CMP_LEN - 1, (nc - 1) * CMP_STRIDE + CMP_LEN - 1, past_len)
    fill(sel_ref, lambda k: k, 0, nk - 1, past_len)
    fill(win_ref, lambda k: k, 0, nw - 1, WINDOW)


def sample_bias_rows(rel_bias, past_len, n_q, n_cmp, n_keys, n_win):
    shapes = [(A_HEADS * n_q, n) for n in (n_cmp, n_keys, n_win)]
    return pl.pallas_call(
        functools.partial(_bias_rows_kernel, past_len=past_len, n_q=n_q),
        grid=(A_HEADS,),
        in_specs=[pl.BlockSpec(memory_space=pltpu.SMEM)],
        out_specs=[pl.BlockSpec((n_q, s[1]), lambda h: (h, 0)) for s in shapes],
        out_shape=[jax.ShapeDtypeStruct(s, F32) for s in shapes],
        compiler_params=_cparams(("arbitrary",)),
        name="sample_bias_rows",
    )(rel_bias)


def _nsa_prompt_kernel(q_ref, cmp_ref, ks_ref, vs_ref, kw_ref, vw_ref, zc_ref, zs_ref, zw_ref,
                       gt_ref, tsel_ref, twin_ref, tcmp_ref, msel_ref, exp_ref, o_ref,
                       qa_ref, acc_ref, m_ref, l_ref, madd_ref, mix_ref, *, n_sel_pad, n_sel, k_top):
    i = pl.program_id(2)
    qb = Q_BLOCK
    n_kt, kt = madd_ref.shape[:2]
    tpk = kt // LANE
    nc = cmp_ref.shape[0] // NCK
    gates = _sigmoid(gt_ref[...])
    hs = [slice(h * HD, (h + 1) * HD) for h in range(A_HPG)]
    for h in range(A_HPG):
        qa_ref[hs[h], :] = q_ref[:, hs[h]]
    qa = qa_ref[...]

    def heads(x):
        return jnp.concatenate([x] * A_HPG, axis=1)

    def gate_row(branch):
        return jnp.concatenate([gates[branch * A_HPG + h: branch * A_HPG + h + 1, :] for h in range(A_HPG)], axis=1)

    def emit(branch, o_t, first):
        z_ref = (zc_ref, zs_ref, zw_ref)[branch]
        for h in range(A_HPG):
            val = o_t[:, hs[h]] * z_ref[hs[h], :]
            if first:
                mix_ref[hs[h], :] = val
            else:
                mix_ref[hs[h], :] = mix_ref[hs[h], :] + val

    grp = pl.program_id(0)
    kc = _cmp_rows(cmp_ref, grp, nc).astype(BF16)
    vc = _cmp_rows(cmp_ref, A_KV + grp, nc).astype(BF16)
    cend = lax.broadcasted_iota(jnp.int32, (nc, qb), 0) * CMP_STRIDE + (CMP_LEN - 1)
    qpos_c = i * qb + lax.broadcasted_iota(jnp.int32, (nc, qb), 1)
    cvalid = heads(cend <= qpos_c)
    s = jnp.where(cvalid, _dot_nt(kc, qa) + tcmp_ref[...], NEG)
    mx = jnp.max(s, axis=0, keepdims=True)
    e = jnp.where(cvalid, jnp.exp2(s - mx), 0.0)
    p = e * (1.0 / jnp.maximum(jnp.sum(e, axis=0, keepdims=True), TINY))
    imp = p[:, hs[0]]
    for h in range(1, A_HPG):
        imp = imp + p[:, hs[h]]
    emit(0, _dot_tn(vc, p.astype(BF16)) * gate_row(0), True)

    hi = imp.astype(BF16)
    r1 = imp - hi.astype(F32)
    mid = r1.astype(BF16)
    lo = (r1 - mid.astype(F32)).astype(BF16)
    msel = msel_ref[...]
    imp_sel = _dot(msel, hi) + _dot(msel, mid) + _dot(msel, lo)
    blk = lax.broadcasted_iota(jnp.int32, (n_sel_pad, qb), 0)
    qpos = i * qb + lax.broadcasted_iota(jnp.int32, (n_sel_pad, qb), 1)
    cur = jnp.right_shift(qpos, int(math.log2(SEL_BLOCK)))
    forced = (blk == 0) | (blk == cur) | (blk == cur - 1)
    valid = blk * SEL_BLOCK <= qpos
    score = jnp.where(valid, jnp.where(forced, FORCE_SCORE, imp_sel), -1.0)
    score = jnp.where(blk < n_sel, score, -jnp.inf)
    sel = _select_topk_t(score, blk, k_top, n_sel).astype(BF16)
    kpos2 = lax.broadcasted_iota(jnp.int32, (kt, qb), 0)
    qpos2 = i * qb + lax.broadcasted_iota(jnp.int32, (kt, qb), 1)
    for jj in range(n_kt):
        hit = _dot(exp_ref[jj * kt:(jj + 1) * kt, :], sel)
        ok = (hit > 0.5) & ((jj * kt + kpos2) <= qpos2)
        madd_ref[jj] = jnp.where(ok, 0.0, NEG)

    def reset():
        m_ref[...] = jnp.full(m_ref.shape, NEG, F32)
        l_ref[...] = jnp.zeros(l_ref.shape, F32)
        acc_ref[...] = jnp.zeros(acc_ref.shape, F32)

    def tile_update(k, v, bias, madd):
        s = _dot_nt(k, qa) + bias
        if madd is not None:
            s = s + heads(madd)
        m_old = m_ref[...]
        m_new = jnp.maximum(m_old, jnp.max(s, axis=0, keepdims=True))
        alpha = jnp.exp2(m_old - m_new)
        p = jnp.exp2(s - m_new)
        l_ref[...] = alpha * l_ref[...] + jnp.sum(p, axis=0, keepdims=True)
        m_ref[...] = m_new
        acc_ref[...] = acc_ref[...] * alpha + _dot_tn(v, p.astype(BF16))

    def finish(branch):
        emit(branch, acc_ref[...] * (gate_row(branch) / l_ref[...]), False)

    reset()

    def sel_body(jj, carry):
        k0 = pl.multiple_of(jj * kt, kt)
        k = ks_ref[pl.ds(k0, kt), :].astype(BF16)
        v = vs_ref[pl.ds(k0, kt), :].astype(BF16)
        off = i - tpk * jj
        bias = jnp.concatenate([tsel_ref[jnp.maximum(off + 1 - r, 0)] for r in range(tpk)], axis=0)
        tile_update(k, v, bias, madd_ref[jj])
        return carry

    lax.fori_loop(0, i // tpk + 1, sel_body, 0)
    finish(1)

    reset()

    def win_body(t, carry):
        k0 = pl.multiple_of((i - t) * qb, qb)
        k = kw_ref[pl.ds(k0, qb), :].astype(BF16)
        v = vw_ref[pl.ds(k0, qb), :].astype(BF16)
        tile_update(k, v, twin_ref[t], None)
        return carry

    lax.fori_loop(0, jnp.minimum(i, WINDOW // qb) + 1, win_body, 0)
    finish(2)
    for h in range(A_HPG):
        o_ref[:, hs[h]] = mix_ref[hs[h], :].T.astype(o_ref.dtype)


def nsa_prompt_attention(qn, cmp_kv, rows, win, z_t, gates_g, tsel, twin, tcmp, msel, expand, nb, t):
    nq = t // Q_BLOCK
    qw = A_HPG * HD
    n_sel = -(-t // SEL_BLOCK)
    k_top = min(SEL_TOPK, n_sel)
    kt = SEL_KEY_TILE if t % SEL_KEY_TILE == 0 else 2 * LANE
    n_kt = t // kt
    n_cmp = cmp_kv.shape[1] // NCK
    n_sel_pad = msel.shape[0]
    kernel = functools.partial(_nsa_prompt_kernel, n_sel_pad=n_sel_pad, n_sel=n_sel, k_top=k_top)
    return pl.pallas_call(
        kernel,
        grid=(A_KV, nb, nq),
        in_specs=[
            pl.BlockSpec((Q_BLOCK, qw), lambda g, b, i: (b * nq + i, g)),
            pl.BlockSpec((None, n_cmp * NCK, HD), lambda g, b, i: (b, 0, 0)),
            pl.BlockSpec((t, HD), lambda g, b, i: (b, 2 * A_KV + g)),
            pl.BlockSpec((t, HD), lambda g, b, i: (b, 3 * A_KV + g)),
            pl.BlockSpec((t, HD), lambda g, b, i: (b, g)),
            pl.BlockSpec((t, HD), lambda g, b, i: (b, A_KV + g)),
            pl.BlockSpec((qw, Q_BLOCK), lambda g, b, i: (g, b * nq + i)),
            pl.BlockSpec((qw, Q_BLOCK), lambda g, b, i: (A_KV + g, b * nq + i)),
            pl.BlockSpec((qw, Q_BLOCK), lambda g, b, i: (2 * A_KV + g, b * nq + i)),
            pl.BlockSpec((None, 3 * A_HPG, Q_BLOCK), lambda g, b, i: (g, 0, b * nq + i)),
            pl.BlockSpec((None, nq + 1, LANE, qw), lambda g, b, i: (g, 0, 0, 0)),
            pl.BlockSpec((None, twin.shape[1], LANE, qw), lambda g, b, i: (g, 0, 0, 0)),
            pl.BlockSpec((None, None, n_cmp, qw), lambda g, b, i: (g, i, 0, 0)),
            pl.BlockSpec((n_sel_pad, n_cmp), lambda g, b, i: (0, 0)),
            pl.BlockSpec((t, n_sel_pad), lambda g, b, i: (0, 0)),
        ],
        out_specs=pl.BlockSpec((Q_BLOCK, qw), lambda g, b, i: (b * nq + i, g)),
        out_shape=jax.ShapeDtypeStruct((nb * t, A_HEADS * HD), BF16),
        scratch_shapes=[
            pltpu.VMEM((qw, HD), BF16),
            pltpu.VMEM((HD, qw), F32),
            pltpu.VMEM((1, qw), F32),
            pltpu.VMEM((1, qw), F32),
            pltpu.VMEM((n_kt, kt, Q_BLOCK), F32),
            pltpu.VMEM((qw, Q_BLOCK), F32),
        ],
        compiler_params=_cparams(("arbitrary", "arbitrary", "arbitrary")),
        name="nsa_prompt_attention",
    )(qn, cmp_kv, rows, rows, win, win, z_t, z_t, z_t, gates_g, tsel, twin, tcmp, msel, expand)


def _diag_blocks(o_full, rows_per_group):
    return jnp.concatenate(
        [o_full[g * rows_per_group:(g + 1) * rows_per_group, g * HD:(g + 1) * HD] for g in range(A_KV)], axis=0)


def _nsa_sample_cmp_kernel(q_ref, cmp_ref, bias_ref, msel_ref, oc_ref, sel_ref,
                           *, past_len, n_sel, k_top, n_q, blocks_per_step):
    n_cmp = cmp_ref.shape[0] // NCK
    kc = jnp.concatenate([_cmp_rows(cmp_ref, g, n_cmp) for g in range(A_KV)], axis=1).astype(BF16)
    vc = jnp.concatenate([_cmp_rows(cmp_ref, A_KV + g, n_cmp) for g in range(A_KV)], axis=1).astype(BF16)
    rows = q_ref.shape[0]
    rpg = A_HPG * n_q
    q = q_ref[...]
    s = _dot_nt(q, kc) + bias_ref[...]
    rowi = lax.broadcasted_iota(jnp.int32, (rows, n_cmp), 0)
    ci = lax.broadcasted_iota(jnp.int32, (rows, n_cmp), 1)
    qpos = past_len + (rowi & (n_q - 1))
    valid = (ci * CMP_STRIDE + (CMP_LEN - 1)) <= qpos
    s = jnp.where(valid, s, NEG)
    mx = jnp.max(s, axis=-1, keepdims=True)
    e = jnp.where(valid, jnp.exp2(s - mx), 0.0)
    p = e / jnp.maximum(jnp.sum(e, axis=-1, keepdims=True), TINY)
    oc_ref[...] = _diag_blocks(_dot(p.astype(BF16), vc), rpg)
    imps = []
    for g in range(A_KV):
        acc = p[g * rpg: g * rpg + n_q]
        for h in range(1, A_HPG):
            acc = acc + p[g * rpg + h * n_q: g * rpg + (h + 1) * n_q]
        imps.append(acc)
    imp = jnp.concatenate(imps, axis=0)
    imp_sel = _split3_dot(imp, msel_ref[...])
    shp = imp_sel.shape
    blk = lax.broadcasted_iota(jnp.int32, shp, 1)
    qp = past_len + (lax.broadcasted_iota(jnp.int32, shp, 0) & (n_q - 1))
    cur = jnp.right_shift(qp, int(math.log2(SEL_BLOCK)))
    forced = (blk == 0) | (blk == cur) | (blk == cur - 1)
    ok = blk * SEL_BLOCK <= qp
    score = jnp.where(ok, jnp.where(forced, FORCE_SCORE, imp_sel), -1.0)
    score = jnp.where(blk < n_sel, score, -jnp.inf)
    sel = _select_topk(score, blk, k_top)
    for st in range(sel_ref.shape[0]):
        sel_ref[st] = sel[:, st * blocks_per_step:(st + 1) * blocks_per_step]


def _nsa_sample_sel_kernel(pt_ref, *refs, n_pages, n_steps, past_len, n_q):
    del pt_ref
    pages = refs[:n_pages]
    q_ref, tail_ref, bias_ref, sel_ref, exp_ref, o_ref, m_ref, l_ref, acc_ref = refs[n_pages:]
    s_id = pl.program_id(1)
    kvw = A_KV * HD
    rows = q_ref.shape[0]
    rpg = A_HPG * n_q

    @pl.when(s_id == 0)
    def _():
        m_ref[...] = jnp.full(m_ref.shape, NEG, F32)
        l_ref[...] = jnp.zeros(l_ref.shape, F32)
        acc_ref[...] = jnp.zeros(acc_ref.shape, F32)

    def update(k, v, nk, key0):
        hit = _dot(sel_ref[...].astype(BF16), exp_ref[:, :nk])
        kpos = key0 + lax.broadcasted_iota(jnp.int32, hit.shape, 1)
        qpos = past_len + (lax.broadcasted_iota(jnp.int32, hit.shape, 0) & (n_q - 1))
        madd = jnp.where((hit > 0.5) & (kpos <= qpos), 0.0, NEG)
        madd = jnp.concatenate(
            [madd[g * n_q:(g + 1) * n_q] for g in range(A_KV) for _ in range(A_HPG)], axis=0)
        s = _dot_nt(q_ref[...], k) + bias_ref[:, :nk] + madd
        m_old = m_ref[...]
        m_new = jnp.maximum(m_old, jnp.max(s, axis=-1, keepdims=True))
        alpha = jnp.exp2(m_old - m_new)
        p = jnp.exp2(s - m_new)
        l_ref[...] = alpha * l_ref[...] + jnp.sum(p, axis=-1, keepdims=True)
        acc_ref[...] = alpha * acc_ref[...] + _dot(p.astype(BF16), v)
        m_ref[...] = m_new

    @pl.when(s_id < n_steps)
    def _():
        by_slot = [jnp.swapaxes(pg[...], 0, 1) for pg in pages]

        def gather(c0):
            cols = [jnp.concatenate([x[c0 + g] for x in by_slot], axis=0) for g in range(A_KV)]
            return jnp.concatenate(cols, axis=1).astype(BF16)

        update(gather(0), gather(A_KV), n_pages * PAGE, s_id * (n_pages * PAGE))

    @pl.when(s_id == n_steps)
    def _():
        update(tail_ref[:, :kvw].astype(BF16), tail_ref[:, kvw:].astype(BF16), PAGE, past_len)
        o_ref[...] = _diag_blocks(acc_ref[...] / l_ref[...], rpg)


def _nsa_sample_mix_kernel(q_ref, win_ref, bias_ref, oc_ref, os_ref, z_ref, gt_ref, o_ref, *, n_q):
    kvw = A_KV * HD
    rows = q_ref.shape[0]
    rpg = A_HPG * n_q
    nk = win_ref.shape[0]
    s = _dot_nt(q_ref[...], win_ref[:, :kvw].astype(BF16)) + bias_ref[...]
    qi = lax.broadcasted_iota(jnp.int32, (rows, nk), 0) & (n_q - 1)
    ki = lax.broadcasted_iota(jnp.int32, (rows, nk), 1)
    dist = qi + WINDOW - ki
    s = s + jnp.where((dist >= 0) & (dist < WINDOW), 0.0, NEG)
    mx = jnp.max(s, axis=-1, keepdims=True)
    e = jnp.exp2(s - mx)
    p = e / jnp.sum(e, axis=-1, keepdims=True)
    ow = _diag_blocks(_dot(p.astype(BF16), win_ref[:, kvw:].astype(BF16)), rpg)
    gates = _sigmoid(gt_ref[...])
    width = A_HEADS * HD
    for hh in range(A_HEADS):
        r = slice(hh * n_q, (hh + 1) * n_q)
        c = slice(hh * HD, (hh + 1) * HD)
        val = None
        for br, o in enumerate((oc_ref, os_ref, ow)):
            zc = slice(br * width + hh * HD, br * width + (hh + 1) * HD)
            gcol = gates[:, br * A_HEADS + hh: br * A_HEADS + hh + 1]
            term = gcol * o[r, :] * z_ref[:, zc]
            val = term if val is None else val + term
        o_ref[:, c] = val.astype(o_ref.dtype)


def nsa_sample_attention(qbd, cmp_kv, cache4d, page_table, tail_sel, win_all, z_s, gl_s,
                         b_cmp, b_sel, b_win, msel, expand, past_len, n_q):
    nb, rows, kvw = qbd.shape
    n_cmp = cmp_kv.shape[1] // NCK
    n_tab = page_table.shape[1]
    n_pages = min(PAGES_PER_STEP, n_tab)
    n_steps = n_tab // n_pages
    bps = n_pages * PAGE // SEL_BLOCK
    n_sel = -(-(past_len + n_q) // SEL_BLOCK)
    k_top = min(SEL_TOPK, n_sel)
    gq = A_KV * n_q

    oc, selmask = pl.pallas_call(
        functools.partial(_nsa_sample_cmp_kernel, past_len=past_len, n_sel=n_sel, k_top=k_top,
                          n_q=n_q, blocks_per_step=bps),
        grid=(nb,),
        in_specs=[
            pl.BlockSpec((None, rows, kvw), lambda b: (b, 0, 0)),
            pl.BlockSpec((None, n_cmp * NCK, HD), lambda b: (b, 0, 0)),
            pl.BlockSpec(b_cmp.shape, lambda b: (0, 0)),
            pl.BlockSpec(msel.shape, lambda b: (0, 0)),
        ],
        out_specs=[
            pl.BlockSpec((None, rows, HD), lambda b: (b, 0, 0)),
            pl.BlockSpec((None, n_steps + 1, gq, bps), lambda b: (b, 0, 0, 0)),
        ],
        out_shape=[
            jax.ShapeDtypeStruct((nb, rows, HD), F32),
            jax.ShapeDtypeStruct((nb, n_steps + 1, gq, bps), F32),
        ],
        compiler_params=_cparams(("arbitrary",)),
        name="nsa_sample_cmp",
    )(qbd, cmp_kv, b_cmp, msel)

    half = 2 * kvw
    last = n_steps - 1

    def page_spec(k):
        return pl.BlockSpec(
            (None, PAGE, NCK, HD),
            lambda b, s, pt: (pt[b, jnp.minimum(s, last) * n_pages + k], 0, 1, 0))

    nkeys = n_pages * PAGE
    grid_spec = pltpu.PrefetchScalarGridSpec(
        num_scalar_prefetch=1,
        grid=(nb, n_steps + 1),
        in_specs=[page_spec(k) for k in range(n_pages)] + [
            pl.BlockSpec((None, rows, kvw), lambda b, s, pt: (b, 0, 0)),
            pl.BlockSpec((None, PAGE, half), lambda b, s, pt: (b, 0, 0)),
            pl.BlockSpec((rows, nkeys), lambda b, s, pt: (0, s)),
            pl.BlockSpec((None, None, gq, bps), lambda b, s, pt: (b, s, 0, 0)),
            pl.BlockSpec(expand.shape, lambda b, s, pt: (0, 0)),
        ],
        out_specs=pl.BlockSpec((None, rows, HD), lambda b, s, pt: (b, 0, 0)),
        scratch_shapes=[
            pltpu.VMEM((rows, 1), F32),
            pltpu.VMEM((rows, 1), F32),
            pltpu.VMEM((rows, kvw), F32),
        ],
    )
    osel = pl.pallas_call(
        functools.partial(_nsa_sample_sel_kernel, n_pages=n_pages, n_steps=n_steps,
                          past_len=past_len, n_q=n_q),
        grid_spec=grid_spec,
        out_shape=jax.ShapeDtypeStruct((nb, rows, HD), F32),
        compiler_params=_cparams(("arbitrary", "arbitrary")),
        name="nsa_sample_sel",
    )(page_table, *([cache4d] * n_pages), qbd, tail_sel, b_sel, selmask, expand)

    nkw = win_all.shape[1]
    zw = z_s.shape[1]
    return pl.pallas_call(
        functools.partial(_nsa_sample_mix_kernel, n_q=n_q),
        grid=(nb,),
        in_specs=[
            pl.BlockSpec((None, rows, kvw), lambda b: (b, 0, 0)),
            pl.BlockSpec((None, nkw, 2 * kvw), lambda b: (b, 0, 0)),
            pl.BlockSpec(b_win.shape, lambda b: (0, 0)),
            pl.BlockSpec((None, rows, HD), lambda b: (b, 0, 0)),
            pl.BlockSpec((None, rows, HD), lambda b: (b, 0, 0)),
            pl.BlockSpec((n_q, zw), lambda b: (b, 0)),
            pl.BlockSpec((n_q, 3 * A_HEADS), lambda b: (b, 0)),
        ],
        out_specs=pl.BlockSpec((n_q, A_HEADS * HD), lambda b: (b, 0)),
        out_shape=jax.ShapeDtypeStruct((nb * n_q, A_HEADS * HD), F32),
        compiler_params=_cparams(("arbitrary",)),
        name="nsa_sample_mix",
    )(qbd, win_all, b_win, oc, osel, z_s, gl_s)


def _retention_kernel(q_ref, k_ref, v_ref, g_ref, cos_ref, sin_ref, dm_ref, qd_ref, kd_ref, cd_ref,
                      s0_ref, o_ref, s_ref, *, dk, dv, hps):
    c = pl.program_id(2)
    half = dk // 2

    @pl.when(c == 0)
    def _():
        s_ref[...] = s0_ref[...]

    cos = cos_ref[...]
    sin = sin_ref[...]

    def rot(x):
        x1, x2 = x[:, :half], x[:, half:]
        return jnp.concatenate([x1 * cos - x2 * sin, x1 * sin + x2 * cos], axis=1)

    for j in range(hps):
        ks, vs = slice(j * dk, (j + 1) * dk), slice(j * dv, (j + 1) * dv)
        q = rot(q_ref[:, ks])
        k = rot(k_ref[:, ks]) * (dk ** -0.5)
        v = v_ref[:, vs].astype(BF16)
        qb = q.astype(BF16)
        att = _dot_nt(qb, k.astype(BF16)) * dm_ref[j]
        state = s_ref[j]
        o = _dot(att.astype(BF16), v) + _dot(qb, state.astype(BF16)) * qd_ref[j]
        s_ref[j] = state * cd_ref[j] + _dot_tn((k * kd_ref[j]).astype(BF16), v)
        o = o * lax.rsqrt(jnp.mean(o * o, axis=-1, keepdims=True) + EPS)
        o_ref[:, vs] = (o * _silu(g_ref[:, vs])).astype(o_ref.dtype)


def retention(proj, state0, q0, nb, t):
    _, nh, dk, dv = state0.shape
    cs = R_CHUNK if t % R_CHUNK == 0 else t
    n = t // cs
    half = dk // 2
    pos = (q0 + jnp.arange(t)).astype(F32)
    inv = jnp.power(ROPE_BASE, -jnp.arange(half, dtype=F32) / half)
    ang = pos[:, None] * inv[None, :]
    cos, sin = jnp.cos(ang), jnp.sin(ang)
    log_g = jnp.log1p(-jnp.exp2(-5.0 - jnp.arange(nh, dtype=F32)))
    ii = jnp.arange(cs, dtype=F32)
    rel = ii[:, None] - ii[None, :]
    causal = rel >= 0
    dmat = jnp.where(causal, jnp.exp(jnp.where(causal, rel, 0.0) * log_g[:, None, None]), 0.0)
    q_dec = jnp.exp((ii + 1.0) * log_g[:, None])[:, :, None]
    k_dec = jnp.exp((cs - 1.0 - ii) * log_g[:, None])[:, :, None]
    c_dec = jnp.exp(cs * log_g)[:, None, None]
    hps = RET_HEADS_PER_STEP
    assert nh % hps == 0
    ng = nh // hps
    vb0 = 2 * nh * dk // (hps * dv)
    return pl.pallas_call(
        functools.partial(_retention_kernel, dk=dk, dv=dv, hps=hps),
        grid=(nb, ng, n),
        in_specs=[
            pl.BlockSpec((cs, hps * dk), lambda b, h, c: (b * n + c, h)),
            pl.BlockSpec((cs, hps * dk), lambda b, h, c: (b * n + c, ng + h)),
            pl.BlockSpec((cs, hps * dv), lambda b, h, c: (b * n + c, vb0 + h)),
            pl.BlockSpec((cs, hps * dv), lambda b, h, c: (b * n + c, vb0 + ng + h)),
            pl.BlockSpec((cs, half), lambda b, h, c: (c, 0)),
            pl.BlockSpec((cs, half), lambda b, h, c: (c, 0)),
            pl.BlockSpec((hps, cs, cs), lambda b, h, c: (h, 0, 0)),
            pl.BlockSpec((hps, cs, 1), lambda b, h, c: (h, 0, 0)),
            pl.BlockSpec((hps, cs, 1), lambda b, h, c: (h, 0, 0)),
            pl.BlockSpec((hps, 1, 1), lambda b, h, c: (h, 0, 0)),
            pl.BlockSpec((None, hps, dk, dv), lambda b, h, c: (b, h, 0, 0)),
        ],
        out_specs=[
            pl.BlockSpec((cs, hps * dv), lambda b, h, c: (b * n + c, h)),
            pl.BlockSpec((None, hps, dk, dv), lambda b, h, c: (b, h, 0, 0)),
        ],
        out_shape=[
            jax.ShapeDtypeStruct((nb * t, nh * dv), BF16 if cs % 16 == 0 else F32),
            jax.ShapeDtypeStruct(state0.shape, F32),
        ],
        compiler_params=_cparams(("arbitrary", "arbitrary", "arbitrary")),
        name="retention",
    )(proj, proj, proj, proj, cos, sin, dmat, q_dec, k_dec, c_dec, state0)


def _t5_bucket_np(dist):
    n = np.maximum(dist, 0)
    exact = N_BUCKETS // 2
    logv = np.log(np.maximum(n, 1).astype(np.float32) / np.float32(exact)) / np.float32(
        math.log(MAX_DISTANCE / exact))
    large = np.minimum(exact + (logv * np.float32(N_BUCKETS - exact)).astype(np.int32), N_BUCKETS - 1)
    return np.where(n < exact, n, large).astype(np.int32)


def _bucket_starts():
    bk = _t5_bucket_np(np.arange(4 * MAX_DISTANCE))
    assert np.all(np.diff(bk) >= 0) and bk[-1] == N_BUCKETS - 1
    return [int(np.argmax(bk >= b)) for b in range(N_BUCKETS)]


_BUCKET_START = _bucket_starts()


def _sel_matrix(n_cmp, n_cmp_pad, n_sel_pad):
    c = np.arange(n_cmp_pad)[:, None]
    j = np.arange(n_sel_pad)[None, :]
    a = (c >= SEL_RATIO * j) & (c < SEL_RATIO * j + SEL_RATIO)
    b = (c >= SEL_RATIO * j - 1) & (c < SEL_RATIO * j + SEL_RATIO - 1)
    m = (a.astype(np.float32) + b.astype(np.float32)) * (c < n_cmp)
    return jnp.asarray(m, BF16)


def _expand_matrix(n_blk_pad, n_keys):
    blk = np.arange(n_blk_pad)[:, None]
    key = np.arange(n_keys)[None, :]
    return jnp.asarray((key // SEL_BLOCK == blk).astype(np.float32), BF16)


def _w1_pairs(w1):
    w = w1.reshape(2, 2, CMP_STRIDE // 2, 2, HD, HD)
    w = jnp.transpose(w, (2, 3, 4, 0, 1, 5))
    return w.reshape(CMP_STRIDE // 2, 2 * HD, 4 * HD).astype(BF16)


def kernel(x_prompt, x_sample, c_prompt, c_sample, cache_nsa_kv, cache_nsa_win, state_ret, page_table,
           norm_g, ada_w, ada_b, rel_bias, a_w_in, a_w_out, a_cmp_pe, a_cmp_w1, a_cmp_w2, a_qk_g,
           r_w_in, r_w_out):
    nbp, t, d = x_prompt.shape
    nbs, ts, _ = x_sample.shape
    n_tab = page_table.shape[1]
    past_len = n_tab * PAGE
    kvw = A_KV * HD
    qw = A_HEADS * HD
    assert t % (2 * LANE) == 0 and t >= WINDOW and ts == 8 and past_len >= WINDOW
    assert cache_nsa_win.shape[2] == WINDOW

    n_c = nbp + nbs
    pad_c = (-n_c) % 8
    c_all = jnp.concatenate([c_prompt, c_sample, jnp.zeros((pad_c, d), F32)], axis=0)
    mod = ada_modulation(c_all, ada_w, ada_b).reshape(ada_w.shape[0], n_c + pad_c, 3, d)

    def mods(layer):
        m = mod[layer]
        return (m[:nbp, 0], m[:nbp, 1], m[:nbp, 2]), (m[nbp:n_c, 0], m[nbp:n_c, 1], m[nbp:n_c, 2])

    (sh_p, sc_p, gt_p), (sh_s, sc_s, gt_s) = mods(0)
    hp = norm_modulate(x_prompt, norm_g[0], sc_p, sh_p)
    hs = norm_modulate(x_sample, norm_g[0], sc_s, sh_s)
    w_in_t = jnp.swapaxes(a_w_in[0], 0, 1)
    n_main = w_in_t.shape[0] - 3 * A_HEADS
    n_qkv = qw + 6 * kvw
    proj_p, proj_s = project(hp, hs, w_in_t, n_cols=n_qkv, w_is_nk=True)
    zt_p, z_s = project_gate_t(hp, hs, w_in_t, n_qkv, n_main - n_qkv)
    w_gate = jnp.pad(w_in_t[n_main:], ((0, LANE - 3 * A_HEADS), (0, 0)))
    gl_p, gl_s = project(hp, hs, w_gate, w_is_nk=True)
    gl_p, gl_s = gl_p[:, :3 * A_HEADS], gl_s[:, :3 * A_HEADS]
    qk_g = a_qk_g[0]

    qn_p, rows_p, win_p, rows_t_p, win_t_p = nsa_prep(proj_p, qk_g)
    qn_s, rows_s, win_s, rows_t_s, _ = nsa_prep(proj_s, qk_g)

    w_pairs = _w1_pairs(a_cmp_w1[0])
    pe = a_cmp_pe[0]
    w1 = a_cmp_w1[0].reshape(2, CMP_LEN, HD, HD)
    w2 = a_cmp_w2[0]
    cpp = PAGE // CMP_STRIDE

    pages_p = t // PAGE
    ident = jnp.arange(nbp * pages_p, dtype=jnp.int32).reshape(nbp, pages_p)
    ab_p = compress_stage_a(rows_t_p.reshape(nbp * pages_p, PAGE, 4 * A_KV, HD), ident, w_pairs,
                            min(PAGES_PER_STEP, pages_p))
    zero_tail = jnp.zeros((nbp, cpp, NCK, 2 * HD), F32)
    cmp_p = compress_stage_b(ab_p, zero_tail, pe, w1, w2, qk_g)

    nq = t // Q_BLOCK
    n_ch_p = t // CMP_STRIDE
    tsel, twin, tcmp = prompt_bias_tables(rel_bias, nq, n_ch_p)
    n_cmp_p = t // CMP_STRIDE - 1
    n_sel_p = -(-t // SEL_BLOCK)
    n_sel_pad_p = -(-n_sel_p // 16) * 16
    msel_p = _sel_matrix(n_cmp_p, n_ch_p, n_sel_pad_p).T
    expand_p = _expand_matrix(n_sel_pad_p, t).T
    gates_g = jnp.transpose(gl_p.reshape(nbp * t, 3, A_KV, A_HPG), (2, 1, 3, 0)).reshape(A_KV, 3 * A_HPG, nbp * t)
    mixed_p = nsa_prompt_attention(qn_p, cmp_p, rows_p, win_p, zt_p, gates_g,
                                   tsel, twin, tcmp, msel_p, expand_p, nbp, t)

    n_pool = cache_nsa_kv.shape[1]
    cache4d = cache_nsa_kv[0].reshape(n_pool, PAGE, 4 * A_KV, HD)
    n_pages = min(PAGES_PER_STEP, n_tab)
    ab_s = compress_stage_a(cache4d, page_table, w_pairs, n_pages)
    tail_rows = jnp.pad(rows_s.reshape(nbs, ts, 4 * kvw), ((0, 0), (0, PAGE - ts), (0, 0)))
    ident_s = jnp.arange(nbs, dtype=jnp.int32).reshape(nbs, 1)
    ab_tail = compress_stage_a(tail_rows.reshape(nbs, PAGE, 4 * A_KV, HD), ident_s, w_pairs, 1)
    cmp_s = compress_stage_b(ab_s, ab_tail, pe, w1, w2, qk_g)

    n_cmp_s = cmp_s.shape[1] // NCK
    n_steps = n_tab // n_pages
    bps = n_pages * PAGE // SEL_BLOCK
    n_sel_pad = (n_steps + 1) * bps
    nkeys = (n_steps + 1) * n_pages * PAGE
    nkw = WINDOW + LANE
    b_cmp, b_sel, b_win = sample_bias_rows(rel_bias, past_len, ts, n_cmp_s, nkeys, nkw)
    msel_s = _sel_matrix(n_cmp_s, n_cmp_s, n_sel_pad)
    expand_s = _expand_matrix(bps, n_pages * PAGE)
    q5 = qn_s.reshape(nbs, ts, A_KV, A_HPG, HD)
    eye = jnp.eye(A_KV, dtype=BF16)
    qbd = jnp.einsum('bqghd,ge->bghqed', q5, eye).reshape(nbs, A_HEADS * ts, kvw)
    win_new = win_s.reshape(nbs, ts, 2 * kvw)
    win_cache = cache_nsa_win[0].reshape(nbs, WINDOW, 2 * kvw)
    win_all = jnp.concatenate([win_cache, win_new], axis=1)
    win_pad = jnp.pad(win_all, ((0, 0), (0, nkw - WINDOW - ts), (0, 0)))
    tail_sel = tail_rows[:, :, 2 * kvw:]
    mixed_s = nsa_sample_attention(qbd, cmp_s, cache4d, page_table, tail_sel, win_pad, z_s, gl_s,
                                   b_cmp, b_sel, b_win, msel_s, expand_s, past_len, ts).astype(BF16)

    xs_flat = x_sample.reshape(nbs * ts, d)
    gate_s_rows = jnp.repeat(gt_s, ts, axis=0)
    x1p, x1s = project_residual(mixed_p, mixed_s, a_w_out[0], x_prompt.reshape(nbp * t, d), gt_p,
                                xs_flat, gate_s_rows, t)

    (sh_p, sc_p, gt_p), (sh_s, sc_s, gt_s) = mods(1)
    hp = norm_modulate(x1p.reshape(nbp, t, d), norm_g[1], sc_p, sh_p)
    hs = norm_modulate(x1s.reshape(nbs, ts, d), norm_g[1], sc_s, sh_s)
    rp, rs = project(hp, hs, r_w_in[0])
    s0 = jnp.zeros((nbp,) + state_ret.shape[2:], F32)
    op, ret_p = retention(rp, s0, 0, nbp, t)
    os_, ret_s = retention(rs, state_ret[0], past_len, nbs, ts)
    os_ = os_.astype(BF16)
    gate_s_rows = jnp.repeat(gt_s, ts, axis=0)
    x2p, x2s = project_residual(op, os_, r_w_out[0], x1p, gt_p, x1s, gate_s_rows, t)

    kv_p = rows_t_p.reshape(1, nbp, t, 4, A_KV, HD)
    kv_s = rows_t_s.reshape(1, nbs, ts, 4, A_KV, HD)
    wst_p = win_t_p.reshape(nbp, t, 2, A_KV, HD)[None, :, t - WINDOW:]
    wst_s = win_all[:, ts:].reshape(1, nbs, WINDOW, 2, A_KV, HD)
    return (x2p.reshape(nbp, t, d), x2s.reshape(nbs, ts, d), kv_p, kv_s, wst_p, wst_s,
            ret_p[None], ret_s[None])
```

```python
import functools
import math

import numpy as np
import jax
import jax.numpy as jnp
from jax import lax
from jax.experimental import pallas as pl
from jax.experimental.pallas import tpu as pltpu

EPS = 1e-6
HD = 128
A_KV = 4
A_HPG = 8
A_HEADS = A_KV * A_HPG
CMP_STRIDE = 16
CMP_LEN = 32
SEL_BLOCK = 64
SEL_RATIO = SEL_BLOCK // CMP_STRIDE
SEL_TOPK = 16
WINDOW = 512
Q_BLOCK = 128
FORCE_SCORE = 1e4
NEG = -1e30
TINY = 1e-30
N_BUCKETS = 32
MAX_DISTANCE = 1024
R_CHUNK = 128
ROPE_BASE = 10000.0
LOG2E = math.log2(math.e)
Q_SCALE = HD ** -0.5 * LOG2E
PAGE = 128
PAGES_PER_STEP = 16
RET_HEADS_PER_STEP = 8
SEL_KEY_TILE = 512

LANE = 128
VMEM_LIMIT = 60 * 1024 * 1024

F32 = jnp.float32
BF16 = jnp.bfloat16


def _cparams(sem):
    return pltpu.CompilerParams(dimension_semantics=sem, vmem_limit_bytes=VMEM_LIMIT)


def _dot(a, b):
    return jnp.dot(a, b, preferred_element_type=F32)


def _dot_nt(a, b):
    return lax.dot_general(a, b, (((1,), (1,)), ((), ())), preferred_element_type=F32)


def _dot_tn(a, b):
    return lax.dot_general(a, b, (((0,), (0,)), ((), ())), preferred_element_type=F32)


def _silu(x):
    h = 0.5 * x
    return h + h * jnp.tanh(h)


def _sigmoid(x):
    return 0.5 + 0.5 * jnp.tanh(0.5 * x)


def _split3_dot(x, m_bf16):
    hi = x.astype(BF16)
    r1 = x - hi.astype(F32)
    mid = r1.astype(BF16)
    lo = (r1 - mid.astype(F32)).astype(BF16)
    return _dot(hi, m_bf16) + _dot(mid, m_bf16) + _dot(lo, m_bf16)


def _ada_kernel(c_ref, w_ref, b_ref, o_ref):
    a = _silu(c_ref[...]).astype(BF16)
    o_ref[...] = _dot(a, w_ref[...].astype(BF16)) + b_ref[...]


def ada_modulation(c_all, ada_w, ada_b):
    depth, d, n = ada_w.shape
    rows = c_all.shape[0]
    tn = 512
    return pl.pallas_call(
        _ada_kernel,
        grid=(depth, n // tn),
        in_specs=[
            pl.BlockSpec((rows, d), lambda l, j: (0, 0)),
            pl.BlockSpec((None, d, tn), lambda l, j: (l, 0, j)),
            pl.BlockSpec((None, 1, tn), lambda l, j: (l, 0, j)),
        ],
        out_specs=pl.BlockSpec((None, rows, tn), lambda l, j: (l, 0, j)),
        out_shape=jax.ShapeDtypeStruct((depth, rows, n), F32),
        compiler_params=_cparams(("arbitrary", "arbitrary")),
        name="ada_modulation",
    )(c_all, ada_w, ada_b.reshape(depth, 1, n))


def _norm_mod_kernel(x_ref, g_ref, sc_ref, sh_ref, o_ref):
    x = x_ref[...]
    y = x * lax.rsqrt(jnp.mean(x * x, axis=-1, keepdims=True) + EPS)
    y = y * g_ref[...]
    o_ref[...] = (y * (1.0 + sc_ref[...]) + sh_ref[...]).astype(o_ref.dtype)


def norm_modulate(x, g, scale, shift):
    b, t, d = x.shape
    tt = min(t, 256)
    out = pl.pallas_call(
        _norm_mod_kernel,
        grid=(b, t // tt),
        in_specs=[
            pl.BlockSpec((None, tt, d), lambda i, j: (i, j, 0)),
            pl.BlockSpec((1, d), lambda i, j: (0, 0)),
            pl.BlockSpec((None, 1, d), lambda i, j: (i, 0, 0)),
            pl.BlockSpec((None, 1, d), lambda i, j: (i, 0, 0)),
        ],
        out_specs=pl.BlockSpec((None, tt, d), lambda i, j: (i, j, 0)),
        out_shape=jax.ShapeDtypeStruct((b, t, d), BF16),
        compiler_params=_cparams(("arbitrary", "arbitrary")),
        name="norm_modulate",
    )(x, g.reshape(1, d), scale.reshape(b, 1, d), shift.reshape(b, 1, d))
    return out.reshape(b * t, d)


def _stage_weight_tile(w_hbm, wf_ref, wb_ref, sem, *, w_is_nk, tn, j0):
    j = pl.program_id(0)

    def copy(jj):
        start = pl.multiple_of((j0 + jj) * tn, tn)
        src = w_hbm.at[pl.ds(start, tn), :] if w_is_nk else w_hbm.at[:, pl.ds(start, tn)]
        return pltpu.make_async_copy(src, wf_ref, sem)

    @pl.when(j == 0)
    def _():
        copy(0).start()

    copy(j).wait()
    wb_ref[...] = wf_ref[...].astype(BF16)

    @pl.when(j + 1 < pl.num_programs(0))
    def _():
        copy(j + 1).start()


def _proj_kernel(xp_ref, xs_ref, w_hbm, op_ref, os_ref, wf_ref, wb_ref, sem, *, stage, mm):
    @pl.when(pl.program_id(1) == 0)
    def _():
        stage(w_hbm, wf_ref, wb_ref, sem)
        os_ref[...] = mm(xs_ref[...], wb_ref[...])

    op_ref[...] = mm(xp_ref[...], wb_ref[...])


def _proj_gate_t_kernel(xp_ref, xs_ref, w_hbm, opt_ref, os_ref, wf_ref, wb_ref, sem, *, stage):
    @pl.when(pl.program_id(1) == 0)
    def _():
        stage(w_hbm, wf_ref, wb_ref, sem)
        os_ref[...] = _silu(_dot_nt(xs_ref[...], wb_ref[...]))

    opt_ref[...] = _silu(_dot_nt(wb_ref[...], xp_ref[...]))


def _weight_scratch(shape):
    return [pltpu.VMEM(shape, F32), pltpu.VMEM(shape, BF16), pltpu.SemaphoreType.DMA(())]


def project_gate_t(xp, xs, w_nk, row0, n_rows):
    mp, k = xp.shape
    ms = xs.shape[0]
    tm, tn = _proj_tiles(mp, k, n_rows)
    assert row0 % tn == 0
    stage = functools.partial(_stage_weight_tile, w_is_nk=True, tn=tn, j0=row0 // tn)
    return pl.pallas_call(
        functools.partial(_proj_gate_t_kernel, stage=stage),
        grid=(n_rows // tn, mp // tm),
        in_specs=[
            pl.BlockSpec((tm, k), lambda j, i: (i, 0)),
            pl.BlockSpec((ms, k), lambda j, i: (0, 0)),
            pl.BlockSpec(memory_space=pl.ANY),
        ],
        out_specs=[
            pl.BlockSpec((tn, tm), lambda j, i: (j, i)),
            pl.BlockSpec((ms, tn), lambda j, i: (0, j)),
        ],
        out_shape=[jax.ShapeDtypeStruct((n_rows, mp), F32), jax.ShapeDtypeStruct((ms, n_rows), F32)],
        scratch_shapes=_weight_scratch((tn, k)),
        compiler_params=_cparams(("arbitrary", "arbitrary")),
        name="project_gate_t",
    )(xp, xs, w_nk)


def _proj_res_kernel(xp_ref, xs_ref, w_hbm, rp_ref, gp_ref, rs_ref, gs_ref, op_ref, os_ref,
                     wf_ref, wb_ref, sem, *, stage):
    @pl.when(pl.program_id(1) == 0)
    def _():
        stage(w_hbm, wf_ref, wb_ref, sem)
        os_ref[...] = rs_ref[...] + gs_ref[...] * _dot(xs_ref[...], wb_ref[...])

    op_ref[...] = rp_ref[...] + gp_ref[...] * _dot(xp_ref[...], wb_ref[...])


def _proj_tiles(mp, k, n, residual=False):
    if k <= 4096:
        tm, tn = (1024, 512) if residual else (1024, 1024)
    else:
        tm, tn = 512, 512
    tm = min(tm, mp)
    while mp % tm:
        tm //= 2
    tn = min(tn, n)
    while n % tn:
        tn //= 2
    return tm, tn


def project(xp, xs, w, n_cols=None, w_is_nk=False):
    mp, k = xp.shape
    ms = xs.shape[0]
    n = w.shape[0 if w_is_nk else 1] if n_cols is None else n_cols
    tm, tn = _proj_tiles(mp, k, n)
    stage = functools.partial(_stage_weight_tile, w_is_nk=w_is_nk, tn=tn, j0=0)
    return pl.pallas_call(
        functools.partial(_proj_kernel, stage=stage, mm=_dot_nt if w_is_nk else _dot),
        grid=(n // tn, mp // tm),
        in_specs=[
            pl.BlockSpec((tm, k), lambda j, i: (i, 0)),
            pl.BlockSpec((ms, k), lambda j, i: (0, 0)),
            pl.BlockSpec(memory_space=pl.ANY),
        ],
        out_specs=[
            pl.BlockSpec((tm, tn), lambda j, i: (i, j)),
            pl.BlockSpec((ms, tn), lambda j, i: (0, j)),
        ],
        out_shape=[jax.ShapeDtypeStruct((mp, n), F32), jax.ShapeDtypeStruct((ms, n), F32)],
        scratch_shapes=_weight_scratch((tn, k) if w_is_nk else (k, tn)),
        compiler_params=_cparams(("arbitrary", "arbitrary")),
        name="project",
    )(xp, xs, w)


def project_residual(xp, xs, w, res_p, gate_p, res_s, gate_s, rows_per_batch):
    mp, k = xp.shape
    ms = xs.shape[0]
    n = w.shape[1]
    tm, tn = _proj_tiles(mp, k, n, residual=True)
    tm = min(tm, rows_per_batch)
    tpb = rows_per_batch // tm
    nb = gate_p.shape[0]
    stage = functools.partial(_stage_weight_tile, w_is_nk=False, tn=tn, j0=0)
    return pl.pallas_call(
        functools.partial(_proj_res_kernel, stage=stage),
        grid=(n // tn, mp // tm),
        in_specs=[
            pl.BlockSpec((tm, k), lambda j, i: (i, 0)),
            pl.BlockSpec((ms, k), lambda j, i: (0, 0)),
            pl.BlockSpec(memory_space=pl.ANY),
            pl.BlockSpec((tm, tn), lambda j, i: (i, j)),
            pl.BlockSpec((None, 1, tn), lambda j, i: (i // tpb, 0, j)),
            pl.BlockSpec((ms, tn), lambda j, i: (0, j)),
            pl.BlockSpec((ms, tn), lambda j, i: (0, j)),
        ],
        out_specs=[
            pl.BlockSpec((tm, tn), lambda j, i: (i, j)),
            pl.BlockSpec((ms, tn), lambda j, i: (0, j)),
        ],
        out_shape=[jax.ShapeDtypeStruct((mp, n), F32), jax.ShapeDtypeStruct((ms, n), F32)],
        scratch_shapes=_weight_scratch((k, tn)),
        compiler_params=_cparams(("arbitrary", "arbitrary")),
        name="project_residual",
    )(xp, xs, w, res_p, gate_p.reshape(nb, 1, n), res_s, gate_s)


def _nsa_prep_kernel(q_ref, a_ref, b_ref, c_ref, g_ref, qn_ref, rows_ref, win_ref, rows_t_ref, win_t_ref):
    def hnorm(x, g):
        return x * lax.rsqrt(jnp.mean(x * x, axis=-1, keepdims=True) + EPS) * g

    g = g_ref[...]
    for j in range(A_HEADS):
        sl = slice(j * HD, (j + 1) * HD)
        qn_ref[:, sl] = (hnorm(q_ref[:, sl], g[0:1]) * Q_SCALE).astype(qn_ref.dtype)
    kvw = A_KV * HD
    sls = [slice(j * HD, (j + 1) * HD) for j in range(2 * A_KV)]
    row_slots = ([a_ref[:, s] for s in sls] + [hnorm(b_ref[:, s], g[2:3]) for s in sls[:A_KV]]
                 + [b_ref[:, s] for s in sls[A_KV:]])
    win_slots = [hnorm(c_ref[:, s], g[3:4]) for s in sls[:A_KV]] + [c_ref[:, s] for s in sls[A_KV:]]
    for j, x in enumerate(row_slots):
        rows_ref[:, j * HD:(j + 1) * HD] = x
    for j, x in enumerate(win_slots):
        win_ref[:, j * HD:(j + 1) * HD] = x
    rows_t_ref[...] = jnp.swapaxes(jnp.stack(row_slots, axis=0), 0, 1)
    win_t_ref[...] = jnp.swapaxes(jnp.stack(win_slots, axis=0), 0, 1)


def nsa_prep(proj, qk_g):
    m = proj.shape[0]
    tm = min(m, 256)
    qw = A_HEADS * HD
    kv2 = 2 * A_KV * HD
    base = qw // kv2
    return pl.pallas_call(
        _nsa_prep_kernel,
        grid=(m // tm,),
        in_specs=[
            pl.BlockSpec((tm, qw), lambda i: (i, 0)),
            pl.BlockSpec((tm, kv2), lambda i: (i, base)),
            pl.BlockSpec((tm, kv2), lambda i: (i, base + 1)),
            pl.BlockSpec((tm, kv2), lambda i: (i, base + 2)),
            pl.BlockSpec((4, HD), lambda i: (0, 0)),
        ],
        out_specs=[
            pl.BlockSpec((tm, qw), lambda i: (i, 0)),
            pl.BlockSpec((tm, 2 * kv2), lambda i: (i, 0)),
            pl.BlockSpec((tm, kv2), lambda i: (i, 0)),
            pl.BlockSpec((tm, 4 * A_KV, HD), lambda i: (i, 0, 0)),
            pl.BlockSpec((tm, 2 * A_KV, HD), lambda i: (i, 0, 0)),
        ],
        out_shape=[
            jax.ShapeDtypeStruct((m, qw), BF16),
            jax.ShapeDtypeStruct((m, 2 * kv2), F32),
            jax.ShapeDtypeStruct((m, kv2), F32),
            jax.ShapeDtypeStruct((m, 4 * A_KV, HD), F32),
            jax.ShapeDtypeStruct((m, 2 * A_KV, HD), F32),
        ],
        compiler_params=_cparams(("arbitrary",)),
        name="nsa_prep",
    )(proj, proj, proj, proj, qk_g)


NCK = 2 * A_KV


def _cmp_a_kernel(pt_ref, *refs, n_pages):
    del pt_ref
    pages = refs[:n_pages]
    w_ref = refs[n_pages]
    o_ref = refs[n_pages + 1]
    cpp = PAGE // CMP_STRIDE
    n_ch = n_pages * cpp

    def halves(tiles, lo):
        return [jnp.concatenate([tiles[a][lo:lo + A_KV], tiles[a + 1][lo:lo + A_KV]], axis=0)
                for a in range(0, len(tiles), 2)]

    acc = [None, None]
    for tp in range(CMP_STRIDE // 2):
        pieces = []
        for pg in pages:
            for ch in range(cpp):
                t0 = ch * CMP_STRIDE + 2 * tp
                pieces.append(jnp.concatenate([pg[t0], pg[t0 + 1]], axis=1))
        for kind in range(2):
            lhs = jnp.concatenate(halves(pieces, kind * A_KV), axis=0).astype(BF16)
            d = _dot(lhs, w_ref[tp, :, kind * 2 * HD:(kind + 1) * 2 * HD])
            acc[kind] = d if acc[kind] is None else acc[kind] + d
    for a in range(0, n_ch, 2):
        r = slice(a * A_KV, (a + 2) * A_KV)
        kk, vv = acc[0][r], acc[1][r]
        o_ref[a] = jnp.concatenate([kk[:A_KV], vv[:A_KV]], axis=0)
        o_ref[a + 1] = jnp.concatenate([kk[A_KV:], vv[A_KV:]], axis=0)


def compress_stage_a(store, page_table, w_pairs, n_pages):
    nb, n_tab = page_table.shape
    steps = n_tab // n_pages
    cpp = PAGE // CMP_STRIDE

    def page_spec(k):
        return pl.BlockSpec((None, PAGE, NCK, HD), lambda b, s, pt: (pt[b, s * n_pages + k], 0, 0, 0))

    grid_spec = pltpu.PrefetchScalarGridSpec(
        num_scalar_prefetch=1,
        grid=(nb, steps),
        in_specs=[page_spec(k) for k in range(n_pages)]
        + [pl.BlockSpec(w_pairs.shape, lambda b, s, pt: (0, 0, 0))],
        out_specs=pl.BlockSpec((None, n_pages * cpp, NCK, 2 * HD), lambda b, s, pt: (b, s, 0, 0)),
    )
    return pl.pallas_call(
        functools.partial(_cmp_a_kernel, n_pages=n_pages),
        grid_spec=grid_spec,
        out_shape=jax.ShapeDtypeStruct((nb, n_tab * cpp, NCK, 2 * HD), F32),
        compiler_params=_cparams(("arbitrary", "arbitrary")),
        name="compress_stage_a",
    )(page_table, *([store] * n_pages), w_pairs)


def _cmp_b_kernel(ab_ref, nx_ref, tail_ref, pe_ref, w1_ref, w2_ref, g_ref, o_ref, const_ref):
    j = pl.program_id(1)
    nch = ab_ref.shape[0]
    rows = nch * NCK

    @pl.when((pl.program_id(0) == 0) & (j == 0))
    def _():
        for kind in range(2):
            const = jnp.zeros((8, HD), F32)
            for t in range(CMP_LEN):
                row = jnp.broadcast_to(pe_ref[kind, t:t + 1, :], (8, HD)).astype(BF16)
                const = const + _dot(row, w1_ref[kind, t].astype(BF16))
            const_ref[kind * A_KV:(kind + 1) * A_KV, :] = const[:A_KV]

    ab = ab_ref[...].reshape(rows, 2 * HD)
    nxt = pltpu.roll(ab[:, HD:], rows - NCK, 0).reshape(nch, NCK, HD)
    first_next = jnp.where(j == pl.num_programs(1) - 1, tail_ref[0, :, HD:], nx_ref[0, :, HD:])
    cid = lax.broadcasted_iota(jnp.int32, (nch, NCK, HD), 0)
    nxt = jnp.where(cid == nch - 1, first_next[None], nxt)
    pre = ab[:, :HD].reshape(nch, NCK, HD) + nxt + const_ref[...][None]
    act = _silu(pre).reshape(rows, HD).astype(BF16)
    w2 = jnp.concatenate([w2_ref[0], w2_ref[1]], axis=1).astype(BF16)
    y2 = _dot(act, w2)
    is_k = (lax.broadcasted_iota(jnp.int32, (rows, HD), 0) & (NCK - 1)) < A_KV
    y = jnp.where(is_k, y2[:, :HD], y2[:, HD:])
    yn = y * lax.rsqrt(jnp.mean(y * y, axis=-1, keepdims=True) + EPS) * g_ref[1:2]
    o_ref[...] = jnp.where(is_k, yn, y).reshape(nch, NCK, HD)


def compress_stage_b(ab, tail, pe, w1, w2, qk_g):
    nb, n_ch = ab.shape[:2]
    tc = min(256, n_ch)
    assert n_ch % tc == 0
    nt = n_ch // tc
    out = pl.pallas_call(
        _cmp_b_kernel,
        grid=(nb, nt),
        in_specs=[
            pl.BlockSpec((None, tc, NCK, 2 * HD), lambda b, j: (b, j, 0, 0)),
            pl.BlockSpec((None, 1, NCK, 2 * HD), lambda b, j: (b, jnp.minimum((j + 1) * tc, n_ch - 1), 0, 0)),
            pl.BlockSpec((None, 1, NCK, 2 * HD), lambda b, j: (b, 0, 0, 0)),
            pl.BlockSpec(pe.shape, lambda b, j: (0, 0, 0)),
            pl.BlockSpec(w1.shape, lambda b, j: (0, 0, 0, 0)),
            pl.BlockSpec(w2.shape, lambda b, j: (0, 0, 0)),
            pl.BlockSpec((4, HD), lambda b, j: (0, 0)),
        ],
        out_specs=pl.BlockSpec((None, tc, NCK, HD), lambda b, j: (b, j, 0, 0)),
        out_shape=jax.ShapeDtypeStruct((nb, n_ch, NCK, HD), F32),
        scratch_shapes=[pltpu.VMEM((NCK, HD), F32)],
        compiler_params=_cparams(("arbitrary", "arbitrary")),
        name="compress_stage_b",
    )(ab, ab, tail, pe, w1, w2, qk_g)
    return out.reshape(nb, n_ch * NCK, HD)


def _cmp_rows(cmp_ref, slot, n):
    return cmp_ref[pl.ds(slot, n, stride=NCK), :]


def _select_topk(score, blk, k_top):
    sel = jnp.zeros(score.shape, F32)
    for _ in range(k_top):
        m = jnp.max(score, axis=-1, keepdims=True)
        cand = jnp.where(score == m, blk, jnp.int32(1 << 30))
        first = jnp.min(cand, axis=-1, keepdims=True)
        pick = blk == first
        sel = jnp.where(pick, 1.0, sel)
        score = jnp.where(pick, -jnp.inf, score)
    return sel


def _select_topk_t(score, blk, k_top, n_blk):
    rank = jnp.zeros(score.shape, F32)
    for i in range(n_blk):
        row = score[i:i + 1, :]
        tie = jnp.where(blk > i, 1.0, 0.0)
        rank = rank + jnp.where(row > score, 1.0, jnp.where(row == score, tie, 0.0))
    return jnp.where(rank < k_top, 1.0, 0.0)


def _bias_select(dist, dmin, dmax, value_of):
    lo = int(_t5_bucket_np(np.array([max(dmin, 0)]))[0])
    hi = int(_t5_bucket_np(np.array([max(dmax, 0)]))[0])
    out = jnp.full(dist.shape, value_of(lo), F32)
    for b in range(lo + 1, hi + 1):
        out = jnp.where(dist >= _BUCKET_START[b], value_of(b), out)
    return out * LOG2E


def _bias_tables_kernel(rb_ref, tsel_ref, twin_ref, tcmp_ref):
    head = pl.program_id(0)
    value_of = lambda b: rb_ref[b, head]
    n_off, nq, nc = tsel_ref.shape[0], tcmp_ref.shape[0], tcmp_ref.shape[1]
    kl = lax.broadcasted_iota(jnp.int32, (LANE, Q_BLOCK), 0)
    ql = lax.broadcasted_iota(jnp.int32, (LANE, Q_BLOCK), 1)
    for o in range(n_off):
        base = (o - 1) * Q_BLOCK
        dist = base + ql - kl
        tile = _bias_select(dist, base - (LANE - 1), base + Q_BLOCK - 1, value_of)
        tsel_ref[o] = tile
        if 1 <= o <= twin_ref.shape[0]:
            twin_ref[o - 1] = tile + jnp.where((dist >= 0) & (dist < WINDOW), 0.0, NEG)
    cend = lax.broadcasted_iota(jnp.int32, (nc, Q_BLOCK), 0) * CMP_STRIDE + (CMP_LEN - 1)
    qc = lax.broadcasted_iota(jnp.int32, (nc, Q_BLOCK), 1)
    for i in range(nq):
        base = i * Q_BLOCK
        tcmp_ref[i] = _bias_select(base + qc - cend, base - ((nc - 1) * CMP_STRIDE + CMP_LEN - 1),
                                   base + Q_BLOCK - 1, value_of)


def prompt_bias_tables(rel_bias, nq, n_cmp):
    n_win = min(WINDOW // Q_BLOCK + 1, nq)
    return pl.pallas_call(
        _bias_tables_kernel,
        grid=(A_HEADS,),
        in_specs=[pl.BlockSpec(memory_space=pltpu.SMEM)],
        out_specs=[
            pl.BlockSpec((None, nq + 1, LANE, Q_BLOCK), lambda h: (h // A_HPG, 0, 0, h % A_HPG)),
            pl.BlockSpec((None, n_win, LANE, Q_BLOCK), lambda h: (h // A_HPG, 0, 0, h % A_HPG)),
            pl.BlockSpec((None, nq, n_cmp, Q_BLOCK), lambda h: (h // A_HPG, 0, 0, h % A_HPG)),
        ],
        out_shape=[
            jax.ShapeDtypeStruct((A_KV, nq + 1, LANE, A_HPG * Q_BLOCK), F32),
            jax.ShapeDtypeStruct((A_KV, n_win, LANE, A_HPG * Q_BLOCK), F32),
            jax.ShapeDtypeStruct((A_KV, nq, n_cmp, A_HPG * Q_BLOCK), F32),
        ],
        compiler_params=_cparams(("arbitrary",)),
        name="prompt_bias_tables",
    )(rel_bias)


def _bias_rows_kernel(rb_ref, cmp_ref, sel_ref, win_ref, *, past_len, n_q):
    head = pl.program_id(0)
    value_of = lambda b: rb_ref[b, head]

    def fill(ref, key_pos, kmin, kmax, q0):
        q = lax.broadcasted_iota(jnp.int32, ref.shape, 0)
        k = lax.broadcasted_iota(jnp.int32, ref.shape, 1)
        ref[...] = _bias_select(q0 + q - key_pos(k), q0 - kmax, q0 + n_q - 1 - kmin, value_of)

    nc, nk, nw = cmp_ref.shape[1], sel_ref.shape[1], win_ref.shape[1]
    fill(cmp_ref, lambda k: k * CMP_STRIDE + (CMP_LEN - 1), CMP_LEN - 1, (nc - 1) * CMP_STRIDE + CMP_LEN - 1, past_len)
    fill(sel_ref, lambda k: k, 0, nk - 1, past_len)
    fill(win_ref, lambda k: k, 0, nw - 1, WINDOW)


def sample_bias_rows(rel_bias, past_len, n_q, n_cmp, n_keys, n_win):
    shapes = [(A_HEADS * n_q, n) for n in (n_cmp, n_keys, n_win)]
    return pl.pallas_call(
        functools.partial(_bias_rows_kernel, past_len=past_len, n_q=n_q),
        grid=(A_HEADS,),
        in_specs=[pl.BlockSpec(memory_space=pltpu.SMEM)],
        out_specs=[pl.BlockSpec((n_q, s[1]), lambda h: (h, 0)) for s in shapes],
        out_shape=[jax.ShapeDtypeStruct(s, F32) for s in shapes],
        compiler_params=_cparams(("arbitrary",)),
        name="sample_bias_rows",
    )(rel_bias)


def _nsa_prompt_kernel(q_ref, cmp_ref, ks_ref, vs_ref, kw_ref, vw_ref, zc_ref, zs_ref, zw_ref,
                       gt_ref, tsel_ref, twin_ref, tcmp_ref, msel_ref, exp_ref, o_ref,
                       qa_ref, acc_ref, m_ref, l_ref, madd_ref, mix_ref, *, n_sel_pad, n_sel, k_top):
    i = pl.program_id(2)
    qb = Q_BLOCK
    n_kt, kt = madd_ref.shape[:2]
    tpk = kt // LANE
    nc = cmp_ref.shape[0] // NCK
    gates = _sigmoid(gt_ref[...])
    hs = [slice(h * HD, (h + 1) * HD) for h in range(A_HPG)]
    for h in range(A_HPG):
        qa_ref[hs[h], :] = q_ref[:, hs[h]]
    qa = qa_ref[...]

    def heads(x):
        return jnp.concatenate([x] * A_HPG, axis=1)

    def gate_row(branch):
        return jnp.concatenate([gates[branch * A_HPG + h: branch * A_HPG + h + 1, :] for h in range(A_HPG)], axis=1)

    def emit(branch, o_t, first):
        z_ref = (zc_ref, zs_ref, zw_ref)[branch]
        for h in range(A_HPG):
            val = o_t[:, hs[h]] * z_ref[hs[h], :]
            if first:
                mix_ref[hs[h], :] = val
            else:
                mix_ref[hs[h], :] = mix_ref[hs[h], :] + val

    grp = pl.program_id(0)
    kc = _cmp_rows(cmp_ref, grp, nc).astype(BF16)
    vc = _cmp_rows(cmp_ref, A_KV + grp, nc).astype(BF16)
    cend = lax.broadcasted_iota(jnp.int32, (nc, qb), 0) * CMP_STRIDE + (CMP_LEN - 1)
    qpos_c = i * qb + lax.broadcasted_iota(jnp.int32, (nc, qb), 1)
    cvalid = heads(cend <= qpos_c)
    s = jnp.where(cvalid, _dot_nt(kc, qa) + tcmp_ref[...], NEG)
    mx = jnp.max(s, axis=0, keepdims=True)
    e = jnp.where(cvalid, jnp.exp2(s - mx), 0.0)
    p = e * (1.0 / jnp.maximum(jnp.sum(e, axis=0, keepdims=True), TINY))
    imp = p[:, hs[0]]
    for h in range(1, A_HPG):
        imp = imp + p[:, hs[h]]
    emit(0, _dot_tn(vc, p.astype(BF16)) * gate_row(0), True)

    hi = imp.astype(BF16)
    r1 = imp - hi.astype(F32)
    mid = r1.astype(BF16)
    lo = (r1 - mid.astype(F32)).astype(BF16)
    msel = msel_ref[...]
    imp_sel = _dot(msel, hi) + _dot(msel, mid) + _dot(msel, lo)
    blk = lax.broadcasted_iota(jnp.int32, (n_sel_pad, qb), 0)
    qpos = i * qb + lax.broadcasted_iota(jnp.int32, (n_sel_pad, qb), 1)
    cur = jnp.right_shift(qpos, int(math.log2(SEL_BLOCK)))
    forced = (blk == 0) | (blk == cur) | (blk == cur - 1)
    valid = blk * SEL_BLOCK <= qpos
    score = jnp.where(valid, jnp.where(forced, FORCE_SCORE, imp_sel), -1.0)
    score = jnp.where(blk < n_sel, score, -jnp.inf)
    sel = _select_topk_t(score, blk, k_top, n_sel).astype(BF16)
    kpos2 = lax.broadcasted_iota(jnp.int32, (kt, qb), 0)
    qpos2 = i * qb + lax.broadcasted_iota(jnp.int32, (kt, qb), 1)
    for jj in range(n_kt):
        hit = _dot(exp_ref[jj * kt:(jj + 1) * kt, :], sel)
        ok = (hit > 0.5) & ((jj * kt + kpos2) <= qpos2)
        madd_ref[jj] = jnp.where(ok, 0.0, NEG)

    def reset():
        m_ref[...] = jnp.full(m_ref.shape, NEG, F32)
        l_ref[...] = jnp.zeros(l_ref.shape, F32)
        acc_ref[...] = jnp.zeros(acc_ref.shape, F32)

    def tile_update(k, v, bias, madd):
        s = _dot_nt(k, qa) + bias
        if madd is not None:
            s = s + heads(madd)
        m_old = m_ref[...]
        m_new = jnp.maximum(m_old, jnp.max(s, axis=0, keepdims=True))
        alpha = jnp.exp2(m_old - m_new)
        p = jnp.exp2(s - m_new)
        l_ref[...] = alpha * l_ref[...] + jnp.sum(p, axis=0, keepdims=True)
        m_ref[...] = m_new
        acc_ref[...] = acc_ref[...] * alpha + _dot_tn(v, p.astype(BF16))

    def finish(branch):
        emit(branch, acc_ref[...] * (gate_row(branch) / l_ref[...]), False)

    reset()

    def sel_body(jj, carry):
        k0 = pl.multiple_of(jj * kt, kt)
        k = ks_ref[pl.ds(k0, kt), :].astype(BF16)
        v = vs_ref[pl.ds(k0, kt), :].astype(BF16)
        off = i - tpk * jj
        bias = jnp.concatenate([tsel_ref[jnp.maximum(off + 1 - r, 0)] for r in range(tpk)], axis=0)
        tile_update(k, v, bias, madd_ref[jj])
        return carry

    lax.fori_loop(0, i // tpk + 1, sel_body, 0)
    finish(1)

    reset()

    def win_body(t, carry):
        k0 = pl.multiple_of((i - t) * qb, qb)
        k = kw_ref[pl.ds(k0, qb), :].astype(BF16)
        v = vw_ref[pl.ds(k0, qb), :].astype(BF16)
        tile_update(k, v, twin_ref[t], None)
        return carry

    lax.fori_loop(0, jnp.minimum(i, WINDOW // qb) + 1, win_body, 0)
    finish(2)
    for h in range(A_HPG):
        o_ref[:, hs[h]] = mix_ref[hs[h], :].T.astype(o_ref.dtype)


def nsa_prompt_attention(qn, cmp_kv, rows, win, z_t, gates_g, tsel, twin, tcmp, msel, expand, nb, t):
    nq = t // Q_BLOCK
    qw = A_HPG * HD
    n_sel = -(-t // SEL_BLOCK)
    k_top = min(SEL_TOPK, n_sel)
    kt = SEL_KEY_TILE if t % SEL_KEY_TILE == 0 else 2 * LANE
    n_kt = t // kt
    n_cmp = cmp_kv.shape[1] // NCK
    n_sel_pad = msel.shape[0]
    kernel = functools.partial(_nsa_prompt_kernel, n_sel_pad=n_sel_pad, n_sel=n_sel, k_top=k_top)
    return pl.pallas_call(
        kernel,
        grid=(A_KV, nb, nq),
        in_specs=[
            pl.BlockSpec((Q_BLOCK, qw), lambda g, b, i: (b * nq + i, g)),
            pl.BlockSpec((None, n_cmp * NCK, HD), lambda g, b, i: (b, 0, 0)),
            pl.BlockSpec((t, HD), lambda g, b, i: (b, 2 * A_KV + g)),
            pl.BlockSpec((t, HD), lambda g, b, i: (b, 3 * A_KV + g)),
            pl.BlockSpec((t, HD), lambda g, b, i: (b, g)),
            pl.BlockSpec((t, HD), lambda g, b, i: (b, A_KV + g)),
            pl.BlockSpec((qw, Q_BLOCK), lambda g, b, i: (g, b * nq + i)),
            pl.BlockSpec((qw, Q_BLOCK), lambda g, b, i: (A_KV + g, b * nq + i)),
            pl.BlockSpec((qw, Q_BLOCK), lambda g, b, i: (2 * A_KV + g, b * nq + i)),
            pl.BlockSpec((None, 3 * A_HPG, Q_BLOCK), lambda g, b, i: (g, 0, b * nq + i)),
            pl.BlockSpec((None, nq + 1, LANE, qw), lambda g, b, i: (g, 0, 0, 0)),
            pl.BlockSpec((None, twin.shape[1], LANE, qw), lambda g, b, i: (g, 0, 0, 0)),
            pl.BlockSpec((None, None, n_cmp, qw), lambda g, b, i: (g, i, 0, 0)),
            pl.BlockSpec((n_sel_pad, n_cmp), lambda g, b, i: (0, 0)),
            pl.BlockSpec((t, n_sel_pad), lambda g, b, i: (0, 0)),
        ],
        out_specs=pl.BlockSpec((Q_BLOCK, qw), lambda g, b, i: (b * nq + i, g)),
        out_shape=jax.ShapeDtypeStruct((nb * t, A_HEADS * HD), BF16),
        scratch_shapes=[
            pltpu.VMEM((qw, HD), BF16),
            pltpu.VMEM((HD, qw), F32),
            pltpu.VMEM((1, qw), F32),
            pltpu.VMEM((1, qw), F32),
            pltpu.VMEM((n_kt, kt, Q_BLOCK), F32),
            pltpu.VMEM((qw, Q_BLOCK), F32),
        ],
        compiler_params=_cparams(("arbitrary", "arbitrary", "arbitrary")),
        name="nsa_prompt_attention",
    )(qn, cmp_kv, rows, rows, win, win, z_t, z_t, z_t, gates_g, tsel, twin, tcmp, msel, expand)


def _diag_blocks(o_full, rows_per_group):
    return jnp.concatenate(
        [o_full[g * rows_per_group:(g + 1) * rows_per_group, g * HD:(g + 1) * HD] for g in range(A_KV)], axis=0)


def _nsa_sample_cmp_kernel(q_ref, cmp_ref, bias_ref, msel_ref, oc_ref, sel_ref,
                           *, past_len, n_sel, k_top, n_q, blocks_per_step):
    n_cmp = cmp_ref.shape[0] // NCK
    kc = jnp.concatenate([_cmp_rows(cmp_ref, g, n_cmp) for g in range(A_KV)], axis=1).astype(BF16)
    vc = jnp.concatenate([_cmp_rows(cmp_ref, A_KV + g, n_cmp) for g in range(A_KV)], axis=1).astype(BF16)
    rows = q_ref.shape[0]
    rpg = A_HPG * n_q
    q = q_ref[...]
    s = _dot_nt(q, kc) + bias_ref[...]
    rowi = lax.broadcasted_iota(jnp.int32, (rows, n_cmp), 0)
    ci = lax.broadcasted_iota(jnp.int32, (rows, n_cmp), 1)
    qpos = past_len + (rowi & (n_q - 1))
    valid = (ci * CMP_STRIDE + (CMP_LEN - 1)) <= qpos
    s = jnp.where(valid, s, NEG)
    mx = jnp.max(s, axis=-1, keepdims=True)
    e = jnp.where(valid, jnp.exp2(s - mx), 0.0)
    p = e / jnp.maximum(jnp.sum(e, axis=-1, keepdims=True), TINY)
    oc_ref[...] = _diag_blocks(_dot(p.astype(BF16), vc), rpg)
    imps = []
    for g in range(A_KV):
        acc = p[g * rpg: g * rpg + n_q]
        for h in range(1, A_HPG):
            acc = acc + p[g * rpg + h * n_q: g * rpg + (h + 1) * n_q]
        imps.append(acc)
    imp = jnp.concatenate(imps, axis=0)
    imp_sel = _split3_dot(imp, msel_ref[...])
    shp = imp_sel.shape
    blk = lax.broadcasted_iota(jnp.int32, shp, 1)
    qp = past_len + (lax.broadcasted_iota(jnp.int32, shp, 0) & (n_q - 1))
    cur = jnp.right_shift(qp, int(math.log2(SEL_BLOCK)))
    forced = (blk == 0) | (blk == cur) | (blk == cur - 1)
    ok = blk * SEL_BLOCK <= qp
    score = jnp.where(ok, jnp.where(forced, FORCE_SCORE, imp_sel), -1.0)
    score = jnp.where(blk < n_sel, score, -jnp.inf)
    sel = _select_topk(score, blk, k_top)
    for st in range(sel_ref.shape[0]):
        sel_ref[st] = sel[:, st * blocks_per_step:(st + 1) * blocks_per_step]


def _nsa_sample_sel_kernel(pt_ref, *refs, n_pages, n_steps, past_len, n_q):
    del pt_ref
    pages = refs[:n_pages]
    q_ref, tail_ref, bias_ref, sel_ref, exp_ref, o_ref, m_ref, l_ref, acc_ref = refs[n_pages:]
    s_id = pl.program_id(1)
    kvw = A_KV * HD
    rows = q_ref.shape[0]
    rpg = A_HPG * n_q

    @pl.when(s_id == 0)
    def _():
        m_ref[...] = jnp.full(m_ref.shape, NEG, F32)
        l_ref[...] = jnp.zeros(l_ref.shape, F32)
        acc_ref[...] = jnp.zeros(acc_ref.shape, F32)

    def update(k, v, nk, key0):
        hit = _dot(sel_ref[...].astype(BF16), exp_ref[:, :nk])
        kpos = key0 + lax.broadcasted_iota(jnp.int32, hit.shape, 1)
        qpos = past_len + (lax.broadcasted_iota(jnp.int32, hit.shape, 0) & (n_q - 1))
        madd = jnp.where((hit > 0.5) & (kpos <= qpos), 0.0, NEG)
        madd = jnp.concatenate(
            [madd[g * n_q:(g + 1) * n_q] for g in range(A_KV) for _ in range(A_HPG)], axis=0)
        s = _dot_nt(q_ref[...], k) + bias_ref[:, :nk] + madd
        m_old = m_ref[...]
        m_new = jnp.maximum(m_old, jnp.max(s, axis=-1, keepdims=True))
        alpha = jnp.exp2(m_old - m_new)
        p = jnp.exp2(s - m_new)
        l_ref[...] = alpha * l_ref[...] + jnp.sum(p, axis=-1, keepdims=True)
        acc_ref[...] = alpha * acc_ref[...] + _dot(p.astype(BF16), v)
        m_ref[...] = m_new

    @pl.when(s_id < n_steps)
    def _():
        by_slot = [jnp.swapaxes(pg[...], 0, 1) for pg in pages]

        def gather(c0):
            cols = [jnp.concatenate([x[c0 + g] for x in by_slot], axis=0) for g in range(A_KV)]
            return jnp.concatenate(cols, axis=1).astype(BF16)

        update(gather(0), gather(A_KV), n_pages * PAGE, s_id * (n_pages * PAGE))

    @pl.when(s_id == n_steps)
    def _():
        update(tail_ref[:, :kvw].astype(BF16), tail_ref[:, kvw:].astype(BF16), PAGE, past_len)
        o_ref[...] = _diag_blocks(acc_ref[...] / l_ref[...], rpg)


def _nsa_sample_mix_kernel(q_ref, win_ref, bias_ref, oc_ref, os_ref, z_ref, gt_ref, o_ref, *, n_q):
    kvw = A_KV * HD
    rows = q_ref.shape[0]
    rpg = A_HPG * n_q
    nk = win_ref.shape[0]
    s = _dot_nt(q_ref[...], win_ref[:, :kvw].astype(BF16)) + bias_ref[...]
    qi = lax.broadcasted_iota(jnp.int32, (rows, nk), 0) & (n_q - 1)
    ki = lax.broadcasted_iota(jnp.int32, (rows, nk), 1)
    dist = qi + WINDOW - ki
    s = s + jnp.where((dist >= 0) & (dist < WINDOW), 0.0, NEG)
    mx = jnp.max(s, axis=-1, keepdims=True)
    e = jnp.exp2(s - mx)
    p = e / jnp.sum(e, axis=-1, keepdims=True)
    ow = _diag_blocks(_dot(p.astype(BF16), win_ref[:, kvw:].astype(BF16)), rpg)
    gates = _sigmoid(gt_ref[...])
    width = A_HEADS * HD
    for hh in range(A_HEADS):
        r = slice(hh * n_q, (hh + 1) * n_q)
        c = slice(hh * HD, (hh + 1) * HD)
        val = None
        for br, o in enumerate((oc_ref, os_ref, ow)):
            zc = slice(br * width + hh * HD, br * width + (hh + 1) * HD)
            gcol = gates[:, br * A_HEADS + hh: br * A_HEADS + hh + 1]
            term = gcol * o[r, :] * z_ref[:, zc]
            val = term if val is None else val + term
        o_ref[:, c] = val.astype(o_ref.dtype)


def nsa_sample_attention(qbd, cmp_kv, cache4d, page_table, tail_sel, win_all, z_s, gl_s,
                         b_cmp, b_sel, b_win, msel, expand, past_len, n_q):
    nb, rows, kvw = qbd.shape
    n_cmp = cmp_kv.shape[1] // NCK
    n_tab = page_table.shape[1]
    n_pages = min(PAGES_PER_STEP, n_tab)
    n_steps = n_tab // n_pages
    bps = n_pages * PAGE // SEL_BLOCK
    n_sel = -(-(past_len + n_q) // SEL_BLOCK)
    k_top = min(SEL_TOPK, n_sel)
    gq = A_KV * n_q

    oc, selmask = pl.pallas_call(
        functools.partial(_nsa_sample_cmp_kernel, past_len=past_len, n_sel=n_sel, k_top=k_top,
                          n_q=n_q, blocks_per_step=bps),
        grid=(nb,),
        in_specs=[
            pl.BlockSpec((None, rows, kvw), lambda b: (b, 0, 0)),
            pl.BlockSpec((None, n_cmp * NCK, HD), lambda b: (b, 0, 0)),
            pl.BlockSpec(b_cmp.shape, lambda b: (0, 0)),
            pl.BlockSpec(msel.shape, lambda b: (0, 0)),
        ],
        out_specs=[
            pl.BlockSpec((None, rows, HD), lambda b: (b, 0, 0)),
            pl.BlockSpec((None, n_steps + 1, gq, bps), lambda b: (b, 0, 0, 0)),
        ],
        out_shape=[
            jax.ShapeDtypeStruct((nb, rows, HD), F32),
            jax.ShapeDtypeStruct((nb, n_steps + 1, gq, bps), F32),
        ],
        compiler_params=_cparams(("arbitrary",)),
        name="nsa_sample_cmp",
    )(qbd, cmp_kv, b_cmp, msel)

    half = 2 * kvw
    last = n_steps - 1

    def page_spec(k):
        return pl.BlockSpec(
            (None, PAGE, NCK, HD),
            lambda b, s, pt: (pt[b, jnp.minimum(s, last) * n_pages + k], 0, 1, 0))

    nkeys = n_pages * PAGE
    grid_spec = pltpu.PrefetchScalarGridSpec(
        num_scalar_prefetch=1,
        grid=(nb, n_steps + 1),
        in_specs=[page_spec(k) for k in range(n_pages)] + [
            pl.BlockSpec((None, rows, kvw), lambda b, s, pt: (b, 0, 0)),
            pl.BlockSpec((None, PAGE, half), lambda b, s, pt: (b, 0, 0)),
            pl.BlockSpec((rows, nkeys), lambda b, s, pt: (0, s)),
            pl.BlockSpec((None, None, gq, bps), lambda b, s, pt: (b, s, 0, 0)),
            pl.BlockSpec(expand.shape, lambda b, s, pt: (0, 0)),
        ],
        out_specs=pl.BlockSpec((None, rows, HD), lambda b, s, pt: (b, 0, 0)),
        scratch_shapes=[
            pltpu.VMEM((rows, 1), F32),
            pltpu.VMEM((rows, 1), F32),
            pltpu.VMEM((rows, kvw), F32),
        ],
    )
    osel = pl.pallas_call(
        functools.partial(_nsa_sample_sel_kernel, n_pages=n_pages, n_steps=n_steps,
                          past_len=past_len, n_q=n_q),
        grid_spec=grid_spec,
        out_shape=jax.ShapeDtypeStruct((nb, rows, HD), F32),
        compiler_params=_cparams(("arbitrary", "arbitrary")),
        name="nsa_sample_sel",
    )(page_table, *([cache4d] * n_pages), qbd, tail_sel, b_sel, selmask, expand)

    nkw = win_all.shape[1]
    zw = z_s.shape[1]
    return pl.pallas_call(
        functools.partial(_nsa_sample_mix_kernel, n_q=n_q),
        grid=(nb,),
        in_specs=[
            pl.BlockSpec((None, rows, kvw), lambda b: (b, 0, 0)),
            pl.BlockSpec((None, nkw, 2 * kvw), lambda b: (b, 0, 0)),
            pl.BlockSpec(b_win.shape, lambda b: (0, 0)),
            pl.BlockSpec((None, rows, HD), lambda b: (b, 0, 0)),
            pl.BlockSpec((None, rows, HD), lambda b: (b, 0, 0)),
            pl.BlockSpec((n_q, zw), lambda b: (b, 0)),
            pl.BlockSpec((n_q, 3 * A_HEADS), lambda b: (b, 0)),
        ],
        out_specs=pl.BlockSpec((n_q, A_HEADS * HD), lambda b: (b, 0)),
        out_shape=jax.ShapeDtypeStruct((nb * n_q, A_HEADS * HD), F32),
        compiler_params=_cparams(("arbitrary",)),
        name="nsa_sample_mix",
    )(qbd, win_all, b_win, oc, osel, z_s, gl_s)


def _retention_kernel(q_ref, k_ref, v_ref, g_ref, cos_ref, sin_ref, dm_ref, qd_ref, kd_ref, cd_ref,
                      s0_ref, o_ref, s_ref, *, dk, dv, hps):
    c = pl.program_id(2)
    half = dk // 2

    @pl.when(c == 0)
    def _():
        s_ref[...] = s0_ref[...]

    cos = cos_ref[...]
    sin = sin_ref[...]

    def rot(x):
        x1, x2 = x[:, :half], x[:, half:]
        return jnp.concatenate([x1 * cos - x2 * sin, x1 * sin + x2 * cos], axis=1)

    for j in range(hps):
        ks, vs = slice(j * dk, (j + 1) * dk), slice(j * dv, (j + 1) * dv)
        q = rot(q_ref[:, ks])
        k = rot(k_ref[:, ks]) * (dk ** -0.5)
        v = v_ref[:, vs].astype(BF16)
        qb = q.astype(BF16)
        att = _dot_nt(qb, k.astype(BF16)) * dm_ref[j]
        state = s_ref[j]
        o = _dot(att.astype(BF16), v) + _dot(qb, state.astype(BF16)) * qd_ref[j]
        s_ref[j] = state * cd_ref[j] + _dot_tn((k * kd_ref[j]).astype(BF16), v)
        o = o * lax.rsqrt(jnp.mean(o * o, axis=-1, keepdims=True) + EPS)
        o_ref[:, vs] = (o * _silu(g_ref[:, vs])).astype(o_ref.dtype)


def retention(proj, state0, q0, nb, t):
    _, nh, dk, dv = state0.shape
    cs = R_CHUNK if t % R_CHUNK == 0 else t
    n = t // cs
    half = dk // 2
    pos = (q0 + jnp.arange(t)).astype(F32)
    inv = jnp.power(ROPE_BASE, -jnp.arange(half, dtype=F32) / half)
    ang = pos[:, None] * inv[None, :]
    cos, sin = jnp.cos(ang), jnp.sin(ang)
    log_g = jnp.log1p(-jnp.exp2(-5.0 - jnp.arange(nh, dtype=F32)))
    ii = jnp.arange(cs, dtype=F32)
    rel = ii[:, None] - ii[None, :]
    causal = rel >= 0
    dmat = jnp.where(causal, jnp.exp(jnp.where(causal, rel, 0.0) * log_g[:, None, None]), 0.0)
    q_dec = jnp.exp((ii + 1.0) * log_g[:, None])[:, :, None]
    k_dec = jnp.exp((cs - 1.0 - ii) * log_g[:, None])[:, :, None]
    c_dec = jnp.exp(cs * log_g)[:, None, None]
    hps = RET_HEADS_PER_STEP
    assert nh % hps == 0
    ng = nh // hps
    vb0 = 2 * nh * dk // (hps * dv)
    return pl.pallas_call(
        functools.partial(_retention_kernel, dk=dk, dv=dv, hps=hps),
        grid=(nb, ng, n),
        in_specs=[
            pl.BlockSpec((cs, hps * dk), lambda b, h, c: (b * n + c, h)),
            pl.BlockSpec((cs, hps * dk), lambda b, h, c: (b * n + c, ng + h)),
            pl.BlockSpec((cs, hps * dv), lambda b, h, c: (b * n + c, vb0 + h)),
            pl.BlockSpec((cs, hps * dv), lambda b, h, c: (b * n + c, vb0 + ng + h)),
            pl.BlockSpec((cs, half), lambda b, h, c: (c, 0)),
            pl.BlockSpec((cs, half), lambda b, h, c: (c, 0)),
            pl.BlockSpec((hps, cs, cs), lambda b, h, c: (h, 0, 0)),
            pl.BlockSpec((hps, cs, 1), lambda b, h, c: (h, 0, 0)),
            pl.BlockSpec((hps, cs, 1), lambda b, h, c: (h, 0, 0)),
            pl.BlockSpec((hps, 1, 1), lambda b, h, c: (h, 0, 0)),
            pl.BlockSpec((None, hps, dk, dv), lambda b, h, c: (b, h, 0, 0)),
        ],
        out_specs=[
            pl.BlockSpec((cs, hps * dv), lambda b, h, c: (b * n + c, h)),
            pl.BlockSpec((None, hps, dk, dv), lambda b, h, c: (b, h, 0, 0)),
        ],
        out_shape=[
            jax.ShapeDtypeStruct((nb * t, nh * dv), BF16 if cs % 16 == 0 else F32),
            jax.ShapeDtypeStruct(state0.shape, F32),
        ],
        compiler_params=_cparams(("arbitrary", "arbitrary", "arbitrary")),
        name="retention",
    )(proj, proj, proj, proj, cos, sin, dmat, q_dec, k_dec, c_dec, state0)


def _t5_bucket_np(dist):
    n = np.maximum(dist, 0)
    exact = N_BUCKETS // 2
    logv = np.log(np.maximum(n, 1).astype(np.float32) / np.float32(exact)) / np.float32(
        math.log(MAX_DISTANCE / exact))
    large = np.minimum(exact + (logv * np.float32(N_BUCKETS - exact)).astype(np.int32), N_BUCKETS - 1)
    return np.where(n < exact, n, large).astype(np.int32)


def _bucket_starts():
    bk = _t5_bucket_np(np.arange(4 * MAX_DISTANCE))
    assert np.all(np.diff(bk) >= 0) and bk[-1] == N_BUCKETS - 1
    return [int(np.argmax(bk >= b)) for b in range(N_BUCKETS)]


_BUCKET_START = _bucket_starts()


def _sel_matrix(n_cmp, n_cmp_pad, n_sel_pad):
    c = np.arange(n_cmp_pad)[:, None]
    j = np.arange(n_sel_pad)[None, :]
    a = (c >= SEL_RATIO * j) & (c < SEL_RATIO * j + SEL_RATIO)
    b = (c >= SEL_RATIO * j - 1) & (c < SEL_RATIO * j + SEL_RATIO - 1)
    m = (a.astype(np.float32) + b.astype(np.float32)) * (c < n_cmp)
    return jnp.asarray(m, BF16)


def _expand_matrix(n_blk_pad, n_keys):
    blk = np.arange(n_blk_pad)[:, None]
    key = np.arange(n_keys)[None, :]
    return jnp.asarray((key // SEL_BLOCK == blk).astype(np.float32), BF16)


def _w1_pairs(w1):
    w = w1.reshape(2, 2, CMP_STRIDE // 2, 2, HD, HD)
    w = jnp.transpose(w, (2, 3, 4, 0, 1, 5))
    return w.reshape(CMP_STRIDE // 2, 2 * HD, 4 * HD).astype(BF16)


def kernel(x_prompt, x_sample, c_prompt, c_sample, cache_nsa_kv, cache_nsa_win, state_ret, page_table,
           norm_g, ada_w, ada_b, rel_bias, a_w_in, a_w_out, a_cmp_pe, a_cmp_w1, a_cmp_w2, a_qk_g,
           r_w_in, r_w_out):
    nbp, t, d = x_prompt.shape
    nbs, ts, _ = x_sample.shape
    n_tab = page_table.shape[1]
    past_len = n_tab * PAGE
    kvw = A_KV * HD
    qw = A_HEADS * HD
    assert t % (2 * LANE) == 0 and t >= WINDOW and ts == 8 and past_len >= WINDOW
    assert cache_nsa_win.shape[2] == WINDOW

    n_c = nbp + nbs
    pad_c = (-n_c) % 8
    c_all = jnp.concatenate([c_prompt, c_sample, jnp.zeros((pad_c, d), F32)], axis=0)
    mod = ada_modulation(c_all, ada_w, ada_b).reshape(ada_w.shape[0], n_c + pad_c, 3, d)

    def mods(layer):
        m = mod[layer]
        return (m[:nbp, 0], m[:nbp, 1], m[:nbp, 2]), (m[nbp:n_c, 0], m[nbp:n_c, 1], m[nbp:n_c, 2])

    (sh_p, sc_p, gt_p), (sh_s, sc_s, gt_s) = mods(0)
    hp = norm_modulate(x_prompt, norm_g[0], sc_p, sh_p)
    hs = norm_modulate(x_sample, norm_g[0], sc_s, sh_s)
    w_in_t = jnp.swapaxes(a_w_in[0], 0, 1)
    n_main = w_in_t.shape[0] - 3 * A_HEADS
    n_qkv = qw + 6 * kvw
    proj_p, proj_s = project(hp, hs, w_in_t, n_cols=n_qkv, w_is_nk=True)
    zt_p, z_s = project_gate_t(hp, hs, w_in_t, n_qkv, n_main - n_qkv)
    w_gate = jnp.pad(w_in_t[n_main:], ((0, LANE - 3 * A_HEADS), (0, 0)))
    gl_p, gl_s = project(hp, hs, w_gate, w_is_nk=True)
    gl_p, gl_s = gl_p[:, :3 * A_HEADS], gl_s[:, :3 * A_HEADS]
    qk_g = a_qk_g[0]

    qn_p, rows_p, win_p, rows_t_p, win_t_p = nsa_prep(proj_p, qk_g)
    qn_s, rows_s, win_s, rows_t_s, _ = nsa_prep(proj_s, qk_g)

    w_pairs = _w1_pairs(a_cmp_w1[0])
    pe = a_cmp_pe[0]
    w1 = a_cmp_w1[0].reshape(2, CMP_LEN, HD, HD)
    w2 = a_cmp_w2[0]
    cpp = PAGE // CMP_STRIDE

    pages_p = t // PAGE
    ident = jnp.arange(nbp * pages_p, dtype=jnp.int32).reshape(nbp, pages_p)
    ab_p = compress_stage_a(rows_t_p.reshape(nbp * pages_p, PAGE, 4 * A_KV, HD), ident, w_pairs,
                            min(PAGES_PER_STEP, pages_p))
    zero_tail = jnp.zeros((nbp, cpp, NCK, 2 * HD), F32)
    cmp_p = compress_stage_b(ab_p, zero_tail, pe, w1, w2, qk_g)

    nq = t // Q_BLOCK
    n_ch_p = t // CMP_STRIDE
    tsel, twin, tcmp = prompt_bias_tables(rel_bias, nq, n_ch_p)
    n_cmp_p = t // CMP_STRIDE - 1
    n_sel_p = -(-t // SEL_BLOCK)
    n_sel_pad_p = -(-n_sel_p // 16) * 16
    msel_p = _sel_matrix(n_cmp_p, n_ch_p, n_sel_pad_p).T
    expand_p = _expand_matrix(n_sel_pad_p, t).T
    gates_g = jnp.transpose(gl_p.reshape(nbp * t, 3, A_KV, A_HPG), (2, 1, 3, 0)).reshape(A_KV, 3 * A_HPG, nbp * t)
    mixed_p = nsa_prompt_attention(qn_p, cmp_p, rows_p, win_p, zt_p, gates_g,
                                   tsel, twin, tcmp, msel_p, expand_p, nbp, t)

    n_pool = cache_nsa_kv.shape[1]
    cache4d = cache_nsa_kv[0].reshape(n_pool, PAGE, 4 * A_KV, HD)
    n_pages = min(PAGES_PER_STEP, n_tab)
    ab_s = compress_stage_a(cache4d, page_table, w_pairs, n_pages)
    tail_rows = jnp.pad(rows_s.reshape(nbs, ts, 4 * kvw), ((0, 0), (0, PAGE - ts), (0, 0)))
    ident_s = jnp.arange(nbs, dtype=jnp.int32).reshape(nbs, 1)
    ab_tail = compress_stage_a(tail_rows.reshape(nbs, PAGE, 4 * A_KV, HD), ident_s, w_pairs, 1)
    cmp_s = compress_stage_b(ab_s, ab_tail, pe, w1, w2, qk_g)

    n_cmp_s = cmp_s.shape[1] // NCK
    n_steps = n_tab // n_pages
    bps = n_pages * PAGE // SEL_BLOCK
    n_sel_pad = (n_steps + 1) * bps
    nkeys = (n_steps + 1) * n_pages * PAGE
    nkw = WINDOW + LANE
    b_cmp, b_sel, b_win = sample_bias_rows(rel_bias, past_len, ts, n_cmp_s, nkeys, nkw)
    msel_s = _sel_matrix(n_cmp_s, n_cmp_s, n_sel_pad)
    expand_s = _expand_matrix(bps, n_pages * PAGE)
    q5 = qn_s.reshape(nbs, ts, A_KV, A_HPG, HD)
    eye = jnp.eye(A_KV, dtype=BF16)
    qbd = jnp.einsum('bqghd,ge->bghqed', q5, eye).reshape(nbs, A_HEADS * ts, kvw)
    win_new = win_s.reshape(nbs, ts, 2 * kvw)
    win_cache = cache_nsa_win[0].reshape(nbs, WINDOW, 2 * kvw)
    win_all = jnp.concatenate([win_cache, win_new], axis=1)
    win_pad = jnp.pad(win_all, ((0, 0), (0, nkw - WINDOW - ts), (0, 0)))
    tail_sel = tail_rows[:, :, 2 * kvw:]
    mixed_s = nsa_sample_attention(qbd, cmp_s, cache4d, page_table, tail_sel, win_pad, z_s, gl_s,
                                   b_cmp, b_sel, b_win, msel_s, expand_s, past_len, ts).astype(BF16)

    xs_flat = x_sample.reshape(nbs * ts, d)
    gate_s_rows = jnp.repeat(gt_s, ts, axis=0)
    x1p, x1s = project_residual(mixed_p, mixed_s, a_w_out[0], x_prompt.reshape(nbp * t, d), gt_p,
                                xs_flat, gate_s_rows, t)

    (sh_p, sc_p, gt_p), (sh_s, sc_s, gt_s) = mods(1)
    hp = norm_modulate(x1p.reshape(nbp, t, d), norm_g[1], sc_p, sh_p)
    hs = norm_modulate(x1s.reshape(nbs, ts, d), norm_g[1], sc_s, sh_s)
    rp, rs = project(hp, hs, r_w_in[0])
    s0 = jnp.zeros((nbp,) + state_ret.shape[2:], F32)
    op, ret_p = retention(rp, s0, 0, nbp, t)
    os_, ret_s = retention(rs, state_ret[0], past_len, nbs, ts)
    os_ = os_.astype(BF16)
    gate_s_rows = jnp.repeat(gt_s, ts, axis=0)
    x2p, x2s = project_residual(op, os_, r_w_out[0], x1p, gt_p, x1s, gate_s_rows, t)

    kv_p = rows_t_p.reshape(1, nbp, t, 4, A_KV, HD)
    kv_s = rows_t_s.reshape(1, nbs, ts, 4, A_KV, HD)
    wst_p = win_t_p.reshape(nbp, t, 2, A_KV, HD)[None, :, t - WINDOW:]
    wst_s = win_all[:, ts:].reshape(1, nbs, WINDOW, 2, A_KV, HD)
    return (x2p.reshape(nbp, t, d), x2s.reshape(nbs, ts, d), kv_p, kv_s, wst_p, wst_s,
            ret_p[None], ret_s[None])
```

```python
import functools
import math

import numpy as np
import jax
import jax.numpy as jnp
from jax import lax
from jax.experimental import pallas as pl
from jax.experimental.pallas import tpu as pltpu

EPS = 1e-6
HD = 128
A_KV = 4
A_HPG = 8
A_HEADS = A_KV * A_HPG
CMP_STRIDE = 16
CMP_LEN = 32
SEL_BLOCK = 64
SEL_RATIO = SEL_BLOCK // CMP_STRIDE
SEL_TOPK = 16
WINDOW = 512
Q_BLOCK = 128
FORCE_SCORE = 1e4
NEG = -1e30
TINY = 1e-30
N_BUCKETS = 32
MAX_DISTANCE = 1024
R_CHUNK = 128
ROPE_BASE = 10000.0
LOG2E = math.log2(math.e)
Q_SCALE = HD ** -0.5 * LOG2E
PAGE = 128
PAGES_PER_STEP = 16
RET_HEADS_PER_STEP = 8
SEL_KEY_TILE = 512

LANE = 128
VMEM_LIMIT = 60 * 1024 * 1024

F32 = jnp.float32
BF16 = jnp.bfloat16


def _cparams(sem):
    return pltpu.CompilerParams(dimension_semantics=sem, vmem_limit_bytes=VMEM_LIMIT)


def _dot(a, b):
    return jnp.dot(a, b, preferred_element_type=F32)


def _dot_nt(a, b):
    return lax.dot_general(a, b, (((1,), (1,)), ((), ())), preferred_element_type=F32)


def _dot_tn(a, b):
    return lax.dot_general(a, b, (((0,), (0,)), ((), ())), preferred_element_type=F32)


def _silu(x):
    h = 0.5 * x
    return h + h * jnp.tanh(h)


def _sigmoid(x):
    return 0.5 + 0.5 * jnp.tanh(0.5 * x)


def _split3_dot(x, m_bf16):
    hi = x.astype(BF16)
    r1 = x - hi.astype(F32)
    mid = r1.astype(BF16)
    lo = (r1 - mid.astype(F32)).astype(BF16)
    return _dot(hi, m_bf16) + _dot(mid, m_bf16) + _dot(lo, m_bf16)


def _ada_kernel(c_ref, w_ref, b_ref, o_ref):
    a = _silu(c_ref[...]).astype(BF16)
    o_ref[...] = _dot(a, w_ref[...].astype(BF16)) + b_ref[...]


def ada_modulation(c_all, ada_w, ada_b):
    depth, d, n = ada_w.shape
    rows = c_all.shape[0]
    tn = 512
    return pl.pallas_call(
        _ada_kernel,
        grid=(depth, n // tn),
        in_specs=[
            pl.BlockSpec((rows, d), lambda l, j: (0, 0)),
            pl.BlockSpec((None, d, tn), lambda l, j: (l, 0, j)),
            pl.BlockSpec((None, 1, tn), lambda l, j: (l, 0, j)),
        ],
        out_specs=pl.BlockSpec((None, rows, tn), lambda l, j: (l, 0, j)),
        out_shape=jax.ShapeDtypeStruct((depth, rows, n), F32),
        compiler_params=_cparams(("arbitrary", "arbitrary")),
        name="ada_modulation",
    )(c_all, ada_w, ada_b.reshape(depth, 1, n))


def _norm_mod_kernel(x_ref, g_ref, sc_ref, sh_ref, o_ref):
    x = x_ref[...]
    y = x * lax.rsqrt(jnp.mean(x * x, axis=-1, keepdims=True) + EPS)
    y = y * g_ref[...]
    o_ref[...] = (y * (1.0 + sc_ref[...]) + sh_ref[...]).astype(o_ref.dtype)


def norm_modulate(x, g, scale, shift):
    b, t, d = x.shape
    tt = min(t, 256)
    out = pl.pallas_call(
        _norm_mod_kernel,
        grid=(b, t // tt),
        in_specs=[
            pl.BlockSpec((None, tt, d), lambda i, j: (i, j, 0)),
            pl.BlockSpec((1, d), lambda i, j: (0, 0)),
            pl.BlockSpec((None, 1, d), lambda i, j: (i, 0, 0)),
            pl.BlockSpec((None, 1, d), lambda i, j: (i, 0, 0)),
        ],
        out_specs=pl.BlockSpec((None, tt, d), lambda i, j: (i, j, 0)),
        out_shape=jax.ShapeDtypeStruct((b, t, d), BF16),
        compiler_params=_cparams(("arbitrary", "arbitrary")),
        name="norm_modulate",
    )(x, g.reshape(1, d), scale.reshape(b, 1, d), shift.reshape(b, 1, d))
    return out.reshape(b * t, d)


def _stage_weight_tile(w_hbm, wf_ref, wb_ref, sem, *, w_is_nk, tn, j0):
    j = pl.program_id(0)

    def copy(jj):
        start = pl.multiple_of((j0 + jj) * tn, tn)
        src = w_hbm.at[pl.ds(start, tn), :] if w_is_nk else w_hbm.at[:, pl.ds(start, tn)]
        return pltpu.make_async_copy(src, wf_ref, sem)

    @pl.when(j == 0)
    def _():
        copy(0).start()

    copy(j).wait()
    wb_ref[...] = wf_ref[...].astype(BF16)

    @pl.when(j + 1 < pl.num_programs(0))
    def _():
        copy(j + 1).start()


def _proj_kernel(xp_ref, xs_ref, w_hbm, op_ref, os_ref, wf_ref, wb_ref, sem, *, stage, mm):
    @pl.when(pl.program_id(1) == 0)
    def _():
        stage(w_hbm, wf_ref, wb_ref, sem)
        os_ref[...] = mm(xs_ref[...], wb_ref[...])

    op_ref[...] = mm(xp_ref[...], wb_ref[...])


def _proj_gate_t_kernel(xp_ref, xs_ref, w_hbm, opt_ref, os_ref, wf_ref, wb_ref, sem, *, stage):
    @pl.when(pl.program_id(1) == 0)
    def _():
        stage(w_hbm, wf_ref, wb_ref, sem)
        os_ref[...] = _silu(_dot_nt(xs_ref[...], wb_ref[...]))

    opt_ref[...] = _silu(_dot_nt(wb_ref[...], xp_ref[...]))


def _weight_scratch(shape):
    return [pltpu.VMEM(shape, F32), pltpu.VMEM(shape, BF16), pltpu.SemaphoreType.DMA(())]


def project_gate_t(xp, xs, w_nk, row0, n_rows):
    mp, k = xp.shape
    ms = xs.shape[0]
    tm, tn = _proj_tiles(mp, k, n_rows)
    assert row0 % tn == 0
    stage = functools.partial(_stage_weight_tile, w_is_nk=True, tn=tn, j0=row0 // tn)
    return pl.pallas_call(
        functools.partial(_proj_gate_t_kernel, stage=stage),
        grid=(n_rows // tn, mp // tm),
        in_specs=[
            pl.BlockSpec((tm, k), lambda j, i: (i, 0)),
            pl.BlockSpec((ms, k), lambda j, i: (0, 0)),
            pl.BlockSpec(memory_space=pl.ANY),
        ],
        out_specs=[
            pl.BlockSpec((tn, tm), lambda j, i: (j, i)),
            pl.BlockSpec((ms, tn), lambda j, i: (0, j)),
        ],
        out_shape=[jax.ShapeDtypeStruct((n_rows, mp), F32), jax.ShapeDtypeStruct((ms, n_rows), F32)],
        scratch_shapes=_weight_scratch((tn, k)),
        compiler_params=_cparams(("arbitrary", "arbitrary")),
        name="project_gate_t",
    )(xp, xs, w_nk)


def _proj_res_kernel(xp_ref, xs_ref, w_hbm, rp_ref, gp_ref, rs_ref, gs_ref, op_ref, os_ref,
                     wf_ref, wb_ref, sem, *, stage):
    @pl.when(pl.program_id(1) == 0)
    def _():
        stage(w_hbm, wf_ref, wb_ref, sem)
        os_ref[...] = rs_ref[...] + gs_ref[...] * _dot(xs_ref[...], wb_ref[...])

    op_ref[...] = rp_ref[...] + gp_ref[...] * _dot(xp_ref[...], wb_ref[...])


def _proj_tiles(mp, k, n, residual=False):
    if k <= 4096:
        tm, tn = (1024, 512) if residual else (1024, 1024)
    else:
        tm, tn = 512, 512
    tm = min(tm, mp)
    while mp % tm:
        tm //= 2
    tn = min(tn, n)
    while n % tn:
        tn //= 2
    return tm, tn


def project(xp, xs, w, n_cols=None, w_is_nk=False):
    mp, k = xp.shape
    ms = xs.shape[0]
    n = w.shape[0 if w_is_nk else 1] if n_cols is None else n_cols
    tm, tn = _proj_tiles(mp, k, n)
    stage = functools.partial(_stage_weight_tile, w_is_nk=w_is_nk, tn=tn, j0=0)
    return pl.pallas_call(
        functools.partial(_proj_kernel, stage=stage, mm=_dot_nt if w_is_nk else _dot),
        grid=(n // tn, mp // tm),
        in_specs=[
            pl.BlockSpec((tm, k), lambda j, i: (i, 0)),
            pl.BlockSpec((ms, k), lambda j, i: (0, 0)),
            pl.BlockSpec(memory_space=pl.ANY),
        ],
        out_specs=[
            pl.BlockSpec((tm, tn), lambda j, i: (i, j)),
            pl.BlockSpec((ms, tn), lambda j, i: (0, j)),
        ],
        out_shape=[jax.ShapeDtypeStruct((mp, n), F32), jax.ShapeDtypeStruct((ms, n), F32)],
        scratch_shapes=_weight_scratch((tn, k) if w_is_nk else (k, tn)),
        compiler_params=_cparams(("arbitrary", "arbitrary")),
        name="project",
    )(xp, xs, w)


def project_residual(xp, xs, w, res_p, gate_p, res_s, gate_s, rows_per_batch):
    mp, k = xp.shape
    ms = xs.shape[0]
    n = w.shape[1]
    tm, tn = _proj_tiles(mp, k, n, residual=True)
    tm = min(tm, rows_per_batch)
    tpb = rows_per_batch // tm
    nb = gate_p.shape[0]
    stage = functools.partial(_stage_weight_tile, w_is_nk=False, tn=tn, j0=0)
    return pl.pallas_call(
        functools.partial(_proj_res_kernel, stage=stage),
        grid=(n // tn, mp // tm),
        in_specs=[
            pl.BlockSpec((tm, k), lambda j, i: (i, 0)),
            pl.BlockSpec((ms, k), lambda j, i: (0, 0)),
            pl.BlockSpec(memory_space=pl.ANY),
            pl.BlockSpec((tm, tn), lambda j, i: (i, j)),
            pl.BlockSpec((None, 1, tn), lambda j, i: (i // tpb, 0, j)),
            pl.BlockSpec((ms, tn), lambda j, i: (0, j)),
            pl.BlockSpec((ms, tn), lambda j, i: (0, j)),
        ],
        out_specs=[
            pl.BlockSpec((tm, tn), lambda j, i: (i, j)),
            pl.BlockSpec((ms, tn), lambda j, i: (0, j)),
        ],
        out_shape=[jax.ShapeDtypeStruct((mp, n), F32), jax.ShapeDtypeStruct((ms, n), F32)],
        scratch_shapes=_weight_scratch((k, tn)),
        compiler_params=_cparams(("arbitrary", "arbitrary")),
        name="project_residual",
    )(xp, xs, w, res_p, gate_p.reshape(nb, 1, n), res_s, gate_s)


def _nsa_prep_kernel(q_ref, a_ref, b_ref, c_ref, g_ref, qn_ref, rows_ref, win_ref, rows_t_ref, win_t_ref):
    def hnorm(x, g):
        return x * lax.rsqrt(jnp.mean(x * x, axis=-1, keepdims=True) + EPS) * g

    g = g_ref[...]
    for j in range(A_HEADS):
        sl = slice(j * HD, (j + 1) * HD)
        qn_ref[:, sl] = (hnorm(q_ref[:, sl], g[0:1]) * Q_SCALE).astype(qn_ref.dtype)
    kvw = A_KV * HD
    sls = [slice(j * HD, (j + 1) * HD) for j in range(2 * A_KV)]
    row_slots = ([a_ref[:, s] for s in sls] + [hnorm(b_ref[:, s], g[2:3]) for s in sls[:A_KV]]
                 + [b_ref[:, s] for s in sls[A_KV:]])
    win_slots = [hnorm(c_ref[:, s], g[3:4]) for s in sls[:A_KV]] + [c_ref[:, s] for s in sls[A_KV:]]
    for j, x in enumerate(row_slots):
        rows_ref[:, j * HD:(j + 1) * HD] = x
    for j, x in enumerate(win_slots):
        win_ref[:, j * HD:(j + 1) * HD] = x
    rows_t_ref[...] = jnp.swapaxes(jnp.stack(row_slots, axis=0), 0, 1)
    win_t_ref[...] = jnp.swapaxes(jnp.stack(win_slots, axis=0), 0, 1)


def nsa_prep(proj, qk_g):
    m = proj.shape[0]
    tm = min(m, 256)
    qw = A_HEADS * HD
    kv2 = 2 * A_KV * HD
    base = qw // kv2
    return pl.pallas_call(
        _nsa_prep_kernel,
        grid=(m // tm,),
        in_specs=[
            pl.BlockSpec((tm, qw), lambda i: (i, 0)),
            pl.BlockSpec((tm, kv2), lambda i: (i, base)),
            pl.BlockSpec((tm, kv2), lambda i: (i, base + 1)),
            pl.BlockSpec((tm, kv2), lambda i: (i, base + 2)),
            pl.BlockSpec((4, HD), lambda i: (0, 0)),
        ],
        out_specs=[
            pl.BlockSpec((tm, qw), lambda i: (i, 0)),
            pl.BlockSpec((tm, 2 * kv2), lambda i: (i, 0)),
            pl.BlockSpec((tm, kv2), lambda i: (i, 0)),
            pl.BlockSpec((tm, 4 * A_KV, HD), lambda i: (i, 0, 0)),
            pl.BlockSpec((tm, 2 * A_KV, HD), lambda i: (i, 0, 0)),
        ],
        out_shape=[
            jax.ShapeDtypeStruct((m, qw), BF16),
            jax.ShapeDtypeStruct((m, 2 * kv2), F32),
            jax.ShapeDtypeStruct((m, kv2), F32),
            jax.ShapeDtypeStruct((m, 4 * A_KV, HD), F32),
            jax.ShapeDtypeStruct((m, 2 * A_KV, HD), F32),
        ],
        compiler_params=_cparams(("arbitrary",)),
        name="nsa_prep",
    )(proj, proj, proj, proj, qk_g)


NCK = 2 * A_KV


def _cmp_a_kernel(pt_ref, *refs, n_pages):
    del pt_ref
    pages = refs[:n_pages]
    w_ref = refs[n_pages]
    o_ref = refs[n_pages + 1]
    cpp = PAGE // CMP_STRIDE
    n_ch = n_pages * cpp

    def halves(tiles, lo):
        return [jnp.concatenate([tiles[a][lo:lo + A_KV], tiles[a + 1][lo:lo + A_KV]], axis=0)
                for a in range(0, len(tiles), 2)]

    acc = [None, None]
    for tp in range(CMP_STRIDE // 2):
        pieces = []
        for pg in pages:
            for ch in range(cpp):
                t0 = ch * CMP_STRIDE + 2 * tp
                pieces.append(jnp.concatenate([pg[t0], pg[t0 + 1]], axis=1))
        for kind in range(2):
            lhs = jnp.concatenate(halves(pieces, kind * A_KV), axis=0).astype(BF16)
            d = _dot(lhs, w_ref[tp, :, kind * 2 * HD:(kind + 1) * 2 * HD])
            acc[kind] = d if acc[kind] is None else acc[kind] + d
    for a in range(0, n_ch, 2):
        r = slice(a * A_KV, (a + 2) * A_KV)
        kk, vv = acc[0][r], acc[1][r]
        o_ref[a] = jnp.concatenate([kk[:A_KV], vv[:A_KV]], axis=0)
        o_ref[a + 1] = jnp.concatenate([kk[A_KV:], vv[A_KV:]], axis=0)


def compress_stage_a(store, page_table, w_pairs, n_pages):
    nb, n_tab = page_table.shape
    steps = n_tab // n_pages
    cpp = PAGE // CMP_STRIDE

    def page_spec(k):
        return pl.BlockSpec((None, PAGE, NCK, HD), lambda b, s, pt: (pt[b, s * n_pages + k], 0, 0, 0))

    grid_spec = pltpu.PrefetchScalarGridSpec(
        num_scalar_prefetch=1,
        grid=(nb, steps),
        in_specs=[page_spec(k) for k in range(n_pages)]
        + [pl.BlockSpec(w_pairs.shape, lambda b, s, pt: (0, 0, 0))],
        out_specs=pl.BlockSpec((None, n_pages * cpp, NCK, 2 * HD), lambda b, s, pt: (b, s, 0, 0)),
    )
    return pl.pallas_call(
        functools.partial(_cmp_a_kernel, n_pages=n_pages),
        grid_spec=grid_spec,
        out_shape=jax.ShapeDtypeStruct((nb, n_tab * cpp, NCK, 2 * HD), F32),
        compiler_params=_cparams(("arbitrary", "arbitrary")),
        name="compress_stage_a",
    )(page_table, *([store] * n_pages), w_pairs)


def _cmp_b_kernel(ab_ref, nx_ref, tail_ref, pe_ref, w1_ref, w2_ref, g_ref, o_ref, const_ref):
    j = pl.program_id(1)
    nch = ab_ref.shape[0]
    rows = nch * NCK

    @pl.when((pl.program_id(0) == 0) & (j == 0))
    def _():
        for kind in range(2):
            const = jnp.zeros((8, HD), F32)
            for t in range(CMP_LEN):
                row = jnp.broadcast_to(pe_ref[kind, t:t + 1, :], (8, HD)).astype(BF16)
                const = const + _dot(row, w1_ref[kind, t].astype(BF16))
            const_ref[kind * A_KV:(kind + 1) * A_KV, :] = const[:A_KV]

    ab = ab_ref[...].reshape(rows, 2 * HD)
    nxt = pltpu.roll(ab[:, HD:], rows - NCK, 0).reshape(nch, NCK, HD)
    first_next = jnp.where(j == pl.num_programs(1) - 1, tail_ref[0, :, HD:], nx_ref[0, :, HD:])
    cid = lax.broadcasted_iota(jnp.int32, (nch, NCK, HD), 0)
    nxt = jnp.where(cid == nch - 1, first_next[None], nxt)
    pre = ab[:, :HD].reshape(nch, NCK, HD) + nxt + const_ref[...][None]
    act = _silu(pre).reshape(rows, HD).astype(BF16)
    w2 = jnp.concatenate([w2_ref[0], w2_ref[1]], axis=1).astype(BF16)
    y2 = _dot(act, w2)
    is_k = (lax.broadcasted_iota(jnp.int32, (rows, HD), 0) & (NCK - 1)) < A_KV
    y = jnp.where(is_k, y2[:, :HD], y2[:, HD:])
    yn = y * lax.rsqrt(jnp.mean(y * y, axis=-1, keepdims=True) + EPS) * g_ref[1:2]
    o_ref[...] = jnp.where(is_k, yn, y).reshape(nch, NCK, HD)


def compress_stage_b(ab, tail, pe, w1, w2, qk_g):
    nb, n_ch = ab.shape[:2]
    tc = min(256, n_ch)
    assert n_ch % tc == 0
    nt = n_ch // tc
    out = pl.pallas_call(
        _cmp_b_kernel,
        grid=(nb, nt),
        in_specs=[
            pl.BlockSpec((None, tc, NCK, 2 * HD), lambda b, j: (b, j, 0, 0)),
            pl.BlockSpec((None, 1, NCK, 2 * HD), lambda b, j: (b, jnp.minimum((j + 1) * tc, n_ch - 1), 0, 0)),
            pl.BlockSpec((None, 1, NCK, 2 * HD), lambda b, j: (b, 0, 0, 0)),
            pl.BlockSpec(pe.shape, lambda b, j: (0, 0, 0)),
            pl.BlockSpec(w1.shape, lambda b, j: (0, 0, 0, 0)),
            pl.BlockSpec(w2.shape, lambda b, j: (0, 0, 0)),
            pl.BlockSpec((4, HD), lambda b, j: (0, 0)),
        ],
        out_specs=pl.BlockSpec((None, tc, NCK, HD), lambda b, j: (b, j, 0, 0)),
        out_shape=jax.ShapeDtypeStruct((nb, n_ch, NCK, HD), F32),
        scratch_shapes=[pltpu.VMEM((NCK, HD), F32)],
        compiler_params=_cparams(("arbitrary", "arbitrary")),
        name="compress_stage_b",
    )(ab, ab, tail, pe, w1, w2, qk_g)
    return out.reshape(nb, n_ch * NCK, HD)


def _cmp_rows(cmp_ref, slot, n):
    return cmp_ref[pl.ds(slot, n, stride=NCK), :]


def _select_topk(score, blk, k_top):
    sel = jnp.zeros(score.shape, F32)
    for _ in range(k_top):
        m = jnp.max(score, axis=-1, keepdims=True)
        cand = jnp.where(score == m, blk, jnp.int32(1 << 30))
        first = jnp.min(cand, axis=-1, keepdims=True)
        pick = blk == first
        sel = jnp.where(pick, 1.0, sel)
        score = jnp.where(pick, -jnp.inf, score)
    return sel


def _select_topk_t(score, blk, k_top, n_blk):
    rank = jnp.zeros(score.shape, F32)
    for i in range(n_blk):
        row = score[i:i + 1, :]
        tie = jnp.where(blk > i, 1.0, 0.0)
        rank = rank + jnp.where(row > score, 1.0, jnp.where(row == score, tie, 0.0))
    return jnp.where(rank < k_top, 1.0, 0.0)


def _bias_select(dist, dmin, dmax, value_of):
    lo = int(_t5_bucket_np(np.array([max(dmin, 0)]))[0])
    hi = int(_t5_bucket_np(np.array([max(dmax, 0)]))[0])
    out = jnp.full(dist.shape, value_of(lo), F32)
    for b in range(lo + 1, hi + 1):
        out = jnp.where(dist >= _BUCKET_START[b], value_of(b), out)
    return out * LOG2E


def _bias_tables_kernel(rb_ref, tsel_ref, twin_ref, tcmp_ref):
    head = pl.program_id(0)
    value_of = lambda b: rb_ref[b, head]
    n_off, nq, nc = tsel_ref.shape[0], tcmp_ref.shape[0], tcmp_ref.shape[1]
    kl = lax.broadcasted_iota(jnp.int32, (LANE, Q_BLOCK), 0)
    ql = lax.broadcasted_iota(jnp.int32, (LANE, Q_BLOCK), 1)
    for o in range(n_off):
        base = (o - 1) * Q_BLOCK
        dist = base + ql - kl
        tile = _bias_select(dist, base - (LANE - 1), base + Q_BLOCK - 1, value_of)
        tsel_ref[o] = tile
        if 1 <= o <= twin_ref.shape[0]:
            twin_ref[o - 1] = tile + jnp.where((dist >= 0) & (dist < WINDOW), 0.0, NEG)
    step = Q_BLOCK // CMP_STRIDE
    rows = nc + (nq - 1) * step
    base = (nq - 1) * Q_BLOCK
    cend = lax.broadcasted_iota(jnp.int32, (rows, Q_BLOCK), 0) * CMP_STRIDE + (CMP_LEN - 1)
    qc = lax.broadcasted_iota(jnp.int32, (rows, Q_BLOCK), 1)
    tall = _bias_select(base + qc - cend, base - ((rows - 1) * CMP_STRIDE + CMP_LEN - 1),
                        base + Q_BLOCK - 1, value_of)
    for i in range(nq):
        r0 = (nq - 1 - i) * step
        tcmp_ref[i] = tall[r0:r0 + nc]


def prompt_bias_tables(rel_bias, nq, n_cmp):
    n_win = min(WINDOW // Q_BLOCK + 1, nq)
    return pl.pallas_call(
        _bias_tables_kernel,
        grid=(A_HEADS,),
        in_specs=[pl.BlockSpec(memory_space=pltpu.SMEM)],
        out_specs=[
            pl.BlockSpec((None, nq + 1, LANE, Q_BLOCK), lambda h: (h // A_HPG, 0, 0, h % A_HPG)),
            pl.BlockSpec((None, n_win, LANE, Q_BLOCK), lambda h: (h // A_HPG, 0, 0, h % A_HPG)),
            pl.BlockSpec((None, nq, n_cmp, Q_BLOCK), lambda h: (h // A_HPG, 0, 0, h % A_HPG)),
        ],
        out_shape=[
            jax.ShapeDtypeStruct((A_KV, nq + 1, LANE, A_HPG * Q_BLOCK), F32),
            jax.ShapeDtypeStruct((A_KV, n_win, LANE, A_HPG * Q_BLOCK), F32),
            jax.ShapeDtypeStruct((A_KV, nq, n_cmp, A_HPG * Q_BLOCK), F32),
        ],
        compiler_params=_cparams(("arbitrary",)),
        name="prompt_bias_tables",
    )(rel_bias)


def _bias_rows_kernel(rb_ref, cmp_ref, sel_ref, win_ref, *, past_len, n_q):
    head = pl.program_id(0)
    value_of = lambda b: rb_ref[b, head]

    def fill(ref, key_pos, kmin, kmax, q0):
        q = lax.broadcasted_iota(jnp.int32, ref.shape, 0)
        k = lax.broadcasted_iota(jnp.int32, ref.shape, 1)
        ref[...] = _bias_select(q0 + q - key_pos(k), q0 - kmax, q0 + n_q - 1 - kmin, value_of)

    nc, nk, nw = cmp_ref.shape[1], sel_ref.shape[1], win_ref.shape[1]
    fill(cmp_ref, lambda k: k * CMP_STRIDE + (CMP_LEN - 1), CMP_LEN - 1, (nc - 1) * CMP_STRIDE + CMP_LEN - 1, past_len)
    fill(sel_ref, lambda k: k, 0, nk - 1, past_len)
    fill(win_ref, lambda k: k, 0, nw - 1, WINDOW)


def sample_bias_rows(rel_bias, past_len, n_q, n_cmp, n_keys, n_win):
    shapes = [(A_HEADS * n_q, n) for n in (n_cmp, n_keys, n_win)]
    return pl.pallas_call(
        functools.partial(_bias_rows_kernel, past_len=past_len, n_q=n_q),
        grid=(A_HEADS,),
        in_specs=[pl.BlockSpec(memory_space=pltpu.SMEM)],
        out_specs=[pl.BlockSpec((n_q, s[1]), lambda h: (h, 0)) for s in shapes],
        out_shape=[jax.ShapeDtypeStruct(s, F32) for s in shapes],
        compiler_params=_cparams(("arbitrary",)),
        name="sample_bias_rows",
    )(rel_bias)


def _nsa_prompt_kernel(q_ref, cmp_ref, ks_ref, vs_ref, kw_ref, vw_ref, zc_ref, zs_ref, zw_ref,
                       gt_ref, tsel_ref, twin_ref, tcmp_ref, msel_ref, exp_ref, o_ref,
                       qa_ref, acc_ref, m_ref, l_ref, madd_ref, mix_ref, *, n_sel_pad, n_sel, k_top):
    i = pl.program_id(2)
    qb = Q_BLOCK
    n_kt, kt = madd_ref.shape[:2]
    tpk = kt // LANE
    nc = cmp_ref.shape[0] // NCK
    gates = _sigmoid(gt_ref[...])
    hs = [slice(h * HD, (h + 1) * HD) for h in range(A_HPG)]
    for h in range(A_HPG):
        qa_ref[hs[h], :] = q_ref[:, hs[h]]
    qa = qa_ref[...]

    def heads(x):
        return jnp.concatenate([x] * A_HPG, axis=1)

    def gate_row(branch):
        return jnp.concatenate([gates[branch * A_HPG + h: branch * A_HPG + h + 1, :] for h in range(A_HPG)], axis=1)

    def emit(branch, o_t, first):
        z_ref = (zc_ref, zs_ref, zw_ref)[branch]
        for h in range(A_HPG):
            val = o_t[:, hs[h]] * z_ref[hs[h], :]
            if first:
                mix_ref[hs[h], :] = val
            else:
                mix_ref[hs[h], :] = mix_ref[hs[h], :] + val

    grp = pl.program_id(0)
    kc = _cmp_rows(cmp_ref, grp, nc).astype(BF16)
    vc = _cmp_rows(cmp_ref, A_KV + grp, nc).astype(BF16)
    cend = lax.broadcasted_iota(jnp.int32, (nc, qb), 0) * CMP_STRIDE + (CMP_LEN - 1)
    qpos_c = i * qb + lax.broadcasted_iota(jnp.int32, (nc, qb), 1)
    cvalid = heads(cend <= qpos_c)
    s = jnp.where(cvalid, _dot_nt(kc, qa) + tcmp_ref[...], NEG)
    mx = jnp.max(s, axis=0, keepdims=True)
    e = jnp.where(cvalid, jnp.exp2(s - mx), 0.0)
    p = e * (1.0 / jnp.maximum(jnp.sum(e, axis=0, keepdims=True), TINY))
    imp = p[:, hs[0]]
    for h in range(1, A_HPG):
        imp = imp + p[:, hs[h]]
    emit(0, _dot_tn(vc, p.astype(BF16)) * gate_row(0), True)

    hi = imp.astype(BF16)
    r1 = imp - hi.astype(F32)
    mid = r1.astype(BF16)
    lo = (r1 - mid.astype(F32)).astype(BF16)
    msel = msel_ref[...]
    imp_sel = _dot(msel, hi) + _dot(msel, mid) + _dot(msel, lo)
    blk = lax.broadcasted_iota(jnp.int32, (n_sel_pad, qb), 0)
    qpos = i * qb + lax.broadcasted_iota(jnp.int32, (n_sel_pad, qb), 1)
    cur = jnp.right_shift(qpos, int(math.log2(SEL_BLOCK)))
    forced = (blk == 0) | (blk == cur) | (blk == cur - 1)
    valid = blk * SEL_BLOCK <= qpos
    score = jnp.where(valid, jnp.where(forced, FORCE_SCORE, imp_sel), -1.0)
    score = jnp.where(blk < n_sel, score, -jnp.inf)
    sel = _select_topk_t(score, blk, k_top, n_sel).astype(BF16)
    kpos2 = lax.broadcasted_iota(jnp.int32, (kt, qb), 0)
    qpos2 = i * qb + lax.broadcasted_iota(jnp.int32, (kt, qb), 1)
    for jj in range(n_kt):
        hit = _dot(exp_ref[jj * kt:(jj + 1) * kt, :], sel)
        ok = (hit > 0.5) & ((jj * kt + kpos2) <= qpos2)
        madd_ref[jj] = jnp.where(ok, 0.0, NEG)

    def reset():
        m_ref[...] = jnp.full(m_ref.shape, NEG, F32)
        l_ref[...] = jnp.zeros(l_ref.shape, F32)
        acc_ref[...] = jnp.zeros(acc_ref.shape, F32)

    def tile_update(k, v, bias, madd):
        s = _dot_nt(k, qa) + bias
        if madd is not None:
            s = s + heads(madd)
        m_old = m_ref[...]
        m_new = jnp.maximum(m_old, jnp.max(s, axis=0, keepdims=True))
        alpha = jnp.exp2(m_old - m_new)
        p = jnp.exp2(s - m_new)
        l_ref[...] = alpha * l_ref[...] + jnp.sum(p, axis=0, keepdims=True)
        m_ref[...] = m_new
        acc_ref[...] = acc_ref[...] * alpha + _dot_tn(v, p.astype(BF16))

    def finish(branch):
        emit(branch, acc_ref[...] * (gate_row(branch) / l_ref[...]), False)

    reset()

    def sel_body(jj, carry):
        k0 = pl.multiple_of(jj * kt, kt)
        k = ks_ref[pl.ds(k0, kt), :].astype(BF16)
        v = vs_ref[pl.ds(k0, kt), :].astype(BF16)
        off = i - tpk * jj
        bias = jnp.concatenate([tsel_ref[jnp.maximum(off + 1 - r, 0)] for r in range(tpk)], axis=0)
        tile_update(k, v, bias, madd_ref[jj])
        return carry

    lax.fori_loop(0, i // tpk + 1, sel_body, 0)
    finish(1)

    reset()

    def win_body(t, carry):
        k0 = pl.multiple_of((i - t) * qb, qb)
        k = kw_ref[pl.ds(k0, qb), :].astype(BF16)
        v = vw_ref[pl.ds(k0, qb), :].astype(BF16)
        tile_update(k, v, twin_ref[t], None)
        return carry

    lax.fori_loop(0, jnp.minimum(i, WINDOW // qb) + 1, win_body, 0)
    finish(2)
    for h in range(A_HPG):
        o_ref[:, hs[h]] = mix_ref[hs[h], :].T.astype(o_ref.dtype)


def nsa_prompt_attention(qn, cmp_kv, rows, win, z_t, gates_g, tsel, twin, tcmp, msel, expand, nb, t):
    nq = t // Q_BLOCK
    qw = A_HPG * HD
    n_sel = -(-t // SEL_BLOCK)
    k_top = min(SEL_TOPK, n_sel)
    kt = SEL_KEY_TILE if t % SEL_KEY_TILE == 0 else 2 * LANE
    n_kt = t // kt
    n_cmp = cmp_kv.shape[1] // NCK
    n_sel_pad = msel.shape[0]
    kernel = functools.partial(_nsa_prompt_kernel, n_sel_pad=n_sel_pad, n_sel=n_sel, k_top=k_top)
    return pl.pallas_call(
        kernel,
        grid=(A_KV, nb, nq),
        in_specs=[
            pl.BlockSpec((Q_BLOCK, qw), lambda g, b, i: (b * nq + i, g)),
            pl.BlockSpec((None, n_cmp * NCK, HD), lambda g, b, i: (b, 0, 0)),
            pl.BlockSpec((t, HD), lambda g, b, i: (b, 2 * A_KV + g)),
            pl.BlockSpec((t, HD), lambda g, b, i: (b, 3 * A_KV + g)),
            pl.BlockSpec((t, HD), lambda g, b, i: (b, g)),
            pl.BlockSpec((t, HD), lambda g, b, i: (b, A_KV + g)),
            pl.BlockSpec((qw, Q_BLOCK), lambda g, b, i: (g, b * nq + i)),
            pl.BlockSpec((qw, Q_BLOCK), lambda g, b, i: (A_KV + g, b * nq + i)),
            pl.BlockSpec((qw, Q_BLOCK), lambda g, b, i: (2 * A_KV + g, b * nq + i)),
            pl.BlockSpec((None, 3 * A_HPG, Q_BLOCK), lambda g, b, i: (g, 0, b * nq + i)),
            pl.BlockSpec((None, nq + 1, LANE, qw), lambda g, b, i: (g, 0, 0, 0)),
            pl.BlockSpec((None, twin.shape[1], LANE, qw), lambda g, b, i: (g, 0, 0, 0)),
            pl.BlockSpec((None, None, n_cmp, qw), lambda g, b, i: (g, i, 0, 0)),
            pl.BlockSpec((n_sel_pad, n_cmp), lambda g, b, i: (0, 0)),
            pl.BlockSpec((t, n_sel_pad), lambda g, b, i: (0, 0)),
        ],
        out_specs=pl.BlockSpec((Q_BLOCK, qw), lambda g, b, i: (b * nq + i, g)),
        out_shape=jax.ShapeDtypeStruct((nb * t, A_HEADS * HD), BF16),
        scratch_shapes=[
            pltpu.VMEM((qw, HD), BF16),
            pltpu.VMEM((HD, qw), F32),
            pltpu.VMEM((1, qw), F32),
            pltpu.VMEM((1, qw), F32),
            pltpu.VMEM((n_kt, kt, Q_BLOCK), F32),
            pltpu.VMEM((qw, Q_BLOCK), F32),
        ],
        compiler_params=_cparams(("arbitrary", "arbitrary", "arbitrary")),
        name="nsa_prompt_attention",
    )(qn, cmp_kv, rows, rows, win, win, z_t, z_t, z_t, gates_g, tsel, twin, tcmp, msel, expand)


def _diag_blocks(o_full, rows_per_group):
    return jnp.concatenate(
        [o_full[g * rows_per_group:(g + 1) * rows_per_group, g * HD:(g + 1) * HD] for g in range(A_KV)], axis=0)


def _nsa_sample_cmp_kernel(q_ref, cmp_ref, bias_ref, msel_ref, oc_ref, sel_ref,
                           *, past_len, n_sel, k_top, n_q, blocks_per_step):
    n_cmp = cmp_ref.shape[0] // NCK
    kc = jnp.concatenate([_cmp_rows(cmp_ref, g, n_cmp) for g in range(A_KV)], axis=1).astype(BF16)
    vc = jnp.concatenate([_cmp_rows(cmp_ref, A_KV + g, n_cmp) for g in range(A_KV)], axis=1).astype(BF16)
    rows = q_ref.shape[0]
    rpg = A_HPG * n_q
    q = q_ref[...]
    s = _dot_nt(q, kc) + bias_ref[...]
    rowi = lax.broadcasted_iota(jnp.int32, (rows, n_cmp), 0)
    ci = lax.broadcasted_iota(jnp.int32, (rows, n_cmp), 1)
    qpos = past_len + (rowi & (n_q - 1))
    valid = (ci * CMP_STRIDE + (CMP_LEN - 1)) <= qpos
    s = jnp.where(valid, s, NEG)
    mx = jnp.max(s, axis=-1, keepdims=True)
    e = jnp.where(valid, jnp.exp2(s - mx), 0.0)
    p = e / jnp.maximum(jnp.sum(e, axis=-1, keepdims=True), TINY)
    oc_ref[...] = _diag_blocks(_dot(p.astype(BF16), vc), rpg)
    imps = []
    for g in range(A_KV):
        acc = p[g * rpg: g * rpg + n_q]
        for h in range(1, A_HPG):
            acc = acc + p[g * rpg + h * n_q: g * rpg + (h + 1) * n_q]
        imps.append(acc)
    imp = jnp.concatenate(imps, axis=0)
    imp_sel = _split3_dot(imp, msel_ref[...])
    shp = imp_sel.shape
    blk = lax.broadcasted_iota(jnp.int32, shp, 1)
    qp = past_len + (lax.broadcasted_iota(jnp.int32, shp, 0) & (n_q - 1))
    cur = jnp.right_shift(qp, int(math.log2(SEL_BLOCK)))
    forced = (blk == 0) | (blk == cur) | (blk == cur - 1)
    ok = blk * SEL_BLOCK <= qp
    score = jnp.where(ok, jnp.where(forced, FORCE_SCORE, imp_sel), -1.0)
    score = jnp.where(blk < n_sel, score, -jnp.inf)
    sel = _select_topk(score, blk, k_top)
    for st in range(sel_ref.shape[0]):
        sel_ref[st] = sel[:, st * blocks_per_step:(st + 1) * blocks_per_step]


def _nsa_sample_sel_kernel(pt_ref, *refs, n_pages, n_steps, past_len, n_q):
    del pt_ref
    pages = refs[:n_pages]
    q_ref, tail_ref, bias_ref, sel_ref, exp_ref, o_ref, m_ref, l_ref, acc_ref = refs[n_pages:]
    s_id = pl.program_id(1)
    kvw = A_KV * HD
    rows = q_ref.shape[0]
    rpg = A_HPG * n_q

    @pl.when(s_id == 0)
    def _():
        m_ref[...] = jnp.full(m_ref.shape, NEG, F32)
        l_ref[...] = jnp.zeros(l_ref.shape, F32)
        acc_ref[...] = jnp.zeros(acc_ref.shape, F32)

    def update(k, v, nk, key0):
        hit = _dot(sel_ref[...].astype(BF16), exp_ref[:, :nk])
        kpos = key0 + lax.broadcasted_iota(jnp.int32, hit.shape, 1)
        qpos = past_len + (lax.broadcasted_iota(jnp.int32, hit.shape, 0) & (n_q - 1))
        madd = jnp.where((hit > 0.5) & (kpos <= qpos), 0.0, NEG)
        madd = jnp.concatenate(
            [madd[g * n_q:(g + 1) * n_q] for g in range(A_KV) for _ in range(A_HPG)], axis=0)
        s = _dot_nt(q_ref[...], k) + bias_ref[:, :nk] + madd
        m_old = m_ref[...]
        m_new = jnp.maximum(m_old, jnp.max(s, axis=-1, keepdims=True))
        alpha = jnp.exp2(m_old - m_new)
        p = jnp.exp2(s - m_new)
        l_ref[...] = alpha * l_ref[...] + jnp.sum(p, axis=-1, keepdims=True)
        acc_ref[...] = alpha * acc_ref[...] + _dot(p.astype(BF16), v)
        m_ref[...] = m_new

    @pl.when(s_id < n_steps)
    def _():
        by_slot = [jnp.swapaxes(pg[...], 0, 1) for pg in pages]

        def gather(c0):
            cols = [jnp.concatenate([x[c0 + g] for x in by_slot], axis=0) for g in range(A_KV)]
            return jnp.concatenate(cols, axis=1).astype(BF16)

        update(gather(0), gather(A_KV), n_pages * PAGE, s_id * (n_pages * PAGE))

    @pl.when(s_id == n_steps)
    def _():
        update(tail_ref[:, :kvw].astype(BF16), tail_ref[:, kvw:].astype(BF16), PAGE, past_len)
        o_ref[...] = _diag_blocks(acc_ref[...] / l_ref[...], rpg)


def _nsa_sample_mix_kernel(q_ref, win_ref, bias_ref, oc_ref, os_ref, z_ref, gt_ref, o_ref, *, n_q):
    kvw = A_KV * HD
    rows = q_ref.shape[0]
    rpg = A_HPG * n_q
    nk = win_ref.shape[0]
    s = _dot_nt(q_ref[...], win_ref[:, :kvw].astype(BF16)) + bias_ref[...]
    qi = lax.broadcasted_iota(jnp.int32, (rows, nk), 0) & (n_q - 1)
    ki = lax.broadcasted_iota(jnp.int32, (rows, nk), 1)
    dist = qi + WINDOW - ki
    s = s + jnp.where((dist >= 0) & (dist < WINDOW), 0.0, NEG)
    mx = jnp.max(s, axis=-1, keepdims=True)
    e = jnp.exp2(s - mx)
    p = e / jnp.sum(e, axis=-1, keepdims=True)
    ow = _diag_blocks(_dot(p.astype(BF16), win_ref[:, kvw:].astype(BF16)), rpg)
    gates = _sigmoid(gt_ref[...])
    width = A_HEADS * HD
    for hh in range(A_HEADS):
        r = slice(hh * n_q, (hh + 1) * n_q)
        c = slice(hh * HD, (hh + 1) * HD)
        val = None
        for br, o in enumerate((oc_ref, os_ref, ow)):
            zc = slice(br * width + hh * HD, br * width + (hh + 1) * HD)
            gcol = gates[:, br * A_HEADS + hh: br * A_HEADS + hh + 1]
            term = gcol * o[r, :] * z_ref[:, zc]
            val = term if val is None else val + term
        o_ref[:, c] = val.astype(o_ref.dtype)


def nsa_sample_attention(qbd, cmp_kv, cache4d, page_table, tail_sel, win_all, z_s, gl_s,
                         b_cmp, b_sel, b_win, msel, expand, past_len, n_q):
    nb, rows, kvw = qbd.shape
    n_cmp = cmp_kv.shape[1] // NCK
    n_tab = page_table.shape[1]
    n_pages = min(PAGES_PER_STEP, n_tab)
    n_steps = n_tab // n_pages
    bps = n_pages * PAGE // SEL_BLOCK
    n_sel = -(-(past_len + n_q) // SEL_BLOCK)
    k_top = min(SEL_TOPK, n_sel)
    gq = A_KV * n_q

    oc, selmask = pl.pallas_call(
        functools.partial(_nsa_sample_cmp_kernel, past_len=past_len, n_sel=n_sel, k_top=k_top,
                          n_q=n_q, blocks_per_step=bps),
        grid=(nb,),
        in_specs=[
            pl.BlockSpec((None, rows, kvw), lambda b: (b, 0, 0)),
            pl.BlockSpec((None, n_cmp * NCK, HD), lambda b: (b, 0, 0)),
            pl.BlockSpec(b_cmp.shape, lambda b: (0, 0)),
            pl.BlockSpec(msel.shape, lambda b: (0, 0)),
        ],
        out_specs=[
            pl.BlockSpec((None, rows, HD), lambda b: (b, 0, 0)),
            pl.BlockSpec((None, n_steps + 1, gq, bps), lambda b: (b, 0, 0, 0)),
        ],
        out_shape=[
            jax.ShapeDtypeStruct((nb, rows, HD), F32),
            jax.ShapeDtypeStruct((nb, n_steps + 1, gq, bps), F32),
        ],
        compiler_params=_cparams(("arbitrary",)),
        name="nsa_sample_cmp",
    )(qbd, cmp_kv, b_cmp, msel)

    half = 2 * kvw
    last = n_steps - 1

    def page_spec(k):
        return pl.BlockSpec(
            (None, PAGE, NCK, HD),
            lambda b, s, pt: (pt[b, jnp.minimum(s, last) * n_pages + k], 0, 1, 0))

    nkeys = n_pages * PAGE
    grid_spec = pltpu.PrefetchScalarGridSpec(
        num_scalar_prefetch=1,
        grid=(nb, n_steps + 1),
        in_specs=[page_spec(k) for k in range(n_pages)] + [
            pl.BlockSpec((None, rows, kvw), lambda b, s, pt: (b, 0, 0)),
            pl.BlockSpec((None, PAGE, half), lambda b, s, pt: (b, 0, 0)),
            pl.BlockSpec((rows, nkeys), lambda b, s, pt: (0, s)),
            pl.BlockSpec((None, None, gq, bps), lambda b, s, pt: (b, s, 0, 0)),
            pl.BlockSpec(expand.shape, lambda b, s, pt: (0, 0)),
        ],
        out_specs=pl.BlockSpec((None, rows, HD), lambda b, s, pt: (b, 0, 0)),
        scratch_shapes=[
            pltpu.VMEM((rows, 1), F32),
            pltpu.VMEM((rows, 1), F32),
            pltpu.VMEM((rows, kvw), F32),
        ],
    )
    osel = pl.pallas_call(
        functools.partial(_nsa_sample_sel_kernel, n_pages=n_pages, n_steps=n_steps,
                          past_len=past_len, n_q=n_q),
        grid_spec=grid_spec,
        out_shape=jax.ShapeDtypeStruct((nb, rows, HD), F32),
        compiler_params=_cparams(("arbitrary", "arbitrary")),
        name="nsa_sample_sel",
    )(page_table, *([cache4d] * n_pages), qbd, tail_sel, b_sel, selmask, expand)

    nkw = win_all.shape[1]
    zw = z_s.shape[1]
    return pl.pallas_call(
        functools.partial(_nsa_sample_mix_kernel, n_q=n_q),
        grid=(nb,),
        in_specs=[
            pl.BlockSpec((None, rows, kvw), lambda b: (b, 0, 0)),
            pl.BlockSpec((None, nkw, 2 * kvw), lambda b: (b, 0, 0)),
            pl.BlockSpec(b_win.shape, lambda b: (0, 0)),
            pl.BlockSpec((None, rows, HD), lambda b: (b, 0, 0)),
            pl.BlockSpec((None, rows, HD), lambda b: (b, 0, 0)),
            pl.BlockSpec((n_q, zw), lambda b: (b, 0)),
            pl.BlockSpec((n_q, 3 * A_HEADS), lambda b: (b, 0)),
        ],
        out_specs=pl.BlockSpec((n_q, A_HEADS * HD), lambda b: (b, 0)),
        out_shape=jax.ShapeDtypeStruct((nb * n_q, A_HEADS * HD), F32),
        compiler_params=_cparams(("arbitrary",)),
        name="nsa_sample_mix",
    )(qbd, win_all, b_win, oc, osel, z_s, gl_s)


def _retention_kernel(q_ref, k_ref, v_ref, g_ref, cos_ref, sin_ref, dm_ref, qd_ref, kd_ref, cd_ref,
                      s0_ref, o_ref, s_ref, *, dk, dv, hps):
    c = pl.program_id(2)
    half = dk // 2

    @pl.when(c == 0)
    def _():
        s_ref[...] = s0_ref[...]

    cos = cos_ref[...]
    sin = sin_ref[...]

    def rot(x):
        x1, x2 = x[:, :half], x[:, half:]
        return jnp.concatenate([x1 * cos - x2 * sin, x1 * sin + x2 * cos], axis=1)

    for j in range(hps):
        ks, vs = slice(j * dk, (j + 1) * dk), slice(j * dv, (j + 1) * dv)
        q = rot(q_ref[:, ks])
        k = rot(k_ref[:, ks]) * (dk ** -0.5)
        v = v_ref[:, vs].astype(BF16)
        qb = q.astype(BF16)
        att = _dot_nt(qb, k.astype(BF16)) * dm_ref[j]
        state = s_ref[j]
        o = _dot(att.astype(BF16), v) + _dot(qb, state.astype(BF16)) * qd_ref[j]
        s_ref[j] = state * cd_ref[j] + _dot_tn((k * kd_ref[j]).astype(BF16), v)
        o = o * lax.rsqrt(jnp.mean(o * o, axis=-1, keepdims=True) + EPS)
        o_ref[:, vs] = (o * _silu(g_ref[:, vs])).astype(o_ref.dtype)


def retention(proj, state0, q0, nb, t):
    _, nh, dk, dv = state0.shape
    cs = R_CHUNK if t % R_CHUNK == 0 else t
    n = t // cs
    half = dk // 2
    pos = (q0 + jnp.arange(t)).astype(F32)
    inv = jnp.power(ROPE_BASE, -jnp.arange(half, dtype=F32) / half)
    ang = pos[:, None] * inv[None, :]
    cos, sin = jnp.cos(ang), jnp.sin(ang)
    log_g = jnp.log1p(-jnp.exp2(-5.0 - jnp.arange(nh, dtype=F32)))
    ii = jnp.arange(cs, dtype=F32)
    rel = ii[:, None] - ii[None, :]
    causal = rel >= 0
    dmat = jnp.where(causal, jnp.exp(jnp.where(causal, rel, 0.0) * log_g[:, None, None]), 0.0)
    q_dec = jnp.exp((ii + 1.0) * log_g[:, None])[:, :, None]
    k_dec = jnp.exp((cs - 1.0 - ii) * log_g[:, None])[:, :, None]
    c_dec = jnp.exp(cs * log_g)[:, None, None]
    hps = RET_HEADS_PER_STEP
    assert nh % hps == 0
    ng = nh // hps
    vb0 = 2 * nh * dk // (hps * dv)
    return pl.pallas_call(
        functools.partial(_retention_kernel, dk=dk, dv=dv, hps=hps),
        grid=(nb, ng, n),
        in_specs=[
            pl.BlockSpec((cs, hps * dk), lambda b, h, c: (b * n + c, h)),
            pl.BlockSpec((cs, hps * dk), lambda b, h, c: (b * n + c, ng + h)),
            pl.BlockSpec((cs, hps * dv), lambda b, h, c: (b * n + c, vb0 + h)),
            pl.BlockSpec((cs, hps * dv), lambda b, h, c: (b * n + c, vb0 + ng + h)),
            pl.BlockSpec((cs, half), lambda b, h, c: (c, 0)),
            pl.BlockSpec((cs, half), lambda b, h, c: (c, 0)),
            pl.BlockSpec((hps, cs, cs), lambda b, h, c: (h, 0, 0)),
            pl.BlockSpec((hps, cs, 1), lambda b, h, c: (h, 0, 0)),
            pl.BlockSpec((hps, cs, 1), lambda b, h, c: (h, 0, 0)),
            pl.BlockSpec((hps, 1, 1), lambda b, h, c: (h, 0, 0)),
            pl.BlockSpec((None, hps, dk, dv), lambda b, h, c: (b, h, 0, 0)),
        ],
        out_specs=[
            pl.BlockSpec((cs, hps * dv), lambda b, h, c: (b * n + c, h)),
            pl.BlockSpec((None, hps, dk, dv), lambda b, h, c: (b, h, 0, 0)),
        ],
        out_shape=[
            jax.ShapeDtypeStruct((nb * t, nh * dv), BF16 if cs % 16 == 0 else F32),
            jax.ShapeDtypeStruct(state0.shape, F32),
        ],
        compiler_params=_cparams(("arbitrary", "arbitrary", "arbitrary")),
        name="retention",
    )(proj, proj, proj, proj, cos, sin, dmat, q_dec, k_dec, c_dec, state0)


def _t5_bucket_np(dist):
    n = np.maximum(dist, 0)
    exact = N_BUCKETS // 2
    logv = np.log(np.maximum(n, 1).astype(np.float32) / np.float32(exact)) / np.float32(
        math.log(MAX_DISTANCE / exact))
    large = np.minimum(exact + (logv * np.float32(N_BUCKETS - exact)).astype(np.int32), N_BUCKETS - 1)
    return np.where(n < exact, n, large).astype(np.int32)


def _bucket_starts():
    bk = _t5_bucket_np(np.arange(4 * MAX_DISTANCE))
    assert np.all(np.diff(bk) >= 0) and bk[-1] == N_BUCKETS - 1
    return [int(np.argmax(bk >= b)) for b in range(N_BUCKETS)]


_BUCKET_START = _bucket_starts()


def _sel_matrix(n_cmp, n_cmp_pad, n_sel_pad):
    c = np.arange(n_cmp_pad)[:, None]
    j = np.arange(n_sel_pad)[None, :]
    a = (c >= SEL_RATIO * j) & (c < SEL_RATIO * j + SEL_RATIO)
    b = (c >= SEL_RATIO * j - 1) & (c < SEL_RATIO * j + SEL_RATIO - 1)
    m = (a.astype(np.float32) + b.astype(np.float32)) * (c < n_cmp)
    return jnp.asarray(m, BF16)


def _expand_matrix(n_blk_pad, n_keys):
    blk = np.arange(n_blk_pad)[:, None]
    key = np.arange(n_keys)[None, :]
    return jnp.asarray((key // SEL_BLOCK == blk).astype(np.float32), BF16)


def _w1_pairs(w1):
    w = w1.reshape(2, 2, CMP_STRIDE // 2, 2, HD, HD)
    w = jnp.transpose(w, (2, 3, 4, 0, 1, 5))
    return w.reshape(CMP_STRIDE // 2, 2 * HD, 4 * HD).astype(BF16)


def kernel(x_prompt, x_sample, c_prompt, c_sample, cache_nsa_kv, cache_nsa_win, state_ret, page_table,
           norm_g, ada_w, ada_b, rel_bias, a_w_in, a_w_out, a_cmp_pe, a_cmp_w1, a_cmp_w2, a_qk_g,
           r_w_in, r_w_out):
    nbp, t, d = x_prompt.shape
    nbs, ts, _ = x_sample.shape
    n_tab = page_table.shape[1]
    past_len = n_tab * PAGE
    kvw = A_KV * HD
    qw = A_HEADS * HD
    assert t % (2 * LANE) == 0 and t >= WINDOW and ts == 8 and past_len >= WINDOW
    assert cache_nsa_win.shape[2] == WINDOW

    n_c = nbp + nbs
    pad_c = (-n_c) % 8
    c_all = jnp.concatenate([c_prompt, c_sample, jnp.zeros((pad_c, d), F32)], axis=0)
    mod = ada_modulation(c_all, ada_w, ada_b).reshape(ada_w.shape[0], n_c + pad_c, 3, d)

    def mods(layer):
        m = mod[layer]
        return (m[:nbp, 0], m[:nbp, 1], m[:nbp, 2]), (m[nbp:n_c, 0], m[nbp:n_c, 1], m[nbp:n_c, 2])

    (sh_p, sc_p, gt_p), (sh_s, sc_s, gt_s) = mods(0)
    hp = norm_modulate(x_prompt, norm_g[0], sc_p, sh_p)
    hs = norm_modulate(x_sample, norm_g[0], sc_s, sh_s)
    w_in_t = jnp.swapaxes(a_w_in[0], 0, 1)
    n_main = w_in_t.shape[0] - 3 * A_HEADS
    n_qkv = qw + 6 * kvw
    proj_p, proj_s = project(hp, hs, w_in_t, n_cols=n_qkv, w_is_nk=True)
    zt_p, z_s = project_gate_t(hp, hs, w_in_t, n_qkv, n_main - n_qkv)
    w_gate = jnp.pad(w_in_t[n_main:], ((0, LANE - 3 * A_HEADS), (0, 0)))
    gl_p, gl_s = project(hp, hs, w_gate, w_is_nk=True)
    gl_p, gl_s = gl_p[:, :3 * A_HEADS], gl_s[:, :3 * A_HEADS]
    qk_g = a_qk_g[0]

    qn_p, rows_p, win_p, rows_t_p, win_t_p = nsa_prep(proj_p, qk_g)
    qn_s, rows_s, win_s, rows_t_s, _ = nsa_prep(proj_s, qk_g)

    w_pairs = _w1_pairs(a_cmp_w1[0])
    pe = a_cmp_pe[0]
    w1 = a_cmp_w1[0].reshape(2, CMP_LEN, HD, HD)
    w2 = a_cmp_w2[0]
    cpp = PAGE // CMP_STRIDE

    pages_p = t // PAGE
    ident = jnp.arange(nbp * pages_p, dtype=jnp.int32).reshape(nbp, pages_p)
    ab_p = compress_stage_a(rows_t_p.reshape(nbp * pages_p, PAGE, 4 * A_KV, HD), ident, w_pairs,
                            min(PAGES_PER_STEP, pages_p))
    zero_tail = jnp.zeros((nbp, cpp, NCK, 2 * HD), F32)
    cmp_p = compress_stage_b(ab_p, zero_tail, pe, w1, w2, qk_g)

    nq = t // Q_BLOCK
    n_ch_p = t // CMP_STRIDE
    tsel, twin, tcmp = prompt_bias_tables(rel_bias, nq, n_ch_p)
    n_cmp_p = t // CMP_STRIDE - 1
    n_sel_p = -(-t // SEL_BLOCK)
    n_sel_pad_p = -(-n_sel_p // 16) * 16
    msel_p = _sel_matrix(n_cmp_p, n_ch_p, n_sel_pad_p).T
    expand_p = _expand_matrix(n_sel_pad_p, t).T
    gates_g = jnp.transpose(gl_p.reshape(nbp * t, 3, A_KV, A_HPG), (2, 1, 3, 0)).reshape(A_KV, 3 * A_HPG, nbp * t)
    mixed_p = nsa_prompt_attention(qn_p, cmp_p, rows_p, win_p, zt_p, gates_g,
                                   tsel, twin, tcmp, msel_p, expand_p, nbp, t)

    n_pool = cache_nsa_kv.shape[1]
    cache4d = cache_nsa_kv[0].reshape(n_pool, PAGE, 4 * A_KV, HD)
    n_pages = min(PAGES_PER_STEP, n_tab)
    ab_s = compress_stage_a(cache4d, page_table, w_pairs, n_pages)
    tail_rows = jnp.pad(rows_s.reshape(nbs, ts, 4 * kvw), ((0, 0), (0, PAGE - ts), (0, 0)))
    ident_s = jnp.arange(nbs, dtype=jnp.int32).reshape(nbs, 1)
    ab_tail = compress_stage_a(tail_rows.reshape(nbs, PAGE, 4 * A_KV, HD), ident_s, w_pairs, 1)
    cmp_s = compress_stage_b(ab_s, ab_tail, pe, w1, w2, qk_g)

    n_cmp_s = cmp_s.shape[1] // NCK
    n_steps = n_tab // n_pages
    bps = n_pages * PAGE // SEL_BLOCK
    n_sel_pad = (n_steps + 1) * bps
    nkeys = (n_steps + 1) * n_pages * PAGE
    nkw = WINDOW + LANE
    b_cmp, b_sel, b_win = sample_bias_rows(rel_bias, past_len, ts, n_cmp_s, nkeys, nkw)
    msel_s = _sel_matrix(n_cmp_s, n_cmp_s, n_sel_pad)
    expand_s = _expand_matrix(bps, n_pages * PAGE)
    q5 = qn_s.reshape(nbs, ts, A_KV, A_HPG, HD)
    eye = jnp.eye(A_KV, dtype=BF16)
    qbd = jnp.einsum('bqghd,ge->bghqed', q5, eye).reshape(nbs, A_HEADS * ts, kvw)
    win_new = win_s.reshape(nbs, ts, 2 * kvw)
    win_cache = cache_nsa_win[0].reshape(nbs, WINDOW, 2 * kvw)
    win_all = jnp.concatenate([win_cache, win_new], axis=1)
    win_pad = jnp.pad(win_all, ((0, 0), (0, nkw - WINDOW - ts), (0, 0)))
    tail_sel = tail_rows[:, :, 2 * kvw:]
    mixed_s = nsa_sample_attention(qbd, cmp_s, cache4d, page_table, tail_sel, win_pad, z_s, gl_s,
                                   b_cmp, b_sel, b_win, msel_s, expand_s, past_len, ts).astype(BF16)

    xs_flat = x_sample.reshape(nbs * ts, d)
    gate_s_rows = jnp.repeat(gt_s, ts, axis=0)
    x1p, x1s = project_residual(mixed_p, mixed_s, a_w_out[0], x_prompt.reshape(nbp * t, d), gt_p,
                                xs_flat, gate_s_rows, t)

    (sh_p, sc_p, gt_p), (sh_s, sc_s, gt_s) = mods(1)
    hp = norm_modulate(x1p.reshape(nbp, t, d), norm_g[1], sc_p, sh_p)
    hs = norm_modulate(x1s.reshape(nbs, ts, d), norm_g[1], sc_s, sh_s)
    rp, rs = project(hp, hs, r_w_in[0])
    s0 = jnp.zeros((nbp,) + state_ret.shape[2:], F32)
    op, ret_p = retention(rp, s0, 0, nbp, t)
    os_, ret_s = retention(rs, state_ret[0], past_len, nbs, ts)
    os_ = os_.astype(BF16)
    gate_s_rows = jnp.repeat(gt_s, ts, axis=0)
    x2p, x2s = project_residual(op, os_, r_w_out[0], x1p, gt_p, x1s, gate_s_rows, t)

    kv_p = rows_t_p.reshape(1, nbp, t, 4, A_KV, HD)
    kv_s = rows_t_s.reshape(1, nbs, ts, 4, A_KV, HD)
    wst_p = win_t_p.reshape(nbp, t, 2, A_KV, HD)[None, :, t - WINDOW:]
    wst_s = win_all[:, ts:].reshape(1, nbs, WINDOW, 2, A_KV, HD)
    return (x2p.reshape(nbp, t, d), x2s.reshape(nbs, ts, d), kv_p, kv_s, wst_p, wst_s,
            ret_p[None], ret_s[None])
```

```python
import functools
import math

import numpy as np
import jax
import jax.numpy as jnp
from jax import lax
from jax.experimental import pallas as pl
from jax.experimental.pallas import tpu as pltpu

EPS = 1e-6
HD = 128
A_KV = 4
A_HPG = 8
A_HEADS = A_KV * A_HPG
CMP_STRIDE = 16
CMP_LEN = 32
SEL_BLOCK = 64
SEL_RATIO = SEL_BLOCK // CMP_STRIDE
SEL_TOPK = 16
WINDOW = 512
Q_BLOCK = 128
FORCE_SCORE = 1e4
NEG = -1e30
TINY = 1e-30
N_BUCKETS = 32
MAX_DISTANCE = 1024
R_CHUNK = 128
ROPE_BASE = 10000.0
LOG2E = math.log2(math.e)
Q_SCALE = HD ** -0.5 * LOG2E
PAGE = 128
PAGES_PER_STEP = 16
RET_HEADS_PER_STEP = 8
SEL_KEY_TILE = 512

LANE = 128
VMEM_LIMIT = 60 * 1024 * 1024

F32 = jnp.float32
BF16 = jnp.bfloat16


def _cparams(sem):
    return pltpu.CompilerParams(dimension_semantics=sem, vmem_limit_bytes=VMEM_LIMIT)


def _dot(a, b):
    return jnp.dot(a, b, preferred_element_type=F32)


def _dot_nt(a, b):
    return lax.dot_general(a, b, (((1,), (1,)), ((), ())), preferred_element_type=F32)


def _dot_tn(a, b):
    return lax.dot_general(a, b, (((0,), (0,)), ((), ())), preferred_element_type=F32)


def _silu(x):
    h = 0.5 * x
    return h + h * jnp.tanh(h)


def _sigmoid(x):
    return 0.5 + 0.5 * jnp.tanh(0.5 * x)


def _split3_dot(x, m_bf16):
    hi = x.astype(BF16)
    r1 = x - hi.astype(F32)
    mid = r1.astype(BF16)
    lo = (r1 - mid.astype(F32)).astype(BF16)
    return _dot(hi, m_bf16) + _dot(mid, m_bf16) + _dot(lo, m_bf16)


def _ada_kernel(c_ref, w_ref, b_ref, o_ref):
    a = _silu(c_ref[...]).astype(BF16)
    o_ref[...] = _dot(a, w_ref[...].astype(BF16)) + b_ref[...]


def ada_modulation(c_all, ada_w, ada_b):
    depth, d, n = ada_w.shape
    rows = c_all.shape[0]
    tn = 512
    return pl.pallas_call(
        _ada_kernel,
        grid=(depth, n // tn),
        in_specs=[
            pl.BlockSpec((rows, d), lambda l, j: (0, 0)),
            pl.BlockSpec((None, d, tn), lambda l, j: (l, 0, j)),
            pl.BlockSpec((None, 1, tn), lambda l, j: (l, 0, j)),
        ],
        out_specs=pl.BlockSpec((None, rows, tn), lambda l, j: (l, 0, j)),
        out_shape=jax.ShapeDtypeStruct((depth, rows, n), F32),
        compiler_params=_cparams(("arbitrary", "arbitrary")),
        name="ada_modulation",
    )(c_all, ada_w, ada_b.reshape(depth, 1, n))


def _norm_mod_kernel(x_ref, g_ref, sc_ref, sh_ref, o_ref):
    x = x_ref[...]
    y = x * lax.rsqrt(jnp.mean(x * x, axis=-1, keepdims=True) + EPS)
    y = y * g_ref[...]
    o_ref[...] = (y * (1.0 + sc_ref[...]) + sh_ref[...]).astype(o_ref.dtype)


def norm_modulate(x, g, scale, shift):
    b, t, d = x.shape
    tt = min(t, 256)
    out = pl.pallas_call(
        _norm_mod_kernel,
        grid=(b, t // tt),
        in_specs=[
            pl.BlockSpec((None, tt, d), lambda i, j: (i, j, 0)),
            pl.BlockSpec((1, d), lambda i, j: (0, 0)),
            pl.BlockSpec((None, 1, d), lambda i, j: (i, 0, 0)),
            pl.BlockSpec((None, 1, d), lambda i, j: (i, 0, 0)),
        ],
        out_specs=pl.BlockSpec((None, tt, d), lambda i, j: (i, j, 0)),
        out_shape=jax.ShapeDtypeStruct((b, t, d), BF16),
        compiler_params=_cparams(("arbitrary", "arbitrary")),
        name="norm_modulate",
    )(x, g.reshape(1, d), scale.reshape(b, 1, d), shift.reshape(b, 1, d))
    return out.reshape(b * t, d)


def _stage_weight_tile(w_hbm, wf_ref, wb_ref, sem, *, w_is_nk, tn, j0):
    j = pl.program_id(0)

    def copy(jj):
        start = pl.multiple_of((j0 + jj) * tn, tn)
        src = w_hbm.at[pl.ds(start, tn), :] if w_is_nk else w_hbm.at[:, pl.ds(start, tn)]
        return pltpu.make_async_copy(src, wf_ref, sem)

    @pl.when(j == 0)
    def _():
        copy(0).start()

    copy(j).wait()
    wb_ref[...] = wf_ref[...].astype(BF16)

    @pl.when(j + 1 < pl.num_programs(0))
    def _():
        copy(j + 1).start()


def _proj_kernel(xp_ref, xs_ref, w_hbm, op_ref, os_ref, wf_ref, wb_ref, sem, *, stage, mm):
    @pl.when(pl.program_id(1) == 0)
    def _():
        stage(w_hbm, wf_ref, wb_ref, sem)
        os_ref[...] = mm(xs_ref[...], wb_ref[...])

    op_ref[...] = mm(xp_ref[...], wb_ref[...])


def _proj_gate_t_kernel(xp_ref, xs_ref, w_hbm, opt_ref, os_ref, wf_ref, wb_ref, sem, *, stage):
    @pl.when(pl.program_id(1) == 0)
    def _():
        stage(w_hbm, wf_ref, wb_ref, sem)
        os_ref[...] = _silu(_dot_nt(xs_ref[...], wb_ref[...]))

    opt_ref[...] = _silu(_dot_nt(wb_ref[...], xp_ref[...]))


def _weight_scratch(shape):
    return [pltpu.VMEM(shape, F32), pltpu.VMEM(shape, BF16), pltpu.SemaphoreType.DMA(())]


def project_gate_t(xp, xs, w_nk, row0, n_rows):
    mp, k = xp.shape
    ms = xs.shape[0]
    tm, tn = _proj_tiles(mp, k, n_rows)
    assert row0 % tn == 0
    stage = functools.partial(_stage_weight_tile, w_is_nk=True, tn=tn, j0=row0 // tn)
    return pl.pallas_call(
        functools.partial(_proj_gate_t_kernel, stage=stage),
        grid=(n_rows // tn, mp // tm),
        in_specs=[
            pl.BlockSpec((tm, k), lambda j, i: (i, 0)),
            pl.BlockSpec((ms, k), lambda j, i: (0, 0)),
            pl.BlockSpec(memory_space=pl.ANY),
        ],
        out_specs=[
            pl.BlockSpec((tn, tm), lambda j, i: (j, i)),
            pl.BlockSpec((ms, tn), lambda j, i: (0, j)),
        ],
        out_shape=[jax.ShapeDtypeStruct((n_rows, mp), F32), jax.ShapeDtypeStruct((ms, n_rows), F32)],
        scratch_shapes=_weight_scratch((tn, k)),
        compiler_params=_cparams(("arbitrary", "arbitrary")),
        name="project_gate_t",
    )(xp, xs, w_nk)


def _proj_res_kernel(xp_ref, xs_ref, w_hbm, rp_ref, gp_ref, rs_ref, gs_ref, op_ref, os_ref,
                     wf_ref, wb_ref, sem, *, stage):
    @pl.when(pl.program_id(1) == 0)
    def _():
        stage(w_hbm, wf_ref, wb_ref, sem)
        os_ref[...] = rs_ref[...] + gs_ref[...] * _dot(xs_ref[...], wb_ref[...])

    op_ref[...] = rp_ref[...] + gp_ref[...] * _dot(xp_ref[...], wb_ref[...])


def _proj_tiles(mp, k, n, residual=False):
    if k <= 4096:
        tm, tn = (1024, 512) if residual else (1024, 1024)
    else:
        tm, tn = 512, 512
    tm = min(tm, mp)
    while mp % tm:
        tm //= 2
    tn = min(tn, n)
    while n % tn:
        tn //= 2
    return tm, tn


def project(xp, xs, w, n_cols=None, w_is_nk=False):
    mp, k = xp.shape
    ms = xs.shape[0]
    n = w.shape[0 if w_is_nk else 1] if n_cols is None else n_cols
    tm, tn = _proj_tiles(mp, k, n)
    stage = functools.partial(_stage_weight_tile, w_is_nk=w_is_nk, tn=tn, j0=0)
    return pl.pallas_call(
        functools.partial(_proj_kernel, stage=stage, mm=_dot_nt if w_is_nk else _dot),
        grid=(n // tn, mp // tm),
        in_specs=[
            pl.BlockSpec((tm, k), lambda j, i: (i, 0)),
            pl.BlockSpec((ms, k), lambda j, i: (0, 0)),
            pl.BlockSpec(memory_space=pl.ANY),
        ],
        out_specs=[
            pl.BlockSpec((tm, tn), lambda j, i: (i, j)),
            pl.BlockSpec((ms, tn), lambda j, i: (0, j)),
        ],
        out_shape=[jax.ShapeDtypeStruct((mp, n), F32), jax.ShapeDtypeStruct((ms, n), F32)],
        scratch_shapes=_weight_scratch((tn, k) if w_is_nk else (k, tn)),
        compiler_params=_cparams(("arbitrary", "arbitrary")),
        name="project",
    )(xp, xs, w)


def project_residual(xp, xs, w, res_p, gate_p, res_s, gate_s, rows_per_batch):
    mp, k = xp.shape
    ms = xs.shape[0]
    n = w.shape[1]
    tm, tn = _proj_tiles(mp, k, n, residual=True)
    tm = min(tm, rows_per_batch)
    tpb = rows_per_batch // tm
    nb = gate_p.shape[0]
    stage = functools.partial(_stage_weight_tile, w_is_nk=False, tn=tn, j0=0)
    return pl.pallas_call(
        functools.partial(_proj_res_kernel, stage=stage),
        grid=(n // tn, mp // tm),
        in_specs=[
            pl.BlockSpec((tm, k), lambda j, i: (i, 0)),
            pl.BlockSpec((ms, k), lambda j, i: (0, 0)),
            pl.BlockSpec(memory_space=pl.ANY),
            pl.BlockSpec((tm, tn), lambda j, i: (i, j)),
            pl.BlockSpec((None, 1, tn), lambda j, i: (i // tpb, 0, j)),
            pl.BlockSpec((ms, tn), lambda j, i: (0, j)),
            pl.BlockSpec((ms, tn), lambda j, i: (0, j)),
        ],
        out_specs=[
            pl.BlockSpec((tm, tn), lambda j, i: (i, j)),
            pl.BlockSpec((ms, tn), lambda j, i: (0, j)),
        ],
        out_shape=[jax.ShapeDtypeStruct((mp, n), F32), jax.ShapeDtypeStruct((ms, n), F32)],
        scratch_shapes=_weight_scratch((k, tn)),
        compiler_params=_cparams(("arbitrary", "arbitrary")),
        name="project_residual",
    )(xp, xs, w, res_p, gate_p.reshape(nb, 1, n), res_s, gate_s)


def _nsa_prep_kernel(q_ref, a_ref, b_ref, c_ref, g_ref, qn_ref, rows_ref, win_ref, rows_t_ref, win_t_ref):
    def hnorm(x, g):
        return x * lax.rsqrt(jnp.mean(x * x, axis=-1, keepdims=True) + EPS) * g

    g = g_ref[...]
    for j in range(A_HEADS):
        sl = slice(j * HD, (j + 1) * HD)
        qn_ref[:, sl] = (hnorm(q_ref[:, sl], g[0:1]) * Q_SCALE).astype(qn_ref.dtype)
    kvw = A_KV * HD
    sls = [slice(j * HD, (j + 1) * HD) for j in range(2 * A_KV)]
    row_slots = ([a_ref[:, s] for s in sls] + [hnorm(b_ref[:, s], g[2:3]) for s in sls[:A_KV]]
                 + [b_ref[:, s] for s in sls[A_KV:]])
    win_slots = [hnorm(c_ref[:, s], g[3:4]) for s in sls[:A_KV]] + [c_ref[:, s] for s in sls[A_KV:]]
    for j, x in enumerate(row_slots):
        rows_ref[:, j * HD:(j + 1) * HD] = x
    for j, x in enumerate(win_slots):
        win_ref[:, j * HD:(j + 1) * HD] = x
    rows_t_ref[...] = jnp.swapaxes(jnp.stack(row_slots, axis=0), 0, 1)
    win_t_ref[...] = jnp.swapaxes(jnp.stack(win_slots, axis=0), 0, 1)


def nsa_prep(proj, qk_g):
    m = proj.shape[0]
    tm = min(m, 256)
    qw = A_HEADS * HD
    kv2 = 2 * A_KV * HD
    base = qw // kv2
    return pl.pallas_call(
        _nsa_prep_kernel,
        grid=(m // tm,),
        in_specs=[
            pl.BlockSpec((tm, qw), lambda i: (i, 0)),
            pl.BlockSpec((tm, kv2), lambda i: (i, base)),
            pl.BlockSpec((tm, kv2), lambda i: (i, base + 1)),
            pl.BlockSpec((tm, kv2), lambda i: (i, base + 2)),
            pl.BlockSpec((4, HD), lambda i: (0, 0)),
        ],
        out_specs=[
            pl.BlockSpec((tm, qw), lambda i: (i, 0)),
            pl.BlockSpec((tm, 2 * kv2), lambda i: (i, 0)),
            pl.BlockSpec((tm, kv2), lambda i: (i, 0)),
            pl.BlockSpec((tm, 4 * A_KV, HD), lambda i: (i, 0, 0)),
            pl.BlockSpec((tm, 2 * A_KV, HD), lambda i: (i, 0, 0)),
        ],
        out_shape=[
            jax.ShapeDtypeStruct((m, qw), BF16),
            jax.ShapeDtypeStruct((m, 2 * kv2), F32),
            jax.ShapeDtypeStruct((m, kv2), F32),
            jax.ShapeDtypeStruct((m, 4 * A_KV, HD), F32),
            jax.ShapeDtypeStruct((m, 2 * A_KV, HD), F32),
        ],
        compiler_params=_cparams(("arbitrary",)),
        name="nsa_prep",
    )(proj, proj, proj, proj, qk_g)


NCK = 2 * A_KV


def _cmp_a_kernel(pt_ref, *refs, n_pages):
    del pt_ref
    pages = refs[:n_pages]
    w_ref = refs[n_pages]
    o_ref = refs[n_pages + 1]
    cpp = PAGE // CMP_STRIDE
    n_ch = n_pages * cpp

    def halves(tiles, lo):
        return [jnp.concatenate([tiles[a][lo:lo + A_KV], tiles[a + 1][lo:lo + A_KV]], axis=0)
                for a in range(0, len(tiles), 2)]

    acc = [None, None]
    for tp in range(CMP_STRIDE // 2):
        pieces = []
        for pg in pages:
            for ch in range(cpp):
                t0 = ch * CMP_STRIDE + 2 * tp
                pieces.append(jnp.concatenate([pg[t0], pg[t0 + 1]], axis=1))
        for kind in range(2):
            lhs = jnp.concatenate(halves(pieces, kind * A_KV), axis=0).astype(BF16)
            d = _dot(lhs, w_ref[tp, :, kind * 2 * HD:(kind + 1) * 2 * HD])
            acc[kind] = d if acc[kind] is None else acc[kind] + d
    for a in range(0, n_ch, 2):
        r = slice(a * A_KV, (a + 2) * A_KV)
        kk, vv = acc[0][r], acc[1][r]
        o_ref[a] = jnp.concatenate([kk[:A_KV], vv[:A_KV]], axis=0)
        o_ref[a + 1] = jnp.concatenate([kk[A_KV:], vv[A_KV:]], axis=0)


def compress_stage_a(store, page_table, w_pairs, n_pages):
    nb, n_tab = page_table.shape
    steps = n_tab // n_pages
    cpp = PAGE // CMP_STRIDE

    def page_spec(k):
        return pl.BlockSpec((None, PAGE, NCK, HD), lambda b, s, pt: (pt[b, s * n_pages + k], 0, 0, 0))

    grid_spec = pltpu.PrefetchScalarGridSpec(
        num_scalar_prefetch=1,
        grid=(nb, steps),
        in_specs=[page_spec(k) for k in range(n_pages)]
        + [pl.BlockSpec(w_pairs.shape, lambda b, s, pt: (0, 0, 0))],
        out_specs=pl.BlockSpec((None, n_pages * cpp, NCK, 2 * HD), lambda b, s, pt: (b, s, 0, 0)),
    )
    return pl.pallas_call(
        functools.partial(_cmp_a_kernel, n_pages=n_pages),
        grid_spec=grid_spec,
        out_shape=jax.ShapeDtypeStruct((nb, n_tab * cpp, NCK, 2 * HD), F32),
        compiler_params=_cparams(("arbitrary", "arbitrary")),
        name="compress_stage_a",
    )(page_table, *([store] * n_pages), w_pairs)


def _cmp_b_kernel(ab_ref, nx_ref, tail_ref, pe_ref, w1_ref, w2_ref, g_ref, o_ref, const_ref):
    j = pl.program_id(1)
    nch = ab_ref.shape[0]
    rows = nch * NCK

    @pl.when((pl.program_id(0) == 0) & (j == 0))
    def _():
        for kind in range(2):
            const = jnp.zeros((8, HD), F32)
            for t in range(CMP_LEN):
                row = jnp.broadcast_to(pe_ref[kind, t:t + 1, :], (8, HD)).astype(BF16)
                const = const + _dot(row, w1_ref[kind, t].astype(BF16))
            const_ref[kind * A_KV:(kind + 1) * A_KV, :] = const[:A_KV]

    ab = ab_ref[...].reshape(rows, 2 * HD)
    nxt = pltpu.roll(ab[:, HD:], rows - NCK, 0).reshape(nch, NCK, HD)
    first_next = jnp.where(j == pl.num_programs(1) - 1, tail_ref[0, :, HD:], nx_ref[0, :, HD:])
    cid = lax.broadcasted_iota(jnp.int32, (nch, NCK, HD), 0)
    nxt = jnp.where(cid == nch - 1, first_next[None], nxt)
    pre = ab[:, :HD].reshape(nch, NCK, HD) + nxt + const_ref[...][None]
    act = _silu(pre).reshape(rows, HD).astype(BF16)
    w2 = jnp.concatenate([w2_ref[0], w2_ref[1]], axis=1).astype(BF16)
    y2 = _dot(act, w2)
    is_k = (lax.broadcasted_iota(jnp.int32, (rows, HD), 0) & (NCK - 1)) < A_KV
    y = jnp.where(is_k, y2[:, :HD], y2[:, HD:])
    yn = y * lax.rsqrt(jnp.mean(y * y, axis=-1, keepdims=True) + EPS) * g_ref[1:2]
    o_ref[...] = jnp.where(is_k, yn, y).reshape(nch, NCK, HD)


def compress_stage_b(ab, tail, pe, w1, w2, qk_g):
    nb, n_ch = ab.shape[:2]
    tc = min(256, n_ch)
    assert n_ch % tc == 0
    nt = n_ch // tc
    out = pl.pallas_call(
        _cmp_b_kernel,
        grid=(nb, nt),
        in_specs=[
            pl.BlockSpec((None, tc, NCK, 2 * HD), lambda b, j: (b, j, 0, 0)),
            pl.BlockSpec((None, 1, NCK, 2 * HD), lambda b, j: (b, jnp.minimum((j + 1) * tc, n_ch - 1), 0, 0)),
            pl.BlockSpec((None, 1, NCK, 2 * HD), lambda b, j: (b, 0, 0, 0)),
            pl.BlockSpec(pe.shape, lambda b, j: (0, 0, 0)),
            pl.BlockSpec(w1.shape, lambda b, j: (0, 0, 0, 0)),
            pl.BlockSpec(w2.shape, lambda b, j: (0, 0, 0)),
            pl.BlockSpec((4, HD), lambda b, j: (0, 0)),
        ],
        out_specs=pl.BlockSpec((None, tc, NCK, HD), lambda b, j: (b, j, 0, 0)),
        out_shape=jax.ShapeDtypeStruct((nb, n_ch, NCK, HD), F32),
        scratch_shapes=[pltpu.VMEM((NCK, HD), F32)],
        compiler_params=_cparams(("arbitrary", "arbitrary")),
        name="compress_stage_b",
    )(ab, ab, tail, pe, w1, w2, qk_g)
    return out.reshape(nb, n_ch * NCK, HD)


def _cmp_rows(cmp_ref, slot, n):
    return cmp_ref[pl.ds(slot, n, stride=NCK), :]


def _select_topk(score, blk, k_top):
    sel = jnp.zeros(score.shape, F32)
    for _ in range(k_top):
        m = jnp.max(score, axis=-1, keepdims=True)
        cand = jnp.where(score == m, blk, jnp.int32(1 << 30))
        first = jnp.min(cand, axis=-1, keepdims=True)
        pick = blk == first
        sel = jnp.where(pick, 1.0, sel)
        score = jnp.where(pick, -jnp.inf, score)
    return sel


def _select_topk_t(score, blk, k_top, n_blk):
    rank = jnp.zeros(score.shape, F32)
    for i in range(n_blk):
        row = score[i:i + 1, :]
        tie = jnp.where(blk > i, 1.0, 0.0)
        rank = rank + jnp.where(row > score, 1.0, jnp.where(row == score, tie, 0.0))
    return jnp.where(rank < k_top, 1.0, 0.0)


def _bias_select(dist, dmin, dmax, value_of):
    lo = int(_t5_bucket_np(np.array([max(dmin, 0)]))[0])
    hi = int(_t5_bucket_np(np.array([max(dmax, 0)]))[0])
    out = jnp.full(dist.shape, value_of(lo), F32)
    for b in range(lo + 1, hi + 1):
        out = jnp.where(dist >= _BUCKET_START[b], value_of(b), out)
    return out * LOG2E


def _bias_tables_kernel(rb_ref, tsel_ref, twin_ref, tcmp_ref):
    head = pl.program_id(0)
    value_of = lambda b: rb_ref[b, head]
    n_off, nq, nc = tsel_ref.shape[0], tcmp_ref.shape[0], tcmp_ref.shape[1]
    kl = lax.broadcasted_iota(jnp.int32, (LANE, Q_BLOCK), 0)
    ql = lax.broadcasted_iota(jnp.int32, (LANE, Q_BLOCK), 1)
    for o in range(n_off):
        base = (o - 1) * Q_BLOCK
        dist = base + ql - kl
        tile = _bias_select(dist, base - (LANE - 1), base + Q_BLOCK - 1, value_of)
        tsel_ref[o] = tile
        if o < twin_ref.shape[0]:
            twin_ref[o] = tile + jnp.where((dist >= 0) & (dist < WINDOW), 0.0, NEG)
    step = Q_BLOCK // CMP_STRIDE
    rows = nc + (nq - 1) * step
    base = (nq - 1) * Q_BLOCK
    cend = lax.broadcasted_iota(jnp.int32, (rows, Q_BLOCK), 0) * CMP_STRIDE + (CMP_LEN - 1)
    qc = lax.broadcasted_iota(jnp.int32, (rows, Q_BLOCK), 1)
    tall = _bias_select(base + qc - cend, base - ((rows - 1) * CMP_STRIDE + CMP_LEN - 1),
                        base + Q_BLOCK - 1, value_of)
    for i in range(nq):
        r0 = (nq - 1 - i) * step
        tcmp_ref[i] = tall[r0:r0 + nc]


def prompt_bias_tables(rel_bias, nq, n_cmp):
    n_win = min(WINDOW // Q_BLOCK + 3, nq + 1)
    return pl.pallas_call(
        _bias_tables_kernel,
        grid=(A_HEADS,),
        in_specs=[pl.BlockSpec(memory_space=pltpu.SMEM)],
        out_specs=[
            pl.BlockSpec((None, nq + 1, LANE, Q_BLOCK), lambda h: (h // A_HPG, 0, 0, h % A_HPG)),
            pl.BlockSpec((None, n_win, LANE, Q_BLOCK), lambda h: (h // A_HPG, 0, 0, h % A_HPG)),
            pl.BlockSpec((None, nq, n_cmp, Q_BLOCK), lambda h: (h // A_HPG, 0, 0, h % A_HPG)),
        ],
        out_shape=[
            jax.ShapeDtypeStruct((A_KV, nq + 1, LANE, A_HPG * Q_BLOCK), F32),
            jax.ShapeDtypeStruct((A_KV, n_win, LANE, A_HPG * Q_BLOCK), F32),
            jax.ShapeDtypeStruct((A_KV, nq, n_cmp, A_HPG * Q_BLOCK), F32),
        ],
        compiler_params=_cparams(("arbitrary",)),
        name="prompt_bias_tables",
    )(rel_bias)


def _bias_rows_kernel(rb_ref, cmp_ref, sel_ref, win_ref, *, past_len, n_q):
    head = pl.program_id(0)
    value_of = lambda b: rb_ref[b, head]

    def fill(ref, key_pos, kmin, kmax, q0):
        q = lax.broadcasted_iota(jnp.int32, ref.shape, 0)
        k = lax.broadcasted_iota(jnp.int32, ref.shape, 1)
        ref[...] = _bias_select(q0 + q - key_pos(k), q0 - kmax, q0 + n_q - 1 - kmin, value_of)

    nc, nk, nw = cmp_ref.shape[1], sel_ref.shape[1], win_ref.shape[1]
    fill(cmp_ref, lambda k: k * CMP_STRIDE + (CMP_LEN - 1), CMP_LEN - 1, (nc - 1) * CMP_STRIDE + CMP_LEN - 1, past_len)
    fill(sel_ref, lambda k: k, 0, nk - 1, past_len)
    fill(win_ref, lambda k: k, 0, nw - 1, WINDOW)


def sample_bias_rows(rel_bias, past_len, n_q, n_cmp, n_keys, n_win):
    shapes = [(A_HEADS * n_q, n) for n in (n_cmp, n_keys, n_win)]
    return pl.pallas_call(
        functools.partial(_bias_rows_kernel, past_len=past_len, n_q=n_q),
        grid=(A_HEADS,),
        in_specs=[pl.BlockSpec(memory_space=pltpu.SMEM)],
        out_specs=[pl.BlockSpec((n_q, s[1]), lambda h: (h, 0)) for s in shapes],
        out_shape=[jax.ShapeDtypeStruct(s, F32) for s in shapes],
        compiler_params=_cparams(("arbitrary",)),
        name="sample_bias_rows",
    )(rel_bias)


def _nsa_prompt_kernel(q_ref, cmp_ref, ks_ref, vs_ref, kw_ref, vw_ref, zc_ref, zs_ref, zw_ref,
                       gt_ref, tsel_ref, twin_ref, tcmp_ref, msel_ref, exp_ref, o_ref,
                       qa_ref, acc_ref, m_ref, l_ref, madd_ref, mix_ref, *, n_sel_pad, n_sel, k_top):
    i = pl.program_id(2)
    qb = Q_BLOCK
    n_kt, kt = madd_ref.shape[:2]
    tpk = kt // LANE
    nc = cmp_ref.shape[0] // NCK
    gates = _sigmoid(gt_ref[...])
    hs = [slice(h * HD, (h + 1) * HD) for h in range(A_HPG)]
    for h in range(A_HPG):
        qa_ref[hs[h], :] = q_ref[:, hs[h]]
    qa = qa_ref[...]

    def heads(x):
        return jnp.concatenate([x] * A_HPG, axis=1)

    def gate_row(branch):
        return jnp.concatenate([gates[branch * A_HPG + h: branch * A_HPG + h + 1, :] for h in range(A_HPG)], axis=1)

    def emit(branch, o_t, first):
        z_ref = (zc_ref, zs_ref, zw_ref)[branch]
        for h in range(A_HPG):
            val = o_t[:, hs[h]] * z_ref[hs[h], :]
            if first:
                mix_ref[hs[h], :] = val
            else:
                mix_ref[hs[h], :] = mix_ref[hs[h], :] + val

    grp = pl.program_id(0)
    kc = _cmp_rows(cmp_ref, grp, nc).astype(BF16)
    vc = _cmp_rows(cmp_ref, A_KV + grp, nc).astype(BF16)
    cend = lax.broadcasted_iota(jnp.int32, (nc, qb), 0) * CMP_STRIDE + (CMP_LEN - 1)
    qpos_c = i * qb + lax.broadcasted_iota(jnp.int32, (nc, qb), 1)
    cvalid = heads(cend <= qpos_c)
    s = jnp.where(cvalid, _dot_nt(kc, qa) + tcmp_ref[...], NEG)
    mx = jnp.max(s, axis=0, keepdims=True)
    e = jnp.where(cvalid, jnp.exp2(s - mx), 0.0)
    p = e * (1.0 / jnp.maximum(jnp.sum(e, axis=0, keepdims=True), TINY))
    imp = p[:, hs[0]]
    for h in range(1, A_HPG):
        imp = imp + p[:, hs[h]]
    emit(0, _dot_tn(vc, p.astype(BF16)) * gate_row(0), True)

    hi = imp.astype(BF16)
    r1 = imp - hi.astype(F32)
    mid = r1.astype(BF16)
    lo = (r1 - mid.astype(F32)).astype(BF16)
    msel = msel_ref[...]
    imp_sel = _dot(msel, hi) + _dot(msel, mid) + _dot(msel, lo)
    blk = lax.broadcasted_iota(jnp.int32, (n_sel_pad, qb), 0)
    qpos = i * qb + lax.broadcasted_iota(jnp.int32, (n_sel_pad, qb), 1)
    cur = jnp.right_shift(qpos, int(math.log2(SEL_BLOCK)))
    forced = (blk == 0) | (blk == cur) | (blk == cur - 1)
    valid = blk * SEL_BLOCK <= qpos
    score = jnp.where(valid, jnp.where(forced, FORCE_SCORE, imp_sel), -1.0)
    score = jnp.where(blk < n_sel, score, -jnp.inf)
    sel = _select_topk_t(score, blk, k_top, n_sel).astype(BF16)
    kpos2 = lax.broadcasted_iota(jnp.int32, (kt, qb), 0)
    qpos2 = i * qb + lax.broadcasted_iota(jnp.int32, (kt, qb), 1)
    for jj in range(n_kt):
        hit = _dot(exp_ref[jj * kt:(jj + 1) * kt, :], sel)
        ok = (hit > 0.5) & ((jj * kt + kpos2) <= qpos2)
        madd_ref[jj] = jnp.where(ok, 0.0, NEG)

    def reset():
        m_ref[...] = jnp.full(m_ref.shape, NEG, F32)
        l_ref[...] = jnp.zeros(l_ref.shape, F32)
        acc_ref[...] = jnp.zeros(acc_ref.shape, F32)

    def tile_update(k, v, bias, madd):
        s = _dot_nt(k, qa) + bias
        if madd is not None:
            s = s + heads(madd)
        m_old = m_ref[...]
        m_new = jnp.maximum(m_old, jnp.max(s, axis=0, keepdims=True))
        alpha = jnp.exp2(m_old - m_new)
        p = jnp.exp2(s - m_new)
        l_ref[...] = alpha * l_ref[...] + jnp.sum(p, axis=0, keepdims=True)
        m_ref[...] = m_new
        acc_ref[...] = acc_ref[...] * alpha + _dot_tn(v, p.astype(BF16))

    def finish(branch):
        emit(branch, acc_ref[...] * (gate_row(branch) / l_ref[...]), False)

    reset()

    def sel_body(jj, carry):
        k0 = pl.multiple_of(jj * kt, kt)
        k = ks_ref[pl.ds(k0, kt), :].astype(BF16)
        v = vs_ref[pl.ds(k0, kt), :].astype(BF16)
        off = i - tpk * jj
        bias = jnp.concatenate([tsel_ref[jnp.maximum(off + 1 - r, 0)] for r in range(tpk)], axis=0)
        tile_update(k, v, bias, madd_ref[jj])
        return carry

    lax.fori_loop(0, i // tpk + 1, sel_body, 0)
    finish(1)

    reset()
    wt = 2 * LANE
    first = i // 2
    lowest = jnp.maximum((i - WINDOW // qb) // 2, 0)

    def win_body(t, carry):
        tile = first - t
        k0 = pl.multiple_of(tile * wt, wt)
        k = kw_ref[pl.ds(k0, wt), :].astype(BF16)
        v = vw_ref[pl.ds(k0, wt), :].astype(BF16)
        off = i - 2 * tile
        bias = jnp.concatenate([twin_ref[off + 1], twin_ref[off]], axis=0)
        tile_update(k, v, bias, None)
        return carry

    lax.fori_loop(0, first - lowest + 1, win_body, 0)
    finish(2)
    for h in range(A_HPG):
        o_ref[:, hs[h]] = mix_ref[hs[h], :].T.astype(o_ref.dtype)


def nsa_prompt_attention(qn, cmp_kv, rows, win, z_t, gates_g, tsel, twin, tcmp, msel, expand, nb, t):
    nq = t // Q_BLOCK
    qw = A_HPG * HD
    n_sel = -(-t // SEL_BLOCK)
    k_top = min(SEL_TOPK, n_sel)
    kt = SEL_KEY_TILE if t % SEL_KEY_TILE == 0 else 2 * LANE
    n_kt = t // kt
    n_cmp = cmp_kv.shape[1] // NCK
    n_sel_pad = msel.shape[0]
    kernel = functools.partial(_nsa_prompt_kernel, n_sel_pad=n_sel_pad, n_sel=n_sel, k_top=k_top)
    return pl.pallas_call(
        kernel,
        grid=(A_KV, nb, nq),
        in_specs=[
            pl.BlockSpec((Q_BLOCK, qw), lambda g, b, i: (b * nq + i, g)),
            pl.BlockSpec((None, n_cmp * NCK, HD), lambda g, b, i: (b, 0, 0)),
            pl.BlockSpec((t, HD), lambda g, b, i: (b, 2 * A_KV + g)),
            pl.BlockSpec((t, HD), lambda g, b, i: (b, 3 * A_KV + g)),
            pl.BlockSpec((t, HD), lambda g, b, i: (b, g)),
            pl.BlockSpec((t, HD), lambda g, b, i: (b, A_KV + g)),
            pl.BlockSpec((qw, Q_BLOCK), lambda g, b, i: (g, b * nq + i)),
            pl.BlockSpec((qw, Q_BLOCK), lambda g, b, i: (A_KV + g, b * nq + i)),
            pl.BlockSpec((qw, Q_BLOCK), lambda g, b, i: (2 * A_KV + g, b * nq + i)),
            pl.BlockSpec((None, 3 * A_HPG, Q_BLOCK), lambda g, b, i: (g, 0, b * nq + i)),
            pl.BlockSpec((None, nq + 1, LANE, qw), lambda g, b, i: (g, 0, 0, 0)),
            pl.BlockSpec((None, twin.shape[1], LANE, qw), lambda g, b, i: (g, 0, 0, 0)),
            pl.BlockSpec((None, None, n_cmp, qw), lambda g, b, i: (g, i, 0, 0)),
            pl.BlockSpec((n_sel_pad, n_cmp), lambda g, b, i: (0, 0)),
            pl.BlockSpec((t, n_sel_pad), lambda g, b, i: (0, 0)),
        ],
        out_specs=pl.BlockSpec((Q_BLOCK, qw), lambda g, b, i: (b * nq + i, g)),
        out_shape=jax.ShapeDtypeStruct((nb * t, A_HEADS * HD), BF16),
        scratch_shapes=[
            pltpu.VMEM((qw, HD), BF16),
            pltpu.VMEM((HD, qw), F32),
            pltpu.VMEM((1, qw), F32),
            pltpu.VMEM((1, qw), F32),
            pltpu.VMEM((n_kt, kt, Q_BLOCK), F32),
            pltpu.VMEM((qw, Q_BLOCK), F32),
        ],
        compiler_params=_cparams(("arbitrary", "arbitrary", "arbitrary")),
        name="nsa_prompt_attention",
    )(qn, cmp_kv, rows, rows, win, win, z_t, z_t, z_t, gates_g, tsel, twin, tcmp, msel, expand)


def _diag_blocks(o_full, rows_per_group):
    return jnp.concatenate(
        [o_full[g * rows_per_group:(g + 1) * rows_per_group, g * HD:(g + 1) * HD] for g in range(A_KV)], axis=0)


def _nsa_sample_cmp_kernel(q_ref, cmp_ref, bias_ref, msel_ref, oc_ref, sel_ref,
                           *, past_len, n_sel, k_top, n_q, blocks_per_step):
    n_cmp = cmp_ref.shape[0] // NCK
    kc = jnp.concatenate([_cmp_rows(cmp_ref, g, n_cmp) for g in range(A_KV)], axis=1).astype(BF16)
    vc = jnp.concatenate([_cmp_rows(cmp_ref, A_KV + g, n_cmp) for g in range(A_KV)], axis=1).astype(BF16)
    rows = q_ref.shape[0]
    rpg = A_HPG * n_q
    q = q_ref[...]
    s = _dot_nt(q, kc) + bias_ref[...]
    rowi = lax.broadcasted_iota(jnp.int32, (rows, n_cmp), 0)
    ci = lax.broadcasted_iota(jnp.int32, (rows, n_cmp), 1)
    qpos = past_len + (rowi & (n_q - 1))
    valid = (ci * CMP_STRIDE + (CMP_LEN - 1)) <= qpos
    s = jnp.where(valid, s, NEG)
    mx = jnp.max(s, axis=-1, keepdims=True)
    e = jnp.where(valid, jnp.exp2(s - mx), 0.0)
    p = e / jnp.maximum(jnp.sum(e, axis=-1, keepdims=True), TINY)
    oc_ref[...] = _diag_blocks(_dot(p.astype(BF16), vc), rpg)
    imps = []
    for g in range(A_KV):
        acc = p[g * rpg: g * rpg + n_q]
        for h in range(1, A_HPG):
            acc = acc + p[g * rpg + h * n_q: g * rpg + (h + 1) * n_q]
        imps.append(acc)
    imp = jnp.concatenate(imps, axis=0)
    imp_sel = _split3_dot(imp, msel_ref[...])
    shp = imp_sel.shape
    blk = lax.broadcasted_iota(jnp.int32, shp, 1)
    qp = past_len + (lax.broadcasted_iota(jnp.int32, shp, 0) & (n_q - 1))
    cur = jnp.right_shift(qp, int(math.log2(SEL_BLOCK)))
    forced = (blk == 0) | (blk == cur) | (blk == cur - 1)
    ok = blk * SEL_BLOCK <= qp
    score = jnp.where(ok, jnp.where(forced, FORCE_SCORE, imp_sel), -1.0)
    score = jnp.where(blk < n_sel, score, -jnp.inf)
    sel = _select_topk(score, blk, k_top)
    for st in range(sel_ref.shape[0]):
        sel_ref[st] = sel[:, st * blocks_per_step:(st + 1) * blocks_per_step]


def _nsa_sample_sel_kernel(pt_ref, *refs, n_pages, n_steps, past_len, n_q):
    del pt_ref
    pages = refs[:n_pages]
    q_ref, tail_ref, bias_ref, sel_ref, exp_ref, o_ref, m_ref, l_ref, acc_ref = refs[n_pages:]
    s_id = pl.program_id(1)
    kvw = A_KV * HD
    rows = q_ref.shape[0]
    rpg = A_HPG * n_q

    @pl.when(s_id == 0)
    def _():
        m_ref[...] = jnp.full(m_ref.shape, NEG, F32)
        l_ref[...] = jnp.zeros(l_ref.shape, F32)
        acc_ref[...] = jnp.zeros(acc_ref.shape, F32)

    def update(k, v, nk, key0):
        hit = _dot(sel_ref[...].astype(BF16), exp_ref[:, :nk])
        kpos = key0 + lax.broadcasted_iota(jnp.int32, hit.shape, 1)
        qpos = past_len + (lax.broadcasted_iota(jnp.int32, hit.shape, 0) & (n_q - 1))
        madd = jnp.where((hit > 0.5) & (kpos <= qpos), 0.0, NEG)
        madd = jnp.concatenate(
            [madd[g * n_q:(g + 1) * n_q] for g in range(A_KV) for _ in range(A_HPG)], axis=0)
        s = _dot_nt(q_ref[...], k) + bias_ref[:, :nk] + madd
        m_old = m_ref[...]
        m_new = jnp.maximum(m_old, jnp.max(s, axis=-1, keepdims=True))
        alpha = jnp.exp2(m_old - m_new)
        p = jnp.exp2(s - m_new)
        l_ref[...] = alpha * l_ref[...] + jnp.sum(p, axis=-1, keepdims=True)
        acc_ref[...] = alpha * acc_ref[...] + _dot(p.astype(BF16), v)
        m_ref[...] = m_new

    @pl.when(s_id < n_steps)
    def _():
        by_slot = [jnp.swapaxes(pg[...], 0, 1) for pg in pages]

        def gather(c0):
            cols = [jnp.concatenate([x[c0 + g] for x in by_slot], axis=0) for g in range(A_KV)]
            return jnp.concatenate(cols, axis=1).astype(BF16)

        update(gather(0), gather(A_KV), n_pages * PAGE, s_id * (n_pages * PAGE))

    @pl.when(s_id == n_steps)
    def _():
        update(tail_ref[:, :kvw].astype(BF16), tail_ref[:, kvw:].astype(BF16), PAGE, past_len)
        o_ref[...] = _diag_blocks(acc_ref[...] / l_ref[...], rpg)


def _nsa_sample_mix_kernel(q_ref, win_ref, bias_ref, oc_ref, os_ref, z_ref, gt_ref, o_ref, *, n_q):
    kvw = A_KV * HD
    rows = q_ref.shape[0]
    rpg = A_HPG * n_q
    nk = win_ref.shape[0]
    s = _dot_nt(q_ref[...], win_ref[:, :kvw].astype(BF16)) + bias_ref[...]
    qi = lax.broadcasted_iota(jnp.int32, (rows, nk), 0) & (n_q - 1)
    ki = lax.broadcasted_iota(jnp.int32, (rows, nk), 1)
    dist = qi + WINDOW - ki
    s = s + jnp.where((dist >= 0) & (dist < WINDOW), 0.0, NEG)
    mx = jnp.max(s, axis=-1, keepdims=True)
    e = jnp.exp2(s - mx)
    p = e / jnp.sum(e, axis=-1, keepdims=True)
    ow = _diag_blocks(_dot(p.astype(BF16), win_ref[:, kvw:].astype(BF16)), rpg)
    gates = _sigmoid(gt_ref[...])
    width = A_HEADS * HD
    for hh in range(A_HEADS):
        r = slice(hh * n_q, (hh + 1) * n_q)
        c = slice(hh * HD, (hh + 1) * HD)
        val = None
        for br, o in enumerate((oc_ref, os_ref, ow)):
            zc = slice(br * width + hh * HD, br * width + (hh + 1) * HD)
            gcol = gates[:, br * A_HEADS + hh: br * A_HEADS + hh + 1]
            term = gcol * o[r, :] * z_ref[:, zc]
            val = term if val is None else val + term
        o_ref[:, c] = val.astype(o_ref.dtype)


def nsa_sample_attention(qbd, cmp_kv, cache4d, page_table, tail_sel, win_all, z_s, gl_s,
                         b_cmp, b_sel, b_win, msel, expand, past_len, n_q):
    nb, rows, kvw = qbd.shape
    n_cmp = cmp_kv.shape[1] // NCK
    n_tab = page_table.shape[1]
    n_pages = min(PAGES_PER_STEP, n_tab)
    n_steps = n_tab // n_pages
    bps = n_pages * PAGE // SEL_BLOCK
    n_sel = -(-(past_len + n_q) // SEL_BLOCK)
    k_top = min(SEL_TOPK, n_sel)
    gq = A_KV * n_q

    oc, selmask = pl.pallas_call(
        functools.partial(_nsa_sample_cmp_kernel, past_len=past_len, n_sel=n_sel, k_top=k_top,
                          n_q=n_q, blocks_per_step=bps),
        grid=(nb,),
        in_specs=[
            pl.BlockSpec((None, rows, kvw), lambda b: (b, 0, 0)),
            pl.BlockSpec((None, n_cmp * NCK, HD), lambda b: (b, 0, 0)),
            pl.BlockSpec(b_cmp.shape, lambda b: (0, 0)),
            pl.BlockSpec(msel.shape, lambda b: (0, 0)),
        ],
        out_specs=[
            pl.BlockSpec((None, rows, HD), lambda b: (b, 0, 0)),
            pl.BlockSpec((None, n_steps + 1, gq, bps), lambda b: (b, 0, 0, 0)),
        ],
        out_shape=[
            jax.ShapeDtypeStruct((nb, rows, HD), F32),
            jax.ShapeDtypeStruct((nb, n_steps + 1, gq, bps), F32),
        ],
        compiler_params=_cparams(("arbitrary",)),
        name="nsa_sample_cmp",
    )(qbd, cmp_kv, b_cmp, msel)

    half = 2 * kvw
    last = n_steps - 1

    def page_spec(k):
        return pl.BlockSpec(
            (None, PAGE, NCK, HD),
            lambda b, s, pt: (pt[b, jnp.minimum(s, last) * n_pages + k], 0, 1, 0))

    nkeys = n_pages * PAGE
    grid_spec = pltpu.PrefetchScalarGridSpec(
        num_scalar_prefetch=1,
        grid=(nb, n_steps + 1),
        in_specs=[page_spec(k) for k in range(n_pages)] + [
            pl.BlockSpec((None, rows, kvw), lambda b, s, pt: (b, 0, 0)),
            pl.BlockSpec((None, PAGE, half), lambda b, s, pt: (b, 0, 0)),
            pl.BlockSpec((rows, nkeys), lambda b, s, pt: (0, s)),
            pl.BlockSpec((None, None, gq, bps), lambda b, s, pt: (b, s, 0, 0)),
            pl.BlockSpec(expand.shape, lambda b, s, pt: (0, 0)),
        ],
        out_specs=pl.BlockSpec((None, rows, HD), lambda b, s, pt: (b, 0, 0)),
        scratch_shapes=[
            pltpu.VMEM((rows, 1), F32),
            pltpu.VMEM((rows, 1), F32),
            pltpu.VMEM((rows, kvw), F32),
        ],
    )
    osel = pl.pallas_call(
        functools.partial(_nsa_sample_sel_kernel, n_pages=n_pages, n_steps=n_steps,
                          past_len=past_len, n_q=n_q),
        grid_spec=grid_spec,
        out_shape=jax.ShapeDtypeStruct((nb, rows, HD), F32),
        compiler_params=_cparams(("arbitrary", "arbitrary")),
        name="nsa_sample_sel",
    )(page_table, *([cache4d] * n_pages), qbd, tail_sel, b_sel, selmask, expand)

    nkw = win_all.shape[1]
    zw = z_s.shape[1]
    return pl.pallas_call(
        functools.partial(_nsa_sample_mix_kernel, n_q=n_q),
        grid=(nb,),
        in_specs=[
            pl.BlockSpec((None, rows, kvw), lambda b: (b, 0, 0)),
            pl.BlockSpec((None, nkw, 2 * kvw), lambda b: (b, 0, 0)),
            pl.BlockSpec(b_win.shape, lambda b: (0, 0)),
            pl.BlockSpec((None, rows, HD), lambda b: (b, 0, 0)),
            pl.BlockSpec((None, rows, HD), lambda b: (b, 0, 0)),
            pl.BlockSpec((n_q, zw), lambda b: (b, 0)),
            pl.BlockSpec((n_q, 3 * A_HEADS), lambda b: (b, 0)),
        ],
        out_specs=pl.BlockSpec((n_q, A_HEADS * HD), lambda b: (b, 0)),
        out_shape=jax.ShapeDtypeStruct((nb * n_q, A_HEADS * HD), F32),
        compiler_params=_cparams(("arbitrary",)),
        name="nsa_sample_mix",
    )(qbd, win_all, b_win, oc, osel, z_s, gl_s)


def _retention_kernel(q_ref, k_ref, v_ref, g_ref, cos_ref, sin_ref, dm_ref, qd_ref, kd_ref, cd_ref,
                      s0_ref, o_ref, s_ref, *, dk, dv, hps):
    c = pl.program_id(2)
    half = dk // 2

    @pl.when(c == 0)
    def _():
        s_ref[...] = s0_ref[...]

    cos = cos_ref[...]
    sin = sin_ref[...]

    def rot(x):
        x1, x2 = x[:, :half], x[:, half:]
        return jnp.concatenate([x1 * cos - x2 * sin, x1 * sin + x2 * cos], axis=1)

    for j in range(hps):
        ks, vs = slice(j * dk, (j + 1) * dk), slice(j * dv, (j + 1) * dv)
        q = rot(q_ref[:, ks])
        k = rot(k_ref[:, ks]) * (dk ** -0.5)
        v = v_ref[:, vs].astype(BF16)
        qb = q.astype(BF16)
        att = _dot_nt(qb, k.astype(BF16)) * dm_ref[j]
        state = s_ref[j]
        o = _dot(att.astype(BF16), v) + _dot(qb, state.astype(BF16)) * qd_ref[j]
        s_ref[j] = state * cd_ref[j] + _dot_tn((k * kd_ref[j]).astype(BF16), v)
        o = o * lax.rsqrt(jnp.mean(o * o, axis=-1, keepdims=True) + EPS)
        o_ref[:, vs] = (o * _silu(g_ref[:, vs])).astype(o_ref.dtype)


def retention(proj, state0, q0, nb, t):
    _, nh, dk, dv = state0.shape
    cs = R_CHUNK if t % R_CHUNK == 0 else t
    n = t // cs
    half = dk // 2
    pos = (q0 + jnp.arange(t)).astype(F32)
    inv = jnp.power(ROPE_BASE, -jnp.arange(half, dtype=F32) / half)
    ang = pos[:, None] * inv[None, :]
    cos, sin = jnp.cos(ang), jnp.sin(ang)
    log_g = jnp.log1p(-jnp.exp2(-5.0 - jnp.arange(nh, dtype=F32)))
    ii = jnp.arange(cs, dtype=F32)
    rel = ii[:, None] - ii[None, :]
    causal = rel >= 0
    dmat = jnp.where(causal, jnp.exp(jnp.where(causal, rel, 0.0) * log_g[:, None, None]), 0.0)
    q_dec = jnp.exp((ii + 1.0) * log_g[:, None])[:, :, None]
    k_dec = jnp.exp((cs - 1.0 - ii) * log_g[:, None])[:, :, None]
    c_dec = jnp.exp(cs * log_g)[:, None, None]
    hps = RET_HEADS_PER_STEP
    assert nh % hps == 0
    ng = nh // hps
    vb0 = 2 * nh * dk // (hps * dv)
    return pl.pallas_call(
        functools.partial(_retention_kernel, dk=dk, dv=dv, hps=hps),
        grid=(nb, ng, n),
        in_specs=[
            pl.BlockSpec((cs, hps * dk), lambda b, h, c: (b * n + c, h)),
            pl.BlockSpec((cs, hps * dk), lambda b, h, c: (b * n + c, ng + h)),
            pl.BlockSpec((cs, hps * dv), lambda b, h, c: (b * n + c, vb0 + h)),
            pl.BlockSpec((cs, hps * dv), lambda b, h, c: (b * n + c, vb0 + ng + h)),
            pl.BlockSpec((cs, half), lambda b, h, c: (c, 0)),
            pl.BlockSpec((cs, half), lambda b, h, c: (c, 0)),
            pl.BlockSpec((hps, cs, cs), lambda b, h, c: (h, 0, 0)),
            pl.BlockSpec((hps, cs, 1), lambda b, h, c: (h, 0, 0)),
            pl.BlockSpec((hps, cs, 1), lambda b, h, c: (h, 0, 0)),
            pl.BlockSpec((hps, 1, 1), lambda b, h, c: (h, 0, 0)),
            pl.BlockSpec((None, hps, dk, dv), lambda b, h, c: (b, h, 0, 0)),
        ],
        out_specs=[
            pl.BlockSpec((cs, hps * dv), lambda b, h, c: (b * n + c, h)),
            pl.BlockSpec((None, hps, dk, dv), lambda b, h, c: (b, h, 0, 0)),
        ],
        out_shape=[
            jax.ShapeDtypeStruct((nb * t, nh * dv), BF16 if cs % 16 == 0 else F32),
            jax.ShapeDtypeStruct(state0.shape, F32),
        ],
        compiler_params=_cparams(("arbitrary", "arbitrary", "arbitrary")),
        name="retention",
    )(proj, proj, proj, proj, cos, sin, dmat, q_dec, k_dec, c_dec, state0)


def _t5_bucket_np(dist):
    n = np.maximum(dist, 0)
    exact = N_BUCKETS // 2
    logv = np.log(np.maximum(n, 1).astype(np.float32) / np.float32(exact)) / np.float32(
        math.log(MAX_DISTANCE / exact))
    large = np.minimum(exact + (logv * np.float32(N_BUCKETS - exact)).astype(np.int32), N_BUCKETS - 1)
    return np.where(n < exact, n, large).astype(np.int32)


def _bucket_starts():
    bk = _t5_bucket_np(np.arange(4 * MAX_DISTANCE))
    assert np.all(np.diff(bk) >= 0) and bk[-1] == N_BUCKETS - 1
    return [int(np.argmax(bk >= b)) for b in range(N_BUCKETS)]


_BUCKET_START = _bucket_starts()


def _sel_matrix(n_cmp, n_cmp_pad, n_sel_pad):
    c = np.arange(n_cmp_pad)[:, None]
    j = np.arange(n_sel_pad)[None, :]
    a = (c >= SEL_RATIO * j) & (c < SEL_RATIO * j + SEL_RATIO)
    b = (c >= SEL_RATIO * j - 1) & (c < SEL_RATIO * j + SEL_RATIO - 1)
    m = (a.astype(np.float32) + b.astype(np.float32)) * (c < n_cmp)
    return jnp.asarray(m, BF16)


def _expand_matrix(n_blk_pad, n_keys):
    blk = np.arange(n_blk_pad)[:, None]
    key = np.arange(n_keys)[None, :]
    return jnp.asarray((key // SEL_BLOCK == blk).astype(np.float32), BF16)


def _w1_pairs(w1):
    w = w1.reshape(2, 2, CMP_STRIDE // 2, 2, HD, HD)
    w = jnp.transpose(w, (2, 3, 4, 0, 1, 5))
    return w.reshape(CMP_STRIDE // 2, 2 * HD, 4 * HD).astype(BF16)


def kernel(x_prompt, x_sample, c_prompt, c_sample, cache_nsa_kv, cache_nsa_win, state_ret, page_table,
           norm_g, ada_w, ada_b, rel_bias, a_w_in, a_w_out, a_cmp_pe, a_cmp_w1, a_cmp_w2, a_qk_g,
           r_w_in, r_w_out):
    nbp, t, d = x_prompt.shape
    nbs, ts, _ = x_sample.shape
    n_tab = page_table.shape[1]
    past_len = n_tab * PAGE
    kvw = A_KV * HD
    qw = A_HEADS * HD
    assert t % (2 * LANE) == 0 and t >= WINDOW and ts == 8 and past_len >= WINDOW
    assert cache_nsa_win.shape[2] == WINDOW

    n_c = nbp + nbs
    pad_c = (-n_c) % 8
    c_all = jnp.concatenate([c_prompt, c_sample, jnp.zeros((pad_c, d), F32)], axis=0)
    mod = ada_modulation(c_all, ada_w, ada_b).reshape(ada_w.shape[0], n_c + pad_c, 3, d)

    def mods(layer):
        m = mod[layer]
        return (m[:nbp, 0], m[:nbp, 1], m[:nbp, 2]), (m[nbp:n_c, 0], m[nbp:n_c, 1], m[nbp:n_c, 2])

    (sh_p, sc_p, gt_p), (sh_s, sc_s, gt_s) = mods(0)
    hp = norm_modulate(x_prompt, norm_g[0], sc_p, sh_p)
    hs = norm_modulate(x_sample, norm_g[0], sc_s, sh_s)
    w_in_t = jnp.swapaxes(a_w_in[0], 0, 1)
    n_main = w_in_t.shape[0] - 3 * A_HEADS
    n_qkv = qw + 6 * kvw
    proj_p, proj_s = project(hp, hs, w_in_t, n_cols=n_qkv, w_is_nk=True)
    zt_p, z_s = project_gate_t(hp, hs, w_in_t, n_qkv, n_main - n_qkv)
    w_gate = jnp.pad(w_in_t[n_main:], ((0, LANE - 3 * A_HEADS), (0, 0)))
    gl_p, gl_s = project(hp, hs, w_gate, w_is_nk=True)
    gl_p, gl_s = gl_p[:, :3 * A_HEADS], gl_s[:, :3 * A_HEADS]
    qk_g = a_qk_g[0]

    qn_p, rows_p, win_p, rows_t_p, win_t_p = nsa_prep(proj_p, qk_g)
    qn_s, rows_s, win_s, rows_t_s, _ = nsa_prep(proj_s, qk_g)

    w_pairs = _w1_pairs(a_cmp_w1[0])
    pe = a_cmp_pe[0]
    w1 = a_cmp_w1[0].reshape(2, CMP_LEN, HD, HD)
    w2 = a_cmp_w2[0]
    cpp = PAGE // CMP_STRIDE

    pages_p = t // PAGE
    ident = jnp.arange(nbp * pages_p, dtype=jnp.int32).reshape(nbp, pages_p)
    ab_p = compress_stage_a(rows_t_p.reshape(nbp * pages_p, PAGE, 4 * A_KV, HD), ident, w_pairs,
                            min(PAGES_PER_STEP, pages_p))
    zero_tail = jnp.zeros((nbp, cpp, NCK, 2 * HD), F32)
    cmp_p = compress_stage_b(ab_p, zero_tail, pe, w1, w2, qk_g)

    nq = t // Q_BLOCK
    n_ch_p = t // CMP_STRIDE
    tsel, twin, tcmp = prompt_bias_tables(rel_bias, nq, n_ch_p)
    n_cmp_p = t // CMP_STRIDE - 1
    n_sel_p = -(-t // SEL_BLOCK)
    n_sel_pad_p = -(-n_sel_p // 16) * 16
    msel_p = _sel_matrix(n_cmp_p, n_ch_p, n_sel_pad_p).T
    expand_p = _expand_matrix(n_sel_pad_p, t).T
    gates_g = jnp.transpose(gl_p.reshape(nbp * t, 3, A_KV, A_HPG), (2, 1, 3, 0)).reshape(A_KV, 3 * A_HPG, nbp * t)
    mixed_p = nsa_prompt_attention(qn_p, cmp_p, rows_p, win_p, zt_p, gates_g,
                                   tsel, twin, tcmp, msel_p, expand_p, nbp, t)

    n_pool = cache_nsa_kv.shape[1]
    cache4d = cache_nsa_kv[0].reshape(n_pool, PAGE, 4 * A_KV, HD)
    n_pages = min(PAGES_PER_STEP, n_tab)
    ab_s = compress_stage_a(cache4d, page_table, w_pairs, n_pages)
    tail_rows = jnp.pad(rows_s.reshape(nbs, ts, 4 * kvw), ((0, 0), (0, PAGE - ts), (0, 0)))
    ident_s = jnp.arange(nbs, dtype=jnp.int32).reshape(nbs, 1)
    ab_tail = compress_stage_a(tail_rows.reshape(nbs, PAGE, 4 * A_KV, HD), ident_s, w_pairs, 1)
    cmp_s = compress_stage_b(ab_s, ab_tail, pe, w1, w2, qk_g)

    n_cmp_s = cmp_s.shape[1] // NCK
    n_steps = n_tab // n_pages
    bps = n_pages * PAGE // SEL_BLOCK
    n_sel_pad = (n_steps + 1) * bps
    nkeys = (n_steps + 1) * n_pages * PAGE
    nkw = WINDOW + LANE
    b_cmp, b_sel, b_win = sample_bias_rows(rel_bias, past_len, ts, n_cmp_s, nkeys, nkw)
    msel_s = _sel_matrix(n_cmp_s, n_cmp_s, n_sel_pad)
    expand_s = _expand_matrix(bps, n_pages * PAGE)
    q5 = qn_s.reshape(nbs, ts, A_KV, A_HPG, HD)
    eye = jnp.eye(A_KV, dtype=BF16)
    qbd = jnp.einsum('bqghd,ge->bghqed', q5, eye).reshape(nbs, A_HEADS * ts, kvw)
    win_new = win_s.reshape(nbs, ts, 2 * kvw)
    win_cache = cache_nsa_win[0].reshape(nbs, WINDOW, 2 * kvw)
    win_all = jnp.concatenate([win_cache, win_new], axis=1)
    win_pad = jnp.pad(win_all, ((0, 0), (0, nkw - WINDOW - ts), (0, 0)))
    tail_sel = tail_rows[:, :, 2 * kvw:]
    mixed_s = nsa_sample_attention(qbd, cmp_s, cache4d, page_table, tail_sel, win_pad, z_s, gl_s,
                                   b_cmp, b_sel, b_win, msel_s, expand_s, past_len, ts).astype(BF16)

    xs_flat = x_sample.reshape(nbs * ts, d)
    gate_s_rows = jnp.repeat(gt_s, ts, axis=0)
    x1p, x1s = project_residual(mixed_p, mixed_s, a_w_out[0], x_prompt.reshape(nbp * t, d), gt_p,
                                xs_flat, gate_s_rows, t)

    (sh_p, sc_p, gt_p), (sh_s, sc_s, gt_s) = mods(1)
    hp = norm_modulate(x1p.reshape(nbp, t, d), norm_g[1], sc_p, sh_p)
    hs = norm_modulate(x1s.reshape(nbs, ts, d), norm_g[1], sc_s, sh_s)
    rp, rs = project(hp, hs, r_w_in[0])
    s0 = jnp.zeros((nbp,) + state_ret.shape[2:], F32)
    op, ret_p = retention(rp, s0, 0, nbp, t)
    os_, ret_s = retention(rs, state_ret[0], past_len, nbs, ts)
    os_ = os_.astype(BF16)
    gate_s_rows = jnp.repeat(gt_s, ts, axis=0)
    x2p, x2s = project_residual(op, os_, r_w_out[0], x1p, gt_p, x1s, gate_s_rows, t)

    kv_p = rows_t_p.reshape(1, nbp, t, 4, A_KV, HD)
    kv_s = rows_t_s.reshape(1, nbs, ts, 4, A_KV, HD)
    wst_p = win_t_p.reshape(nbp, t, 2, A_KV, HD)[None, :, t - WINDOW:]
    wst_s = win_all[:, ts:].reshape(1, nbs, WINDOW, 2, A_KV, HD)
    return (x2p.reshape(nbp, t, d), x2s.reshape(nbs, ts, d), kv_p, kv_s, wst_p, wst_s,
            ret_p[None], ret_s[None])
```

```python
import functools
import math

import numpy as np
import jax
import jax.numpy as jnp
from jax import lax
from jax.experimental import pallas as pl
from jax.experimental.pallas import tpu as pltpu

EPS = 1e-6
HD = 128
A_KV = 4
A_HPG = 8
A_HEADS = A_KV * A_HPG
CMP_STRIDE = 16
CMP_LEN = 32
SEL_BLOCK = 64
SEL_RATIO = SEL_BLOCK // CMP_STRIDE
SEL_TOPK = 16
WINDOW = 512
Q_BLOCK = 128
FORCE_SCORE = 1e4
NEG = -1e30
TINY = 1e-30
N_BUCKETS = 32
MAX_DISTANCE = 1024
R_CHUNK = 128
ROPE_BASE = 10000.0
LOG2E = math.log2(math.e)
Q_SCALE = HD ** -0.5 * LOG2E
PAGE = 128
PAGES_PER_STEP = 16
RET_HEADS_PER_STEP = 8
SEL_KEY_TILE = 512

LANE = 128
VMEM_LIMIT = 60 * 1024 * 1024

F32 = jnp.float32
BF16 = jnp.bfloat16


def _cparams(sem):
    return pltpu.CompilerParams(dimension_semantics=sem, vmem_limit_bytes=VMEM_LIMIT)


def _dot(a, b):
    return jnp.dot(a, b, preferred_element_type=F32)


def _dot_nt(a, b):
    return lax.dot_general(a, b, (((1,), (1,)), ((), ())), preferred_element_type=F32)


def _dot_tn(a, b):
    return lax.dot_general(a, b, (((0,), (0,)), ((), ())), preferred_element_type=F32)


def _silu(x):
    h = 0.5 * x
    return h + h * jnp.tanh(h)


def _sigmoid(x):
    return 0.5 + 0.5 * jnp.tanh(0.5 * x)


def _split3_dot(x, m_bf16):
    hi = x.astype(BF16)
    r1 = x - hi.astype(F32)
    mid = r1.astype(BF16)
    lo = (r1 - mid.astype(F32)).astype(BF16)
    return _dot(hi, m_bf16) + _dot(mid, m_bf16) + _dot(lo, m_bf16)


def _ada_kernel(c_ref, w_ref, b_ref, o_ref):
    a = _silu(c_ref[...]).astype(BF16)
    o_ref[...] = _dot(a, w_ref[...].astype(BF16)) + b_ref[...]


def ada_modulation(c_all, ada_w, ada_b):
    depth, d, n = ada_w.shape
    rows = c_all.shape[0]
    tn = 512
    return pl.pallas_call(
        _ada_kernel,
        grid=(depth, n // tn),
        in_specs=[
            pl.BlockSpec((rows, d), lambda l, j: (0, 0)),
            pl.BlockSpec((None, d, tn), lambda l, j: (l, 0, j)),
            pl.BlockSpec((None, 1, tn), lambda l, j: (l, 0, j)),
        ],
        out_specs=pl.BlockSpec((None, rows, tn), lambda l, j: (l, 0, j)),
        out_shape=jax.ShapeDtypeStruct((depth, rows, n), F32),
        compiler_params=_cparams(("arbitrary", "arbitrary")),
        name="ada_modulation",
    )(c_all, ada_w, ada_b.reshape(depth, 1, n))


def _norm_mod_kernel(x_ref, g_ref, sc_ref, sh_ref, o_ref):
    x = x_ref[...]
    y = x * lax.rsqrt(jnp.mean(x * x, axis=-1, keepdims=True) + EPS)
    y = y * g_ref[...]
    o_ref[...] = (y * (1.0 + sc_ref[...]) + sh_ref[...]).astype(o_ref.dtype)


def norm_modulate(x, g, scale, shift):
    b, t, d = x.shape
    tt = min(t, 256)
    out = pl.pallas_call(
        _norm_mod_kernel,
        grid=(b, t // tt),
        in_specs=[
            pl.BlockSpec((None, tt, d), lambda i, j: (i, j, 0)),
            pl.BlockSpec((1, d), lambda i, j: (0, 0)),
            pl.BlockSpec((None, 1, d), lambda i, j: (i, 0, 0)),
            pl.BlockSpec((None, 1, d), lambda i, j: (i, 0, 0)),
        ],
        out_specs=pl.BlockSpec((None, tt, d), lambda i, j: (i, j, 0)),
        out_shape=jax.ShapeDtypeStruct((b, t, d), BF16),
        compiler_params=_cparams(("arbitrary", "arbitrary")),
        name="norm_modulate",
    )(x, g.reshape(1, d), scale.reshape(b, 1, d), shift.reshape(b, 1, d))
    return out.reshape(b * t, d)


def _stage_weight_tile(w_hbm, wf_ref, wb_ref, sem, *, w_is_nk, tn, j0):
    j = pl.program_id(0)

    def copy(jj):
        start = pl.multiple_of((j0 + jj) * tn, tn)
        src = w_hbm.at[pl.ds(start, tn), :] if w_is_nk else w_hbm.at[:, pl.ds(start, tn)]
        return pltpu.make_async_copy(src, wf_ref, sem)

    @pl.when(j == 0)
    def _():
        copy(0).start()

    copy(j).wait()
    wb_ref[...] = wf_ref[...].astype(BF16)

    @pl.when(j + 1 < pl.num_programs(0))
    def _():
        copy(j + 1).start()


def _proj_kernel(xp_ref, xs_ref, w_hbm, op_ref, os_ref, wf_ref, wb_ref, sem, *, stage, mm):
    @pl.when(pl.program_id(1) == 0)
    def _():
        stage(w_hbm, wf_ref, wb_ref, sem)
        os_ref[...] = mm(xs_ref[...], wb_ref[...])

    op_ref[...] = mm(xp_ref[...], wb_ref[...])


def _proj_gate_t_kernel(xp_ref, xs_ref, w_hbm, opt_ref, os_ref, wf_ref, wb_ref, sem, *, stage):
    @pl.when(pl.program_id(1) == 0)
    def _():
        stage(w_hbm, wf_ref, wb_ref, sem)
        os_ref[...] = _silu(_dot_nt(xs_ref[...], wb_ref[...]))

    opt_ref[...] = _silu(_dot_nt(wb_ref[...], xp_ref[...]))


def _weight_scratch(shape):
    return [pltpu.VMEM(shape, F32), pltpu.VMEM(shape, BF16), pltpu.SemaphoreType.DMA(())]


def project_gate_t(xp, xs, w_nk, row0, n_rows):
    mp, k = xp.shape
    ms = xs.shape[0]
    tm, tn = _proj_tiles(mp, k, n_rows)
    assert row0 % tn == 0
    stage = functools.partial(_stage_weight_tile, w_is_nk=True, tn=tn, j0=row0 // tn)
    return pl.pallas_call(
        functools.partial(_proj_gate_t_kernel, stage=stage),
        grid=(n_rows // tn, mp // tm),
        in_specs=[
            pl.BlockSpec((tm, k), lambda j, i: (i, 0)),
            pl.BlockSpec((ms, k), lambda j, i: (0, 0)),
            pl.BlockSpec(memory_space=pl.ANY),
        ],
        out_specs=[
            pl.BlockSpec((tn, tm), lambda j, i: (j, i)),
            pl.BlockSpec((ms, tn), lambda j, i: (0, j)),
        ],
        out_shape=[jax.ShapeDtypeStruct((n_rows, mp), F32), jax.ShapeDtypeStruct((ms, n_rows), F32)],
        scratch_shapes=_weight_scratch((tn, k)),
        compiler_params=_cparams(("arbitrary", "arbitrary")),
        name="project_gate_t",
    )(xp, xs, w_nk)


def _proj_res_kernel(xp_ref, xs_ref, w_hbm, rp_ref, gp_ref, rs_ref, gs_ref, op_ref, os_ref,
                     wf_ref, wb_ref, sem, *, stage):
    @pl.when(pl.program_id(1) == 0)
    def _():
        stage(w_hbm, wf_ref, wb_ref, sem)
        os_ref[...] = rs_ref[...] + gs_ref[...] * _dot(xs_ref[...], wb_ref[...])

    op_ref[...] = rp_ref[...] + gp_ref[...] * _dot(xp_ref[...], wb_ref[...])


def _proj_tiles(mp, k, n, residual=False):
    if k <= 4096:
        tm, tn = (1024, 512) if residual else (1024, 1024)
    else:
        tm, tn = 512, 512
    tm = min(tm, mp)
    while mp % tm:
        tm //= 2
    tn = min(tn, n)
    while n % tn:
        tn //= 2
    return tm, tn


def project(xp, xs, w, n_cols=None, w_is_nk=False):
    mp, k = xp.shape
    ms = xs.shape[0]
    n = w.shape[0 if w_is_nk else 1] if n_cols is None else n_cols
    tm, tn = _proj_tiles(mp, k, n)
    stage = functools.partial(_stage_weight_tile, w_is_nk=w_is_nk, tn=tn, j0=0)
    return pl.pallas_call(
        functools.partial(_proj_kernel, stage=stage, mm=_dot_nt if w_is_nk else _dot),
        grid=(n // tn, mp // tm),
        in_specs=[
            pl.BlockSpec((tm, k), lambda j, i: (i, 0)),
            pl.BlockSpec((ms, k), lambda j, i: (0, 0)),
            pl.BlockSpec(memory_space=pl.ANY),
        ],
        out_specs=[
            pl.BlockSpec((tm, tn), lambda j, i: (i, j)),
            pl.BlockSpec((ms, tn), lambda j, i: (0, j)),
        ],
        out_shape=[jax.ShapeDtypeStruct((mp, n), F32), jax.ShapeDtypeStruct((ms, n), F32)],
        scratch_shapes=_weight_scratch((tn, k) if w_is_nk else (k, tn)),
        compiler_params=_cparams(("arbitrary", "arbitrary")),
        name="project",
    )(xp, xs, w)


def project_residual(xp, xs, w, res_p, gate_p, res_s, gate_s, rows_per_batch):
    mp, k = xp.shape
    ms = xs.shape[0]
    n = w.shape[1]
    tm, tn = _proj_tiles(mp, k, n, residual=True)
    tm = min(tm, rows_per_batch)
    tpb = rows_per_batch // tm
    nb = gate_p.shape[0]
    stage = functools.partial(_stage_weight_tile, w_is_nk=False, tn=tn, j0=0)
    return pl.pallas_call(
        functools.partial(_proj_res_kernel, stage=stage),
        grid=(n // tn, mp // tm),
        in_specs=[
            pl.BlockSpec((tm, k), lambda j, i: (i, 0)),
            pl.BlockSpec((ms, k), lambda j, i: (0, 0)),
            pl.BlockSpec(memory_space=pl.ANY),
            pl.BlockSpec((tm, tn), lambda j, i: (i, j)),
            pl.BlockSpec((None, 1, tn), lambda j, i: (i // tpb, 0, j)),
            pl.BlockSpec((ms, tn), lambda j, i: (0, j)),
            pl.BlockSpec((ms, tn), lambda j, i: (0, j)),
        ],
        out_specs=[
            pl.BlockSpec((tm, tn), lambda j, i: (i, j)),
            pl.BlockSpec((ms, tn), lambda j, i: (0, j)),
        ],
        out_shape=[jax.ShapeDtypeStruct((mp, n), F32), jax.ShapeDtypeStruct((ms, n), F32)],
        scratch_shapes=_weight_scratch((k, tn)),
        compiler_params=_cparams(("arbitrary", "arbitrary")),
        name="project_residual",
    )(xp, xs, w, res_p, gate_p.reshape(nb, 1, n), res_s, gate_s)


def _nsa_prep_kernel(q_ref, a_ref, b_ref, c_ref, g_ref, qn_ref, rows_ref, win_ref, rows_t_ref, win_t_ref):
    def hnorm(x, g):
        return x * lax.rsqrt(jnp.mean(x * x, axis=-1, keepdims=True) + EPS) * g

    g = g_ref[...]
    for j in range(A_HEADS):
        sl = slice(j * HD, (j + 1) * HD)
        qn_ref[:, sl] = (hnorm(q_ref[:, sl], g[0:1]) * Q_SCALE).astype(qn_ref.dtype)
    kvw = A_KV * HD
    sls = [slice(j * HD, (j + 1) * HD) for j in range(2 * A_KV)]
    row_slots = ([a_ref[:, s] for s in sls] + [hnorm(b_ref[:, s], g[2:3]) for s in sls[:A_KV]]
                 + [b_ref[:, s] for s in sls[A_KV:]])
    win_slots = [hnorm(c_ref[:, s], g[3:4]) for s in sls[:A_KV]] + [c_ref[:, s] for s in sls[A_KV:]]
    for j, x in enumerate(row_slots):
        rows_ref[:, j * HD:(j + 1) * HD] = x
    for j, x in enumerate(win_slots):
        win_ref[:, j * HD:(j + 1) * HD] = x
    rows_t_ref[...] = jnp.swapaxes(jnp.stack(row_slots, axis=0), 0, 1)
    win_t_ref[...] = jnp.swapaxes(jnp.stack(win_slots, axis=0), 0, 1)


def nsa_prep(proj, qk_g):
    m = proj.shape[0]
    tm = min(m, 256)
    qw = A_HEADS * HD
    kv2 = 2 * A_KV * HD
    base = qw // kv2
    return pl.pallas_call(
        _nsa_prep_kernel,
        grid=(m // tm,),
        in_specs=[
            pl.BlockSpec((tm, qw), lambda i: (i, 0)),
            pl.BlockSpec((tm, kv2), lambda i: (i, base)),
            pl.BlockSpec((tm, kv2), lambda i: (i, base + 1)),
            pl.BlockSpec((tm, kv2), lambda i: (i, base + 2)),
            pl.BlockSpec((4, HD), lambda i: (0, 0)),
        ],
        out_specs=[
            pl.BlockSpec((tm, qw), lambda i: (i, 0)),
            pl.BlockSpec((tm, 2 * kv2), lambda i: (i, 0)),
            pl.BlockSpec((tm, kv2), lambda i: (i, 0)),
            pl.BlockSpec((tm, 4 * A_KV, HD), lambda i: (i, 0, 0)),
            pl.BlockSpec((tm, 2 * A_KV, HD), lambda i: (i, 0, 0)),
        ],
        out_shape=[
            jax.ShapeDtypeStruct((m, qw), BF16),
            jax.ShapeDtypeStruct((m, 2 * kv2), F32),
            jax.ShapeDtypeStruct((m, kv2), F32),
            jax.ShapeDtypeStruct((m, 4 * A_KV, HD), F32),
            jax.ShapeDtypeStruct((m, 2 * A_KV, HD), F32),
        ],
        compiler_params=_cparams(("arbitrary",)),
        name="nsa_prep",
    )(proj, proj, proj, proj, qk_g)


NCK = 2 * A_KV


def _cmp_a_kernel(pt_ref, *refs, n_pages):
    del pt_ref
    pages = refs[:n_pages]
    w_ref = refs[n_pages]
    o_ref = refs[n_pages + 1]
    cpp = PAGE // CMP_STRIDE
    n_ch = n_pages * cpp

    def halves(tiles, lo):
        return [jnp.concatenate([tiles[a][lo:lo + A_KV], tiles[a + 1][lo:lo + A_KV]], axis=0)
                for a in range(0, len(tiles), 2)]

    acc = [None, None]
    for tp in range(CMP_STRIDE // 2):
        pieces = []
        for pg in pages:
            for ch in range(cpp):
                t0 = ch * CMP_STRIDE + 2 * tp
                pieces.append(jnp.concatenate([pg[t0], pg[t0 + 1]], axis=1))
        for kind in range(2):
            lhs = jnp.concatenate(halves(pieces, kind * A_KV), axis=0).astype(BF16)
            d = _dot(lhs, w_ref[tp, :, kind * 2 * HD:(kind + 1) * 2 * HD])
            acc[kind] = d if acc[kind] is None else acc[kind] + d
    for a in range(0, n_ch, 2):
        r = slice(a * A_KV, (a + 2) * A_KV)
        kk, vv = acc[0][r], acc[1][r]
        o_ref[a] = jnp.concatenate([kk[:A_KV], vv[:A_KV]], axis=0)
        o_ref[a + 1] = jnp.concatenate([kk[A_KV:], vv[A_KV:]], axis=0)


def compress_stage_a(store, page_table, w_pairs, n_pages):
    nb, n_tab = page_table.shape
    steps = n_tab // n_pages
    cpp = PAGE // CMP_STRIDE

    def page_spec(k):
        return pl.BlockSpec((None, PAGE, NCK, HD), lambda b, s, pt: (pt[b, s * n_pages + k], 0, 0, 0))

    grid_spec = pltpu.PrefetchScalarGridSpec(
        num_scalar_prefetch=1,
        grid=(nb, steps),
        in_specs=[page_spec(k) for k in range(n_pages)]
        + [pl.BlockSpec(w_pairs.shape, lambda b, s, pt: (0, 0, 0))],
        out_specs=pl.BlockSpec((None, n_pages * cpp, NCK, 2 * HD), lambda b, s, pt: (b, s, 0, 0)),
    )
    return pl.pallas_call(
        functools.partial(_cmp_a_kernel, n_pages=n_pages),
        grid_spec=grid_spec,
        out_shape=jax.ShapeDtypeStruct((nb, n_tab * cpp, NCK, 2 * HD), F32),
        compiler_params=_cparams(("arbitrary", "arbitrary")),
        name="compress_stage_a",
    )(page_table, *([store] * n_pages), w_pairs)


def _cmp_b_kernel(ab_ref, nx_ref, tail_ref, pe_ref, w1_ref, w2_ref, g_ref, o_ref, const_ref):
    j = pl.program_id(1)
    nch = ab_ref.shape[0]
    rows = nch * NCK

    @pl.when((pl.program_id(0) == 0) & (j == 0))
    def _():
        for kind in range(2):
            const = jnp.zeros((8, HD), F32)
            for t in range(CMP_LEN):
                row = jnp.broadcast_to(pe_ref[kind, t:t + 1, :], (8, HD)).astype(BF16)
                const = const + _dot(row, w1_ref[kind, t].astype(BF16))
            const_ref[kind * A_KV:(kind + 1) * A_KV, :] = const[:A_KV]

    ab = ab_ref[...].reshape(rows, 2 * HD)
    nxt = pltpu.roll(ab[:, HD:], rows - NCK, 0).reshape(nch, NCK, HD)
    first_next = jnp.where(j == pl.num_programs(1) - 1, tail_ref[0, :, HD:], nx_ref[0, :, HD:])
    cid = lax.broadcasted_iota(jnp.int32, (nch, NCK, HD), 0)
    nxt = jnp.where(cid == nch - 1, first_next[None], nxt)
    pre = ab[:, :HD].reshape(nch, NCK, HD) + nxt + const_ref[...][None]
    act = _silu(pre).reshape(rows, HD).astype(BF16)
    w2 = jnp.concatenate([w2_ref[0], w2_ref[1]], axis=1).astype(BF16)
    y2 = _dot(act, w2)
    is_k = (lax.broadcasted_iota(jnp.int32, (rows, HD), 0) & (NCK - 1)) < A_KV
    y = jnp.where(is_k, y2[:, :HD], y2[:, HD:])
    yn = y * lax.rsqrt(jnp.mean(y * y, axis=-1, keepdims=True) + EPS) * g_ref[1:2]
    o_ref[...] = jnp.where(is_k, yn, y).reshape(nch, NCK, HD)


def compress_stage_b(ab, tail, pe, w1, w2, qk_g):
    nb, n_ch = ab.shape[:2]
    tc = min(256, n_ch)
    assert n_ch % tc == 0
    nt = n_ch // tc
    out = pl.pallas_call(
        _cmp_b_kernel,
        grid=(nb, nt),
        in_specs=[
            pl.BlockSpec((None, tc, NCK, 2 * HD), lambda b, j: (b, j, 0, 0)),
            pl.BlockSpec((None, 1, NCK, 2 * HD), lambda b, j: (b, jnp.minimum((j + 1) * tc, n_ch - 1), 0, 0)),
            pl.BlockSpec((None, 1, NCK, 2 * HD), lambda b, j: (b, 0, 0, 0)),
            pl.BlockSpec(pe.shape, lambda b, j: (0, 0, 0)),
            pl.BlockSpec(w1.shape, lambda b, j: (0, 0, 0, 0)),
            pl.BlockSpec(w2.shape, lambda b, j: (0, 0, 0)),
            pl.BlockSpec((4, HD), lambda b, j: (0, 0)),
        ],
        out_specs=pl.BlockSpec((None, tc, NCK, HD), lambda b, j: (b, j, 0, 0)),
        out_shape=jax.ShapeDtypeStruct((nb, n_ch, NCK, HD), F32),
        scratch_shapes=[pltpu.VMEM((NCK, HD), F32)],
        compiler_params=_cparams(("arbitrary", "arbitrary")),
        name="compress_stage_b",
    )(ab, ab, tail, pe, w1, w2, qk_g)
    return out.reshape(nb, n_ch * NCK, HD)


def _cmp_rows(cmp_ref, slot, n):
    return cmp_ref[pl.ds(slot, n, stride=NCK), :]


def _select_topk(score, blk, k_top):
    sel = jnp.zeros(score.shape, F32)
    for _ in range(k_top):
        m = jnp.max(score, axis=-1, keepdims=True)
        cand = jnp.where(score == m, blk, jnp.int32(1 << 30))
        first = jnp.min(cand, axis=-1, keepdims=True)
        pick = blk == first
        sel = jnp.where(pick, 1.0, sel)
        score = jnp.where(pick, -jnp.inf, score)
    return sel


def _select_topk_t(score, blk, k_top, n_blk):
    rank = jnp.zeros(score.shape, F32)
    for i in range(n_blk):
        row = score[i:i + 1, :]
        tie = jnp.where(blk > i, 1.0, 0.0)
        rank = rank + jnp.where(row > score, 1.0, jnp.where(row == score, tie, 0.0))
    return jnp.where(rank < k_top, 1.0, 0.0)


def _bias_select(dist, dmin, dmax, value_of):
    lo = int(_t5_bucket_np(np.array([max(dmin, 0)]))[0])
    hi = int(_t5_bucket_np(np.array([max(dmax, 0)]))[0])
    out = jnp.full(dist.shape, value_of(lo), F32)
    for b in range(lo + 1, hi + 1):
        out = jnp.where(dist >= _BUCKET_START[b], value_of(b), out)
    return out * LOG2E


def _bias_tables_kernel(rb_ref, tsel_ref, twin_ref, tcmp_ref):
    head = pl.program_id(0)
    value_of = lambda b: rb_ref[b, head]
    n_off, nq, nc = tsel_ref.shape[0], tcmp_ref.shape[0], tcmp_ref.shape[1]
    kl = lax.broadcasted_iota(jnp.int32, (LANE, Q_BLOCK), 0)
    ql = lax.broadcasted_iota(jnp.int32, (LANE, Q_BLOCK), 1)
    for o in range(n_off):
        base = (o - 1) * Q_BLOCK
        dist = base + ql - kl
        tile = _bias_select(dist, base - (LANE - 1), base + Q_BLOCK - 1, value_of)
        tsel_ref[o] = tile
        if o < twin_ref.shape[0]:
            twin_ref[o] = tile + jnp.where((dist >= 0) & (dist < WINDOW), 0.0, NEG)
    step = Q_BLOCK // CMP_STRIDE
    rows = nc + (nq - 1) * step
    base = (nq - 1) * Q_BLOCK
    cend = lax.broadcasted_iota(jnp.int32, (rows, Q_BLOCK), 0) * CMP_STRIDE + (CMP_LEN - 1)
    qc = lax.broadcasted_iota(jnp.int32, (rows, Q_BLOCK), 1)
    tall = _bias_select(base + qc - cend, base - ((rows - 1) * CMP_STRIDE + CMP_LEN - 1),
                        base + Q_BLOCK - 1, value_of)
    for i in range(nq):
        r0 = (nq - 1 - i) * step
        tcmp_ref[i] = tall[r0:r0 + nc]


def prompt_bias_tables(rel_bias, nq, n_cmp):
    n_win = min(WINDOW // Q_BLOCK + 3, nq + 1)
    return pl.pallas_call(
        _bias_tables_kernel,
        grid=(A_HEADS,),
        in_specs=[pl.BlockSpec(memory_space=pltpu.SMEM)],
        out_specs=[
            pl.BlockSpec((None, nq + 1, LANE, Q_BLOCK), lambda h: (h // A_HPG, 0, 0, h % A_HPG)),
            pl.BlockSpec((None, n_win, LANE, Q_BLOCK), lambda h: (h // A_HPG, 0, 0, h % A_HPG)),
            pl.BlockSpec((None, nq, n_cmp, Q_BLOCK), lambda h: (h // A_HPG, 0, 0, h % A_HPG)),
        ],
        out_shape=[
            jax.ShapeDtypeStruct((A_KV, nq + 1, LANE, A_HPG * Q_BLOCK), F32),
            jax.ShapeDtypeStruct((A_KV, n_win, LANE, A_HPG * Q_BLOCK), F32),
            jax.ShapeDtypeStruct((A_KV, nq, n_cmp, A_HPG * Q_BLOCK), F32),
        ],
        compiler_params=_cparams(("arbitrary",)),
        name="prompt_bias_tables",
    )(rel_bias)


def _bias_rows_kernel(rb_ref, cmp_ref, sel_ref, win_ref, *, past_len, n_q):
    head = pl.program_id(0)
    value_of = lambda b: rb_ref[b, head]

    def fill(ref, key_pos, kmin, kmax, q0):
        q = lax.broadcasted_iota(jnp.int32, ref.shape, 0)
        k = lax.broadcasted_iota(jnp.int32, ref.shape, 1)
        ref[...] = _bias_select(q0 + q - key_pos(k), q0 - kmax, q0 + n_q - 1 - kmin, value_of)

    nc, nk, nw = cmp_ref.shape[1], sel_ref.shape[1], win_ref.shape[1]
    fill(cmp_ref, lambda k: k * CMP_STRIDE + (CMP_LEN - 1), CMP_LEN - 1, (nc - 1) * CMP_STRIDE + CMP_LEN - 1, past_len)
    fill(sel_ref, lambda k: k, 0, nk - 1, past_len)
    fill(win_ref, lambda k: k, 0, nw - 1, WINDOW)


def sample_bias_rows(rel_bias, past_len, n_q, n_cmp, n_keys, n_win):
    shapes = [(A_HEADS * n_q, n) for n in (n_cmp, n_keys, n_win)]
    return pl.pallas_call(
        functools.partial(_bias_rows_kernel, past_len=past_len, n_q=n_q),
        grid=(A_HEADS,),
        in_specs=[pl.BlockSpec(memory_space=pltpu.SMEM)],
        out_specs=[pl.BlockSpec((n_q, s[1]), lambda h: (h, 0)) for s in shapes],
        out_shape=[jax.ShapeDtypeStruct(s, F32) for s in shapes],
        compiler_params=_cparams(("arbitrary",)),
        name="sample_bias_rows",
    )(rel_bias)


def _nsa_prompt_kernel(q_ref, cmp_ref, ks_ref, vs_ref, kw_ref, vw_ref, zc_ref, zs_ref, zw_ref,
                       gt_ref, tsel_ref, twin_ref, tcmp_ref, msel_ref, exp_ref, o_ref,
                       qa_ref, acc_ref, m_ref, l_ref, mix_ref, *, n_sel_pad, n_sel, k_top, kt):
    i = pl.program_id(2)
    qb = Q_BLOCK
    tpk = kt // LANE
    nc = cmp_ref.shape[0] // NCK
    gates = _sigmoid(gt_ref[...])
    hs = [slice(h * HD, (h + 1) * HD) for h in range(A_HPG)]
    for h in range(A_HPG):
        qa_ref[hs[h], :] = q_ref[:, hs[h]]
    qa = qa_ref[...]

    def heads(x):
        return jnp.concatenate([x] * A_HPG, axis=1)

    def gate_row(branch):
        return jnp.concatenate([gates[branch * A_HPG + h: branch * A_HPG + h + 1, :] for h in range(A_HPG)], axis=1)

    def emit(branch, o_t, first):
        z_ref = (zc_ref, zs_ref, zw_ref)[branch]
        for h in range(A_HPG):
            val = o_t[:, hs[h]] * z_ref[hs[h], :]
            if first:
                mix_ref[hs[h], :] = val
            else:
                mix_ref[hs[h], :] = mix_ref[hs[h], :] + val

    grp = pl.program_id(0)
    kc = _cmp_rows(cmp_ref, grp, nc).astype(BF16)
    vc = _cmp_rows(cmp_ref, A_KV + grp, nc).astype(BF16)
    cend = lax.broadcasted_iota(jnp.int32, (nc, qb), 0) * CMP_STRIDE + (CMP_LEN - 1)
    qpos_c = i * qb + lax.broadcasted_iota(jnp.int32, (nc, qb), 1)
    cvalid = heads(cend <= qpos_c)
    s = jnp.where(cvalid, _dot_nt(kc, qa) + tcmp_ref[...], NEG)
    mx = jnp.max(s, axis=0, keepdims=True)
    e = jnp.where(cvalid, jnp.exp2(s - mx), 0.0)
    p = e * (1.0 / jnp.maximum(jnp.sum(e, axis=0, keepdims=True), TINY))
    imp = p[:, hs[0]]
    for h in range(1, A_HPG):
        imp = imp + p[:, hs[h]]
    emit(0, _dot_tn(vc, p.astype(BF16)) * gate_row(0), True)

    hi = imp.astype(BF16)
    r1 = imp - hi.astype(F32)
    mid = r1.astype(BF16)
    lo = (r1 - mid.astype(F32)).astype(BF16)
    msel = msel_ref[...]
    imp_sel = _dot(msel, hi) + _dot(msel, mid) + _dot(msel, lo)
    blk = lax.broadcasted_iota(jnp.int32, (n_sel_pad, qb), 0)
    qpos = i * qb + lax.broadcasted_iota(jnp.int32, (n_sel_pad, qb), 1)
    cur = jnp.right_shift(qpos, int(math.log2(SEL_BLOCK)))
    forced = (blk == 0) | (blk == cur) | (blk == cur - 1)
    valid = blk * SEL_BLOCK <= qpos
    score = jnp.where(valid, jnp.where(forced, FORCE_SCORE, imp_sel), -1.0)
    score = jnp.where(blk < n_sel, score, -jnp.inf)
    sel = _select_topk_t(score, blk, k_top, n_sel).astype(BF16)
    kpos2 = lax.broadcasted_iota(jnp.int32, (kt, qb), 0)
    qpos2 = i * qb + lax.broadcasted_iota(jnp.int32, (kt, qb), 1)

    def sel_madd(jj):
        k0 = pl.multiple_of(jj * kt, kt)
        hit = _dot(exp_ref[pl.ds(k0, kt), :], sel)
        return jnp.where((hit > 0.5) & ((k0 + kpos2) <= qpos2), 0.0, NEG)

    def reset():
        m_ref[...] = jnp.full(m_ref.shape, NEG, F32)
        l_ref[...] = jnp.zeros(l_ref.shape, F32)
        acc_ref[...] = jnp.zeros(acc_ref.shape, F32)

    def tile_update(k, v, bias, madd):
        s = _dot_nt(k, qa) + bias
        if madd is not None:
            s = s + heads(madd)
        m_old = m_ref[...]
        m_new = jnp.maximum(m_old, jnp.max(s, axis=0, keepdims=True))
        alpha = jnp.exp2(m_old - m_new)
        p = jnp.exp2(s - m_new)
        l_ref[...] = alpha * l_ref[...] + jnp.sum(p, axis=0, keepdims=True)
        m_ref[...] = m_new
        acc_ref[...] = acc_ref[...] * alpha + _dot_tn(v, p.astype(BF16))

    def finish(branch):
        emit(branch, acc_ref[...] * (gate_row(branch) / l_ref[...]), False)

    reset()

    def sel_body(jj, carry):
        k0 = pl.multiple_of(jj * kt, kt)
        k = ks_ref[pl.ds(k0, kt), :].astype(BF16)
        v = vs_ref[pl.ds(k0, kt), :].astype(BF16)
        off = i - tpk * jj
        bias = jnp.concatenate([tsel_ref[jnp.maximum(off + 1 - r, 0)] for r in range(tpk)], axis=0)
        tile_update(k, v, bias, sel_madd(jj))
        return carry

    lax.fori_loop(0, i // tpk + 1, sel_body, 0)
    finish(1)

    reset()
    wt = 2 * LANE
    first = i // 2
    lowest = jnp.maximum((i - WINDOW // qb) // 2, 0)

    def win_body(t, carry):
        tile = first - t
        k0 = pl.multiple_of(tile * wt, wt)
        k = kw_ref[pl.ds(k0, wt), :].astype(BF16)
        v = vw_ref[pl.ds(k0, wt), :].astype(BF16)
        off = i - 2 * tile
        bias = jnp.concatenate([twin_ref[off + 1], twin_ref[off]], axis=0)
        tile_update(k, v, bias, None)
        return carry

    lax.fori_loop(0, first - lowest + 1, win_body, 0)
    finish(2)
    for h in range(A_HPG):
        o_ref[:, hs[h]] = mix_ref[hs[h], :].T.astype(o_ref.dtype)


def nsa_prompt_attention(qn, cmp_kv, rows, win, z_t, gates_g, tsel, twin, tcmp, msel, expand, nb, t):
    nq = t // Q_BLOCK
    qw = A_HPG * HD
    n_sel = -(-t // SEL_BLOCK)
    k_top = min(SEL_TOPK, n_sel)
    kt = SEL_KEY_TILE if t % SEL_KEY_TILE == 0 else 2 * LANE
    n_cmp = cmp_kv.shape[1] // NCK
    n_sel_pad = msel.shape[0]
    kernel = functools.partial(_nsa_prompt_kernel, n_sel_pad=n_sel_pad, n_sel=n_sel, k_top=k_top, kt=kt)
    return pl.pallas_call(
        kernel,
        grid=(A_KV, nb, nq),
        in_specs=[
            pl.BlockSpec((Q_BLOCK, qw), lambda g, b, i: (b * nq + i, g)),
            pl.BlockSpec((None, n_cmp * NCK, HD), lambda g, b, i: (b, 0, 0)),
            pl.BlockSpec((t, HD), lambda g, b, i: (b, 2 * A_KV + g)),
            pl.BlockSpec((t, HD), lambda g, b, i: (b, 3 * A_KV + g)),
            pl.BlockSpec((t, HD), lambda g, b, i: (b, g)),
            pl.BlockSpec((t, HD), lambda g, b, i: (b, A_KV + g)),
            pl.BlockSpec((qw, Q_BLOCK), lambda g, b, i: (g, b * nq + i)),
            pl.BlockSpec((qw, Q_BLOCK), lambda g, b, i: (A_KV + g, b * nq + i)),
            pl.BlockSpec((qw, Q_BLOCK), lambda g, b, i: (2 * A_KV + g, b * nq + i)),
            pl.BlockSpec((None, 3 * A_HPG, Q_BLOCK), lambda g, b, i: (g, 0, b * nq + i)),
            pl.BlockSpec((None, nq + 1, LANE, qw), lambda g, b, i: (g, 0, 0, 0)),
            pl.BlockSpec((None, twin.shape[1], LANE, qw), lambda g, b, i: (g, 0, 0, 0)),
            pl.BlockSpec((None, None, n_cmp, qw), lambda g, b, i: (g, i, 0, 0)),
            pl.BlockSpec((n_sel_pad, n_cmp), lambda g, b, i: (0, 0)),
            pl.BlockSpec((t, n_sel_pad), lambda g, b, i: (0, 0)),
        ],
        out_specs=pl.BlockSpec((Q_BLOCK, qw), lambda g, b, i: (b * nq + i, g)),
        out_shape=jax.ShapeDtypeStruct((nb * t, A_HEADS * HD), BF16),
        scratch_shapes=[
            pltpu.VMEM((qw, HD), BF16),
            pltpu.VMEM((HD, qw), F32),
            pltpu.VMEM((1, qw), F32),
            pltpu.VMEM((1, qw), F32),
            pltpu.VMEM((qw, Q_BLOCK), F32),
        ],
        compiler_params=_cparams(("arbitrary", "arbitrary", "arbitrary")),
        name="nsa_prompt_attention",
    )(qn, cmp_kv, rows, rows, win, win, z_t, z_t, z_t, gates_g, tsel, twin, tcmp, msel, expand)


def _diag_blocks(o_full, rows_per_group):
    return jnp.concatenate(
        [o_full[g * rows_per_group:(g + 1) * rows_per_group, g * HD:(g + 1) * HD] for g in range(A_KV)], axis=0)


def _nsa_sample_cmp_kernel(q_ref, cmp_ref, bias_ref, msel_ref, oc_ref, sel_ref,
                           *, past_len, n_sel, k_top, n_q, blocks_per_step):
    n_cmp = cmp_ref.shape[0] // NCK
    kc = jnp.concatenate([_cmp_rows(cmp_ref, g, n_cmp) for g in range(A_KV)], axis=1).astype(BF16)
    vc = jnp.concatenate([_cmp_rows(cmp_ref, A_KV + g, n_cmp) for g in range(A_KV)], axis=1).astype(BF16)
    rows = q_ref.shape[0]
    rpg = A_HPG * n_q
    q = q_ref[...]
    s = _dot_nt(q, kc) + bias_ref[...]
    rowi = lax.broadcasted_iota(jnp.int32, (rows, n_cmp), 0)
    ci = lax.broadcasted_iota(jnp.int32, (rows, n_cmp), 1)
    qpos = past_len + (rowi & (n_q - 1))
    valid = (ci * CMP_STRIDE + (CMP_LEN - 1)) <= qpos
    s = jnp.where(valid, s, NEG)
    mx = jnp.max(s, axis=-1, keepdims=True)
    e = jnp.where(valid, jnp.exp2(s - mx), 0.0)
    p = e / jnp.maximum(jnp.sum(e, axis=-1, keepdims=True), TINY)
    oc_ref[...] = _diag_blocks(_dot(p.astype(BF16), vc), rpg)
    imps = []
    for g in range(A_KV):
        acc = p[g * rpg: g * rpg + n_q]
        for h in range(1, A_HPG):
            acc = acc + p[g * rpg + h * n_q: g * rpg + (h + 1) * n_q]
        imps.append(acc)
    imp = jnp.concatenate(imps, axis=0)
    imp_sel = _split3_dot(imp, msel_ref[...])
    shp = imp_sel.shape
    blk = lax.broadcasted_iota(jnp.int32, shp, 1)
    qp = past_len + (lax.broadcasted_iota(jnp.int32, shp, 0) & (n_q - 1))
    cur = jnp.right_shift(qp, int(math.log2(SEL_BLOCK)))
    forced = (blk == 0) | (blk == cur) | (blk == cur - 1)
    ok = blk * SEL_BLOCK <= qp
    score = jnp.where(ok, jnp.where(forced, FORCE_SCORE, imp_sel), -1.0)
    score = jnp.where(blk < n_sel, score, -jnp.inf)
    sel = _select_topk(score, blk, k_top)
    for st in range(sel_ref.shape[0]):
        sel_ref[st] = sel[:, st * blocks_per_step:(st + 1) * blocks_per_step]


def _nsa_sample_sel_kernel(pt_ref, *refs, n_pages, n_steps, past_len, n_q):
    del pt_ref
    pages = refs[:n_pages]
    q_ref, tail_ref, bias_ref, sel_ref, exp_ref, o_ref, m_ref, l_ref, acc_ref = refs[n_pages:]
    s_id = pl.program_id(1)
    kvw = A_KV * HD
    rows = q_ref.shape[0]
    rpg = A_HPG * n_q

    @pl.when(s_id == 0)
    def _():
        m_ref[...] = jnp.full(m_ref.shape, NEG, F32)
        l_ref[...] = jnp.zeros(l_ref.shape, F32)
        acc_ref[...] = jnp.zeros(acc_ref.shape, F32)

    def update(k, v, nk, key0):
        hit = _dot(sel_ref[...].astype(BF16), exp_ref[:, :nk])
        kpos = key0 + lax.broadcasted_iota(jnp.int32, hit.shape, 1)
        qpos = past_len + (lax.broadcasted_iota(jnp.int32, hit.shape, 0) & (n_q - 1))
        madd = jnp.where((hit > 0.5) & (kpos <= qpos), 0.0, NEG)
        madd = jnp.concatenate(
            [madd[g * n_q:(g + 1) * n_q] for g in range(A_KV) for _ in range(A_HPG)], axis=0)
        s = _dot_nt(q_ref[...], k) + bias_ref[:, :nk] + madd
        m_old = m_ref[...]
        m_new = jnp.maximum(m_old, jnp.max(s, axis=-1, keepdims=True))
        alpha = jnp.exp2(m_old - m_new)
        p = jnp.exp2(s - m_new)
        l_ref[...] = alpha * l_ref[...] + jnp.sum(p, axis=-1, keepdims=True)
        acc_ref[...] = alpha * acc_ref[...] + _dot(p.astype(BF16), v)
        m_ref[...] = m_new

    @pl.when(s_id < n_steps)
    def _():
        by_slot = [jnp.swapaxes(pg[...], 0, 1) for pg in pages]

        def gather(c0):
            cols = [jnp.concatenate([x[c0 + g] for x in by_slot], axis=0) for g in range(A_KV)]
            return jnp.concatenate(cols, axis=1).astype(BF16)

        update(gather(0), gather(A_KV), n_pages * PAGE, s_id * (n_pages * PAGE))

    @pl.when(s_id == n_steps)
    def _():
        update(tail_ref[:, :kvw].astype(BF16), tail_ref[:, kvw:].astype(BF16), PAGE, past_len)
        o_ref[...] = _diag_blocks(acc_ref[...] / l_ref[...], rpg)


def _nsa_sample_mix_kernel(q_ref, win_ref, bias_ref, oc_ref, os_ref, z_ref, gt_ref, o_ref, *, n_q):
    kvw = A_KV * HD
    rows = q_ref.shape[0]
    rpg = A_HPG * n_q
    nk = win_ref.shape[0]
    s = _dot_nt(q_ref[...], win_ref[:, :kvw].astype(BF16)) + bias_ref[...]
    qi = lax.broadcasted_iota(jnp.int32, (rows, nk), 0) & (n_q - 1)
    ki = lax.broadcasted_iota(jnp.int32, (rows, nk), 1)
    dist = qi + WINDOW - ki
    s = s + jnp.where((dist >= 0) & (dist < WINDOW), 0.0, NEG)
    mx = jnp.max(s, axis=-1, keepdims=True)
    e = jnp.exp2(s - mx)
    p = e / jnp.sum(e, axis=-1, keepdims=True)
    ow = _diag_blocks(_dot(p.astype(BF16), win_ref[:, kvw:].astype(BF16)), rpg)
    gates = _sigmoid(gt_ref[...])
    width = A_HEADS * HD
    for hh in range(A_HEADS):
        r = slice(hh * n_q, (hh + 1) * n_q)
        c = slice(hh * HD, (hh + 1) * HD)
        val = None
        for br, o in enumerate((oc_ref, os_ref, ow)):
            zc = slice(br * width + hh * HD, br * width + (hh + 1) * HD)
            gcol = gates[:, br * A_HEADS + hh: br * A_HEADS + hh + 1]
            term = gcol * o[r, :] * z_ref[:, zc]
            val = term if val is None else val + term
        o_ref[:, c] = val.astype(o_ref.dtype)


def nsa_sample_attention(qbd, cmp_kv, cache4d, page_table, tail_sel, win_all, z_s, gl_s,
                         b_cmp, b_sel, b_win, msel, expand, past_len, n_q):
    nb, rows, kvw = qbd.shape
    n_cmp = cmp_kv.shape[1] // NCK
    n_tab = page_table.shape[1]
    n_pages = min(PAGES_PER_STEP, n_tab)
    n_steps = n_tab // n_pages
    bps = n_pages * PAGE // SEL_BLOCK
    n_sel = -(-(past_len + n_q) // SEL_BLOCK)
    k_top = min(SEL_TOPK, n_sel)
    gq = A_KV * n_q

    oc, selmask = pl.pallas_call(
        functools.partial(_nsa_sample_cmp_kernel, past_len=past_len, n_sel=n_sel, k_top=k_top,
                          n_q=n_q, blocks_per_step=bps),
        grid=(nb,),
        in_specs=[
            pl.BlockSpec((None, rows, kvw), lambda b: (b, 0, 0)),
            pl.BlockSpec((None, n_cmp * NCK, HD), lambda b: (b, 0, 0)),
            pl.BlockSpec(b_cmp.shape, lambda b: (0, 0)),
            pl.BlockSpec(msel.shape, lambda b: (0, 0)),
        ],
        out_specs=[
            pl.BlockSpec((None, rows, HD), lambda b: (b, 0, 0)),
            pl.BlockSpec((None, n_steps + 1, gq, bps), lambda b: (b, 0, 0, 0)),
        ],
        out_shape=[
            jax.ShapeDtypeStruct((nb, rows, HD), F32),
            jax.ShapeDtypeStruct((nb, n_steps + 1, gq, bps), F32),
        ],
        compiler_params=_cparams(("arbitrary",)),
        name="nsa_sample_cmp",
    )(qbd, cmp_kv, b_cmp, msel)

    half = 2 * kvw
    last = n_steps - 1

    def page_spec(k):
        return pl.BlockSpec(
            (None, PAGE, NCK, HD),
            lambda b, s, pt: (pt[b, jnp.minimum(s, last) * n_pages + k], 0, 1, 0))

    nkeys = n_pages * PAGE
    grid_spec = pltpu.PrefetchScalarGridSpec(
        num_scalar_prefetch=1,
        grid=(nb, n_steps + 1),
        in_specs=[page_spec(k) for k in range(n_pages)] + [
            pl.BlockSpec((None, rows, kvw), lambda b, s, pt: (b, 0, 0)),
            pl.BlockSpec((None, PAGE, half), lambda b, s, pt: (b, 0, 0)),
            pl.BlockSpec((rows, nkeys), lambda b, s, pt: (0, s)),
            pl.BlockSpec((None, None, gq, bps), lambda b, s, pt: (b, s, 0, 0)),
            pl.BlockSpec(expand.shape, lambda b, s, pt: (0, 0)),
        ],
        out_specs=pl.BlockSpec((None, rows, HD), lambda b, s, pt: (b, 0, 0)),
        scratch_shapes=[
            pltpu.VMEM((rows, 1), F32),
            pltpu.VMEM((rows, 1), F32),
            pltpu.VMEM((rows, kvw), F32),
        ],
    )
    osel = pl.pallas_call(
        functools.partial(_nsa_sample_sel_kernel, n_pages=n_pages, n_steps=n_steps,
                          past_len=past_len, n_q=n_q),
        grid_spec=grid_spec,
        out_shape=jax.ShapeDtypeStruct((nb, rows, HD), F32),
        compiler_params=_cparams(("arbitrary", "arbitrary")),
        name="nsa_sample_sel",
    )(page_table, *([cache4d] * n_pages), qbd, tail_sel, b_sel, selmask, expand)

    nkw = win_all.shape[1]
    zw = z_s.shape[1]
    return pl.pallas_call(
        functools.partial(_nsa_sample_mix_kernel, n_q=n_q),
        grid=(nb,),
        in_specs=[
            pl.BlockSpec((None, rows, kvw), lambda b: (b, 0, 0)),
            pl.BlockSpec((None, nkw, 2 * kvw), lambda b: (b, 0, 0)),
            pl.BlockSpec(b_win.shape, lambda b: (0, 0)),
            pl.BlockSpec((None, rows, HD), lambda b: (b, 0, 0)),
            pl.BlockSpec((None, rows, HD), lambda b: (b, 0, 0)),
            pl.BlockSpec((n_q, zw), lambda b: (b, 0)),
            pl.BlockSpec((n_q, 3 * A_HEADS), lambda b: (b, 0)),
        ],
        out_specs=pl.BlockSpec((n_q, A_HEADS * HD), lambda b: (b, 0)),
        out_shape=jax.ShapeDtypeStruct((nb * n_q, A_HEADS * HD), F32),
        compiler_params=_cparams(("arbitrary",)),
        name="nsa_sample_mix",
    )(qbd, win_all, b_win, oc, osel, z_s, gl_s)


def _retention_kernel(q_ref, k_ref, v_ref, g_ref, cos_ref, sin_ref, dm_ref, qd_ref, kd_ref, cd_ref,
                      s0_ref, o_ref, s_ref, *, dk, dv, hps):
    c = pl.program_id(2)
    half = dk // 2

    @pl.when(c == 0)
    def _():
        s_ref[...] = s0_ref[...]

    cos = cos_ref[...]
    sin = sin_ref[...]

    def rot(x):
        x1, x2 = x[:, :half], x[:, half:]
        return jnp.concatenate([x1 * cos - x2 * sin, x1 * sin + x2 * cos], axis=1)

    for j in range(hps):
        ks, vs = slice(j * dk, (j + 1) * dk), slice(j * dv, (j + 1) * dv)
        q = rot(q_ref[:, ks])
        k = rot(k_ref[:, ks]) * (dk ** -0.5)
        v = v_ref[:, vs].astype(BF16)
        qb = q.astype(BF16)
        att = _dot_nt(qb, k.astype(BF16)) * dm_ref[j]
        state = s_ref[j]
        o = _dot(att.astype(BF16), v) + _dot(qb, state.astype(BF16)) * qd_ref[j]
        s_ref[j] = state * cd_ref[j] + _dot_tn((k * kd_ref[j]).astype(BF16), v)
        o = o * lax.rsqrt(jnp.mean(o * o, axis=-1, keepdims=True) + EPS)
        o_ref[:, vs] = (o * _silu(g_ref[:, vs])).astype(o_ref.dtype)


def retention(proj, state0, q0, nb, t):
    _, nh, dk, dv = state0.shape
    cs = R_CHUNK if t % R_CHUNK == 0 else t
    n = t // cs
    half = dk // 2
    pos = (q0 + jnp.arange(t)).astype(F32)
    inv = jnp.power(ROPE_BASE, -jnp.arange(half, dtype=F32) / half)
    ang = pos[:, None] * inv[None, :]
    cos, sin = jnp.cos(ang), jnp.sin(ang)
    log_g = jnp.log1p(-jnp.exp2(-5.0 - jnp.arange(nh, dtype=F32)))
    ii = jnp.arange(cs, dtype=F32)
    rel = ii[:, None] - ii[None, :]
    causal = rel >= 0
    dmat = jnp.where(causal, jnp.exp(jnp.where(causal, rel, 0.0) * log_g[:, None, None]), 0.0)
    q_dec = jnp.exp((ii + 1.0) * log_g[:, None])[:, :, None]
    k_dec = jnp.exp((cs - 1.0 - ii) * log_g[:, None])[:, :, None]
    c_dec = jnp.exp(cs * log_g)[:, None, None]
    hps = RET_HEADS_PER_STEP
    assert nh % hps == 0
    ng = nh // hps
    vb0 = 2 * nh * dk // (hps * dv)
    return pl.pallas_call(
        functools.partial(_retention_kernel, dk=dk, dv=dv, hps=hps),
        grid=(nb, ng, n),
        in_specs=[
            pl.BlockSpec((cs, hps * dk), lambda b, h, c: (b * n + c, h)),
            pl.BlockSpec((cs, hps * dk), lambda b, h, c: (b * n + c, ng + h)),
            pl.BlockSpec((cs, hps * dv), lambda b, h, c: (b * n + c, vb0 + h)),
            pl.BlockSpec((cs, hps * dv), lambda b, h, c: (b * n + c, vb0 + ng + h)),
            pl.BlockSpec((cs, half), lambda b, h, c: (c, 0)),
            pl.BlockSpec((cs, half), lambda b, h, c: (c, 0)),
            pl.BlockSpec((hps, cs, cs), lambda b, h, c: (h, 0, 0)),
            pl.BlockSpec((hps, cs, 1), lambda b, h, c: (h, 0, 0)),
            pl.BlockSpec((hps, cs, 1), lambda b, h, c: (h, 0, 0)),
            pl.BlockSpec((hps, 1, 1), lambda b, h, c: (h, 0, 0)),
            pl.BlockSpec((None, hps, dk, dv), lambda b, h, c: (b, h, 0, 0)),
        ],
        out_specs=[
            pl.BlockSpec((cs, hps * dv), lambda b, h, c: (b * n + c, h)),
            pl.BlockSpec((None, hps, dk, dv), lambda b, h, c: (b, h, 0, 0)),
        ],
        out_shape=[
            jax.ShapeDtypeStruct((nb * t, nh * dv), BF16 if cs % 16 == 0 else F32),
            jax.ShapeDtypeStruct(state0.shape, F32),
        ],
        compiler_params=_cparams(("arbitrary", "arbitrary", "arbitrary")),
        name="retention",
    )(proj, proj, proj, proj, cos, sin, dmat, q_dec, k_dec, c_dec, state0)


def _t5_bucket_np(dist):
    n = np.maximum(dist, 0)
    exact = N_BUCKETS // 2
    logv = np.log(np.maximum(n, 1).astype(np.float32) / np.float32(exact)) / np.float32(
        math.log(MAX_DISTANCE / exact))
    large = np.minimum(exact + (logv * np.float32(N_BUCKETS - exact)).astype(np.int32), N_BUCKETS - 1)
    return np.where(n < exact, n, large).astype(np.int32)


def _bucket_starts():
    bk = _t5_bucket_np(np.arange(4 * MAX_DISTANCE))
    assert np.all(np.diff(bk) >= 0) and bk[-1] == N_BUCKETS - 1
    return [int(np.argmax(bk >= b)) for b in range(N_BUCKETS)]


_BUCKET_START = _bucket_starts()


def _sel_matrix(n_cmp, n_cmp_pad, n_sel_pad):
    c = np.arange(n_cmp_pad)[:, None]
    j = np.arange(n_sel_pad)[None, :]
    a = (c >= SEL_RATIO * j) & (c < SEL_RATIO * j + SEL_RATIO)
    b = (c >= SEL_RATIO * j - 1) & (c < SEL_RATIO * j + SEL_RATIO - 1)
    m = (a.astype(np.float32) + b.astype(np.float32)) * (c < n_cmp)
    return jnp.asarray(m, BF16)


def _expand_matrix(n_blk_pad, n_keys):
    blk = np.arange(n_blk_pad)[:, None]
    key = np.arange(n_keys)[None, :]
    return jnp.asarray((key // SEL_BLOCK == blk).astype(np.float32), BF16)


def _w1_pairs(w1):
    w = w1.reshape(2, 2, CMP_STRIDE // 2, 2, HD, HD)
    w = jnp.transpose(w, (2, 3, 4, 0, 1, 5))
    return w.reshape(CMP_STRIDE // 2, 2 * HD, 4 * HD).astype(BF16)


def kernel(x_prompt, x_sample, c_prompt, c_sample, cache_nsa_kv, cache_nsa_win, state_ret, page_table,
           norm_g, ada_w, ada_b, rel_bias, a_w_in, a_w_out, a_cmp_pe, a_cmp_w1, a_cmp_w2, a_qk_g,
           r_w_in, r_w_out):
    nbp, t, d = x_prompt.shape
    nbs, ts, _ = x_sample.shape
    n_tab = page_table.shape[1]
    past_len = n_tab * PAGE
    kvw = A_KV * HD
    qw = A_HEADS * HD
    assert t % (2 * LANE) == 0 and t >= WINDOW and ts == 8 and past_len >= WINDOW
    assert cache_nsa_win.shape[2] == WINDOW

    n_c = nbp + nbs
    pad_c = (-n_c) % 8
    c_all = jnp.concatenate([c_prompt, c_sample, jnp.zeros((pad_c, d), F32)], axis=0)
    mod = ada_modulation(c_all, ada_w, ada_b).reshape(ada_w.shape[0], n_c + pad_c, 3, d)

    def mods(layer):
        m = mod[layer]
        return (m[:nbp, 0], m[:nbp, 1], m[:nbp, 2]), (m[nbp:n_c, 0], m[nbp:n_c, 1], m[nbp:n_c, 2])

    (sh_p, sc_p, gt_p), (sh_s, sc_s, gt_s) = mods(0)
    hp = norm_modulate(x_prompt, norm_g[0], sc_p, sh_p)
    hs = norm_modulate(x_sample, norm_g[0], sc_s, sh_s)
    w_in_t = jnp.swapaxes(a_w_in[0], 0, 1)
    n_main = w_in_t.shape[0] - 3 * A_HEADS
    n_qkv = qw + 6 * kvw
    proj_p, proj_s = project(hp, hs, w_in_t, n_cols=n_qkv, w_is_nk=True)
    zt_p, z_s = project_gate_t(hp, hs, w_in_t, n_qkv, n_main - n_qkv)
    w_gate = jnp.pad(w_in_t[n_main:], ((0, LANE - 3 * A_HEADS), (0, 0)))
    gl_p, gl_s = project(hp, hs, w_gate, w_is_nk=True)
    gl_p, gl_s = gl_p[:, :3 * A_HEADS], gl_s[:, :3 * A_HEADS]
    qk_g = a_qk_g[0]

    qn_p, rows_p, win_p, rows_t_p, win_t_p = nsa_prep(proj_p, qk_g)
    qn_s, rows_s, win_s, rows_t_s, _ = nsa_prep(proj_s, qk_g)

    w_pairs = _w1_pairs(a_cmp_w1[0])
    pe = a_cmp_pe[0]
    w1 = a_cmp_w1[0].reshape(2, CMP_LEN, HD, HD)
    w2 = a_cmp_w2[0]
    cpp = PAGE // CMP_STRIDE

    pages_p = t // PAGE
    ident = jnp.arange(nbp * pages_p, dtype=jnp.int32).reshape(nbp, pages_p)
    ab_p = compress_stage_a(rows_t_p.reshape(nbp * pages_p, PAGE, 4 * A_KV, HD), ident, w_pairs,
                            min(PAGES_PER_STEP, pages_p))
    zero_tail = jnp.zeros((nbp, cpp, NCK, 2 * HD), F32)
    cmp_p = compress_stage_b(ab_p, zero_tail, pe, w1, w2, qk_g)

    nq = t // Q_BLOCK
    n_ch_p = t // CMP_STRIDE
    tsel, twin, tcmp = prompt_bias_tables(rel_bias, nq, n_ch_p)
    n_cmp_p = t // CMP_STRIDE - 1
    n_sel_p = -(-t // SEL_BLOCK)
    n_sel_pad_p = -(-n_sel_p // 16) * 16
    msel_p = _sel_matrix(n_cmp_p, n_ch_p, n_sel_pad_p).T
    expand_p = _expand_matrix(n_sel_pad_p, t).T
    gates_g = jnp.transpose(gl_p.reshape(nbp * t, 3, A_KV, A_HPG), (2, 1, 3, 0)).reshape(A_KV, 3 * A_HPG, nbp * t)
    mixed_p = nsa_prompt_attention(qn_p, cmp_p, rows_p, win_p, zt_p, gates_g,
                                   tsel, twin, tcmp, msel_p, expand_p, nbp, t)

    n_pool = cache_nsa_kv.shape[1]
    cache4d = cache_nsa_kv[0].reshape(n_pool, PAGE, 4 * A_KV, HD)
    n_pages = min(PAGES_PER_STEP, n_tab)
    ab_s = compress_stage_a(cache4d, page_table, w_pairs, n_pages)
    tail_rows = jnp.pad(rows_s.reshape(nbs, ts, 4 * kvw), ((0, 0), (0, PAGE - ts), (0, 0)))
    ident_s = jnp.arange(nbs, dtype=jnp.int32).reshape(nbs, 1)
    ab_tail = compress_stage_a(tail_rows.reshape(nbs, PAGE, 4 * A_KV, HD), ident_s, w_pairs, 1)
    cmp_s = compress_stage_b(ab_s, ab_tail, pe, w1, w2, qk_g)

    n_cmp_s = cmp_s.shape[1] // NCK
    n_steps = n_tab // n_pages
    bps = n_pages * PAGE // SEL_BLOCK
    n_sel_pad = (n_steps + 1) * bps
    nkeys = (n_steps + 1) * n_pages * PAGE
    nkw = WINDOW + LANE
    b_cmp, b_sel, b_win = sample_bias_rows(rel_bias, past_len, ts, n_cmp_s, nkeys, nkw)
    msel_s = _sel_matrix(n_cmp_s, n_cmp_s, n_sel_pad)
    expand_s = _expand_matrix(bps, n_pages * PAGE)
    q5 = qn_s.reshape(nbs, ts, A_KV, A_HPG, HD)
    eye = jnp.eye(A_KV, dtype=BF16)
    qbd = jnp.einsum('bqghd,ge->bghqed', q5, eye).reshape(nbs, A_HEADS * ts, kvw)
    win_new = win_s.reshape(nbs, ts, 2 * kvw)
    win_cache = cache_nsa_win[0].reshape(nbs, WINDOW, 2 * kvw)
    win_all = jnp.concatenate([win_cache, win_new], axis=1)
    win_pad = jnp.pad(win_all, ((0, 0), (0, nkw - WINDOW - ts), (0, 0)))
    tail_sel = tail_rows[:, :, 2 * kvw:]
    mixed_s = nsa_sample_attention(qbd, cmp_s, cache4d, page_table, tail_sel, win_pad, z_s, gl_s,
                                   b_cmp, b_sel, b_win, msel_s, expand_s, past_len, ts).astype(BF16)

    xs_flat = x_sample.reshape(nbs * ts, d)
    gate_s_rows = jnp.repeat(gt_s, ts, axis=0)
    x1p, x1s = project_residual(mixed_p, mixed_s, a_w_out[0], x_prompt.reshape(nbp * t, d), gt_p,
                                xs_flat, gate_s_rows, t)

    (sh_p, sc_p, gt_p), (sh_s, sc_s, gt_s) = mods(1)
    hp = norm_modulate(x1p.reshape(nbp, t, d), norm_g[1], sc_p, sh_p)
    hs = norm_modulate(x1s.reshape(nbs, ts, d), norm_g[1], sc_s, sh_s)
    rp, rs = project(hp, hs, r_w_in[0])
    s0 = jnp.zeros((nbp,) + state_ret.shape[2:], F32)
    op, ret_p = retention(rp, s0, 0, nbp, t)
    os_, ret_s = retention(rs, state_ret[0], past_len, nbs, ts)
    os_ = os_.astype(BF16)
    gate_s_rows = jnp.repeat(gt_s, ts, axis=0)
    x2p, x2s = project_residual(op, os_, r_w_out[0], x1p, gt_p, x1s, gate_s_rows, t)

    kv_p = rows_t_p.reshape(1, nbp, t, 4, A_KV, HD)
    kv_s = rows_t_s.reshape(1, nbs, ts, 4, A_KV, HD)
    wst_p = win_t_p.reshape(nbp, t, 2, A_KV, HD)[None, :, t - WINDOW:]
    wst_s = win_all[:, ts:].reshape(1, nbs, WINDOW, 2, A_KV, HD)
    return (x2p.reshape(nbp, t, d), x2s.reshape(nbs, ts, d), kv_p, kv_s, wst_p, wst_s,
            ret_p[None], ret_s[None])
```

```python
import functools
import math

import numpy as np
import jax
import jax.numpy as jnp
from jax import lax
from jax.experimental import pallas as pl
from jax.experimental.pallas import tpu as pltpu

EPS = 1e-6
HD = 128
A_KV = 4
A_HPG = 8
A_HEADS = A_KV * A_HPG
CMP_STRIDE = 16
CMP_LEN = 32
SEL_BLOCK = 64
SEL_RATIO = SEL_BLOCK // CMP_STRIDE
SEL_TOPK = 16
WINDOW = 512
Q_BLOCK = 128
FORCE_SCORE = 1e4
NEG = -1e30
TINY = 1e-30
N_BUCKETS = 32
MAX_DISTANCE = 1024
R_CHUNK = 128
ROPE_BASE = 10000.0
LOG2E = math.log2(math.e)
Q_SCALE = HD ** -0.5 * LOG2E
PAGE = 128
PAGES_PER_STEP = 16
RET_HEADS_PER_STEP = 8
SEL_KEY_TILE = 512

LANE = 128
VMEM_LIMIT = 60 * 1024 * 1024

F32 = jnp.float32
BF16 = jnp.bfloat16


def _cparams(sem):
    return pltpu.CompilerParams(dimension_semantics=sem, vmem_limit_bytes=VMEM_LIMIT)


def _dot(a, b):
    return jnp.dot(a, b, preferred_element_type=F32)


def _dot_nt(a, b):
    return lax.dot_general(a, b, (((1,), (1,)), ((), ())), preferred_element_type=F32)


def _dot_tn(a, b):
    return lax.dot_general(a, b, (((0,), (0,)), ((), ())), preferred_element_type=F32)


def _silu(x):
    h = 0.5 * x
    return h + h * jnp.tanh(h)


def _sigmoid(x):
    return 0.5 + 0.5 * jnp.tanh(0.5 * x)


def _split3_dot(x, m_bf16):
    hi = x.astype(BF16)
    r1 = x - hi.astype(F32)
    mid = r1.astype(BF16)
    lo = (r1 - mid.astype(F32)).astype(BF16)
    return _dot(hi, m_bf16) + _dot(mid, m_bf16) + _dot(lo, m_bf16)


def _ada_kernel(c_ref, w_ref, b_ref, o_ref):
    a = _silu(c_ref[...]).astype(BF16)
    o_ref[...] = _dot(a, w_ref[...].astype(BF16)) + b_ref[...]


def ada_modulation(c_all, ada_w, ada_b):
    depth, d, n = ada_w.shape
    rows = c_all.shape[0]
    tn = 512
    return pl.pallas_call(
        _ada_kernel,
        grid=(depth, n // tn),
        in_specs=[
            pl.BlockSpec((rows, d), lambda l, j: (0, 0)),
            pl.BlockSpec((None, d, tn), lambda l, j: (l, 0, j)),
            pl.BlockSpec((None, 1, tn), lambda l, j: (l, 0, j)),
        ],
        out_specs=pl.BlockSpec((None, rows, tn), lambda l, j: (l, 0, j)),
        out_shape=jax.ShapeDtypeStruct((depth, rows, n), F32),
        compiler_params=_cparams(("arbitrary", "arbitrary")),
        name="ada_modulation",
    )(c_all, ada_w, ada_b.reshape(depth, 1, n))


def _norm_mod_kernel(x_ref, g_ref, sc_ref, sh_ref, o_ref):
    x = x_ref[...]
    y = x * lax.rsqrt(jnp.mean(x * x, axis=-1, keepdims=True) + EPS)
    y = y * g_ref[...]
    o_ref[...] = (y * (1.0 + sc_ref[...]) + sh_ref[...]).astype(o_ref.dtype)


def norm_modulate(x, g, scale, shift):
    b, t, d = x.shape
    tt = min(t, 256)
    out = pl.pallas_call(
        _norm_mod_kernel,
        grid=(b, t // tt),
        in_specs=[
            pl.BlockSpec((None, tt, d), lambda i, j: (i, j, 0)),
            pl.BlockSpec((1, d), lambda i, j: (0, 0)),
            pl.BlockSpec((None, 1, d), lambda i, j: (i, 0, 0)),
            pl.BlockSpec((None, 1, d), lambda i, j: (i, 0, 0)),
        ],
        out_specs=pl.BlockSpec((None, tt, d), lambda i, j: (i, j, 0)),
        out_shape=jax.ShapeDtypeStruct((b, t, d), BF16),
        compiler_params=_cparams(("arbitrary", "arbitrary")),
        name="norm_modulate",
    )(x, g.reshape(1, d), scale.reshape(b, 1, d), shift.reshape(b, 1, d))
    return out.reshape(b * t, d)


def _stage_weight_tile(w_hbm, wf_ref, wb_ref, sem, *, w_is_nk, tn, j0):
    j = pl.program_id(0)

    def copy(jj):
        start = pl.multiple_of((j0 + jj) * tn, tn)
        src = w_hbm.at[pl.ds(start, tn), :] if w_is_nk else w_hbm.at[:, pl.ds(start, tn)]
        return pltpu.make_async_copy(src, wf_ref, sem)

    @pl.when(j == 0)
    def _():
        copy(0).start()

    copy(j).wait()
    wb_ref[...] = wf_ref[...].astype(BF16)

    @pl.when(j + 1 < pl.num_programs(0))
    def _():
        copy(j + 1).start()


def _proj_kernel(xp_ref, xs_ref, w_hbm, op_ref, os_ref, wf_ref, wb_ref, sem, *, stage, mm):
    @pl.when(pl.program_id(1) == 0)
    def _():
        stage(w_hbm, wf_ref, wb_ref, sem)
        os_ref[...] = mm(xs_ref[...], wb_ref[...])

    op_ref[...] = mm(xp_ref[...], wb_ref[...])


def _proj_gate_t_kernel(xp_ref, xs_ref, w_hbm, opt_ref, os_ref, wf_ref, wb_ref, sem, *, stage):
    @pl.when(pl.program_id(1) == 0)
    def _():
        stage(w_hbm, wf_ref, wb_ref, sem)
        os_ref[...] = _silu(_dot_nt(xs_ref[...], wb_ref[...]))

    opt_ref[...] = _silu(_dot_nt(wb_ref[...], xp_ref[...]))


def _weight_scratch(shape):
    return [pltpu.VMEM(shape, F32), pltpu.VMEM(shape, BF16), pltpu.SemaphoreType.DMA(())]


def project_gate_t(xp, xs, w_nk, row0, n_rows):
    mp, k = xp.shape
    ms = xs.shape[0]
    tm, tn = _proj_tiles(mp, k, n_rows)
    assert row0 % tn == 0
    stage = functools.partial(_stage_weight_tile, w_is_nk=True, tn=tn, j0=row0 // tn)
    return pl.pallas_call(
        functools.partial(_proj_gate_t_kernel, stage=stage),
        grid=(n_rows // tn, mp // tm),
        in_specs=[
            pl.BlockSpec((tm, k), lambda j, i: (i, 0)),
            pl.BlockSpec((ms, k), lambda j, i: (0, 0)),
            pl.BlockSpec(memory_space=pl.ANY),
        ],
        out_specs=[
            pl.BlockSpec((tn, tm), lambda j, i: (j, i)),
            pl.BlockSpec((ms, tn), lambda j, i: (0, j)),
        ],
        out_shape=[jax.ShapeDtypeStruct((n_rows, mp), F32), jax.ShapeDtypeStruct((ms, n_rows), F32)],
        scratch_shapes=_weight_scratch((tn, k)),
        compiler_params=_cparams(("arbitrary", "arbitrary")),
        name="project_gate_t",
    )(xp, xs, w_nk)


def _proj_res_kernel(xp_ref, xs_ref, w_hbm, rp_ref, gp_ref, rs_ref, gs_ref, op_ref, os_ref,
                     wf_ref, wb_ref, sem, *, stage):
    @pl.when(pl.program_id(1) == 0)
    def _():
        stage(w_hbm, wf_ref, wb_ref, sem)
        os_ref[...] = rs_ref[...] + gs_ref[...] * _dot(xs_ref[...], wb_ref[...])

    op_ref[...] = rp_ref[...] + gp_ref[...] * _dot(xp_ref[...], wb_ref[...])


def _proj_tiles(mp, k, n, residual=False):
    if k <= 4096:
        tm, tn = (1024, 512) if residual else (1024, 1024)
    else:
        tm, tn = 512, 512
    tm = min(tm, mp)
    while mp % tm:
        tm //= 2
    tn = min(tn, n)
    while n % tn:
        tn //= 2
    return tm, tn


def project(xp, xs, w, n_cols=None, w_is_nk=False):
    mp, k = xp.shape
    ms = xs.shape[0]
    n = w.shape[0 if w_is_nk else 1] if n_cols is None else n_cols
    tm, tn = _proj_tiles(mp, k, n)
    stage = functools.partial(_stage_weight_tile, w_is_nk=w_is_nk, tn=tn, j0=0)
    return pl.pallas_call(
        functools.partial(_proj_kernel, stage=stage, mm=_dot_nt if w_is_nk else _dot),
        grid=(n // tn, mp // tm),
        in_specs=[
            pl.BlockSpec((tm, k), lambda j, i: (i, 0)),
            pl.BlockSpec((ms, k), lambda j, i: (0, 0)),
            pl.BlockSpec(memory_space=pl.ANY),
        ],
        out_specs=[
            pl.BlockSpec((tm, tn), lambda j, i: (i, j)),
            pl.BlockSpec((ms, tn), lambda j, i: (0, j)),
        ],
        out_shape=[jax.ShapeDtypeStruct((mp, n), F32), jax.ShapeDtypeStruct((ms, n), F32)],
        scratch_shapes=_weight_scratch((tn, k) if w_is_nk else (k, tn)),
        compiler_params=_cparams(("arbitrary", "arbitrary")),
        name="project",
    )(xp, xs, w)


def project_residual(xp, xs, w, res_p, gate_p, res_s, gate_s, rows_per_batch):
    mp, k = xp.shape
    ms = xs.shape[0]
    n = w.shape[1]
    tm, tn = _proj_tiles(mp, k, n, residual=True)
    tm = min(tm, rows_per_batch)
    tpb = rows_per_batch // tm
    nb = gate_p.shape[0]
    stage = functools.partial(_stage_weight_tile, w_is_nk=False, tn=tn, j0=0)
    return pl.pallas_call(
        functools.partial(_proj_res_kernel, stage=stage),
        grid=(n // tn, mp // tm),
        in_specs=[
            pl.BlockSpec((tm, k), lambda j, i: (i, 0)),
            pl.BlockSpec((ms, k), lambda j, i: (0, 0)),
            pl.BlockSpec(memory_space=pl.ANY),
            pl.BlockSpec((tm, tn), lambda j, i: (i, j)),
            pl.BlockSpec((None, 1, tn), lambda j, i: (i // tpb, 0, j)),
            pl.BlockSpec((ms, tn), lambda j, i: (0, j)),
            pl.BlockSpec((ms, tn), lambda j, i: (0, j)),
        ],
        out_specs=[
            pl.BlockSpec((tm, tn), lambda j, i: (i, j)),
            pl.BlockSpec((ms, tn), lambda j, i: (0, j)),
        ],
        out_shape=[jax.ShapeDtypeStruct((mp, n), F32), jax.ShapeDtypeStruct((ms, n), F32)],
        scratch_shapes=_weight_scratch((k, tn)),
        compiler_params=_cparams(("arbitrary", "arbitrary")),
        name="project_residual",
    )(xp, xs, w, res_p, gate_p.reshape(nb, 1, n), res_s, gate_s)


def _nsa_prep_kernel(q_ref, a_ref, b_ref, c_ref, g_ref, qn_ref, rows_ref, win_ref, rows_t_ref, win_t_ref):
    def hnorm(x, g):
        return x * lax.rsqrt(jnp.mean(x * x, axis=-1, keepdims=True) + EPS) * g

    g = g_ref[...]
    for j in range(A_HEADS):
        sl = slice(j * HD, (j + 1) * HD)
        qn_ref[:, sl] = (hnorm(q_ref[:, sl], g[0:1]) * Q_SCALE).astype(qn_ref.dtype)
    kvw = A_KV * HD
    sls = [slice(j * HD, (j + 1) * HD) for j in range(2 * A_KV)]
    row_slots = ([a_ref[:, s] for s in sls] + [hnorm(b_ref[:, s], g[2:3]) for s in sls[:A_KV]]
                 + [b_ref[:, s] for s in sls[A_KV:]])
    win_slots = [hnorm(c_ref[:, s], g[3:4]) for s in sls[:A_KV]] + [c_ref[:, s] for s in sls[A_KV:]]
    for j, x in enumerate(row_slots):
        rows_ref[:, j * HD:(j + 1) * HD] = x
    for j, x in enumerate(win_slots):
        win_ref[:, j * HD:(j + 1) * HD] = x
    rows_t_ref[...] = jnp.swapaxes(jnp.stack(row_slots, axis=0), 0, 1)
    win_t_ref[...] = jnp.swapaxes(jnp.stack(win_slots, axis=0), 0, 1)


def nsa_prep(proj, qk_g):
    m = proj.shape[0]
    tm = min(m, 256)
    qw = A_HEADS * HD
    kv2 = 2 * A_KV * HD
    base = qw // kv2
    return pl.pallas_call(
        _nsa_prep_kernel,
        grid=(m // tm,),
        in_specs=[
            pl.BlockSpec((tm, qw), lambda i: (i, 0)),
            pl.BlockSpec((tm, kv2), lambda i: (i, base)),
            pl.BlockSpec((tm, kv2), lambda i: (i, base + 1)),
            pl.BlockSpec((tm, kv2), lambda i: (i, base + 2)),
            pl.BlockSpec((4, HD), lambda i: (0, 0)),
        ],
        out_specs=[
            pl.BlockSpec((tm, qw), lambda i: (i, 0)),
            pl.BlockSpec((tm, 2 * kv2), lambda i: (i, 0)),
            pl.BlockSpec((tm, kv2), lambda i: (i, 0)),
            pl.BlockSpec((tm, 4 * A_KV, HD), lambda i: (i, 0, 0)),
            pl.BlockSpec((tm, 2 * A_KV, HD), lambda i: (i, 0, 0)),
        ],
        out_shape=[
            jax.ShapeDtypeStruct((m, qw), BF16),
            jax.ShapeDtypeStruct((m, 2 * kv2), F32),
            jax.ShapeDtypeStruct((m, kv2), F32),
            jax.ShapeDtypeStruct((m, 4 * A_KV, HD), F32),
            jax.ShapeDtypeStruct((m, 2 * A_KV, HD), F32),
        ],
        compiler_params=_cparams(("arbitrary",)),
        name="nsa_prep",
    )(proj, proj, proj, proj, qk_g)


NCK = 2 * A_KV


def _cmp_a_kernel(pt_ref, *refs, n_pages):
    del pt_ref
    pages = refs[:n_pages]
    w_ref = refs[n_pages]
    o_ref = refs[n_pages + 1]
    cpp = PAGE // CMP_STRIDE
    n_ch = n_pages * cpp

    def halves(tiles, lo):
        return [jnp.concatenate([tiles[a][lo:lo + A_KV], tiles[a + 1][lo:lo + A_KV]], axis=0)
                for a in range(0, len(tiles), 2)]

    acc = [None, None]
    for tp in range(CMP_STRIDE // 2):
        pieces = []
        for pg in pages:
            for ch in range(cpp):
                t0 = ch * CMP_STRIDE + 2 * tp
                pieces.append(jnp.concatenate([pg[t0], pg[t0 + 1]], axis=1))
        for kind in range(2):
            lhs = jnp.concatenate(halves(pieces, kind * A_KV), axis=0).astype(BF16)
            d = _dot(lhs, w_ref[tp, :, kind * 2 * HD:(kind + 1) * 2 * HD])
            acc[kind] = d if acc[kind] is None else acc[kind] + d
    for a in range(0, n_ch, 2):
        r = slice(a * A_KV, (a + 2) * A_KV)
        kk, vv = acc[0][r], acc[1][r]
        o_ref[a] = jnp.concatenate([kk[:A_KV], vv[:A_KV]], axis=0)
        o_ref[a + 1] = jnp.concatenate([kk[A_KV:], vv[A_KV:]], axis=0)


def compress_stage_a(store, page_table, w_pairs, n_pages):
    nb, n_tab = page_table.shape
    steps = n_tab // n_pages
    cpp = PAGE // CMP_STRIDE

    def page_spec(k):
        return pl.BlockSpec((None, PAGE, NCK, HD), lambda b, s, pt: (pt[b, s * n_pages + k], 0, 0, 0))

    grid_spec = pltpu.PrefetchScalarGridSpec(
        num_scalar_prefetch=1,
        grid=(nb, steps),
        in_specs=[page_spec(k) for k in range(n_pages)]
        + [pl.BlockSpec(w_pairs.shape, lambda b, s, pt: (0, 0, 0))],
        out_specs=pl.BlockSpec((None, n_pages * cpp, NCK, 2 * HD), lambda b, s, pt: (b, s, 0, 0)),
    )
    return pl.pallas_call(
        functools.partial(_cmp_a_kernel, n_pages=n_pages),
        grid_spec=grid_spec,
        out_shape=jax.ShapeDtypeStruct((nb, n_tab * cpp, NCK, 2 * HD), F32),
        compiler_params=_cparams(("arbitrary", "arbitrary")),
        name="compress_stage_a",
    )(page_table, *([store] * n_pages), w_pairs)


def _cmp_b_kernel(ab_ref, nx_ref, tail_ref, pe_ref, w1_ref, w2_ref, g_ref, o_ref, const_ref):
    j = pl.program_id(1)
    nch = ab_ref.shape[0]
    rows = nch * NCK

    @pl.when((pl.program_id(0) == 0) & (j == 0))
    def _():
        for kind in range(2):
            const = jnp.zeros((8, HD), F32)
            for t in range(CMP_LEN):
                row = jnp.broadcast_to(pe_ref[kind, t:t + 1, :], (8, HD)).astype(BF16)
                const = const + _dot(row, w1_ref[kind, t].astype(BF16))
            const_ref[kind * A_KV:(kind + 1) * A_KV, :] = const[:A_KV]

    ab = ab_ref[...].reshape(rows, 2 * HD)
    nxt = pltpu.roll(ab[:, HD:], rows - NCK, 0).reshape(nch, NCK, HD)
    first_next = jnp.where(j == pl.num_programs(1) - 1, tail_ref[0, :, HD:], nx_ref[0, :, HD:])
    cid = lax.broadcasted_iota(jnp.int32, (nch, NCK, HD), 0)
    nxt = jnp.where(cid == nch - 1, first_next[None], nxt)
    pre = ab[:, :HD].reshape(nch, NCK, HD) + nxt + const_ref[...][None]
    act = _silu(pre).reshape(rows, HD).astype(BF16)
    w2 = jnp.concatenate([w2_ref[0], w2_ref[1]], axis=1).astype(BF16)
    y2 = _dot(act, w2)
    is_k = (lax.broadcasted_iota(jnp.int32, (rows, HD), 0) & (NCK - 1)) < A_KV
    y = jnp.where(is_k, y2[:, :HD], y2[:, HD:])
    yn = y * lax.rsqrt(jnp.mean(y * y, axis=-1, keepdims=True) + EPS) * g_ref[1:2]
    o_ref[...] = jnp.where(is_k, yn, y).reshape(nch, NCK, HD)


def compress_stage_b(ab, tail, pe, w1, w2, qk_g):
    nb, n_ch = ab.shape[:2]
    tc = min(256, n_ch)
    assert n_ch % tc == 0
    nt = n_ch // tc
    out = pl.pallas_call(
        _cmp_b_kernel,
        grid=(nb, nt),
        in_specs=[
            pl.BlockSpec((None, tc, NCK, 2 * HD), lambda b, j: (b, j, 0, 0)),
            pl.BlockSpec((None, 1, NCK, 2 * HD), lambda b, j: (b, jnp.minimum((j + 1) * tc, n_ch - 1), 0, 0)),
            pl.BlockSpec((None, 1, NCK, 2 * HD), lambda b, j: (b, 0, 0, 0)),
            pl.BlockSpec(pe.shape, lambda b, j: (0, 0, 0)),
            pl.BlockSpec(w1.shape, lambda b, j: (0, 0, 0, 0)),
            pl.BlockSpec(w2.shape, lambda b, j: (0, 0, 0)),
            pl.BlockSpec((4, HD), lambda b, j: (0, 0)),
        ],
        out_specs=pl.BlockSpec((None, tc, NCK, HD), lambda b, j: (b, j, 0, 0)),
        out_shape=jax.ShapeDtypeStruct((nb, n_ch, NCK, HD), F32),
        scratch_shapes=[pltpu.VMEM((NCK, HD), F32)],
        compiler_params=_cparams(("arbitrary", "arbitrary")),
        name="compress_stage_b",
    )(ab, ab, tail, pe, w1, w2, qk_g)
    return out.reshape(nb, n_ch * NCK, HD)


def _cmp_rows(cmp_ref, slot, n):
    return cmp_ref[pl.ds(slot, n, stride=NCK), :]


def _select_topk(score, blk, k_top):
    sel = jnp.zeros(score.shape, F32)
    for _ in range(k_top):
        m = jnp.max(score, axis=-1, keepdims=True)
        cand = jnp.where(score == m, blk, jnp.int32(1 << 30))
        first = jnp.min(cand, axis=-1, keepdims=True)
        pick = blk == first
        sel = jnp.where(pick, 1.0, sel)
        score = jnp.where(pick, -jnp.inf, score)
    return sel


def _select_topk_t(score, blk, k_top, n_blk):
    rank = jnp.zeros(score.shape, F32)
    for i in range(n_blk):
        row = score[i:i + 1, :]
        tie = jnp.where(blk > i, 1.0, 0.0)
        rank = rank + jnp.where(row > score, 1.0, jnp.where(row == score, tie, 0.0))
    return jnp.where(rank < k_top, 1.0, 0.0)


def _bias_select(dist, dmin, dmax, value_of):
    lo = int(_t5_bucket_np(np.array([max(dmin, 0)]))[0])
    hi = int(_t5_bucket_np(np.array([max(dmax, 0)]))[0])
    out = jnp.full(dist.shape, value_of(lo), F32)
    for b in range(lo + 1, hi + 1):
        out = jnp.where(dist >= _BUCKET_START[b], value_of(b), out)
    return out * LOG2E


def _bias_tables_kernel(rb_ref, tsel_ref, twin_ref, tcmp_ref):
    head = pl.program_id(0)
    value_of = lambda b: rb_ref[b, head]
    n_off, nq, nc = tsel_ref.shape[0], tcmp_ref.shape[0], tcmp_ref.shape[1]
    kl = lax.broadcasted_iota(jnp.int32, (LANE, Q_BLOCK), 0)
    ql = lax.broadcasted_iota(jnp.int32, (LANE, Q_BLOCK), 1)
    for o in range(n_off):
        base = (o - 1) * Q_BLOCK
        dist = base + ql - kl
        tile = _bias_select(dist, base - (LANE - 1), base + Q_BLOCK - 1, value_of)
        tsel_ref[o] = tile
        if o < twin_ref.shape[0]:
            twin_ref[o] = tile + jnp.where((dist >= 0) & (dist < WINDOW), 0.0, NEG)
    step = Q_BLOCK // CMP_STRIDE
    rows = nc + (nq - 1) * step
    base = (nq - 1) * Q_BLOCK
    cend = lax.broadcasted_iota(jnp.int32, (rows, Q_BLOCK), 0) * CMP_STRIDE + (CMP_LEN - 1)
    qc = lax.broadcasted_iota(jnp.int32, (rows, Q_BLOCK), 1)
    tall = _bias_select(base + qc - cend, base - ((rows - 1) * CMP_STRIDE + CMP_LEN - 1),
                        base + Q_BLOCK - 1, value_of)
    for i in range(nq):
        r0 = (nq - 1 - i) * step
        tcmp_ref[i] = tall[r0:r0 + nc]


def prompt_bias_tables(rel_bias, nq, n_cmp):
    n_win = min(WINDOW // Q_BLOCK + 3, nq + 1)
    return pl.pallas_call(
        _bias_tables_kernel,
        grid=(A_HEADS,),
        in_specs=[pl.BlockSpec(memory_space=pltpu.SMEM)],
        out_specs=[
            pl.BlockSpec((None, nq + 1, LANE, Q_BLOCK), lambda h: (h // A_HPG, 0, 0, h % A_HPG)),
            pl.BlockSpec((None, n_win, LANE, Q_BLOCK), lambda h: (h // A_HPG, 0, 0, h % A_HPG)),
            pl.BlockSpec((None, nq, n_cmp, Q_BLOCK), lambda h: (h // A_HPG, 0, 0, h % A_HPG)),
        ],
        out_shape=[
            jax.ShapeDtypeStruct((A_KV, nq + 1, LANE, A_HPG * Q_BLOCK), F32),
            jax.ShapeDtypeStruct((A_KV, n_win, LANE, A_HPG * Q_BLOCK), F32),
            jax.ShapeDtypeStruct((A_KV, nq, n_cmp, A_HPG * Q_BLOCK), F32),
        ],
        compiler_params=_cparams(("arbitrary",)),
        name="prompt_bias_tables",
    )(rel_bias)


def _bias_rows_kernel(rb_ref, cmp_ref, sel_ref, win_ref, *, past_len, n_q):
    head = pl.program_id(0)
    value_of = lambda b: rb_ref[b, head]

    def fill(ref, key_pos, kmin, kmax, q0):
        q = lax.broadcasted_iota(jnp.int32, ref.shape, 0)
        k = lax.broadcasted_iota(jnp.int32, ref.shape, 1)
        ref[...] = _bias_select(q0 + q - key_pos(k), q0 - kmax, q0 + n_q - 1 - kmin, value_of)

    nc, nk, nw = cmp_ref.shape[1], sel_ref.shape[1], win_ref.shape[1]
    fill(cmp_ref, lambda k: k * CMP_STRIDE + (CMP_LEN - 1), CMP_LEN - 1, (nc - 1) * CMP_STRIDE + CMP_LEN - 1, past_len)
    fill(sel_ref, lambda k: k, 0, nk - 1, past_len)
    fill(win_ref, lambda k: k, 0, nw - 1, WINDOW)


def sample_bias_rows(rel_bias, past_len, n_q, n_cmp, n_keys, n_win):
    shapes = [(A_HEADS * n_q, n) for n in (n_cmp, n_keys, n_win)]
    return pl.pallas_call(
        functools.partial(_bias_rows_kernel, past_len=past_len, n_q=n_q),
        grid=(A_HEADS,),
        in_specs=[pl.BlockSpec(memory_space=pltpu.SMEM)],
        out_specs=[pl.BlockSpec((n_q, s[1]), lambda h: (h, 0)) for s in shapes],
        out_shape=[jax.ShapeDtypeStruct(s, F32) for s in shapes],
        compiler_params=_cparams(("arbitrary",)),
        name="sample_bias_rows",
    )(rel_bias)


def _nsa_prompt_kernel(q_ref, cmp_ref, ks_ref, vs_ref, kw_ref, vw_ref, zc_ref, zs_ref, zw_ref,
                       gt_ref, tsel_ref, twin_ref, tcmp_ref, msel_ref, exp_ref, o_ref,
                       qa_ref, acc_ref, m_ref, l_ref, acc2_ref, m2_ref, l2_ref, mix_ref,
                       *, n_sel_pad, n_sel, k_top, kt):
    i = pl.program_id(2)
    qb = Q_BLOCK
    tpk = kt // LANE
    nc = cmp_ref.shape[0] // NCK
    gates = _sigmoid(gt_ref[...])
    hs = [slice(h * HD, (h + 1) * HD) for h in range(A_HPG)]
    for h in range(A_HPG):
        qa_ref[hs[h], :] = q_ref[:, hs[h]]
    qa = qa_ref[...]

    def heads(x):
        return jnp.concatenate([x] * A_HPG, axis=1)

    def gate_row(branch):
        return jnp.concatenate([gates[branch * A_HPG + h: branch * A_HPG + h + 1, :] for h in range(A_HPG)], axis=1)

    def emit(branch, o_t, first):
        z_ref = (zc_ref, zs_ref, zw_ref)[branch]
        for h in range(A_HPG):
            val = o_t[:, hs[h]] * z_ref[hs[h], :]
            if first:
                mix_ref[hs[h], :] = val
            else:
                mix_ref[hs[h], :] = mix_ref[hs[h], :] + val

    sel_state, win_state = (m_ref, l_ref, acc_ref), (m2_ref, l2_ref, acc2_ref)

    def reset(state):
        m, l, acc = state
        m[...] = jnp.full(m.shape, NEG, F32)
        l[...] = jnp.zeros(l.shape, F32)
        acc[...] = jnp.zeros(acc.shape, F32)

    def tile_update(state, k, v, bias, madd):
        m, l, acc = state
        s = _dot_nt(k, qa) + bias
        if madd is not None:
            s = s + heads(madd)
        m_old = m[...]
        m_new = jnp.maximum(m_old, jnp.max(s, axis=0, keepdims=True))
        alpha = jnp.exp2(m_old - m_new)
        p = jnp.exp2(s - m_new)
        l[...] = alpha * l[...] + jnp.sum(p, axis=0, keepdims=True)
        m[...] = m_new
        acc[...] = acc[...] * alpha + _dot_tn(v, p.astype(BF16))

    def finish(branch, state):
        m, l, acc = state
        emit(branch, acc[...] * (gate_row(branch) / l[...]), False)

    wt = 2 * LANE
    first = i // 2
    lowest = jnp.maximum((i - WINDOW // qb) // 2, 0)

    def win_body(t, carry):
        tile = first - t
        k0 = pl.multiple_of(tile * wt, wt)
        k = kw_ref[pl.ds(k0, wt), :].astype(BF16)
        v = vw_ref[pl.ds(k0, wt), :].astype(BF16)
        off = i - 2 * tile
        bias = jnp.concatenate([twin_ref[off + 1], twin_ref[off]], axis=0)
        tile_update(win_state, k, v, bias, None)
        return carry

    reset(win_state)
    win_body(0, 0)

    grp = pl.program_id(0)
    kc = _cmp_rows(cmp_ref, grp, nc).astype(BF16)
    vc = _cmp_rows(cmp_ref, A_KV + grp, nc).astype(BF16)
    cend = lax.broadcasted_iota(jnp.int32, (nc, qb), 0) * CMP_STRIDE + (CMP_LEN - 1)
    qpos_c = i * qb + lax.broadcasted_iota(jnp.int32, (nc, qb), 1)
    cvalid = heads(cend <= qpos_c)
    s = jnp.where(cvalid, _dot_nt(kc, qa) + tcmp_ref[...], NEG)
    mx = jnp.max(s, axis=0, keepdims=True)
    e = jnp.where(cvalid, jnp.exp2(s - mx), 0.0)
    p = e * (1.0 / jnp.maximum(jnp.sum(e, axis=0, keepdims=True), TINY))
    imp = p[:, hs[0]]
    for h in range(1, A_HPG):
        imp = imp + p[:, hs[h]]
    emit(0, _dot_tn(vc, p.astype(BF16)) * gate_row(0), True)

    hi = imp.astype(BF16)
    r1 = imp - hi.astype(F32)
    mid = r1.astype(BF16)
    lo = (r1 - mid.astype(F32)).astype(BF16)
    msel = msel_ref[...]
    imp_sel = _dot(msel, hi) + _dot(msel, mid) + _dot(msel, lo)
    blk = lax.broadcasted_iota(jnp.int32, (n_sel_pad, qb), 0)
    qpos = i * qb + lax.broadcasted_iota(jnp.int32, (n_sel_pad, qb), 1)
    cur = jnp.right_shift(qpos, int(math.log2(SEL_BLOCK)))
    forced = (blk == 0) | (blk == cur) | (blk == cur - 1)
    valid = blk * SEL_BLOCK <= qpos
    score = jnp.where(valid, jnp.where(forced, FORCE_SCORE, imp_sel), -1.0)
    score = jnp.where(blk < n_sel, score, -jnp.inf)
    sel = _select_topk_t(score, blk, k_top, n_sel).astype(BF16)
    kpos2 = lax.broadcasted_iota(jnp.int32, (kt, qb), 0)
    qpos2 = i * qb + lax.broadcasted_iota(jnp.int32, (kt, qb), 1)

    def sel_madd(jj):
        k0 = pl.multiple_of(jj * kt, kt)
        hit = _dot(exp_ref[pl.ds(k0, kt), :], sel)
        return jnp.where((hit > 0.5) & ((k0 + kpos2) <= qpos2), 0.0, NEG)

    reset(sel_state)

    def sel_body(jj, carry):
        k0 = pl.multiple_of(jj * kt, kt)
        k = ks_ref[pl.ds(k0, kt), :].astype(BF16)
        v = vs_ref[pl.ds(k0, kt), :].astype(BF16)
        off = i - tpk * jj
        bias = jnp.concatenate([tsel_ref[jnp.maximum(off + 1 - r, 0)] for r in range(tpk)], axis=0)
        tile_update(sel_state, k, v, bias, sel_madd(jj))
        return carry

    lax.fori_loop(0, i // tpk + 1, sel_body, 0)
    finish(1, sel_state)

    lax.fori_loop(1, first - lowest + 1, win_body, 0)
    finish(2, win_state)
    for h in range(A_HPG):
        o_ref[:, hs[h]] = mix_ref[hs[h], :].T.astype(o_ref.dtype)


def nsa_prompt_attention(qn, cmp_kv, rows, win, z_t, gates_g, tsel, twin, tcmp, msel, expand, nb, t):
    nq = t // Q_BLOCK
    qw = A_HPG * HD
    n_sel = -(-t // SEL_BLOCK)
    k_top = min(SEL_TOPK, n_sel)
    kt = SEL_KEY_TILE if t % SEL_KEY_TILE == 0 else 2 * LANE
    n_cmp = cmp_kv.shape[1] // NCK
    n_sel_pad = msel.shape[0]
    kernel = functools.partial(_nsa_prompt_kernel, n_sel_pad=n_sel_pad, n_sel=n_sel, k_top=k_top, kt=kt)
    return pl.pallas_call(
        kernel,
        grid=(A_KV, nb, nq),
        in_specs=[
            pl.BlockSpec((Q_BLOCK, qw), lambda g, b, i: (b * nq + i, g)),
            pl.BlockSpec((None, n_cmp * NCK, HD), lambda g, b, i: (b, 0, 0)),
            pl.BlockSpec((t, HD), lambda g, b, i: (b, 2 * A_KV + g)),
            pl.BlockSpec((t, HD), lambda g, b, i: (b, 3 * A_KV + g)),
            pl.BlockSpec((t, HD), lambda g, b, i: (b, g)),
            pl.BlockSpec((t, HD), lambda g, b, i: (b, A_KV + g)),
            pl.BlockSpec((qw, Q_BLOCK), lambda g, b, i: (g, b * nq + i)),
            pl.BlockSpec((qw, Q_BLOCK), lambda g, b, i: (A_KV + g, b * nq + i)),
            pl.BlockSpec((qw, Q_BLOCK), lambda g, b, i: (2 * A_KV + g, b * nq + i)),
            pl.BlockSpec((None, 3 * A_HPG, Q_BLOCK), lambda g, b, i: (g, 0, b * nq + i)),
            pl.BlockSpec((None, nq + 1, LANE, qw), lambda g, b, i: (g, 0, 0, 0)),
            pl.BlockSpec((None, twin.shape[1], LANE, qw), lambda g, b, i: (g, 0, 0, 0)),
            pl.BlockSpec((None, None, n_cmp, qw), lambda g, b, i: (g, i, 0, 0)),
            pl.BlockSpec((n_sel_pad, n_cmp), lambda g, b, i: (0, 0)),
            pl.BlockSpec((t, n_sel_pad), lambda g, b, i: (0, 0)),
        ],
        out_specs=pl.BlockSpec((Q_BLOCK, qw), lambda g, b, i: (b * nq + i, g)),
        out_shape=jax.ShapeDtypeStruct((nb * t, A_HEADS * HD), BF16),
        scratch_shapes=[
            pltpu.VMEM((qw, HD), BF16),
            pltpu.VMEM((HD, qw), F32),
            pltpu.VMEM((1, qw), F32),
            pltpu.VMEM((1, qw), F32),
            pltpu.VMEM((HD, qw), F32),
            pltpu.VMEM((1, qw), F32),
            pltpu.VMEM((1, qw), F32),
            pltpu.VMEM((qw, Q_BLOCK), F32),
        ],
        compiler_params=_cparams(("arbitrary", "arbitrary", "arbitrary")),
        name="nsa_prompt_attention",
    )(qn, cmp_kv, rows, rows, win, win, z_t, z_t, z_t, gates_g, tsel, twin, tcmp, msel, expand)


def _diag_blocks(o_full, rows_per_group):
    return jnp.concatenate(
        [o_full[g * rows_per_group:(g + 1) * rows_per_group, g * HD:(g + 1) * HD] for g in range(A_KV)], axis=0)


def _nsa_sample_cmp_kernel(q_ref, cmp_ref, bias_ref, msel_ref, oc_ref, sel_ref,
                           *, past_len, n_sel, k_top, n_q, blocks_per_step):
    n_cmp = cmp_ref.shape[0] // NCK
    kc = jnp.concatenate([_cmp_rows(cmp_ref, g, n_cmp) for g in range(A_KV)], axis=1).astype(BF16)
    vc = jnp.concatenate([_cmp_rows(cmp_ref, A_KV + g, n_cmp) for g in range(A_KV)], axis=1).astype(BF16)
    rows = q_ref.shape[0]
    rpg = A_HPG * n_q
    q = q_ref[...]
    s = _dot_nt(q, kc) + bias_ref[...]
    rowi = lax.broadcasted_iota(jnp.int32, (rows, n_cmp), 0)
    ci = lax.broadcasted_iota(jnp.int32, (rows, n_cmp), 1)
    qpos = past_len + (rowi & (n_q - 1))
    valid = (ci * CMP_STRIDE + (CMP_LEN - 1)) <= qpos
    s = jnp.where(valid, s, NEG)
    mx = jnp.max(s, axis=-1, keepdims=True)
    e = jnp.where(valid, jnp.exp2(s - mx), 0.0)
    p = e / jnp.maximum(jnp.sum(e, axis=-1, keepdims=True), TINY)
    oc_ref[...] = _diag_blocks(_dot(p.astype(BF16), vc), rpg)
    imps = []
    for g in range(A_KV):
        acc = p[g * rpg: g * rpg + n_q]
        for h in range(1, A_HPG):
            acc = acc + p[g * rpg + h * n_q: g * rpg + (h + 1) * n_q]
        imps.append(acc)
    imp = jnp.concatenate(imps, axis=0)
    imp_sel = _split3_dot(imp, msel_ref[...])
    shp = imp_sel.shape
    blk = lax.broadcasted_iota(jnp.int32, shp, 1)
    qp = past_len + (lax.broadcasted_iota(jnp.int32, shp, 0) & (n_q - 1))
    cur = jnp.right_shift(qp, int(math.log2(SEL_BLOCK)))
    forced = (blk == 0) | (blk == cur) | (blk == cur - 1)
    ok = blk * SEL_BLOCK <= qp
    score = jnp.where(ok, jnp.where(forced, FORCE_SCORE, imp_sel), -1.0)
    score = jnp.where(blk < n_sel, score, -jnp.inf)
    sel = _select_topk(score, blk, k_top)
    for st in range(sel_ref.shape[0]):
        sel_ref[st] = sel[:, st * blocks_per_step:(st + 1) * blocks_per_step]


def _nsa_sample_sel_kernel(pt_ref, *refs, n_pages, n_steps, past_len, n_q):
    del pt_ref
    pages = refs[:n_pages]
    q_ref, tail_ref, bias_ref, sel_ref, exp_ref, o_ref, m_ref, l_ref, acc_ref = refs[n_pages:]
    s_id = pl.program_id(1)
    kvw = A_KV * HD
    rows = q_ref.shape[0]
    rpg = A_HPG * n_q

    @pl.when(s_id == 0)
    def _():
        m_ref[...] = jnp.full(m_ref.shape, NEG, F32)
        l_ref[...] = jnp.zeros(l_ref.shape, F32)
        acc_ref[...] = jnp.zeros(acc_ref.shape, F32)

    def update(k, v, nk, key0):
        hit = _dot(sel_ref[...].astype(BF16), exp_ref[:, :nk])
        kpos = key0 + lax.broadcasted_iota(jnp.int32, hit.shape, 1)
        qpos = past_len + (lax.broadcasted_iota(jnp.int32, hit.shape, 0) & (n_q - 1))
        madd = jnp.where((hit > 0.5) & (kpos <= qpos), 0.0, NEG)
        madd = jnp.concatenate(
            [madd[g * n_q:(g + 1) * n_q] for g in range(A_KV) for _ in range(A_HPG)], axis=0)
        s = _dot_nt(q_ref[...], k) + bias_ref[:, :nk] + madd
        m_old = m_ref[...]
        m_new = jnp.maximum(m_old, jnp.max(s, axis=-1, keepdims=True))
        alpha = jnp.exp2(m_old - m_new)
        p = jnp.exp2(s - m_new)
        l_ref[...] = alpha * l_ref[...] + jnp.sum(p, axis=-1, keepdims=True)
        acc_ref[...] = alpha * acc_ref[...] + _dot(p.astype(BF16), v)
        m_ref[...] = m_new

    @pl.when(s_id < n_steps)
    def _():
        by_slot = [jnp.swapaxes(pg[...], 0, 1) for pg in pages]

        def gather(c0):
            cols = [jnp.concatenate([x[c0 + g] for x in by_slot], axis=0) for g in range(A_KV)]
            return jnp.concatenate(cols, axis=1).astype(BF16)

        update(gather(0), gather(A_KV), n_pages * PAGE, s_id * (n_pages * PAGE))

    @pl.when(s_id == n_steps)
    def _():
        update(tail_ref[:, :kvw].astype(BF16), tail_ref[:, kvw:].astype(BF16), PAGE, past_len)
        o_ref[...] = _diag_blocks(acc_ref[...] / l_ref[...], rpg)


def _nsa_sample_mix_kernel(q_ref, win_ref, bias_ref, oc_ref, os_ref, z_ref, gt_ref, o_ref, *, n_q):
    kvw = A_KV * HD
    rows = q_ref.shape[0]
    rpg = A_HPG * n_q
    nk = win_ref.shape[0]
    s = _dot_nt(q_ref[...], win_ref[:, :kvw].astype(BF16)) + bias_ref[...]
    qi = lax.broadcasted_iota(jnp.int32, (rows, nk), 0) & (n_q - 1)
    ki = lax.broadcasted_iota(jnp.int32, (rows, nk), 1)
    dist = qi + WINDOW - ki
    s = s + jnp.where((dist >= 0) & (dist < WINDOW), 0.0, NEG)
    mx = jnp.max(s, axis=-1, keepdims=True)
    e = jnp.exp2(s - mx)
    p = e / jnp.sum(e, axis=-1, keepdims=True)
    ow = _diag_blocks(_dot(p.astype(BF16), win_ref[:, kvw:].astype(BF16)), rpg)
    gates = _sigmoid(gt_ref[...])
    width = A_HEADS * HD
    for hh in range(A_HEADS):
        r = slice(hh * n_q, (hh + 1) * n_q)
        c = slice(hh * HD, (hh + 1) * HD)
        val = None
        for br, o in enumerate((oc_ref, os_ref, ow)):
            zc = slice(br * width + hh * HD, br * width + (hh + 1) * HD)
            gcol = gates[:, br * A_HEADS + hh: br * A_HEADS + hh + 1]
            term = gcol * o[r, :] * z_ref[:, zc]
            val = term if val is None else val + term
        o_ref[:, c] = val.astype(o_ref.dtype)


def nsa_sample_attention(qbd, cmp_kv, cache4d, page_table, tail_sel, win_all, z_s, gl_s,
                         b_cmp, b_sel, b_win, msel, expand, past_len, n_q):
    nb, rows, kvw = qbd.shape
    n_cmp = cmp_kv.shape[1] // NCK
    n_tab = page_table.shape[1]
    n_pages = min(PAGES_PER_STEP, n_tab)
    n_steps = n_tab // n_pages
    bps = n_pages * PAGE // SEL_BLOCK
    n_sel = -(-(past_len + n_q) // SEL_BLOCK)
    k_top = min(SEL_TOPK, n_sel)
    gq = A_KV * n_q

    oc, selmask = pl.pallas_call(
        functools.partial(_nsa_sample_cmp_kernel, past_len=past_len, n_sel=n_sel, k_top=k_top,
                          n_q=n_q, blocks_per_step=bps),
        grid=(nb,),
        in_specs=[
            pl.BlockSpec((None, rows, kvw), lambda b: (b, 0, 0)),
            pl.BlockSpec((None, n_cmp * NCK, HD), lambda b: (b, 0, 0)),
            pl.BlockSpec(b_cmp.shape, lambda b: (0, 0)),
            pl.BlockSpec(msel.shape, lambda b: (0, 0)),
        ],
        out_specs=[
            pl.BlockSpec((None, rows, HD), lambda b: (b, 0, 0)),
            pl.BlockSpec((None, n_steps + 1, gq, bps), lambda b: (b, 0, 0, 0)),
        ],
        out_shape=[
            jax.ShapeDtypeStruct((nb, rows, HD), F32),
            jax.ShapeDtypeStruct((nb, n_steps + 1, gq, bps), F32),
        ],
        compiler_params=_cparams(("arbitrary",)),
        name="nsa_sample_cmp",
    )(qbd, cmp_kv, b_cmp, msel)

    half = 2 * kvw
    last = n_steps - 1

    def page_spec(k):
        return pl.BlockSpec(
            (None, PAGE, NCK, HD),
            lambda b, s, pt: (pt[b, jnp.minimum(s, last) * n_pages + k], 0, 1, 0))

    nkeys = n_pages * PAGE
    grid_spec = pltpu.PrefetchScalarGridSpec(
        num_scalar_prefetch=1,
        grid=(nb, n_steps + 1),
        in_specs=[page_spec(k) for k in range(n_pages)] + [
            pl.BlockSpec((None, rows, kvw), lambda b, s, pt: (b, 0, 0)),
            pl.BlockSpec((None, PAGE, half), lambda b, s, pt: (b, 0, 0)),
            pl.BlockSpec((rows, nkeys), lambda b, s, pt: (0, s)),
            pl.BlockSpec((None, None, gq, bps), lambda b, s, pt: (b, s, 0, 0)),
            pl.BlockSpec(expand.shape, lambda b, s, pt: (0, 0)),
        ],
        out_specs=pl.BlockSpec((None, rows, HD), lambda b, s, pt: (b, 0, 0)),
        scratch_shapes=[
            pltpu.VMEM((rows, 1), F32),
            pltpu.VMEM((rows, 1), F32),
            pltpu.VMEM((rows, kvw), F32),
        ],
    )
    osel = pl.pallas_call(
        functools.partial(_nsa_sample_sel_kernel, n_pages=n_pages, n_steps=n_steps,
                          past_len=past_len, n_q=n_q),
        grid_spec=grid_spec,
        out_shape=jax.ShapeDtypeStruct((nb, rows, HD), F32),
        compiler_params=_cparams(("arbitrary", "arbitrary")),
        name="nsa_sample_sel",
    )(page_table, *([cache4d] * n_pages), qbd, tail_sel, b_sel, selmask, expand)

    nkw = win_all.shape[1]
    zw = z_s.shape[1]
    return pl.pallas_call(
        functools.partial(_nsa_sample_mix_kernel, n_q=n_q),
        grid=(nb,),
        in_specs=[
            pl.BlockSpec((None, rows, kvw), lambda b: (b, 0, 0)),
            pl.BlockSpec((None, nkw, 2 * kvw), lambda b: (b, 0, 0)),
            pl.BlockSpec(b_win.shape, lambda b: (0, 0)),
            pl.BlockSpec((None, rows, HD), lambda b: (b, 0, 0)),
            pl.BlockSpec((None, rows, HD), lambda b: (b, 0, 0)),
            pl.BlockSpec((n_q, zw), lambda b: (b, 0)),
            pl.BlockSpec((n_q, 3 * A_HEADS), lambda b: (b, 0)),
        ],
        out_specs=pl.BlockSpec((n_q, A_HEADS * HD), lambda b: (b, 0)),
        out_shape=jax.ShapeDtypeStruct((nb * n_q, A_HEADS * HD), F32),
        compiler_params=_cparams(("arbitrary",)),
        name="nsa_sample_mix",
    )(qbd, win_all, b_win, oc, osel, z_s, gl_s)


def _retention_kernel(q_ref, k_ref, v_ref, g_ref, cos_ref, sin_ref, dm_ref, qd_ref, kd_ref, cd_ref,
                      s0_ref, o_ref, s_ref, *, dk, dv, hps):
    c = pl.program_id(2)
    half = dk // 2

    @pl.when(c == 0)
    def _():
        s_ref[...] = s0_ref[...]

    cos = cos_ref[...]
    sin = sin_ref[...]

    def rot(x):
        x1, x2 = x[:, :half], x[:, half:]
        return jnp.concatenate([x1 * cos - x2 * sin, x1 * sin + x2 * cos], axis=1)

    for j in range(hps):
        ks, vs = slice(j * dk, (j + 1) * dk), slice(j * dv, (j + 1) * dv)
        q = rot(q_ref[:, ks])
        k = rot(k_ref[:, ks]) * (dk ** -0.5)
        v = v_ref[:, vs].astype(BF16)
        qb = q.astype(BF16)
        att = _dot_nt(qb, k.astype(BF16)) * dm_ref[j]
        state = s_ref[j]
        o = _dot(att.astype(BF16), v) + _dot(qb, state.astype(BF16)) * qd_ref[j]
        s_ref[j] = state * cd_ref[j] + _dot_tn((k * kd_ref[j]).astype(BF16), v)
        o = o * lax.rsqrt(jnp.mean(o * o, axis=-1, keepdims=True) + EPS)
        o_ref[:, vs] = (o * _silu(g_ref[:, vs])).astype(o_ref.dtype)


def retention(proj, state0, q0, nb, t):
    _, nh, dk, dv = state0.shape
    cs = R_CHUNK if t % R_CHUNK == 0 else t
    n = t // cs
    half = dk // 2
    pos = (q0 + jnp.arange(t)).astype(F32)
    inv = jnp.power(ROPE_BASE, -jnp.arange(half, dtype=F32) / half)
    ang = pos[:, None] * inv[None, :]
    cos, sin = jnp.cos(ang), jnp.sin(ang)
    log_g = jnp.log1p(-jnp.exp2(-5.0 - jnp.arange(nh, dtype=F32)))
    ii = jnp.arange(cs, dtype=F32)
    rel = ii[:, None] - ii[None, :]
    causal = rel >= 0
    dmat = jnp.where(causal, jnp.exp(jnp.where(causal, rel, 0.0) * log_g[:, None, None]), 0.0)
    q_dec = jnp.exp((ii + 1.0) * log_g[:, None])[:, :, None]
    k_dec = jnp.exp((cs - 1.0 - ii) * log_g[:, None])[:, :, None]
    c_dec = jnp.exp(cs * log_g)[:, None, None]
    hps = RET_HEADS_PER_STEP
    assert nh % hps == 0
    ng = nh // hps
    vb0 = 2 * nh * dk // (hps * dv)
    return pl.pallas_call(
        functools.partial(_retention_kernel, dk=dk, dv=dv, hps=hps),
        grid=(nb, ng, n),
        in_specs=[
            pl.BlockSpec((cs, hps * dk), lambda b, h, c: (b * n + c, h)),
            pl.BlockSpec((cs, hps * dk), lambda b, h, c: (b * n + c, ng + h)),
            pl.BlockSpec((cs, hps * dv), lambda b, h, c: (b * n + c, vb0 + h)),
            pl.BlockSpec((cs, hps * dv), lambda b, h, c: (b * n + c, vb0 + ng + h)),
            pl.BlockSpec((cs, half), lambda b, h, c: (c, 0)),
            pl.BlockSpec((cs, half), lambda b, h, c: (c, 0)),
            pl.BlockSpec((hps, cs, cs), lambda b, h, c: (h, 0, 0)),
            pl.BlockSpec((hps, cs, 1), lambda b, h, c: (h, 0, 0)),
            pl.BlockSpec((hps, cs, 1), lambda b, h, c: (h, 0, 0)),
            pl.BlockSpec((hps, 1, 1), lambda b, h, c: (h, 0, 0)),
            pl.BlockSpec((None, hps, dk, dv), lambda b, h, c: (b, h, 0, 0)),
        ],
        out_specs=[
            pl.BlockSpec((cs, hps * dv), lambda b, h, c: (b * n + c, h)),
            pl.BlockSpec((None, hps, dk, dv), lambda b, h, c: (b, h, 0, 0)),
        ],
        out_shape=[
            jax.ShapeDtypeStruct((nb * t, nh * dv), BF16 if cs % 16 == 0 else F32),
            jax.ShapeDtypeStruct(state0.shape, F32),
        ],
        compiler_params=_cparams(("arbitrary", "arbitrary", "arbitrary")),
        name="retention",
    )(proj, proj, proj, proj, cos, sin, dmat, q_dec, k_dec, c_dec, state0)


def _t5_bucket_np(dist):
    n = np.maximum(dist, 0)
    exact = N_BUCKETS // 2
    logv = np.log(np.maximum(n, 1).astype(np.float32) / np.float32(exact)) / np.float32(
        math.log(MAX_DISTANCE / exact))
    large = np.minimum(exact + (logv * np.float32(N_BUCKETS - exact)).astype(np.int32), N_BUCKETS - 1)
    return np.where(n < exact, n, large).astype(np.int32)


def _bucket_starts():
    bk = _t5_bucket_np(np.arange(4 * MAX_DISTANCE))
    assert np.all(np.diff(bk) >= 0) and bk[-1] == N_BUCKETS - 1
    return [int(np.argmax(bk >= b)) for b in range(N_BUCKETS)]


_BUCKET_START = _bucket_starts()


def _sel_matrix(n_cmp, n_cmp_pad, n_sel_pad):
    c = np.arange(n_cmp_pad)[:, None]
    j = np.arange(n_sel_pad)[None, :]
    a = (c >= SEL_RATIO * j) & (c < SEL_RATIO * j + SEL_RATIO)
    b = (c >= SEL_RATIO * j - 1) & (c < SEL_RATIO * j + SEL_RATIO - 1)
    m = (a.astype(np.float32) + b.astype(np.float32)) * (c < n_cmp)
    return jnp.asarray(m, BF16)


def _expand_matrix(n_blk_pad, n_keys):
    blk = np.arange(n_blk_pad)[:, None]
    key = np.arange(n_keys)[None, :]
    return jnp.asarray((key // SEL_BLOCK == blk).astype(np.float32), BF16)


def _w1_pairs(w1):
    w = w1.reshape(2, 2, CMP_STRIDE // 2, 2, HD, HD)
    w = jnp.transpose(w, (2, 3, 4, 0, 1, 5))
    return w.reshape(CMP_STRIDE // 2, 2 * HD, 4 * HD).astype(BF16)


def kernel(x_prompt, x_sample, c_prompt, c_sample, cache_nsa_kv, cache_nsa_win, state_ret, page_table,
           norm_g, ada_w, ada_b, rel_bias, a_w_in, a_w_out, a_cmp_pe, a_cmp_w1, a_cmp_w2, a_qk_g,
           r_w_in, r_w_out):
    nbp, t, d = x_prompt.shape
    nbs, ts, _ = x_sample.shape
    n_tab = page_table.shape[1]
    past_len = n_tab * PAGE
    kvw = A_KV * HD
    qw = A_HEADS * HD
    assert t % (2 * LANE) == 0 and t >= WINDOW and ts == 8 and past_len >= WINDOW
    assert cache_nsa_win.shape[2] == WINDOW

    n_c = nbp + nbs
    pad_c = (-n_c) % 8
    c_all = jnp.concatenate([c_prompt, c_sample, jnp.zeros((pad_c, d), F32)], axis=0)
    mod = ada_modulation(c_all, ada_w, ada_b).reshape(ada_w.shape[0], n_c + pad_c, 3, d)

    def mods(layer):
        m = mod[layer]
        return (m[:nbp, 0], m[:nbp, 1], m[:nbp, 2]), (m[nbp:n_c, 0], m[nbp:n_c, 1], m[nbp:n_c, 2])

    (sh_p, sc_p, gt_p), (sh_s, sc_s, gt_s) = mods(0)
    hp = norm_modulate(x_prompt, norm_g[0], sc_p, sh_p)
    hs = norm_modulate(x_sample, norm_g[0], sc_s, sh_s)
    w_in_t = jnp.swapaxes(a_w_in[0], 0, 1)
    n_main = w_in_t.shape[0] - 3 * A_HEADS
    n_qkv = qw + 6 * kvw
    proj_p, proj_s = project(hp, hs, w_in_t, n_cols=n_qkv, w_is_nk=True)
    zt_p, z_s = project_gate_t(hp, hs, w_in_t, n_qkv, n_main - n_qkv)
    w_gate = jnp.pad(w_in_t[n_main:], ((0, LANE - 3 * A_HEADS), (0, 0)))
    gl_p, gl_s = project(hp, hs, w_gate, w_is_nk=True)
    gl_p, gl_s = gl_p[:, :3 * A_HEADS], gl_s[:, :3 * A_HEADS]
    qk_g = a_qk_g[0]

    qn_p, rows_p, win_p, rows_t_p, win_t_p = nsa_prep(proj_p, qk_g)
    qn_s, rows_s, win_s, rows_t_s, _ = nsa_prep(proj_s, qk_g)

    w_pairs = _w1_pairs(a_cmp_w1[0])
    pe = a_cmp_pe[0]
    w1 = a_cmp_w1[0].reshape(2, CMP_LEN, HD, HD)
    w2 = a_cmp_w2[0]
    cpp = PAGE // CMP_STRIDE

    pages_p = t // PAGE
    ident = jnp.arange(nbp * pages_p, dtype=jnp.int32).reshape(nbp, pages_p)
    ab_p = compress_stage_a(rows_t_p.reshape(nbp * pages_p, PAGE, 4 * A_KV, HD), ident, w_pairs,
                            min(PAGES_PER_STEP, pages_p))
    zero_tail = jnp.zeros((nbp, cpp, NCK, 2 * HD), F32)
    cmp_p = compress_stage_b(ab_p, zero_tail, pe, w1, w2, qk_g)

    nq = t // Q_BLOCK
    n_ch_p = t // CMP_STRIDE
    tsel, twin, tcmp = prompt_bias_tables(rel_bias, nq, n_ch_p)
    n_cmp_p = t // CMP_STRIDE - 1
    n_sel_p = -(-t // SEL_BLOCK)
    n_sel_pad_p = -(-n_sel_p // 16) * 16
    msel_p = _sel_matrix(n_cmp_p, n_ch_p, n_sel_pad_p).T
    expand_p = _expand_matrix(n_sel_pad_p, t).T
    gates_g = jnp.transpose(gl_p.reshape(nbp * t, 3, A_KV, A_HPG), (2, 1, 3, 0)).reshape(A_KV, 3 * A_HPG, nbp * t)
    mixed_p = nsa_prompt_attention(qn_p, cmp_p, rows_p, win_p, zt_p, gates_g,
                                   tsel, twin, tcmp, msel_p, expand_p, nbp, t)

    n_pool = cache_nsa_kv.shape[1]
    cache4d = cache_nsa_kv[0].reshape(n_pool, PAGE, 4 * A_KV, HD)
    n_pages = min(PAGES_PER_STEP, n_tab)
    ab_s = compress_stage_a(cache4d, page_table, w_pairs, n_pages)
    tail_rows = jnp.pad(rows_s.reshape(nbs, ts, 4 * kvw), ((0, 0), (0, PAGE - ts), (0, 0)))
    ident_s = jnp.arange(nbs, dtype=jnp.int32).reshape(nbs, 1)
    ab_tail = compress_stage_a(tail_rows.reshape(nbs, PAGE, 4 * A_KV, HD), ident_s, w_pairs, 1)
    cmp_s = compress_stage_b(ab_s, ab_tail, pe, w1, w2, qk_g)

    n_cmp_s = cmp_s.shape[1] // NCK
    n_steps = n_tab // n_pages
    bps = n_pages * PAGE // SEL_BLOCK
    n_sel_pad = (n_steps + 1) * bps
    nkeys = (n_steps + 1) * n_pages * PAGE
    nkw = WINDOW + LANE
    b_cmp, b_sel, b_win = sample_bias_rows(rel_bias, past_len, ts, n_cmp_s, nkeys, nkw)
    msel_s = _sel_matrix(n_cmp_s, n_cmp_s, n_sel_pad)
    expand_s = _expand_matrix(bps, n_pages * PAGE)
    q5 = qn_s.reshape(nbs, ts, A_KV, A_HPG, HD)
    eye = jnp.eye(A_KV, dtype=BF16)
    qbd = jnp.einsum('bqghd,ge->bghqed', q5, eye).reshape(nbs, A_HEADS * ts, kvw)
    win_new = win_s.reshape(nbs, ts, 2 * kvw)
    win_cache = cache_nsa_win[0].reshape(nbs, WINDOW, 2 * kvw)
    win_all = jnp.concatenate([win_cache, win_new], axis=1)
    win_pad = jnp.pad(win_all, ((0, 0), (0, nkw - WINDOW - ts), (0, 0)))
    tail_sel = tail_rows[:, :, 2 * kvw:]
    mixed_s = nsa_sample_attention(qbd, cmp_s, cache4d, page_table, tail_sel, win_pad, z_s, gl_s,
                                   b_cmp, b_sel, b_win, msel_s, expand_s, past_len, ts).astype(BF16)

    xs_flat = x_sample.reshape(nbs * ts, d)
    gate_s_rows = jnp.repeat(gt_s, ts, axis=0)
    x1p, x1s = project_residual(mixed_p, mixed_s, a_w_out[0], x_prompt.reshape(nbp * t, d), gt_p,
                                xs_flat, gate_s_rows, t)

    (sh_p, sc_p, gt_p), (sh_s, sc_s, gt_s) = mods(1)
    hp = norm_modulate(x1p.reshape(nbp, t, d), norm_g[1], sc_p, sh_p)
    hs = norm_modulate(x1s.reshape(nbs, ts, d), norm_g[1], sc_s, sh_s)
    rp, rs = project(hp, hs, r_w_in[0])
    s0 = jnp.zeros((nbp,) + state_ret.shape[2:], F32)
    op, ret_p = retention(rp, s0, 0, nbp, t)
    os_, ret_s = retention(rs, state_ret[0], past_len, nbs, ts)
    os_ = os_.astype(BF16)
    gate_s_rows = jnp.repeat(gt_s, ts, axis=0)
    x2p, x2s = project_residual(op, os_, r_w_out[0], x1p, gt_p, x1s, gate_s_rows, t)

    kv_p = rows_t_p.reshape(1, nbp, t, 4, A_KV, HD)
    kv_s = rows_t_s.reshape(1, nbs, ts, 4, A_KV, HD)
    wst_p = win_t_p.reshape(nbp, t, 2, A_KV, HD)[None, :, t - WINDOW:]
    wst_s = win_all[:, ts:].reshape(1, nbs, WINDOW, 2, A_KV, HD)
    return (x2p.reshape(nbp, t, d), x2s.reshape(nbs, ts, d), kv_p, kv_s, wst_p, wst_s,
            ret_p[None], ret_s[None])
```

```python
import functools
import math

import numpy as np
import jax
import jax.numpy as jnp
from jax import lax
from jax.experimental import pallas as pl
from jax.experimental.pallas import tpu as pltpu

EPS = 1e-6
HD = 128
A_KV = 4
A_HPG = 8
A_HEADS = A_KV * A_HPG
CMP_STRIDE = 16
CMP_LEN = 32
SEL_BLOCK = 64
SEL_RATIO = SEL_BLOCK // CMP_STRIDE
SEL_TOPK = 16
WINDOW = 512
Q_BLOCK = 128
FORCE_SCORE = 1e4
NEG = -1e30
TINY = 1e-30
N_BUCKETS = 32
MAX_DISTANCE = 1024
R_CHUNK = 128
ROPE_BASE = 10000.0
LOG2E = math.log2(math.e)
Q_SCALE = HD ** -0.5 * LOG2E
PAGE = 128
PAGES_PER_STEP = 16
RET_HEADS_PER_STEP = 8
SEL_KEY_TILE = 512
ROW_TILE = 256
ADA_COL_TILE = 512

LANE = 128
VMEM_LIMIT = 60 * 1024 * 1024

F32 = jnp.float32
BF16 = jnp.bfloat16


def _cparams(sem):
    return pltpu.CompilerParams(dimension_semantics=sem, vmem_limit_bytes=VMEM_LIMIT)


def _dot(a, b):
    return jnp.dot(a, b, preferred_element_type=F32)


def _dot_nt(a, b):
    return lax.dot_general(a, b, (((1,), (1,)), ((), ())), preferred_element_type=F32)


def _dot_tn(a, b):
    return lax.dot_general(a, b, (((0,), (0,)), ((), ())), preferred_element_type=F32)


def _silu(x):
    h = 0.5 * x
    return h + h * jnp.tanh(h)


def _sigmoid(x):
    return 0.5 + 0.5 * jnp.tanh(0.5 * x)


def _split3_dot(x, m_bf16):
    hi = x.astype(BF16)
    r1 = x - hi.astype(F32)
    mid = r1.astype(BF16)
    lo = (r1 - mid.astype(F32)).astype(BF16)
    return _dot(hi, m_bf16) + _dot(mid, m_bf16) + _dot(lo, m_bf16)


def _ada_kernel(c_ref, w_ref, b_ref, o_ref):
    a = _silu(c_ref[...]).astype(BF16)
    o_ref[...] = _dot(a, w_ref[...].astype(BF16)) + b_ref[...]


def ada_modulation(c_all, ada_w, ada_b):
    depth, d, n = ada_w.shape
    rows = c_all.shape[0]
    tn = ADA_COL_TILE
    return pl.pallas_call(
        _ada_kernel,
        grid=(depth, n // tn),
        in_specs=[
            pl.BlockSpec((rows, d), lambda l, j: (0, 0)),
            pl.BlockSpec((None, d, tn), lambda l, j: (l, 0, j)),
            pl.BlockSpec((None, 1, tn), lambda l, j: (l, 0, j)),
        ],
        out_specs=pl.BlockSpec((None, rows, tn), lambda l, j: (l, 0, j)),
        out_shape=jax.ShapeDtypeStruct((depth, rows, n), F32),
        compiler_params=_cparams(("arbitrary", "arbitrary")),
        name="ada_modulation",
    )(c_all, ada_w, ada_b.reshape(depth, 1, n))


def _norm_mod_kernel(x_ref, g_ref, sc_ref, sh_ref, o_ref):
    x = x_ref[...]
    y = x * lax.rsqrt(jnp.mean(x * x, axis=-1, keepdims=True) + EPS)
    y = y * g_ref[...]
    o_ref[...] = (y * (1.0 + sc_ref[...]) + sh_ref[...]).astype(o_ref.dtype)


def norm_modulate(x, g, scale, shift):
    b, t, d = x.shape
    tt = min(t, ROW_TILE)
    out = pl.pallas_call(
        _norm_mod_kernel,
        grid=(b, t // tt),
        in_specs=[
            pl.BlockSpec((None, tt, d), lambda i, j: (i, j, 0)),
            pl.BlockSpec((1, d), lambda i, j: (0, 0)),
            pl.BlockSpec((None, 1, d), lambda i, j: (i, 0, 0)),
            pl.BlockSpec((None, 1, d), lambda i, j: (i, 0, 0)),
        ],
        out_specs=pl.BlockSpec((None, tt, d), lambda i, j: (i, j, 0)),
        out_shape=jax.ShapeDtypeStruct((b, t, d), BF16),
        compiler_params=_cparams(("arbitrary", "arbitrary")),
        name="norm_modulate",
    )(x, g.reshape(1, d), scale.reshape(b, 1, d), shift.reshape(b, 1, d))
    return out.reshape(b * t, d)


def _stage_weight_tile(w_hbm, wf_ref, wb_ref, sem, *, w_is_nk, tn, j0):
    j = pl.program_id(0)

    def copy(jj):
        start = pl.multiple_of((j0 + jj) * tn, tn)
        src = w_hbm.at[pl.ds(start, tn), :] if w_is_nk else w_hbm.at[:, pl.ds(start, tn)]
        return pltpu.make_async_copy(src, wf_ref, sem)

    @pl.when(j == 0)
    def _():
        copy(0).start()

    copy(j).wait()
    wb_ref[...] = wf_ref[...].astype(BF16)

    @pl.when(j + 1 < pl.num_programs(0))
    def _():
        copy(j + 1).start()


def _proj_kernel(xp_ref, xs_ref, w_hbm, op_ref, os_ref, wf_ref, wb_ref, sem, *, stage, mm):
    @pl.when(pl.program_id(1) == 0)
    def _():
        stage(w_hbm, wf_ref, wb_ref, sem)
        os_ref[...] = mm(xs_ref[...], wb_ref[...])

    op_ref[...] = mm(xp_ref[...], wb_ref[...])


def _proj_gate_t_kernel(xp_ref, xs_ref, w_hbm, opt_ref, os_ref, wf_ref, wb_ref, sem, *, stage):
    @pl.when(pl.program_id(1) == 0)
    def _():
        stage(w_hbm, wf_ref, wb_ref, sem)
        os_ref[...] = _silu(_dot_nt(xs_ref[...], wb_ref[...]))

    opt_ref[...] = _silu(_dot_nt(wb_ref[...], xp_ref[...]))


def _weight_scratch(shape):
    return [pltpu.VMEM(shape, F32), pltpu.VMEM(shape, BF16), pltpu.SemaphoreType.DMA(())]


def project_gate_t(xp, xs, w_nk, row0, n_rows):
    mp, k = xp.shape
    ms = xs.shape[0]
    tm, tn = _proj_tiles(mp, k, n_rows)
    assert row0 % tn == 0
    stage = functools.partial(_stage_weight_tile, w_is_nk=True, tn=tn, j0=row0 // tn)
    return pl.pallas_call(
        functools.partial(_proj_gate_t_kernel, stage=stage),
        grid=(n_rows // tn, mp // tm),
        in_specs=[
            pl.BlockSpec((tm, k), lambda j, i: (i, 0)),
            pl.BlockSpec((ms, k), lambda j, i: (0, 0)),
            pl.BlockSpec(memory_space=pl.ANY),
        ],
        out_specs=[
            pl.BlockSpec((tn, tm), lambda j, i: (j, i)),
            pl.BlockSpec((ms, tn), lambda j, i: (0, j)),
        ],
        out_shape=[jax.ShapeDtypeStruct((n_rows, mp), F32), jax.ShapeDtypeStruct((ms, n_rows), F32)],
        scratch_shapes=_weight_scratch((tn, k)),
        compiler_params=_cparams(("arbitrary", "arbitrary")),
        name="project_gate_t",
    )(xp, xs, w_nk)


def _proj_res_kernel(xp_ref, xs_ref, w_hbm, rp_ref, gp_ref, rs_ref, gs_ref, op_ref, os_ref,
                     wf_ref, wb_ref, sem, *, stage):
    @pl.when(pl.program_id(1) == 0)
    def _():
        stage(w_hbm, wf_ref, wb_ref, sem)
        os_ref[...] = rs_ref[...] + gs_ref[...] * _dot(xs_ref[...], wb_ref[...])

    op_ref[...] = rp_ref[...] + gp_ref[...] * _dot(xp_ref[...], wb_ref[...])


def _proj_tiles(mp, k, n, residual=False):
    if k <= 4096:
        tm, tn = (1024, 512) if residual else (1024, 1024)
    else:
        tm, tn = 512, 512
    tm = min(tm, mp)
    while mp % tm:
        tm //= 2
    tn = min(tn, n)
    while n % tn:
        tn //= 2
    return tm, tn


def project(xp, xs, w, n_cols=None, w_is_nk=False):
    mp, k = xp.shape
    ms = xs.shape[0]
    n = w.shape[0 if w_is_nk else 1] if n_cols is None else n_cols
    tm, tn = _proj_tiles(mp, k, n)
    stage = functools.partial(_stage_weight_tile, w_is_nk=w_is_nk, tn=tn, j0=0)
    return pl.pallas_call(
        functools.partial(_proj_kernel, stage=stage, mm=_dot_nt if w_is_nk else _dot),
        grid=(n // tn, mp // tm),
        in_specs=[
            pl.BlockSpec((tm, k), lambda j, i: (i, 0)),
            pl.BlockSpec((ms, k), lambda j, i: (0, 0)),
            pl.BlockSpec(memory_space=pl.ANY),
        ],
        out_specs=[
            pl.BlockSpec((tm, tn), lambda j, i: (i, j)),
            pl.BlockSpec((ms, tn), lambda j, i: (0, j)),
        ],
        out_shape=[jax.ShapeDtypeStruct((mp, n), F32), jax.ShapeDtypeStruct((ms, n), F32)],
        scratch_shapes=_weight_scratch((tn, k) if w_is_nk else (k, tn)),
        compiler_params=_cparams(("arbitrary", "arbitrary")),
        name="project",
    )(xp, xs, w)


def project_residual(xp, xs, w, res_p, gate_p, res_s, gate_s, rows_per_batch):
    mp, k = xp.shape
    ms = xs.shape[0]
    n = w.shape[1]
    tm, tn = _proj_tiles(mp, k, n, residual=True)
    tm = min(tm, rows_per_batch)
    tpb = rows_per_batch // tm
    nb = gate_p.shape[0]
    stage = functools.partial(_stage_weight_tile, w_is_nk=False, tn=tn, j0=0)
    return pl.pallas_call(
        functools.partial(_proj_res_kernel, stage=stage),
        grid=(n // tn, mp // tm),
        in_specs=[
            pl.BlockSpec((tm, k), lambda j, i: (i, 0)),
            pl.BlockSpec((ms, k), lambda j, i: (0, 0)),
            pl.BlockSpec(memory_space=pl.ANY),
            pl.BlockSpec((tm, tn), lambda j, i: (i, j)),
            pl.BlockSpec((None, 1, tn), lambda j, i: (i // tpb, 0, j)),
            pl.BlockSpec((ms, tn), lambda j, i: (0, j)),
            pl.BlockSpec((ms, tn), lambda j, i: (0, j)),
        ],
        out_specs=[
            pl.BlockSpec((tm, tn), lambda j, i: (i, j)),
            pl.BlockSpec((ms, tn), lambda j, i: (0, j)),
        ],
        out_shape=[jax.ShapeDtypeStruct((mp, n), F32), jax.ShapeDtypeStruct((ms, n), F32)],
        scratch_shapes=_weight_scratch((k, tn)),
        compiler_params=_cparams(("arbitrary", "arbitrary")),
        name="project_residual",
    )(xp, xs, w, res_p, gate_p.reshape(nb, 1, n), res_s, gate_s)


def _nsa_prep_kernel(q_ref, a_ref, b_ref, c_ref, g_ref, qn_ref, rows_ref, win_ref, rows_t_ref, win_t_ref):
    def hnorm(x, g):
        return x * lax.rsqrt(jnp.mean(x * x, axis=-1, keepdims=True) + EPS) * g

    g = g_ref[...]
    for j in range(A_HEADS):
        sl = slice(j * HD, (j + 1) * HD)
        qn_ref[:, sl] = (hnorm(q_ref[:, sl], g[0:1]) * Q_SCALE).astype(qn_ref.dtype)
    kvw = A_KV * HD
    sls = [slice(j * HD, (j + 1) * HD) for j in range(2 * A_KV)]
    row_slots = ([a_ref[:, s] for s in sls] + [hnorm(b_ref[:, s], g[2:3]) for s in sls[:A_KV]]
                 + [b_ref[:, s] for s in sls[A_KV:]])
    win_slots = [hnorm(c_ref[:, s], g[3:4]) for s in sls[:A_KV]] + [c_ref[:, s] for s in sls[A_KV:]]
    for j, x in enumerate(row_slots):
        rows_ref[:, j * HD:(j + 1) * HD] = x
    for j, x in enumerate(win_slots):
        win_ref[:, j * HD:(j + 1) * HD] = x
    rows_t_ref[...] = jnp.swapaxes(jnp.stack(row_slots, axis=0), 0, 1)
    win_t_ref[...] = jnp.swapaxes(jnp.stack(win_slots, axis=0), 0, 1)


def nsa_prep(proj, qk_g):
    m = proj.shape[0]
    tm = min(m, ROW_TILE)
    qw = A_HEADS * HD
    kv2 = 2 * A_KV * HD
    base = qw // kv2
    return pl.pallas_call(
        _nsa_prep_kernel,
        grid=(m // tm,),
        in_specs=[
            pl.BlockSpec((tm, qw), lambda i: (i, 0)),
            pl.BlockSpec((tm, kv2), lambda i: (i, base)),
            pl.BlockSpec((tm, kv2), lambda i: (i, base + 1)),
            pl.BlockSpec((tm, kv2), lambda i: (i, base + 2)),
            pl.BlockSpec((4, HD), lambda i: (0, 0)),
        ],
        out_specs=[
            pl.BlockSpec((tm, qw), lambda i: (i, 0)),
            pl.BlockSpec((tm, 2 * kv2), lambda i: (i, 0)),
            pl.BlockSpec((tm, kv2), lambda i: (i, 0)),
            pl.BlockSpec((tm, 4 * A_KV, HD), lambda i: (i, 0, 0)),
            pl.BlockSpec((tm, 2 * A_KV, HD), lambda i: (i, 0, 0)),
        ],
        out_shape=[
            jax.ShapeDtypeStruct((m, qw), BF16),
            jax.ShapeDtypeStruct((m, 2 * kv2), F32),
            jax.ShapeDtypeStruct((m, kv2), F32),
            jax.ShapeDtypeStruct((m, 4 * A_KV, HD), F32),
            jax.ShapeDtypeStruct((m, 2 * A_KV, HD), F32),
        ],
        compiler_params=_cparams(("arbitrary",)),
        name="nsa_prep",
    )(proj, proj, proj, proj, qk_g)


NCK = 2 * A_KV


def _cmp_a_kernel(pt_ref, *refs, n_pages):
    del pt_ref
    pages = refs[:n_pages]
    w_ref = refs[n_pages]
    o_ref = refs[n_pages + 1]
    cpp = PAGE // CMP_STRIDE
    n_ch = n_pages * cpp

    def halves(tiles, lo):
        return [jnp.concatenate([tiles[a][lo:lo + A_KV], tiles[a + 1][lo:lo + A_KV]], axis=0)
                for a in range(0, len(tiles), 2)]

    acc = [None, None]
    for tp in range(CMP_STRIDE // 2):
        pieces = []
        for pg in pages:
            for ch in range(cpp):
                t0 = ch * CMP_STRIDE + 2 * tp
                pieces.append(jnp.concatenate([pg[t0], pg[t0 + 1]], axis=1))
        for kind in range(2):
            lhs = jnp.concatenate(halves(pieces, kind * A_KV), axis=0).astype(BF16)
            d = _dot(lhs, w_ref[tp, :, kind * 2 * HD:(kind + 1) * 2 * HD])
            acc[kind] = d if acc[kind] is None else acc[kind] + d
    for a in range(0, n_ch, 2):
        r = slice(a * A_KV, (a + 2) * A_KV)
        kk, vv = acc[0][r], acc[1][r]
        o_ref[a] = jnp.concatenate([kk[:A_KV], vv[:A_KV]], axis=0)
        o_ref[a + 1] = jnp.concatenate([kk[A_KV:], vv[A_KV:]], axis=0)


def compress_stage_a(store, page_table, w_pairs, n_pages):
    nb, n_tab = page_table.shape
    steps = n_tab // n_pages
    cpp = PAGE // CMP_STRIDE

    def page_spec(k):
        return pl.BlockSpec((None, PAGE, NCK, HD), lambda b, s, pt: (pt[b, s * n_pages + k], 0, 0, 0))

    grid_spec = pltpu.PrefetchScalarGridSpec(
        num_scalar_prefetch=1,
        grid=(nb, steps),
        in_specs=[page_spec(k) for k in range(n_pages)]
        + [pl.BlockSpec(w_pairs.shape, lambda b, s, pt: (0, 0, 0))],
        out_specs=pl.BlockSpec((None, n_pages * cpp, NCK, 2 * HD), lambda b, s, pt: (b, s, 0, 0)),
    )
    return pl.pallas_call(
        functools.partial(_cmp_a_kernel, n_pages=n_pages),
        grid_spec=grid_spec,
        out_shape=jax.ShapeDtypeStruct((nb, n_tab * cpp, NCK, 2 * HD), F32),
        compiler_params=_cparams(("arbitrary", "arbitrary")),
        name="compress_stage_a",
    )(page_table, *([store] * n_pages), w_pairs)


def _cmp_b_kernel(ab_ref, nx_ref, tail_ref, pe_ref, w1_ref, w2_ref, g_ref, o_ref, const_ref):
    j = pl.program_id(1)
    nch = ab_ref.shape[0]
    rows = nch * NCK

    @pl.when((pl.program_id(0) == 0) & (j == 0))
    def _():
        for kind in range(2):
            const = jnp.zeros((8, HD), F32)
            for t in range(CMP_LEN):
                row = jnp.broadcast_to(pe_ref[kind, t:t + 1, :], (8, HD)).astype(BF16)
                const = const + _dot(row, w1_ref[kind, t].astype(BF16))
            const_ref[kind * A_KV:(kind + 1) * A_KV, :] = const[:A_KV]

    ab = ab_ref[...].reshape(rows, 2 * HD)
    nxt = pltpu.roll(ab[:, HD:], rows - NCK, 0).reshape(nch, NCK, HD)
    first_next = jnp.where(j == pl.num_programs(1) - 1, tail_ref[0, :, HD:], nx_ref[0, :, HD:])
    cid = lax.broadcasted_iota(jnp.int32, (nch, NCK, HD), 0)
    nxt = jnp.where(cid == nch - 1, first_next[None], nxt)
    pre = ab[:, :HD].reshape(nch, NCK, HD) + nxt + const_ref[...][None]
    act = _silu(pre).reshape(rows, HD).astype(BF16)
    w2 = jnp.concatenate([w2_ref[0], w2_ref[1]], axis=1).astype(BF16)
    y2 = _dot(act, w2)
    is_k = (lax.broadcasted_iota(jnp.int32, (rows, HD), 0) & (NCK - 1)) < A_KV
    y = jnp.where(is_k, y2[:, :HD], y2[:, HD:])
    yn = y * lax.rsqrt(jnp.mean(y * y, axis=-1, keepdims=True) + EPS) * g_ref[1:2]
    o_ref[...] = jnp.where(is_k, yn, y).reshape(nch, NCK, HD)


def compress_stage_b(ab, tail, pe, w1, w2, qk_g):
    nb, n_ch = ab.shape[:2]
    tc = min(ROW_TILE, n_ch)
    assert n_ch % tc == 0
    nt = n_ch // tc
    out = pl.pallas_call(
        _cmp_b_kernel,
        grid=(nb, nt),
        in_specs=[
            pl.BlockSpec((None, tc, NCK, 2 * HD), lambda b, j: (b, j, 0, 0)),
            pl.BlockSpec((None, 1, NCK, 2 * HD), lambda b, j: (b, jnp.minimum((j + 1) * tc, n_ch - 1), 0, 0)),
            pl.BlockSpec((None, 1, NCK, 2 * HD), lambda b, j: (b, 0, 0, 0)),
            pl.BlockSpec(pe.shape, lambda b, j: (0, 0, 0)),
            pl.BlockSpec(w1.shape, lambda b, j: (0, 0, 0, 0)),
            pl.BlockSpec(w2.shape, lambda b, j: (0, 0, 0)),
            pl.BlockSpec((4, HD), lambda b, j: (0, 0)),
        ],
        out_specs=pl.BlockSpec((None, tc, NCK, HD), lambda b, j: (b, j, 0, 0)),
        out_shape=jax.ShapeDtypeStruct((nb, n_ch, NCK, HD), F32),
        scratch_shapes=[pltpu.VMEM((NCK, HD), F32)],
        compiler_params=_cparams(("arbitrary", "arbitrary")),
        name="compress_stage_b",
    )(ab, ab, tail, pe, w1, w2, qk_g)
    return out.reshape(nb, n_ch * NCK, HD)


def _cmp_rows(cmp_ref, slot, n):
    return cmp_ref[pl.ds(slot, n, stride=NCK), :]


def _select_topk(score, blk, k_top):
    sel = jnp.zeros(score.shape, F32)
    for _ in range(k_top):
        m = jnp.max(score, axis=-1, keepdims=True)
        cand = jnp.where(score == m, blk, jnp.int32(1 << 30))
        first = jnp.min(cand, axis=-1, keepdims=True)
        pick = blk == first
        sel = jnp.where(pick, 1.0, sel)
        score = jnp.where(pick, -jnp.inf, score)
    return sel


def _select_topk_t(score, blk, k_top, n_blk):
    rank = jnp.zeros(score.shape, F32)
    for i in range(n_blk):
        row = score[i:i + 1, :]
        tie = jnp.where(blk > i, 1.0, 0.0)
        rank = rank + jnp.where(row > score, 1.0, jnp.where(row == score, tie, 0.0))
    return jnp.where(rank < k_top, 1.0, 0.0)


def _bias_select(dist, dmin, dmax, value_of):
    lo = int(_t5_bucket_np(np.array([max(dmin, 0)]))[0])
    hi = int(_t5_bucket_np(np.array([max(dmax, 0)]))[0])
    out = jnp.full(dist.shape, value_of(lo), F32)
    for b in range(lo + 1, hi + 1):
        out = jnp.where(dist >= _BUCKET_START[b], value_of(b), out)
    return out * LOG2E


def _bias_tables_kernel(rb_ref, tsel_ref, twin_ref, tcmp_ref):
    head = pl.program_id(0)
    value_of = lambda b: rb_ref[b, head]
    n_off, nq, nc = tsel_ref.shape[0], tcmp_ref.shape[0], tcmp_ref.shape[1]
    kl = lax.broadcasted_iota(jnp.int32, (LANE, Q_BLOCK), 0)
    ql = lax.broadcasted_iota(jnp.int32, (LANE, Q_BLOCK), 1)
    for o in range(n_off):
        base = (o - 1) * Q_BLOCK
        dist = base + ql - kl
        tile = _bias_select(dist, base - (LANE - 1), base + Q_BLOCK - 1, value_of)
        tsel_ref[o] = tile
        if o < twin_ref.shape[0]:
            twin_ref[o] = tile + jnp.where((dist >= 0) & (dist < WINDOW), 0.0, NEG)
    step = Q_BLOCK // CMP_STRIDE
    rows = nc + (nq - 1) * step
    base = (nq - 1) * Q_BLOCK
    cend = lax.broadcasted_iota(jnp.int32, (rows, Q_BLOCK), 0) * CMP_STRIDE + (CMP_LEN - 1)
    qc = lax.broadcasted_iota(jnp.int32, (rows, Q_BLOCK), 1)
    tall = _bias_select(base + qc - cend, base - ((rows - 1) * CMP_STRIDE + CMP_LEN - 1),
                        base + Q_BLOCK - 1, value_of)
    for i in range(nq):
        r0 = (nq - 1 - i) * step
        tcmp_ref[i] = tall[r0:r0 + nc]


def prompt_bias_tables(rel_bias, nq, n_cmp):
    n_win = min(WINDOW // Q_BLOCK + 3, nq + 1)
    return pl.pallas_call(
        _bias_tables_kernel,
        grid=(A_HEADS,),
        in_specs=[pl.BlockSpec(memory_space=pltpu.SMEM)],
        out_specs=[
            pl.BlockSpec((None, nq + 1, LANE, Q_BLOCK), lambda h: (h // A_HPG, 0, 0, h % A_HPG)),
            pl.BlockSpec((None, n_win, LANE, Q_BLOCK), lambda h: (h // A_HPG, 0, 0, h % A_HPG)),
            pl.BlockSpec((None, nq, n_cmp, Q_BLOCK), lambda h: (h // A_HPG, 0, 0, h % A_HPG)),
        ],
        out_shape=[
            jax.ShapeDtypeStruct((A_KV, nq + 1, LANE, A_HPG * Q_BLOCK), F32),
            jax.ShapeDtypeStruct((A_KV, n_win, LANE, A_HPG * Q_BLOCK), F32),
            jax.ShapeDtypeStruct((A_KV, nq, n_cmp, A_HPG * Q_BLOCK), F32),
        ],
        compiler_params=_cparams(("arbitrary",)),
        name="prompt_bias_tables",
    )(rel_bias)


def _bias_rows_kernel(rb_ref, cmp_ref, sel_ref, win_ref, *, past_len, n_q):
    head = pl.program_id(0)
    value_of = lambda b: rb_ref[b, head]

    def fill(ref, key_pos, kmin, kmax, q0):
        q = lax.broadcasted_iota(jnp.int32, ref.shape, 0)
        k = lax.broadcasted_iota(jnp.int32, ref.shape, 1)
        ref[...] = _bias_select(q0 + q - key_pos(k), q0 - kmax, q0 + n_q - 1 - kmin, value_of)

    nc, nk, nw = cmp_ref.shape[1], sel_ref.shape[1], win_ref.shape[1]
    fill(cmp_ref, lambda k: k * CMP_STRIDE + (CMP_LEN - 1), CMP_LEN - 1, (nc - 1) * CMP_STRIDE + CMP_LEN - 1, past_len)
    fill(sel_ref, lambda k: k, 0, nk - 1, past_len)
    fill(win_ref, lambda k: k, 0, nw - 1, WINDOW)


def sample_bias_rows(rel_bias, past_len, n_q, n_cmp, n_keys, n_win):
    shapes = [(A_HEADS * n_q, n) for n in (n_cmp, n_keys, n_win)]
    return pl.pallas_call(
        functools.partial(_bias_rows_kernel, past_len=past_len, n_q=n_q),
        grid=(A_HEADS,),
        in_specs=[pl.BlockSpec(memory_space=pltpu.SMEM)],
        out_specs=[pl.BlockSpec((n_q, s[1]), lambda h: (h, 0)) for s in shapes],
        out_shape=[jax.ShapeDtypeStruct(s, F32) for s in shapes],
        compiler_params=_cparams(("arbitrary",)),
        name="sample_bias_rows",
    )(rel_bias)


def _nsa_prompt_kernel(q_ref, cmp_ref, ks_ref, vs_ref, kw_ref, vw_ref, zc_ref, zs_ref, zw_ref,
                       gt_ref, tsel_ref, twin_ref, tcmp_ref, msel_ref, exp_ref, o_ref,
                       qa_ref, acc_ref, m_ref, l_ref, acc2_ref, m2_ref, l2_ref, mix_ref,
                       *, n_sel_pad, n_sel, k_top, kt):
    i = pl.program_id(2)
    qb = Q_BLOCK
    tpk = kt // LANE
    nc = cmp_ref.shape[0] // NCK
    gates = _sigmoid(gt_ref[...])
    hs = [slice(h * HD, (h + 1) * HD) for h in range(A_HPG)]
    for h in range(A_HPG):
        qa_ref[hs[h], :] = q_ref[:, hs[h]]
    qa = qa_ref[...]

    def heads(x):
        return jnp.concatenate([x] * A_HPG, axis=1)

    def gate_row(branch):
        return jnp.concatenate([gates[branch * A_HPG + h: branch * A_HPG + h + 1, :] for h in range(A_HPG)], axis=1)

    def emit(branch, o_t, first):
        z_ref = (zc_ref, zs_ref, zw_ref)[branch]
        for h in range(A_HPG):
            val = o_t[:, hs[h]] * z_ref[hs[h], :]
            if first:
                mix_ref[hs[h], :] = val
            else:
                mix_ref[hs[h], :] = mix_ref[hs[h], :] + val

    sel_state, win_state = (m_ref, l_ref, acc_ref), (m2_ref, l2_ref, acc2_ref)

    def reset(state):
        m, l, acc = state
        m[...] = jnp.full(m.shape, NEG, F32)
        l[...] = jnp.zeros(l.shape, F32)
        acc[...] = jnp.zeros(acc.shape, F32)

    def tile_update(state, k, v, bias, madd):
        m, l, acc = state
        s = _dot_nt(k, qa) + bias
        if madd is not None:
            s = s + heads(madd)
        m_old = m[...]
        m_new = jnp.maximum(m_old, jnp.max(s, axis=0, keepdims=True))
        alpha = jnp.exp2(m_old - m_new)
        p = jnp.exp2(s - m_new)
        l[...] = alpha * l[...] + jnp.sum(p, axis=0, keepdims=True)
        m[...] = m_new
        acc[...] = acc[...] * alpha + _dot_tn(v, p.astype(BF16))

    def finish(branch, state):
        m, l, acc = state
        emit(branch, acc[...] * (gate_row(branch) / l[...]), False)

    wt = 2 * LANE
    first = i // 2
    lowest = jnp.maximum((i - WINDOW // qb) // 2, 0)

    def win_body(t, carry):
        tile = first - t
        k0 = pl.multiple_of(tile * wt, wt)
        k = kw_ref[pl.ds(k0, wt), :].astype(BF16)
        v = vw_ref[pl.ds(k0, wt), :].astype(BF16)
        off = i - 2 * tile
        bias = jnp.concatenate([twin_ref[off + 1], twin_ref[off]], axis=0)
        tile_update(win_state, k, v, bias, None)
        return carry

    reset(win_state)
    win_body(0, 0)

    grp = pl.program_id(0)
    kc = _cmp_rows(cmp_ref, grp, nc).astype(BF16)
    vc = _cmp_rows(cmp_ref, A_KV + grp, nc).astype(BF16)
    cend = lax.broadcasted_iota(jnp.int32, (nc, qb), 0) * CMP_STRIDE + (CMP_LEN - 1)
    qpos_c = i * qb + lax.broadcasted_iota(jnp.int32, (nc, qb), 1)
    cvalid = heads(cend <= qpos_c)
    s = jnp.where(cvalid, _dot_nt(kc, qa) + tcmp_ref[...], NEG)
    mx = jnp.max(s, axis=0, keepdims=True)
    e = jnp.where(cvalid, jnp.exp2(s - mx), 0.0)
    p = e * (1.0 / jnp.maximum(jnp.sum(e, axis=0, keepdims=True), TINY))
    imp = p[:, hs[0]]
    for h in range(1, A_HPG):
        imp = imp + p[:, hs[h]]
    emit(0, _dot_tn(vc, p.astype(BF16)) * gate_row(0), True)

    hi = imp.astype(BF16)
    r1 = imp - hi.astype(F32)
    mid = r1.astype(BF16)
    lo = (r1 - mid.astype(F32)).astype(BF16)
    msel = msel_ref[...]
    imp_sel = _dot(msel, hi) + _dot(msel, mid) + _dot(msel, lo)
    blk = lax.broadcasted_iota(jnp.int32, (n_sel_pad, qb), 0)
    qpos = i * qb + lax.broadcasted_iota(jnp.int32, (n_sel_pad, qb), 1)
    cur = jnp.right_shift(qpos, int(math.log2(SEL_BLOCK)))
    forced = (blk == 0) | (blk == cur) | (blk == cur - 1)
    valid = blk * SEL_BLOCK <= qpos
    score = jnp.where(valid, jnp.where(forced, FORCE_SCORE, imp_sel), -1.0)
    score = jnp.where(blk < n_sel, score, -jnp.inf)
    sel = _select_topk_t(score, blk, k_top, n_sel).astype(BF16)
    kpos2 = lax.broadcasted_iota(jnp.int32, (kt, qb), 0)
    qpos2 = i * qb + lax.broadcasted_iota(jnp.int32, (kt, qb), 1)

    def sel_madd(jj):
        k0 = pl.multiple_of(jj * kt, kt)
        hit = _dot(exp_ref[pl.ds(k0, kt), :], sel)
        return jnp.where((hit > 0.5) & ((k0 + kpos2) <= qpos2), 0.0, NEG)

    reset(sel_state)

    def sel_body(jj, carry):
        k0 = pl.multiple_of(jj * kt, kt)
        k = ks_ref[pl.ds(k0, kt), :].astype(BF16)
        v = vs_ref[pl.ds(k0, kt), :].astype(BF16)
        off = i - tpk * jj
        bias = jnp.concatenate([tsel_ref[jnp.maximum(off + 1 - r, 0)] for r in range(tpk)], axis=0)
        tile_update(sel_state, k, v, bias, sel_madd(jj))
        return carry

    lax.fori_loop(0, i // tpk + 1, sel_body, 0)
    finish(1, sel_state)

    lax.fori_loop(1, first - lowest + 1, win_body, 0)
    finish(2, win_state)
    for h in range(A_HPG):
        o_ref[:, hs[h]] = mix_ref[hs[h], :].T.astype(o_ref.dtype)


def nsa_prompt_attention(qn, cmp_kv, rows, win, z_t, gates_g, tsel, twin, tcmp, msel, expand, nb, t):
    nq = t // Q_BLOCK
    qw = A_HPG * HD
    n_sel = -(-t // SEL_BLOCK)
    k_top = min(SEL_TOPK, n_sel)
    kt = SEL_KEY_TILE if t % SEL_KEY_TILE == 0 else 2 * LANE
    n_cmp = cmp_kv.shape[1] // NCK
    n_sel_pad = msel.shape[0]
    kernel = functools.partial(_nsa_prompt_kernel, n_sel_pad=n_sel_pad, n_sel=n_sel, k_top=k_top, kt=kt)
    return pl.pallas_call(
        kernel,
        grid=(A_KV, nb, nq),
        in_specs=[
            pl.BlockSpec((Q_BLOCK, qw), lambda g, b, i: (b * nq + i, g)),
            pl.BlockSpec((None, n_cmp * NCK, HD), lambda g, b, i: (b, 0, 0)),
            pl.BlockSpec((t, HD), lambda g, b, i: (b, 2 * A_KV + g)),
            pl.BlockSpec((t, HD), lambda g, b, i: (b, 3 * A_KV + g)),
            pl.BlockSpec((t, HD), lambda g, b, i: (b, g)),
            pl.BlockSpec((t, HD), lambda g, b, i: (b, A_KV + g)),
            pl.BlockSpec((qw, Q_BLOCK), lambda g, b, i: (g, b * nq + i)),
            pl.BlockSpec((qw, Q_BLOCK), lambda g, b, i: (A_KV + g, b * nq + i)),
            pl.BlockSpec((qw, Q_BLOCK), lambda g, b, i: (2 * A_KV + g, b * nq + i)),
            pl.BlockSpec((None, 3 * A_HPG, Q_BLOCK), lambda g, b, i: (g, 0, b * nq + i)),
            pl.BlockSpec((None, nq + 1, LANE, qw), lambda g, b, i: (g, 0, 0, 0)),
            pl.BlockSpec((None, twin.shape[1], LANE, qw), lambda g, b, i: (g, 0, 0, 0)),
            pl.BlockSpec((None, None, n_cmp, qw), lambda g, b, i: (g, i, 0, 0)),
            pl.BlockSpec((n_sel_pad, n_cmp), lambda g, b, i: (0, 0)),
            pl.BlockSpec((t, n_sel_pad), lambda g, b, i: (0, 0)),
        ],
        out_specs=pl.BlockSpec((Q_BLOCK, qw), lambda g, b, i: (b * nq + i, g)),
        out_shape=jax.ShapeDtypeStruct((nb * t, A_HEADS * HD), BF16),
        scratch_shapes=[
            pltpu.VMEM((qw, HD), BF16),
            pltpu.VMEM((HD, qw), F32),
            pltpu.VMEM((1, qw), F32),
            pltpu.VMEM((1, qw), F32),
            pltpu.VMEM((HD, qw), F32),
            pltpu.VMEM((1, qw), F32),
            pltpu.VMEM((1, qw), F32),
            pltpu.VMEM((qw, Q_BLOCK), F32),
        ],
        compiler_params=_cparams(("arbitrary", "arbitrary", "arbitrary")),
        name="nsa_prompt_attention",
    )(qn, cmp_kv, rows, rows, win, win, z_t, z_t, z_t, gates_g, tsel, twin, tcmp, msel, expand)


def _diag_blocks(o_full, rows_per_group):
    return jnp.concatenate(
        [o_full[g * rows_per_group:(g + 1) * rows_per_group, g * HD:(g + 1) * HD] for g in range(A_KV)], axis=0)


def _nsa_sample_cmp_kernel(q_ref, cmp_ref, bias_ref, msel_ref, oc_ref, sel_ref,
                           *, past_len, n_sel, k_top, n_q, blocks_per_step):
    n_cmp = cmp_ref.shape[0] // NCK
    kc = jnp.concatenate([_cmp_rows(cmp_ref, g, n_cmp) for g in range(A_KV)], axis=1).astype(BF16)
    vc = jnp.concatenate([_cmp_rows(cmp_ref, A_KV + g, n_cmp) for g in range(A_KV)], axis=1).astype(BF16)
    rows = q_ref.shape[0]
    rpg = A_HPG * n_q
    q = q_ref[...]
    s = _dot_nt(q, kc) + bias_ref[...]
    rowi = lax.broadcasted_iota(jnp.int32, (rows, n_cmp), 0)
    ci = lax.broadcasted_iota(jnp.int32, (rows, n_cmp), 1)
    qpos = past_len + (rowi & (n_q - 1))
    valid = (ci * CMP_STRIDE + (CMP_LEN - 1)) <= qpos
    s = jnp.where(valid, s, NEG)
    mx = jnp.max(s, axis=-1, keepdims=True)
    e = jnp.where(valid, jnp.exp2(s - mx), 0.0)
    p = e / jnp.maximum(jnp.sum(e, axis=-1, keepdims=True), TINY)
    oc_ref[...] = _diag_blocks(_dot(p.astype(BF16), vc), rpg)
    imps = []
    for g in range(A_KV):
        acc = p[g * rpg: g * rpg + n_q]
        for h in range(1, A_HPG):
            acc = acc + p[g * rpg + h * n_q: g * rpg + (h + 1) * n_q]
        imps.append(acc)
    imp = jnp.concatenate(imps, axis=0)
    imp_sel = _split3_dot(imp, msel_ref[...])
    shp = imp_sel.shape
    blk = lax.broadcasted_iota(jnp.int32, shp, 1)
    qp = past_len + (lax.broadcasted_iota(jnp.int32, shp, 0) & (n_q - 1))
    cur = jnp.right_shift(qp, int(math.log2(SEL_BLOCK)))
    forced = (blk == 0) | (blk == cur) | (blk == cur - 1)
    ok = blk * SEL_BLOCK <= qp
    score = jnp.where(ok, jnp.where(forced, FORCE_SCORE, imp_sel), -1.0)
    score = jnp.where(blk < n_sel, score, -jnp.inf)
    sel = _select_topk(score, blk, k_top)
    for st in range(sel_ref.shape[0]):
        sel_ref[st] = sel[:, st * blocks_per_step:(st + 1) * blocks_per_step]


def _nsa_sample_sel_kernel(pt_ref, *refs, n_pages, n_steps, past_len, n_q):
    del pt_ref
    pages = refs[:n_pages]
    q_ref, tail_ref, bias_ref, sel_ref, exp_ref, o_ref, m_ref, l_ref, acc_ref = refs[n_pages:]
    s_id = pl.program_id(1)
    kvw = A_KV * HD
    rows = q_ref.shape[0]
    rpg = A_HPG * n_q

    @pl.when(s_id == 0)
    def _():
        m_ref[...] = jnp.full(m_ref.shape, NEG, F32)
        l_ref[...] = jnp.zeros(l_ref.shape, F32)
        acc_ref[...] = jnp.zeros(acc_ref.shape, F32)

    def update(k, v, nk, key0):
        hit = _dot(sel_ref[...].astype(BF16), exp_ref[:, :nk])
        kpos = key0 + lax.broadcasted_iota(jnp.int32, hit.shape, 1)
        qpos = past_len + (lax.broadcasted_iota(jnp.int32, hit.shape, 0) & (n_q - 1))
        madd = jnp.where((hit > 0.5) & (kpos <= qpos), 0.0, NEG)
        madd = jnp.concatenate(
            [madd[g * n_q:(g + 1) * n_q] for g in range(A_KV) for _ in range(A_HPG)], axis=0)
        s = _dot_nt(q_ref[...], k) + bias_ref[:, :nk] + madd
        m_old = m_ref[...]
        m_new = jnp.maximum(m_old, jnp.max(s, axis=-1, keepdims=True))
        alpha = jnp.exp2(m_old - m_new)
        p = jnp.exp2(s - m_new)
        l_ref[...] = alpha * l_ref[...] + jnp.sum(p, axis=-1, keepdims=True)
        acc_ref[...] = alpha * acc_ref[...] + _dot(p.astype(BF16), v)
        m_ref[...] = m_new

    @pl.when(s_id < n_steps)
    def _():
        by_slot = [jnp.swapaxes(pg[...].astype(BF16), 0, 1) for pg in pages]

        def gather(c0):
            cols = [jnp.concatenate([x[c0 + g] for x in by_slot], axis=0) for g in range(A_KV)]
            return jnp.concatenate(cols, axis=1)

        update(gather(0), gather(A_KV), n_pages * PAGE, s_id * (n_pages * PAGE))

    @pl.when(s_id == n_steps)
    def _():
        update(tail_ref[:, :kvw].astype(BF16), tail_ref[:, kvw:].astype(BF16), PAGE, past_len)
        o_ref[...] = _diag_blocks(acc_ref[...] / l_ref[...], rpg)


def _nsa_sample_mix_kernel(q_ref, win_ref, bias_ref, oc_ref, os_ref, z_ref, gt_ref, o_ref, *, n_q):
    kvw = A_KV * HD
    rows = q_ref.shape[0]
    rpg = A_HPG * n_q
    nk = win_ref.shape[0]
    s = _dot_nt(q_ref[...], win_ref[:, :kvw].astype(BF16)) + bias_ref[...]
    qi = lax.broadcasted_iota(jnp.int32, (rows, nk), 0) & (n_q - 1)
    ki = lax.broadcasted_iota(jnp.int32, (rows, nk), 1)
    dist = qi + WINDOW - ki
    s = s + jnp.where((dist >= 0) & (dist < WINDOW), 0.0, NEG)
    mx = jnp.max(s, axis=-1, keepdims=True)
    e = jnp.exp2(s - mx)
    p = e / jnp.sum(e, axis=-1, keepdims=True)
    ow = _diag_blocks(_dot(p.astype(BF16), win_ref[:, kvw:].astype(BF16)), rpg)
    gates = _sigmoid(gt_ref[...])
    width = A_HEADS * HD
    for hh in range(A_HEADS):
        r = slice(hh * n_q, (hh + 1) * n_q)
        c = slice(hh * HD, (hh + 1) * HD)
        val = None
        for br, o in enumerate((oc_ref, os_ref, ow)):
            zc = slice(br * width + hh * HD, br * width + (hh + 1) * HD)
            gcol = gates[:, br * A_HEADS + hh: br * A_HEADS + hh + 1]
            term = gcol * o[r, :] * z_ref[:, zc]
            val = term if val is None else val + term
        o_ref[:, c] = val.astype(o_ref.dtype)


def nsa_sample_attention(qbd, cmp_kv, cache4d, page_table, tail_sel, win_all, z_s, gl_s,
                         b_cmp, b_sel, b_win, msel, expand, past_len, n_q):
    nb, rows, kvw = qbd.shape
    n_cmp = cmp_kv.shape[1] // NCK
    n_tab = page_table.shape[1]
    n_pages = min(PAGES_PER_STEP, n_tab)
    n_steps = n_tab // n_pages
    bps = n_pages * PAGE // SEL_BLOCK
    n_sel = -(-(past_len + n_q) // SEL_BLOCK)
    k_top = min(SEL_TOPK, n_sel)
    gq = A_KV * n_q

    oc, selmask = pl.pallas_call(
        functools.partial(_nsa_sample_cmp_kernel, past_len=past_len, n_sel=n_sel, k_top=k_top,
                          n_q=n_q, blocks_per_step=bps),
        grid=(nb,),
        in_specs=[
            pl.BlockSpec((None, rows, kvw), lambda b: (b, 0, 0)),
            pl.BlockSpec((None, n_cmp * NCK, HD), lambda b: (b, 0, 0)),
            pl.BlockSpec(b_cmp.shape, lambda b: (0, 0)),
            pl.BlockSpec(msel.shape, lambda b: (0, 0)),
        ],
        out_specs=[
            pl.BlockSpec((None, rows, HD), lambda b: (b, 0, 0)),
            pl.BlockSpec((None, n_steps + 1, gq, bps), lambda b: (b, 0, 0, 0)),
        ],
        out_shape=[
            jax.ShapeDtypeStruct((nb, rows, HD), F32),
            jax.ShapeDtypeStruct((nb, n_steps + 1, gq, bps), F32),
        ],
        compiler_params=_cparams(("arbitrary",)),
        name="nsa_sample_cmp",
    )(qbd, cmp_kv, b_cmp, msel)

    half = 2 * kvw
    last = n_steps - 1

    def page_spec(k):
        return pl.BlockSpec(
            (None, PAGE, NCK, HD),
            lambda b, s, pt: (pt[b, jnp.minimum(s, last) * n_pages + k], 0, 1, 0))

    nkeys = n_pages * PAGE
    grid_spec = pltpu.PrefetchScalarGridSpec(
        num_scalar_prefetch=1,
        grid=(nb, n_steps + 1),
        in_specs=[page_spec(k) for k in range(n_pages)] + [
            pl.BlockSpec((None, rows, kvw), lambda b, s, pt: (b, 0, 0)),
            pl.BlockSpec((None, PAGE, half), lambda b, s, pt: (b, 0, 0)),
            pl.BlockSpec((rows, nkeys), lambda b, s, pt: (0, s)),
            pl.BlockSpec((None, None, gq, bps), lambda b, s, pt: (b, s, 0, 0)),
            pl.BlockSpec(expand.shape, lambda b, s, pt: (0, 0)),
        ],
        out_specs=pl.BlockSpec((None, rows, HD), lambda b, s, pt: (b, 0, 0)),
        scratch_shapes=[
            pltpu.VMEM((rows, 1), F32),
            pltpu.VMEM((rows, 1), F32),
            pltpu.VMEM((rows, kvw), F32),
        ],
    )
    osel = pl.pallas_call(
        functools.partial(_nsa_sample_sel_kernel, n_pages=n_pages, n_steps=n_steps,
                          past_len=past_len, n_q=n_q),
        grid_spec=grid_spec,
        out_shape=jax.ShapeDtypeStruct((nb, rows, HD), F32),
        compiler_params=_cparams(("arbitrary", "arbitrary")),
        name="nsa_sample_sel",
    )(page_table, *([cache4d] * n_pages), qbd, tail_sel, b_sel, selmask, expand)

    nkw = win_all.shape[1]
    zw = z_s.shape[1]
    return pl.pallas_call(
        functools.partial(_nsa_sample_mix_kernel, n_q=n_q),
        grid=(nb,),
        in_specs=[
            pl.BlockSpec((None, rows, kvw), lambda b: (b, 0, 0)),
            pl.BlockSpec((None, nkw, 2 * kvw), lambda b: (b, 0, 0)),
            pl.BlockSpec(b_win.shape, lambda b: (0, 0)),
            pl.BlockSpec((None, rows, HD), lambda b: (b, 0, 0)),
            pl.BlockSpec((None, rows, HD), lambda b: (b, 0, 0)),
            pl.BlockSpec((n_q, zw), lambda b: (b, 0)),
            pl.BlockSpec((n_q, 3 * A_HEADS), lambda b: (b, 0)),
        ],
        out_specs=pl.BlockSpec((n_q, A_HEADS * HD), lambda b: (b, 0)),
        out_shape=jax.ShapeDtypeStruct((nb * n_q, A_HEADS * HD), F32),
        compiler_params=_cparams(("arbitrary",)),
        name="nsa_sample_mix",
    )(qbd, win_all, b_win, oc, osel, z_s, gl_s)


def _retention_kernel(q_ref, k_ref, v_ref, g_ref, cos_ref, sin_ref, dm_ref, qd_ref, kd_ref, cd_ref,
                      *rest, dk, dv, hps):
    s0_ref, o_ref, s_ref = rest if len(rest) == 3 else (None,) + tuple(rest)
    c = pl.program_id(2)
    half = dk // 2

    @pl.when(c == 0)
    def _():
        s_ref[...] = jnp.zeros(s_ref.shape, F32) if s0_ref is None else s0_ref[...]

    cos = cos_ref[...]
    sin = sin_ref[...]

    def rot(x):
        x1, x2 = x[:, :half], x[:, half:]
        return jnp.concatenate([x1 * cos - x2 * sin, x1 * sin + x2 * cos], axis=1)

    for j in range(hps):
        ks, vs = slice(j * dk, (j + 1) * dk), slice(j * dv, (j + 1) * dv)
        q = rot(q_ref[:, ks])
        k = rot(k_ref[:, ks]) * (dk ** -0.5)
        v = v_ref[:, vs].astype(BF16)
        qb = q.astype(BF16)
        att = _dot_nt(qb, k.astype(BF16)) * dm_ref[j]
        state = s_ref[j]
        o = _dot(att.astype(BF16), v) + _dot(qb, state.astype(BF16)) * qd_ref[j]
        s_ref[j] = state * cd_ref[j] + _dot_tn((k * kd_ref[j]).astype(BF16), v)
        o = o * lax.rsqrt(jnp.mean(o * o, axis=-1, keepdims=True) + EPS)
        o_ref[:, vs] = (o * _silu(g_ref[:, vs])).astype(o_ref.dtype)


def retention(proj, state0, q0, nb, t, state_shape=None):
    state_shape = state0.shape if state0 is not None else state_shape
    _, nh, dk, dv = state_shape
    cs = R_CHUNK if t % R_CHUNK == 0 else t
    n = t // cs
    half = dk // 2
    pos = (q0 + jnp.arange(t)).astype(F32)
    inv = jnp.power(ROPE_BASE, -jnp.arange(half, dtype=F32) / half)
    ang = pos[:, None] * inv[None, :]
    cos, sin = jnp.cos(ang), jnp.sin(ang)
    log_g = jnp.log1p(-jnp.exp2(-5.0 - jnp.arange(nh, dtype=F32)))
    ii = jnp.arange(cs, dtype=F32)
    rel = ii[:, None] - ii[None, :]
    causal = rel >= 0
    dmat = jnp.where(causal, jnp.exp(jnp.where(causal, rel, 0.0) * log_g[:, None, None]), 0.0)
    q_dec = jnp.exp((ii + 1.0) * log_g[:, None])[:, :, None]
    k_dec = jnp.exp((cs - 1.0 - ii) * log_g[:, None])[:, :, None]
    c_dec = jnp.exp(cs * log_g)[:, None, None]
    hps = RET_HEADS_PER_STEP
    assert nh % hps == 0
    ng = nh // hps
    vb0 = 2 * nh * dk // (hps * dv)
    return pl.pallas_call(
        functools.partial(_retention_kernel, dk=dk, dv=dv, hps=hps),
        grid=(nb, ng, n),
        in_specs=[
            pl.BlockSpec((cs, hps * dk), lambda b, h, c: (b * n + c, h)),
            pl.BlockSpec((cs, hps * dk), lambda b, h, c: (b * n + c, ng + h)),
            pl.BlockSpec((cs, hps * dv), lambda b, h, c: (b * n + c, vb0 + h)),
            pl.BlockSpec((cs, hps * dv), lambda b, h, c: (b * n + c, vb0 + ng + h)),
            pl.BlockSpec((cs, half), lambda b, h, c: (c, 0)),
            pl.BlockSpec((cs, half), lambda b, h, c: (c, 0)),
            pl.BlockSpec((hps, cs, cs), lambda b, h, c: (h, 0, 0)),
            pl.BlockSpec((hps, cs, 1), lambda b, h, c: (h, 0, 0)),
            pl.BlockSpec((hps, cs, 1), lambda b, h, c: (h, 0, 0)),
            pl.BlockSpec((hps, 1, 1), lambda b, h, c: (h, 0, 0)),
        ] + ([] if state0 is None else [pl.BlockSpec((None, hps, dk, dv), lambda b, h, c: (b, h, 0, 0))]),
        out_specs=[
            pl.BlockSpec((cs, hps * dv), lambda b, h, c: (b * n + c, h)),
            pl.BlockSpec((None, hps, dk, dv), lambda b, h, c: (b, h, 0, 0)),
        ],
        out_shape=[
            jax.ShapeDtypeStruct((nb * t, nh * dv), BF16 if cs % 16 == 0 else F32),
            jax.ShapeDtypeStruct(state_shape, F32),
        ],
        compiler_params=_cparams(("arbitrary", "arbitrary", "arbitrary")),
        name="retention",
    )(proj, proj, proj, proj, cos, sin, dmat, q_dec, k_dec, c_dec, *([] if state0 is None else [state0]))


def _t5_bucket_np(dist):
    n = np.maximum(dist, 0)
    exact = N_BUCKETS // 2
    logv = np.log(np.maximum(n, 1).astype(np.float32) / np.float32(exact)) / np.float32(
        math.log(MAX_DISTANCE / exact))
    large = np.minimum(exact + (logv * np.float32(N_BUCKETS - exact)).astype(np.int32), N_BUCKETS - 1)
    return np.where(n < exact, n, large).astype(np.int32)


def _bucket_starts():
    bk = _t5_bucket_np(np.arange(4 * MAX_DISTANCE))
    assert np.all(np.diff(bk) >= 0) and bk[-1] == N_BUCKETS - 1
    return [int(np.argmax(bk >= b)) for b in range(N_BUCKETS)]


_BUCKET_START = _bucket_starts()


def _sel_matrix(n_cmp, n_cmp_pad, n_sel_pad):
    c = np.arange(n_cmp_pad)[:, None]
    j = np.arange(n_sel_pad)[None, :]
    a = (c >= SEL_RATIO * j) & (c < SEL_RATIO * j + SEL_RATIO)
    b = (c >= SEL_RATIO * j - 1) & (c < SEL_RATIO * j + SEL_RATIO - 1)
    m = (a.astype(np.float32) + b.astype(np.float32)) * (c < n_cmp)
    return jnp.asarray(m, BF16)


def _expand_matrix(n_blk_pad, n_keys):
    blk = np.arange(n_blk_pad)[:, None]
    key = np.arange(n_keys)[None, :]
    return jnp.asarray((key // SEL_BLOCK == blk).astype(np.float32), BF16)


def _w1_pairs(w1):
    w = w1.reshape(2, 2, CMP_STRIDE // 2, 2, HD, HD)
    w = jnp.transpose(w, (2, 3, 4, 0, 1, 5))
    return w.reshape(CMP_STRIDE // 2, 2 * HD, 4 * HD).astype(BF16)


def kernel(x_prompt, x_sample, c_prompt, c_sample, cache_nsa_kv, cache_nsa_win, state_ret, page_table,
           norm_g, ada_w, ada_b, rel_bias, a_w_in, a_w_out, a_cmp_pe, a_cmp_w1, a_cmp_w2, a_qk_g,
           r_w_in, r_w_out):
    nbp, t, d = x_prompt.shape
    nbs, ts, _ = x_sample.shape
    n_tab = page_table.shape[1]
    past_len = n_tab * PAGE
    kvw = A_KV * HD
    qw = A_HEADS * HD
    assert t % (2 * LANE) == 0 and t >= WINDOW and ts == 8 and past_len >= WINDOW
    assert cache_nsa_win.shape[2] == WINDOW

    n_c = nbp + nbs
    pad_c = (-n_c) % 8
    c_all = jnp.concatenate([c_prompt, c_sample, jnp.zeros((pad_c, d), F32)], axis=0)
    mod = ada_modulation(c_all, ada_w, ada_b).reshape(ada_w.shape[0], n_c + pad_c, 3, d)

    def mods(layer):
        m = mod[layer]
        return (m[:nbp, 0], m[:nbp, 1], m[:nbp, 2]), (m[nbp:n_c, 0], m[nbp:n_c, 1], m[nbp:n_c, 2])

    (sh_p, sc_p, gt_p), (sh_s, sc_s, gt_s) = mods(0)
    hp = norm_modulate(x_prompt, norm_g[0], sc_p, sh_p)
    hs = norm_modulate(x_sample, norm_g[0], sc_s, sh_s)
    w_in_t = jnp.swapaxes(a_w_in[0], 0, 1)
    n_main = w_in_t.shape[0] - 3 * A_HEADS
    n_qkv = qw + 6 * kvw
    proj_p, proj_s = project(hp, hs, w_in_t, n_cols=n_qkv, w_is_nk=True)
    zt_p, z_s = project_gate_t(hp, hs, w_in_t, n_qkv, n_main - n_qkv)
    w_gate = jnp.pad(w_in_t[n_main:], ((0, LANE - 3 * A_HEADS), (0, 0)))
    gl_p, gl_s = project(hp, hs, w_gate, w_is_nk=True)
    gl_p, gl_s = gl_p[:, :3 * A_HEADS], gl_s[:, :3 * A_HEADS]
    qk_g = a_qk_g[0]

    qn_p, rows_p, win_p, rows_t_p, win_t_p = nsa_prep(proj_p, qk_g)
    qn_s, rows_s, win_s, rows_t_s, _ = nsa_prep(proj_s, qk_g)

    w_pairs = _w1_pairs(a_cmp_w1[0])
    pe = a_cmp_pe[0]
    w1 = a_cmp_w1[0].reshape(2, CMP_LEN, HD, HD)
    w2 = a_cmp_w2[0]
    cpp = PAGE // CMP_STRIDE

    pages_p = t // PAGE
    ident = jnp.arange(nbp * pages_p, dtype=jnp.int32).reshape(nbp, pages_p)
    ab_p = compress_stage_a(rows_t_p.reshape(nbp * pages_p, PAGE, 4 * A_KV, HD), ident, w_pairs,
                            min(PAGES_PER_STEP, pages_p))
    zero_tail = jnp.zeros((nbp, cpp, NCK, 2 * HD), F32)
    cmp_p = compress_stage_b(ab_p, zero_tail, pe, w1, w2, qk_g)

    nq = t // Q_BLOCK
    n_ch_p = t // CMP_STRIDE
    tsel, twin, tcmp = prompt_bias_tables(rel_bias, nq, n_ch_p)
    n_cmp_p = t // CMP_STRIDE - 1
    n_sel_p = -(-t // SEL_BLOCK)
    n_sel_pad_p = -(-n_sel_p // 16) * 16
    msel_p = _sel_matrix(n_cmp_p, n_ch_p, n_sel_pad_p).T
    expand_p = _expand_matrix(n_sel_pad_p, t).T
    gates_g = jnp.transpose(gl_p.reshape(nbp * t, 3, A_KV, A_HPG), (2, 1, 3, 0)).reshape(A_KV, 3 * A_HPG, nbp * t)
    mixed_p = nsa_prompt_attention(qn_p, cmp_p, rows_p, win_p, zt_p, gates_g,
                                   tsel, twin, tcmp, msel_p, expand_p, nbp, t)

    n_pool = cache_nsa_kv.shape[1]
    cache4d = cache_nsa_kv[0].reshape(n_pool, PAGE, 4 * A_KV, HD)
    n_pages = min(PAGES_PER_STEP, n_tab)
    ab_s = compress_stage_a(cache4d, page_table, w_pairs, n_pages)
    tail_rows = jnp.pad(rows_s.reshape(nbs, ts, 4 * kvw), ((0, 0), (0, PAGE - ts), (0, 0)))
    ident_s = jnp.arange(nbs, dtype=jnp.int32).reshape(nbs, 1)
    ab_tail = compress_stage_a(tail_rows.reshape(nbs, PAGE, 4 * A_KV, HD), ident_s, w_pairs, 1)
    cmp_s = compress_stage_b(ab_s, ab_tail, pe, w1, w2, qk_g)

    n_cmp_s = cmp_s.shape[1] // NCK
    n_steps = n_tab // n_pages
    bps = n_pages * PAGE // SEL_BLOCK
    n_sel_pad = (n_steps + 1) * bps
    nkeys = (n_steps + 1) * n_pages * PAGE
    nkw = WINDOW + LANE
    b_cmp, b_sel, b_win = sample_bias_rows(rel_bias, past_len, ts, n_cmp_s, nkeys, nkw)
    msel_s = _sel_matrix(n_cmp_s, n_cmp_s, n_sel_pad)
    expand_s = _expand_matrix(bps, n_pages * PAGE)
    q5 = qn_s.reshape(nbs, ts, A_KV, A_HPG, HD)
    eye = jnp.eye(A_KV, dtype=BF16)
    qbd = jnp.einsum('bqghd,ge->bghqed', q5, eye).reshape(nbs, A_HEADS * ts, kvw)
    win_new = win_s.reshape(nbs, ts, 2 * kvw)
    win_cache = cache_nsa_win[0].reshape(nbs, WINDOW, 2 * kvw)
    win_all = jnp.concatenate([win_cache, win_new], axis=1)
    win_pad = jnp.pad(win_all, ((0, 0), (0, nkw - WINDOW - ts), (0, 0)))
    tail_sel = tail_rows[:, :, 2 * kvw:]
    mixed_s = nsa_sample_attention(qbd, cmp_s, cache4d, page_table, tail_sel, win_pad, z_s, gl_s,
                                   b_cmp, b_sel, b_win, msel_s, expand_s, past_len, ts).astype(BF16)

    xs_flat = x_sample.reshape(nbs * ts, d)
    gate_s_rows = jnp.repeat(gt_s, ts, axis=0)
    x1p, x1s = project_residual(mixed_p, mixed_s, a_w_out[0], x_prompt.reshape(nbp * t, d), gt_p,
                                xs_flat, gate_s_rows, t)

    (sh_p, sc_p, gt_p), (sh_s, sc_s, gt_s) = mods(1)
    hp = norm_modulate(x1p.reshape(nbp, t, d), norm_g[1], sc_p, sh_p)
    hs = norm_modulate(x1s.reshape(nbs, ts, d), norm_g[1], sc_s, sh_s)
    rp, rs = project(hp, hs, r_w_in[0])
    op, ret_p = retention(rp, None, 0, nbp, t, state_shape=(nbp,) + state_ret.shape[2:])
    os_, ret_s = retention(rs, state_ret[0], past_len, nbs, ts)
    os_ = os_.astype(BF16)
    gate_s_rows = jnp.repeat(gt_s, ts, axis=0)
    x2p, x2s = project_residual(op, os_, r_w_out[0], x1p, gt_p, x1s, gate_s_rows, t)

    kv_p = rows_t_p.reshape(1, nbp, t, 4, A_KV, HD)
    kv_s = rows_t_s.reshape(1, nbs, ts, 4, A_KV, HD)
    wst_p = win_t_p.reshape(nbp, t, 2, A_KV, HD)[None, :, t - WINDOW:]
    wst_s = win_all[:, ts:].reshape(1, nbs, WINDOW, 2, A_KV, HD)
    return (x2p.reshape(nbp, t, d), x2s.reshape(nbs, ts, d), kv_p, kv_s, wst_p, wst_s,
            ret_p[None], ret_s[None])
```

```python
import functools
import math

import numpy as np
import jax
import jax.numpy as jnp
from jax import lax
from jax.experimental import pallas as pl
from jax.experimental.pallas import tpu as pltpu

EPS = 1e-6
HD = 128
A_KV = 4
A_HPG = 8
A_HEADS = A_KV * A_HPG
CMP_STRIDE = 16
CMP_LEN = 32
SEL_BLOCK = 64
SEL_RATIO = SEL_BLOCK // CMP_STRIDE
SEL_TOPK = 16
WINDOW = 512
Q_BLOCK = 128
FORCE_SCORE = 1e4
NEG = -1e30
TINY = 1e-30
N_BUCKETS = 32
MAX_DISTANCE = 1024
R_CHUNK = 128
ROPE_BASE = 10000.0
LOG2E = math.log2(math.e)
Q_SCALE = HD ** -0.5 * LOG2E
PAGE = 128
PAGES_PER_STEP = 16
RET_HEADS_PER_STEP = 8
SEL_KEY_TILE = 512
ROW_TILE = 256
ADA_COL_TILE = 512

LANE = 128
VMEM_LIMIT = 60 * 1024 * 1024

F32 = jnp.float32
BF16 = jnp.bfloat16


def _cparams(sem):
    return pltpu.CompilerParams(dimension_semantics=sem, vmem_limit_bytes=VMEM_LIMIT)


def _dot(a, b):
    return jnp.dot(a, b, preferred_element_type=F32)


def _dot_nt(a, b):
    return lax.dot_general(a, b, (((1,), (1,)), ((), ())), preferred_element_type=F32)


def _dot_tn(a, b):
    return lax.dot_general(a, b, (((0,), (0,)), ((), ())), preferred_element_type=F32)


def _silu(x):
    h = 0.5 * x
    return h + h * jnp.tanh(h)


def _sigmoid(x):
    return 0.5 + 0.5 * jnp.tanh(0.5 * x)


def _split3_dot(x, m_bf16):
    hi = x.astype(BF16)
    r1 = x - hi.astype(F32)
    mid = r1.astype(BF16)
    lo = (r1 - mid.astype(F32)).astype(BF16)
    return _dot(hi, m_bf16) + _dot(mid, m_bf16) + _dot(lo, m_bf16)


def _ada_kernel(c_ref, w_ref, b_ref, o_ref):
    a = _silu(c_ref[...]).astype(BF16)
    o_ref[...] = _dot(a, w_ref[...].astype(BF16)) + b_ref[...]


def ada_modulation(c_all, ada_w, ada_b):
    depth, d, n = ada_w.shape
    rows = c_all.shape[0]
    tn = ADA_COL_TILE
    return pl.pallas_call(
        _ada_kernel,
        grid=(depth, n // tn),
        in_specs=[
            pl.BlockSpec((rows, d), lambda l, j: (0, 0)),
            pl.BlockSpec((None, d, tn), lambda l, j: (l, 0, j)),
            pl.BlockSpec((None, 1, tn), lambda l, j: (l, 0, j)),
        ],
        out_specs=pl.BlockSpec((None, rows, tn), lambda l, j: (l, 0, j)),
        out_shape=jax.ShapeDtypeStruct((depth, rows, n), F32),
        compiler_params=_cparams(("arbitrary", "arbitrary")),
        name="ada_modulation",
    )(c_all, ada_w, ada_b.reshape(depth, 1, n))


def _norm_mod_kernel(x_ref, g_ref, sc_ref, sh_ref, o_ref):
    x = x_ref[...]
    y = x * lax.rsqrt(jnp.mean(x * x, axis=-1, keepdims=True) + EPS)
    y = y * g_ref[...]
    o_ref[...] = (y * (1.0 + sc_ref[...]) + sh_ref[...]).astype(o_ref.dtype)


def norm_modulate(x, g, scale, shift):
    b, t, d = x.shape
    tt = min(t, ROW_TILE)
    out = pl.pallas_call(
        _norm_mod_kernel,
        grid=(b, t // tt),
        in_specs=[
            pl.BlockSpec((None, tt, d), lambda i, j: (i, j, 0)),
            pl.BlockSpec((1, d), lambda i, j: (0, 0)),
            pl.BlockSpec((None, 1, d), lambda i, j: (i, 0, 0)),
            pl.BlockSpec((None, 1, d), lambda i, j: (i, 0, 0)),
        ],
        out_specs=pl.BlockSpec((None, tt, d), lambda i, j: (i, j, 0)),
        out_shape=jax.ShapeDtypeStruct((b, t, d), BF16),
        compiler_params=_cparams(("arbitrary", "arbitrary")),
        name="norm_modulate",
    )(x, g.reshape(1, d), scale.reshape(b, 1, d), shift.reshape(b, 1, d))
    return out.reshape(b * t, d)


def _stage_weight_tile(w_hbm, wf_ref, wb_ref, sem, *, w_is_nk, tn, j0):
    j = pl.program_id(0)

    def copy(jj):
        start = pl.multiple_of((j0 + jj) * tn, tn)
        src = w_hbm.at[pl.ds(start, tn), :] if w_is_nk else w_hbm.at[:, pl.ds(start, tn)]
        return pltpu.make_async_copy(src, wf_ref, sem)

    @pl.when(j == 0)
    def _():
        copy(0).start()

    copy(j).wait()
    wb_ref[...] = wf_ref[...].astype(BF16)

    @pl.when(j + 1 < pl.num_programs(0))
    def _():
        copy(j + 1).start()


def _proj_kernel(xp_ref, xs_ref, w_hbm, op_ref, os_ref, wf_ref, wb_ref, sem, *, stage, mm):
    @pl.when(pl.program_id(1) == 0)
    def _():
        stage(w_hbm, wf_ref, wb_ref, sem)
        os_ref[...] = mm(xs_ref[...], wb_ref[...])

    op_ref[...] = mm(xp_ref[...], wb_ref[...])


def _proj_gate_t_kernel(xp_ref, xs_ref, w_hbm, opt_ref, os_ref, wf_ref, wb_ref, sem, *, stage):
    @pl.when(pl.program_id(1) == 0)
    def _():
        stage(w_hbm, wf_ref, wb_ref, sem)
        os_ref[...] = _silu(_dot_nt(xs_ref[...], wb_ref[...]))

    opt_ref[...] = _silu(_dot_nt(wb_ref[...], xp_ref[...]))


def _weight_scratch(shape):
    return [pltpu.VMEM(shape, F32), pltpu.VMEM(shape, BF16), pltpu.SemaphoreType.DMA(())]


def project_gate_t(xp, xs, w_nk, row0, n_rows):
    mp, k = xp.shape
    ms = xs.shape[0]
    tm, tn = _proj_tiles(mp, k, n_rows)
    assert row0 % tn == 0
    stage = functools.partial(_stage_weight_tile, w_is_nk=True, tn=tn, j0=row0 // tn)
    return pl.pallas_call(
        functools.partial(_proj_gate_t_kernel, stage=stage),
        grid=(n_rows // tn, mp // tm),
        in_specs=[
            pl.BlockSpec((tm, k), lambda j, i: (i, 0)),
            pl.BlockSpec((ms, k), lambda j, i: (0, 0)),
            pl.BlockSpec(memory_space=pl.ANY),
        ],
        out_specs=[
            pl.BlockSpec((tn, tm), lambda j, i: (j, i)),
            pl.BlockSpec((ms, tn), lambda j, i: (0, j)),
        ],
        out_shape=[jax.ShapeDtypeStruct((n_rows, mp), F32), jax.ShapeDtypeStruct((ms, n_rows), F32)],
        scratch_shapes=_weight_scratch((tn, k)),
        compiler_params=_cparams(("arbitrary", "arbitrary")),
        name="project_gate_t",
    )(xp, xs, w_nk)


def _proj_res_kernel(xp_ref, xs_ref, w_hbm, rp_ref, gp_ref, rs_ref, gs_ref, op_ref, os_ref,
                     wf_ref, wb_ref, sem, *, stage):
    @pl.when(pl.program_id(1) == 0)
    def _():
        stage(w_hbm, wf_ref, wb_ref, sem)
        os_ref[...] = rs_ref[...] + gs_ref[...] * _dot(xs_ref[...], wb_ref[...])

    op_ref[...] = rp_ref[...] + gp_ref[...] * _dot(xp_ref[...], wb_ref[...])


def _proj_tiles(mp, k, n, residual=False):
    if k <= 4096:
        tm, tn = (1024, 512) if residual else (1024, 1024)
    else:
        tm, tn = 512, 512
    tm = min(tm, mp)
    while mp % tm:
        tm //= 2
    tn = min(tn, n)
    while n % tn:
        tn //= 2
    return tm, tn


def project(xp, xs, w, n_cols=None, w_is_nk=False):
    mp, k = xp.shape
    ms = xs.shape[0]
    n = w.shape[0 if w_is_nk else 1] if n_cols is None else n_cols
    tm, tn = _proj_tiles(mp, k, n)
    stage = functools.partial(_stage_weight_tile, w_is_nk=w_is_nk, tn=tn, j0=0)
    return pl.pallas_call(
        functools.partial(_proj_kernel, stage=stage, mm=_dot_nt if w_is_nk else _dot),
        grid=(n // tn, mp // tm),
        in_specs=[
            pl.BlockSpec((tm, k), lambda j, i: (i, 0)),
            pl.BlockSpec((ms, k), lambda j, i: (0, 0)),
            pl.BlockSpec(memory_space=pl.ANY),
        ],
        out_specs=[
            pl.BlockSpec((tm, tn), lambda j, i: (i, j)),
            pl.BlockSpec((ms, tn), lambda j, i: (0, j)),
        ],
        out_shape=[jax.ShapeDtypeStruct((mp, n), F32), jax.ShapeDtypeStruct((ms, n), F32)],
        scratch_shapes=_weight_scratch((tn, k) if w_is_nk else (k, tn)),
        compiler_params=_cparams(("arbitrary", "arbitrary")),
        name="project",
    )(xp, xs, w)


def project_residual(xp, xs, w, res_p, gate_p, res_s, gate_s, rows_per_batch):
    mp, k = xp.shape
    ms = xs.shape[0]
    n = w.shape[1]
    tm, tn = _proj_tiles(mp, k, n, residual=True)
    tm = min(tm, rows_per_batch)
    tpb = rows_per_batch // tm
    nb = gate_p.shape[0]
    stage = functools.partial(_stage_weight_tile, w_is_nk=False, tn=tn, j0=0)
    return pl.pallas_call(
        functools.partial(_proj_res_kernel, stage=stage),
        grid=(n // tn, mp // tm),
        in_specs=[
            pl.BlockSpec((tm, k), lambda j, i: (i, 0)),
            pl.BlockSpec((ms, k), lambda j, i: (0, 0)),
            pl.BlockSpec(memory_space=pl.ANY),
            pl.BlockSpec((tm, tn), lambda j, i: (i, j)),
            pl.BlockSpec((None, 1, tn), lambda j, i: (i // tpb, 0, j)),
            pl.BlockSpec((ms, tn), lambda j, i: (0, j)),
            pl.BlockSpec((ms, tn), lambda j, i: (0, j)),
        ],
        out_specs=[
            pl.BlockSpec((tm, tn), lambda j, i: (i, j)),
            pl.BlockSpec((ms, tn), lambda j, i: (0, j)),
        ],
        out_shape=[jax.ShapeDtypeStruct((mp, n), F32), jax.ShapeDtypeStruct((ms, n), F32)],
        scratch_shapes=_weight_scratch((k, tn)),
        compiler_params=_cparams(("arbitrary", "arbitrary")),
        name="project_residual",
    )(xp, xs, w, res_p, gate_p.reshape(nb, 1, n), res_s, gate_s)


def _nsa_prep_kernel(q_ref, a_ref, b_ref, c_ref, g_ref, qn_ref, rows_ref, win_ref, rows_t_ref, win_t_ref):
    def hnorm(x, g):
        return x * lax.rsqrt(jnp.mean(x * x, axis=-1, keepdims=True) + EPS) * g

    g = g_ref[...]
    for j in range(A_HEADS):
        sl = slice(j * HD, (j + 1) * HD)
        qn_ref[:, sl] = (hnorm(q_ref[:, sl], g[0:1]) * Q_SCALE).astype(qn_ref.dtype)
    kvw = A_KV * HD
    sls = [slice(j * HD, (j + 1) * HD) for j in range(2 * A_KV)]
    row_slots = ([a_ref[:, s] for s in sls] + [hnorm(b_ref[:, s], g[2:3]) for s in sls[:A_KV]]
                 + [b_ref[:, s] for s in sls[A_KV:]])
    win_slots = [hnorm(c_ref[:, s], g[3:4]) for s in sls[:A_KV]] + [c_ref[:, s] for s in sls[A_KV:]]
    for j, x in enumerate(row_slots):
        rows_ref[:, j * HD:(j + 1) * HD] = x
    for j, x in enumerate(win_slots):
        win_ref[:, j * HD:(j + 1) * HD] = x
    rows_t_ref[...] = jnp.swapaxes(jnp.stack(row_slots, axis=0), 0, 1)
    win_t_ref[...] = jnp.swapaxes(jnp.stack(win_slots, axis=0), 0, 1)


def nsa_prep(proj, qk_g):
    m = proj.shape[0]
    tm = min(m, ROW_TILE)
    qw = A_HEADS * HD
    kv2 = 2 * A_KV * HD
    base = qw // kv2
    return pl.pallas_call(
        _nsa_prep_kernel,
        grid=(m // tm,),
        in_specs=[
            pl.BlockSpec((tm, qw), lambda i: (i, 0)),
            pl.BlockSpec((tm, kv2), lambda i: (i, base)),
            pl.BlockSpec((tm, kv2), lambda i: (i, base + 1)),
            pl.BlockSpec((tm, kv2), lambda i: (i, base + 2)),
            pl.BlockSpec((4, HD), lambda i: (0, 0)),
        ],
        out_specs=[
            pl.BlockSpec((tm, qw), lambda i: (i, 0)),
            pl.BlockSpec((tm, 2 * kv2), lambda i: (i, 0)),
            pl.BlockSpec((tm, kv2), lambda i: (i, 0)),
            pl.BlockSpec((tm, 4 * A_KV, HD), lambda i: (i, 0, 0)),
            pl.BlockSpec((tm, 2 * A_KV, HD), lambda i: (i, 0, 0)),
        ],
        out_shape=[
            jax.ShapeDtypeStruct((m, qw), BF16),
            jax.ShapeDtypeStruct((m, 2 * kv2), F32),
            jax.ShapeDtypeStruct((m, kv2), F32),
            jax.ShapeDtypeStruct((m, 4 * A_KV, HD), F32),
            jax.ShapeDtypeStruct((m, 2 * A_KV, HD), F32),
        ],
        compiler_params=_cparams(("arbitrary",)),
        name="nsa_prep",
    )(proj, proj, proj, proj, qk_g)


NCK = 2 * A_KV


def _cmp_a_kernel(pt_ref, *refs, n_pages):
    del pt_ref
    pages = refs[:n_pages]
    w_ref = refs[n_pages]
    o_ref = refs[n_pages + 1]
    cpp = PAGE // CMP_STRIDE
    n_ch = n_pages * cpp

    def halves(tiles, lo):
        return [jnp.concatenate([tiles[a][lo:lo + A_KV], tiles[a + 1][lo:lo + A_KV]], axis=0)
                for a in range(0, len(tiles), 2)]

    acc = [None, None]
    for tp in range(CMP_STRIDE // 2):
        pieces = []
        for pg in pages:
            for ch in range(cpp):
                t0 = ch * CMP_STRIDE + 2 * tp
                pieces.append(jnp.concatenate([pg[t0], pg[t0 + 1]], axis=1))
        for kind in range(2):
            lhs = jnp.concatenate(halves(pieces, kind * A_KV), axis=0).astype(BF16)
            d = _dot(lhs, w_ref[tp, :, kind * 2 * HD:(kind + 1) * 2 * HD])
            acc[kind] = d if acc[kind] is None else acc[kind] + d
    for a in range(0, n_ch, 2):
        r = slice(a * A_KV, (a + 2) * A_KV)
        kk, vv = acc[0][r], acc[1][r]
        o_ref[a] = jnp.concatenate([kk[:A_KV], vv[:A_KV]], axis=0)
        o_ref[a + 1] = jnp.concatenate([kk[A_KV:], vv[A_KV:]], axis=0)


def compress_stage_a(store, page_table, w_pairs, n_pages):
    nb, n_tab = page_table.shape
    steps = n_tab // n_pages
    cpp = PAGE // CMP_STRIDE

    def page_spec(k):
        return pl.BlockSpec((None, PAGE, NCK, HD), lambda b, s, pt: (pt[b, s * n_pages + k], 0, 0, 0))

    grid_spec = pltpu.PrefetchScalarGridSpec(
        num_scalar_prefetch=1,
        grid=(nb, steps),
        in_specs=[page_spec(k) for k in range(n_pages)]
        + [pl.BlockSpec(w_pairs.shape, lambda b, s, pt: (0, 0, 0))],
        out_specs=pl.BlockSpec((None, n_pages * cpp, NCK, 2 * HD), lambda b, s, pt: (b, s, 0, 0)),
    )
    return pl.pallas_call(
        functools.partial(_cmp_a_kernel, n_pages=n_pages),
        grid_spec=grid_spec,
        out_shape=jax.ShapeDtypeStruct((nb, n_tab * cpp, NCK, 2 * HD), F32),
        compiler_params=_cparams(("arbitrary", "arbitrary")),
        name="compress_stage_a",
    )(page_table, *([store] * n_pages), w_pairs)


def _cmp_b_kernel(ab_ref, nx_ref, tail_ref, pe_ref, w1_ref, w2_ref, g_ref, o_ref, const_ref):
    j = pl.program_id(1)
    nch = ab_ref.shape[0]
    rows = nch * NCK

    @pl.when((pl.program_id(0) == 0) & (j == 0))
    def _():
        for kind in range(2):
            const = jnp.zeros((8, HD), F32)
            for t in range(CMP_LEN):
                row = jnp.broadcast_to(pe_ref[kind, t:t + 1, :], (8, HD)).astype(BF16)
                const = const + _dot(row, w1_ref[kind, t].astype(BF16))
            const_ref[kind * A_KV:(kind + 1) * A_KV, :] = const[:A_KV]

    ab = ab_ref[...].reshape(rows, 2 * HD)
    nxt = pltpu.roll(ab[:, HD:], rows - NCK, 0).reshape(nch, NCK, HD)
    first_next = jnp.where(j == pl.num_programs(1) - 1, tail_ref[0, :, HD:], nx_ref[0, :, HD:])
    cid = lax.broadcasted_iota(jnp.int32, (nch, NCK, HD), 0)
    nxt = jnp.where(cid == nch - 1, first_next[None], nxt)
    pre = ab[:, :HD].reshape(nch, NCK, HD) + nxt + const_ref[...][None]
    act = _silu(pre).reshape(rows, HD).astype(BF16)
    w2 = jnp.concatenate([w2_ref[0], w2_ref[1]], axis=1).astype(BF16)
    y2 = _dot(act, w2)
    is_k = (lax.broadcasted_iota(jnp.int32, (rows, HD), 0) & (NCK - 1)) < A_KV
    y = jnp.where(is_k, y2[:, :HD], y2[:, HD:])
    yn = y * lax.rsqrt(jnp.mean(y * y, axis=-1, keepdims=True) + EPS) * g_ref[1:2]
    o_ref[...] = jnp.where(is_k, yn, y).reshape(nch, NCK, HD)


def compress_stage_b(ab, tail, pe, w1, w2, qk_g):
    nb, n_ch = ab.shape[:2]
    tc = min(ROW_TILE, n_ch)
    assert n_ch % tc == 0
    nt = n_ch // tc
    out = pl.pallas_call(
        _cmp_b_kernel,
        grid=(nb, nt),
        in_specs=[
            pl.BlockSpec((None, tc, NCK, 2 * HD), lambda b, j: (b, j, 0, 0)),
            pl.BlockSpec((None, 1, NCK, 2 * HD), lambda b, j: (b, jnp.minimum((j + 1) * tc, n_ch - 1), 0, 0)),
            pl.BlockSpec((None, 1, NCK, 2 * HD), lambda b, j: (b, 0, 0, 0)),
            pl.BlockSpec(pe.shape, lambda b, j: (0, 0, 0)),
            pl.BlockSpec(w1.shape, lambda b, j: (0, 0, 0, 0)),
            pl.BlockSpec(w2.shape, lambda b, j: (0, 0, 0)),
            pl.BlockSpec((4, HD), lambda b, j: (0, 0)),
        ],
        out_specs=pl.BlockSpec((None, tc, NCK, HD), lambda b, j: (b, j, 0, 0)),
        out_shape=jax.ShapeDtypeStruct((nb, n_ch, NCK, HD), F32),
        scratch_shapes=[pltpu.VMEM((NCK, HD), F32)],
        compiler_params=_cparams(("arbitrary", "arbitrary")),
        name="compress_stage_b",
    )(ab, ab, tail, pe, w1, w2, qk_g)
    return out.reshape(nb, n_ch * NCK, HD)


def _cmp_rows(cmp_ref, slot, n):
    return cmp_ref[pl.ds(slot, n, stride=NCK), :]


def _select_topk(score, blk, k_top):
    sel = jnp.zeros(score.shape, F32)
    for _ in range(k_top):
        m = jnp.max(score, axis=-1, keepdims=True)
        cand = jnp.where(score == m, blk, jnp.int32(1 << 30))
        first = jnp.min(cand, axis=-1, keepdims=True)
        pick = blk == first
        sel = jnp.where(pick, 1.0, sel)
        score = jnp.where(pick, -jnp.inf, score)
    return sel


def _select_topk_t(score, blk, k_top, n_blk):
    rank = jnp.zeros(score.shape, F32)
    for i in range(n_blk):
        row = score[i:i + 1, :]
        tie = jnp.where(blk > i, 1.0, 0.0)
        rank = rank + jnp.where(row > score, 1.0, jnp.where(row == score, tie, 0.0))
    return jnp.where(rank < k_top, 1.0, 0.0)


def _bias_select(dist, dmin, dmax, value_of):
    lo = int(_t5_bucket_np(np.array([max(dmin, 0)]))[0])
    hi = int(_t5_bucket_np(np.array([max(dmax, 0)]))[0])
    out = jnp.full(dist.shape, value_of(lo), F32)
    for b in range(lo + 1, hi + 1):
        out = jnp.where(dist >= _BUCKET_START[b], value_of(b), out)
    return out * LOG2E


def _bias_tables_kernel(rb_ref, tsel_ref, twin_ref, tcmp_ref):
    head = pl.program_id(0)
    value_of = lambda b: rb_ref[b, head]
    n_off, nq, nc = tsel_ref.shape[0], tcmp_ref.shape[0], tcmp_ref.shape[1]
    kl = lax.broadcasted_iota(jnp.int32, (LANE, Q_BLOCK), 0)
    ql = lax.broadcasted_iota(jnp.int32, (LANE, Q_BLOCK), 1)
    for o in range(n_off):
        base = (o - 1) * Q_BLOCK
        dist = base + ql - kl
        tile = _bias_select(dist, base - (LANE - 1), base + Q_BLOCK - 1, value_of)
        tsel_ref[o] = tile
        if o < twin_ref.shape[0]:
            twin_ref[o] = tile + jnp.where((dist >= 0) & (dist < WINDOW), 0.0, NEG)
    step = Q_BLOCK // CMP_STRIDE
    rows = nc + (nq - 1) * step
    base = (nq - 1) * Q_BLOCK
    cend = lax.broadcasted_iota(jnp.int32, (rows, Q_BLOCK), 0) * CMP_STRIDE + (CMP_LEN - 1)
    qc = lax.broadcasted_iota(jnp.int32, (rows, Q_BLOCK), 1)
    tall = _bias_select(base + qc - cend, base - ((rows - 1) * CMP_STRIDE + CMP_LEN - 1),
                        base + Q_BLOCK - 1, value_of)
    for i in range(nq):
        r0 = (nq - 1 - i) * step
        tcmp_ref[i] = tall[r0:r0 + nc]


def prompt_bias_tables(rel_bias, nq, n_cmp):
    n_win = min(WINDOW // Q_BLOCK + 3, nq + 1)
    return pl.pallas_call(
        _bias_tables_kernel,
        grid=(A_HEADS,),
        in_specs=[pl.BlockSpec(memory_space=pltpu.SMEM)],
        out_specs=[
            pl.BlockSpec((None, nq + 1, LANE, Q_BLOCK), lambda h: (h // A_HPG, 0, 0, h % A_HPG)),
            pl.BlockSpec((None, n_win, LANE, Q_BLOCK), lambda h: (h // A_HPG, 0, 0, h % A_HPG)),
            pl.BlockSpec((None, nq, n_cmp, Q_BLOCK), lambda h: (h // A_HPG, 0, 0, h % A_HPG)),
        ],
        out_shape=[
            jax.ShapeDtypeStruct((A_KV, nq + 1, LANE, A_HPG * Q_BLOCK), F32),
            jax.ShapeDtypeStruct((A_KV, n_win, LANE, A_HPG * Q_BLOCK), F32),
            jax.ShapeDtypeStruct((A_KV, nq, n_cmp, A_HPG * Q_BLOCK), F32),
        ],
        compiler_params=_cparams(("arbitrary",)),
        name="prompt_bias_tables",
    )(rel_bias)


def _bias_rows_kernel(rb_ref, cmp_ref, sel_ref, win_ref, *, past_len, n_q):
    head = pl.program_id(0)
    value_of = lambda b: rb_ref[b, head]

    def fill(ref, key_pos, kmin, kmax, q0):
        q = lax.broadcasted_iota(jnp.int32, ref.shape, 0)
        k = lax.broadcasted_iota(jnp.int32, ref.shape, 1)
        ref[...] = _bias_select(q0 + q - key_pos(k), q0 - kmax, q0 + n_q - 1 - kmin, value_of)

    nc, nk, nw = cmp_ref.shape[1], sel_ref.shape[1], win_ref.shape[1]
    fill(cmp_ref, lambda k: k * CMP_STRIDE + (CMP_LEN - 1), CMP_LEN - 1, (nc - 1) * CMP_STRIDE + CMP_LEN - 1, past_len)
    fill(sel_ref, lambda k: k, 0, nk - 1, past_len)
    fill(win_ref, lambda k: k, 0, nw - 1, WINDOW)


def sample_bias_rows(rel_bias, past_len, n_q, n_cmp, n_keys, n_win):
    shapes = [(A_HEADS * n_q, n) for n in (n_cmp, n_keys, n_win)]
    return pl.pallas_call(
        functools.partial(_bias_rows_kernel, past_len=past_len, n_q=n_q),
        grid=(A_HEADS,),
        in_specs=[pl.BlockSpec(memory_space=pltpu.SMEM)],
        out_specs=[pl.BlockSpec((n_q, s[1]), lambda h: (h, 0)) for s in shapes],
        out_shape=[jax.ShapeDtypeStruct(s, F32) for s in shapes],
        compiler_params=_cparams(("arbitrary",)),
        name="sample_bias_rows",
    )(rel_bias)


def _nsa_prompt_kernel(q_ref, cmp_ref, ks_ref, vs_ref, kw_ref, vw_ref, zc_ref, zs_ref, zw_ref,
                       gt_ref, tsel_ref, twin_ref, tcmp_ref, msel_ref, exp_ref, o_ref,
                       qa_ref, acc_ref, m_ref, l_ref, acc2_ref, m2_ref, l2_ref, mix_ref,
                       *, n_sel_pad, n_sel, k_top, kt):
    i = pl.program_id(2)
    qb = Q_BLOCK
    tpk = kt // LANE
    nc = cmp_ref.shape[0] // NCK
    gates = _sigmoid(gt_ref[...])
    hs = [slice(h * HD, (h + 1) * HD) for h in range(A_HPG)]
    for h in range(A_HPG):
        qa_ref[hs[h], :] = q_ref[:, hs[h]]
    qa = qa_ref[...]

    def heads(x):
        return jnp.concatenate([x] * A_HPG, axis=1)

    def gate_row(branch):
        return jnp.concatenate([gates[branch * A_HPG + h: branch * A_HPG + h + 1, :] for h in range(A_HPG)], axis=1)

    def emit(branch, o_t, first):
        z_ref = (zc_ref, zs_ref, zw_ref)[branch]
        for h in range(A_HPG):
            val = o_t[:, hs[h]] * z_ref[hs[h], :]
            if first:
                mix_ref[hs[h], :] = val
            else:
                mix_ref[hs[h], :] = mix_ref[hs[h], :] + val

    sel_state, win_state = (m_ref, l_ref, acc_ref), (m2_ref, l2_ref, acc2_ref)

    def reset(state):
        m, l, acc = state
        m[...] = jnp.full(m.shape, NEG, F32)
        l[...] = jnp.zeros(l.shape, F32)
        acc[...] = jnp.zeros(acc.shape, F32)

    def tile_update(state, k, v, bias, madd):
        m, l, acc = state
        s = _dot_nt(k, qa) + bias
        if madd is not None:
            s = s + heads(madd)
        m_old = m[...]
        m_new = jnp.maximum(m_old, jnp.max(s, axis=0, keepdims=True))
        alpha = jnp.exp2(m_old - m_new)
        p = jnp.exp2(s - m_new)
        l[...] = alpha * l[...] + jnp.sum(p, axis=0, keepdims=True)
        m[...] = m_new
        acc[...] = acc[...] * alpha + _dot_tn(v, p.astype(BF16))

    def finish(branch, state):
        m, l, acc = state
        emit(branch, acc[...] * (gate_row(branch) / l[...]), False)

    wt = 2 * LANE
    first = i // 2
    lowest = jnp.maximum((i - WINDOW // qb) // 2, 0)

    def win_body(t, carry):
        tile = first - t
        k0 = pl.multiple_of(tile * wt, wt)
        k = kw_ref[pl.ds(k0, wt), :].astype(BF16)
        v = vw_ref[pl.ds(k0, wt), :].astype(BF16)
        off = i - 2 * tile
        bias = jnp.concatenate([twin_ref[off + 1], twin_ref[off]], axis=0)
        tile_update(win_state, k, v, bias, None)
        return carry

    reset(win_state)
    win_body(0, 0)

    grp = pl.program_id(0)
    kc = _cmp_rows(cmp_ref, grp, nc).astype(BF16)
    vc = _cmp_rows(cmp_ref, A_KV + grp, nc).astype(BF16)
    cend = lax.broadcasted_iota(jnp.int32, (nc, qb), 0) * CMP_STRIDE + (CMP_LEN - 1)
    qpos_c = i * qb + lax.broadcasted_iota(jnp.int32, (nc, qb), 1)
    cvalid = heads(cend <= qpos_c)
    s = jnp.where(cvalid, _dot_nt(kc, qa) + tcmp_ref[...], NEG)
    mx = jnp.max(s, axis=0, keepdims=True)
    e = jnp.where(cvalid, jnp.exp2(s - mx), 0.0)
    p = e * (1.0 / jnp.maximum(jnp.sum(e, axis=0, keepdims=True), TINY))
    imp = p[:, hs[0]]
    for h in range(1, A_HPG):
        imp = imp + p[:, hs[h]]
    emit(0, _dot_tn(vc, p.astype(BF16)) * gate_row(0), True)

    hi = imp.astype(BF16)
    r1 = imp - hi.astype(F32)
    mid = r1.astype(BF16)
    lo = (r1 - mid.astype(F32)).astype(BF16)
    msel = msel_ref[...]
    imp_sel = _dot(msel, hi) + _dot(msel, mid) + _dot(msel, lo)
    blk = lax.broadcasted_iota(jnp.int32, (n_sel_pad, qb), 0)
    qpos = i * qb + lax.broadcasted_iota(jnp.int32, (n_sel_pad, qb), 1)
    cur = jnp.right_shift(qpos, int(math.log2(SEL_BLOCK)))
    forced = (blk == 0) | (blk == cur) | (blk == cur - 1)
    valid = blk * SEL_BLOCK <= qpos
    score = jnp.where(valid, jnp.where(forced, FORCE_SCORE, imp_sel), -1.0)
    score = jnp.where(blk < n_sel, score, -jnp.inf)
    sel = _select_topk_t(score, blk, k_top, n_sel).astype(BF16)
    kpos2 = lax.broadcasted_iota(jnp.int32, (kt, qb), 0)
    qpos2 = i * qb + lax.broadcasted_iota(jnp.int32, (kt, qb), 1)

    def sel_madd(jj):
        k0 = pl.multiple_of(jj * kt, kt)
        hit = _dot(exp_ref[pl.ds(k0, kt), :], sel)
        return jnp.where((hit > 0.5) & ((k0 + kpos2) <= qpos2), 0.0, NEG)

    reset(sel_state)

    def sel_body(jj, carry):
        k0 = pl.multiple_of(jj * kt, kt)
        k = ks_ref[pl.ds(k0, kt), :].astype(BF16)
        v = vs_ref[pl.ds(k0, kt), :].astype(BF16)
        off = i - tpk * jj
        bias = jnp.concatenate([tsel_ref[jnp.maximum(off + 1 - r, 0)] for r in range(tpk)], axis=0)
        tile_update(sel_state, k, v, bias, sel_madd(jj))
        return carry

    lax.fori_loop(0, i // tpk + 1, sel_body, 0)
    finish(1, sel_state)

    lax.fori_loop(1, first - lowest + 1, win_body, 0)
    finish(2, win_state)
    for h in range(A_HPG):
        o_ref[:, hs[h]] = mix_ref[hs[h], :].T.astype(o_ref.dtype)


def nsa_prompt_attention(qn, cmp_kv, rows, win, z_t, gates_g, tsel, twin, tcmp, msel, expand, nb, t):
    nq = t // Q_BLOCK
    qw = A_HPG * HD
    n_sel = -(-t // SEL_BLOCK)
    k_top = min(SEL_TOPK, n_sel)
    kt = SEL_KEY_TILE if t % SEL_KEY_TILE == 0 else 2 * LANE
    n_cmp = cmp_kv.shape[1] // NCK
    n_sel_pad = msel.shape[0]
    kernel = functools.partial(_nsa_prompt_kernel, n_sel_pad=n_sel_pad, n_sel=n_sel, k_top=k_top, kt=kt)
    return pl.pallas_call(
        kernel,
        grid=(A_KV, nb, nq),
        in_specs=[
            pl.BlockSpec((Q_BLOCK, qw), lambda g, b, i: (b * nq + i, g)),
            pl.BlockSpec((None, n_cmp * NCK, HD), lambda g, b, i: (b, 0, 0)),
            pl.BlockSpec((t, HD), lambda g, b, i: (b, 2 * A_KV + g)),
            pl.BlockSpec((t, HD), lambda g, b, i: (b, 3 * A_KV + g)),
            pl.BlockSpec((t, HD), lambda g, b, i: (b, g)),
            pl.BlockSpec((t, HD), lambda g, b, i: (b, A_KV + g)),
            pl.BlockSpec((qw, Q_BLOCK), lambda g, b, i: (g, b * nq + i)),
            pl.BlockSpec((qw, Q_BLOCK), lambda g, b, i: (A_KV + g, b * nq + i)),
            pl.BlockSpec((qw, Q_BLOCK), lambda g, b, i: (2 * A_KV + g, b * nq + i)),
            pl.BlockSpec((None, 3 * A_HPG, Q_BLOCK), lambda g, b, i: (g, 0, b * nq + i)),
            pl.BlockSpec((None, nq + 1, LANE, qw), lambda g, b, i: (g, 0, 0, 0)),
            pl.BlockSpec((None, twin.shape[1], LANE, qw), lambda g, b, i: (g, 0, 0, 0)),
            pl.BlockSpec((None, None, n_cmp, qw), lambda g, b, i: (g, i, 0, 0)),
            pl.BlockSpec((n_sel_pad, n_cmp), lambda g, b, i: (0, 0)),
            pl.BlockSpec((t, n_sel_pad), lambda g, b, i: (0, 0)),
        ],
        out_specs=pl.BlockSpec((Q_BLOCK, qw), lambda g, b, i: (b * nq + i, g)),
        out_shape=jax.ShapeDtypeStruct((nb * t, A_HEADS * HD), BF16),
        scratch_shapes=[
            pltpu.VMEM((qw, HD), BF16),
            pltpu.VMEM((HD, qw), F32),
            pltpu.VMEM((1, qw), F32),
            pltpu.VMEM((1, qw), F32),
            pltpu.VMEM((HD, qw), F32),
            pltpu.VMEM((1, qw), F32),
            pltpu.VMEM((1, qw), F32),
            pltpu.VMEM((qw, Q_BLOCK), F32),
        ],
        compiler_params=_cparams(("arbitrary", "arbitrary", "arbitrary")),
        name="nsa_prompt_attention",
    )(qn, cmp_kv, rows, rows, win, win, z_t, z_t, z_t, gates_g, tsel, twin, tcmp, msel, expand)


def _diag_blocks(o_full, rows_per_group):
    return jnp.concatenate(
        [o_full[g * rows_per_group:(g + 1) * rows_per_group, g * HD:(g + 1) * HD] for g in range(A_KV)], axis=0)


def _nsa_sample_cmp_kernel(q_ref, cmp_ref, bias_ref, msel_ref, oc_ref, sel_ref,
                           *, past_len, n_sel, k_top, n_q, blocks_per_step):
    n_cmp = cmp_ref.shape[0] // NCK
    kc = jnp.concatenate([_cmp_rows(cmp_ref, g, n_cmp) for g in range(A_KV)], axis=1).astype(BF16)
    vc = jnp.concatenate([_cmp_rows(cmp_ref, A_KV + g, n_cmp) for g in range(A_KV)], axis=1).astype(BF16)
    rows = q_ref.shape[0]
    rpg = A_HPG * n_q
    q = q_ref[...]
    s = _dot_nt(q, kc) + bias_ref[...]
    rowi = lax.broadcasted_iota(jnp.int32, (rows, n_cmp), 0)
    ci = lax.broadcasted_iota(jnp.int32, (rows, n_cmp), 1)
    qpos = past_len + (rowi & (n_q - 1))
    valid = (ci * CMP_STRIDE + (CMP_LEN - 1)) <= qpos
    s = jnp.where(valid, s, NEG)
    mx = jnp.max(s, axis=-1, keepdims=True)
    e = jnp.where(valid, jnp.exp2(s - mx), 0.0)
    p = e / jnp.maximum(jnp.sum(e, axis=-1, keepdims=True), TINY)
    oc_ref[...] = _diag_blocks(_dot(p.astype(BF16), vc), rpg)
    imps = []
    for g in range(A_KV):
        acc = p[g * rpg: g * rpg + n_q]
        for h in range(1, A_HPG):
            acc = acc + p[g * rpg + h * n_q: g * rpg + (h + 1) * n_q]
        imps.append(acc)
    imp = jnp.concatenate(imps, axis=0)
    imp_sel = _split3_dot(imp, msel_ref[...])
    shp = imp_sel.shape
    blk = lax.broadcasted_iota(jnp.int32, shp, 1)
    qp = past_len + (lax.broadcasted_iota(jnp.int32, shp, 0) & (n_q - 1))
    cur = jnp.right_shift(qp, int(math.log2(SEL_BLOCK)))
    forced = (blk == 0) | (blk == cur) | (blk == cur - 1)
    ok = blk * SEL_BLOCK <= qp
    score = jnp.where(ok, jnp.where(forced, FORCE_SCORE, imp_sel), -1.0)
    score = jnp.where(blk < n_sel, score, -jnp.inf)
    sel = _select_topk(score, blk, k_top)
    for st in range(sel_ref.shape[0]):
        sel_ref[st] = sel[:, st * blocks_per_step:(st + 1) * blocks_per_step]


def _nsa_sample_sel_kernel(pt_ref, *refs, n_pages, n_steps, past_len, n_q):
    del pt_ref
    pages = refs[:n_pages]
    q_ref, tail_ref, bias_ref, sel_ref, exp_ref, o_ref, m_ref, l_ref, acc_ref = refs[n_pages:]
    s_id = pl.program_id(1)
    kvw = A_KV * HD
    rows = q_ref.shape[0]
    rpg = A_HPG * n_q

    @pl.when(s_id == 0)
    def _():
        m_ref[...] = jnp.full(m_ref.shape, NEG, F32)
        l_ref[...] = jnp.zeros(l_ref.shape, F32)
        acc_ref[...] = jnp.zeros(acc_ref.shape, F32)

    def update(k, v, nk, key0):
        hit = _dot(sel_ref[...].astype(BF16), exp_ref[:, :nk])
        kpos = key0 + lax.broadcasted_iota(jnp.int32, hit.shape, 1)
        qpos = past_len + (lax.broadcasted_iota(jnp.int32, hit.shape, 0) & (n_q - 1))
        madd = jnp.where((hit > 0.5) & (kpos <= qpos), 0.0, NEG)
        madd = jnp.concatenate(
            [madd[g * n_q:(g + 1) * n_q] for g in range(A_KV) for _ in range(A_HPG)], axis=0)
        s = _dot_nt(q_ref[...], k) + bias_ref[:, :nk] + madd
        m_old = m_ref[...]
        m_new = jnp.maximum(m_old, jnp.max(s, axis=-1, keepdims=True))
        alpha = jnp.exp2(m_old - m_new)
        p = jnp.exp2(s - m_new)
        l_ref[...] = alpha * l_ref[...] + jnp.sum(p, axis=-1, keepdims=True)
        acc_ref[...] = alpha * acc_ref[...] + _dot(p.astype(BF16), v)
        m_ref[...] = m_new

    @pl.when(s_id < n_steps)
    def _():
        by_slot = [jnp.swapaxes(pg[...].astype(BF16), 0, 1) for pg in pages]

        def gather(c0):
            cols = [jnp.concatenate([x[c0 + g] for x in by_slot], axis=0) for g in range(A_KV)]
            return jnp.concatenate(cols, axis=1)

        update(gather(0), gather(A_KV), n_pages * PAGE, s_id * (n_pages * PAGE))

    @pl.when(s_id == n_steps)
    def _():
        update(tail_ref[:, :kvw].astype(BF16), tail_ref[:, kvw:].astype(BF16), PAGE, past_len)
        o_ref[...] = _diag_blocks(acc_ref[...] / l_ref[...], rpg)


def _nsa_sample_mix_kernel(q_ref, win_ref, bias_ref, oc_ref, os_ref, z_ref, gt_ref, o_ref, *, n_q):
    kvw = A_KV * HD
    rows = q_ref.shape[0]
    rpg = A_HPG * n_q
    nk = win_ref.shape[0]
    by_slot = jnp.swapaxes(win_ref[...].astype(BF16), 0, 1)
    wk = jnp.concatenate([by_slot[g] for g in range(A_KV)], axis=1)
    wv = jnp.concatenate([by_slot[A_KV + g] for g in range(A_KV)], axis=1)
    s = _dot_nt(q_ref[...], wk) + bias_ref[...]
    qi = lax.broadcasted_iota(jnp.int32, (rows, nk), 0) & (n_q - 1)
    ki = lax.broadcasted_iota(jnp.int32, (rows, nk), 1)
    dist = qi + WINDOW - ki
    s = s + jnp.where((dist >= 0) & (dist < WINDOW), 0.0, NEG)
    mx = jnp.max(s, axis=-1, keepdims=True)
    e = jnp.exp2(s - mx)
    p = e / jnp.sum(e, axis=-1, keepdims=True)
    ow = _diag_blocks(_dot(p.astype(BF16), wv), rpg)
    gates = _sigmoid(gt_ref[...])
    width = A_HEADS * HD
    for hh in range(A_HEADS):
        r = slice(hh * n_q, (hh + 1) * n_q)
        c = slice(hh * HD, (hh + 1) * HD)
        val = None
        for br, o in enumerate((oc_ref, os_ref, ow)):
            zc = slice(br * width + hh * HD, br * width + (hh + 1) * HD)
            gcol = gates[:, br * A_HEADS + hh: br * A_HEADS + hh + 1]
            term = gcol * o[r, :] * z_ref[:, zc]
            val = term if val is None else val + term
        o_ref[:, c] = val.astype(o_ref.dtype)


def nsa_sample_attention(qbd, cmp_kv, cache4d, page_table, tail_sel, win_all, z_s, gl_s,
                         b_cmp, b_sel, b_win, msel, expand, past_len, n_q):
    nb, rows, kvw = qbd.shape
    n_cmp = cmp_kv.shape[1] // NCK
    n_tab = page_table.shape[1]
    n_pages = min(PAGES_PER_STEP, n_tab)
    n_steps = n_tab // n_pages
    bps = n_pages * PAGE // SEL_BLOCK
    n_sel = -(-(past_len + n_q) // SEL_BLOCK)
    k_top = min(SEL_TOPK, n_sel)
    gq = A_KV * n_q

    oc, selmask = pl.pallas_call(
        functools.partial(_nsa_sample_cmp_kernel, past_len=past_len, n_sel=n_sel, k_top=k_top,
                          n_q=n_q, blocks_per_step=bps),
        grid=(nb,),
        in_specs=[
            pl.BlockSpec((None, rows, kvw), lambda b: (b, 0, 0)),
            pl.BlockSpec((None, n_cmp * NCK, HD), lambda b: (b, 0, 0)),
            pl.BlockSpec(b_cmp.shape, lambda b: (0, 0)),
            pl.BlockSpec(msel.shape, lambda b: (0, 0)),
        ],
        out_specs=[
            pl.BlockSpec((None, rows, HD), lambda b: (b, 0, 0)),
            pl.BlockSpec((None, n_steps + 1, gq, bps), lambda b: (b, 0, 0, 0)),
        ],
        out_shape=[
            jax.ShapeDtypeStruct((nb, rows, HD), F32),
            jax.ShapeDtypeStruct((nb, n_steps + 1, gq, bps), F32),
        ],
        compiler_params=_cparams(("arbitrary",)),
        name="nsa_sample_cmp",
    )(qbd, cmp_kv, b_cmp, msel)

    half = 2 * kvw
    last = n_steps - 1

    def page_spec(k):
        return pl.BlockSpec(
            (None, PAGE, NCK, HD),
            lambda b, s, pt: (pt[b, jnp.minimum(s, last) * n_pages + k], 0, 1, 0))

    nkeys = n_pages * PAGE
    grid_spec = pltpu.PrefetchScalarGridSpec(
        num_scalar_prefetch=1,
        grid=(nb, n_steps + 1),
        in_specs=[page_spec(k) for k in range(n_pages)] + [
            pl.BlockSpec((None, rows, kvw), lambda b, s, pt: (b, 0, 0)),
            pl.BlockSpec((None, PAGE, half), lambda b, s, pt: (b, 0, 0)),
            pl.BlockSpec((rows, nkeys), lambda b, s, pt: (0, s)),
            pl.BlockSpec((None, None, gq, bps), lambda b, s, pt: (b, s, 0, 0)),
            pl.BlockSpec(expand.shape, lambda b, s, pt: (0, 0)),
        ],
        out_specs=pl.BlockSpec((None, rows, HD), lambda b, s, pt: (b, 0, 0)),
        scratch_shapes=[
            pltpu.VMEM((rows, 1), F32),
            pltpu.VMEM((rows, 1), F32),
            pltpu.VMEM((rows, kvw), F32),
        ],
    )
    osel = pl.pallas_call(
        functools.partial(_nsa_sample_sel_kernel, n_pages=n_pages, n_steps=n_steps,
                          past_len=past_len, n_q=n_q),
        grid_spec=grid_spec,
        out_shape=jax.ShapeDtypeStruct((nb, rows, HD), F32),
        compiler_params=_cparams(("arbitrary", "arbitrary")),
        name="nsa_sample_sel",
    )(page_table, *([cache4d] * n_pages), qbd, tail_sel, b_sel, selmask, expand)

    nkw = win_all.shape[1]
    zw = z_s.shape[1]
    return pl.pallas_call(
        functools.partial(_nsa_sample_mix_kernel, n_q=n_q),
        grid=(nb,),
        in_specs=[
            pl.BlockSpec((None, rows, kvw), lambda b: (b, 0, 0)),
            pl.BlockSpec((None, nkw, 2 * A_KV, HD), lambda b: (b, 0, 0, 0)),
            pl.BlockSpec(b_win.shape, lambda b: (0, 0)),
            pl.BlockSpec((None, rows, HD), lambda b: (b, 0, 0)),
            pl.BlockSpec((None, rows, HD), lambda b: (b, 0, 0)),
            pl.BlockSpec((n_q, zw), lambda b: (b, 0)),
            pl.BlockSpec((n_q, 3 * A_HEADS), lambda b: (b, 0)),
        ],
        out_specs=pl.BlockSpec((n_q, A_HEADS * HD), lambda b: (b, 0)),
        out_shape=jax.ShapeDtypeStruct((nb * n_q, A_HEADS * HD), F32),
        compiler_params=_cparams(("arbitrary",)),
        name="nsa_sample_mix",
    )(qbd, win_all, b_win, oc, osel, z_s, gl_s)


def _retention_kernel(q_ref, k_ref, v_ref, g_ref, cos_ref, sin_ref, dm_ref, qd_ref, kd_ref, cd_ref,
                      *rest, dk, dv, hps):
    s0_ref, o_ref, s_ref = rest if len(rest) == 3 else (None,) + tuple(rest)
    c = pl.program_id(2)
    half = dk // 2

    @pl.when(c == 0)
    def _():
        s_ref[...] = jnp.zeros(s_ref.shape, F32) if s0_ref is None else s0_ref[...]

    cos = cos_ref[...]
    sin = sin_ref[...]

    def rot(x):
        x1, x2 = x[:, :half], x[:, half:]
        return jnp.concatenate([x1 * cos - x2 * sin, x1 * sin + x2 * cos], axis=1)

    for j in range(hps):
        ks, vs = slice(j * dk, (j + 1) * dk), slice(j * dv, (j + 1) * dv)
        q = rot(q_ref[:, ks])
        k = rot(k_ref[:, ks]) * (dk ** -0.5)
        v = v_ref[:, vs].astype(BF16)
        qb = q.astype(BF16)
        att = _dot_nt(qb, k.astype(BF16)) * dm_ref[j]
        state = s_ref[j]
        o = _dot(att.astype(BF16), v) + _dot(qb, state.astype(BF16)) * qd_ref[j]
        s_ref[j] = state * cd_ref[j] + _dot_tn((k * kd_ref[j]).astype(BF16), v)
        o = o * lax.rsqrt(jnp.mean(o * o, axis=-1, keepdims=True) + EPS)
        o_ref[:, vs] = (o * _silu(g_ref[:, vs])).astype(o_ref.dtype)


def retention(proj, state0, q0, nb, t, state_shape=None):
    state_shape = state0.shape if state0 is not None else state_shape
    _, nh, dk, dv = state_shape
    cs = R_CHUNK if t % R_CHUNK == 0 else t
    n = t // cs
    half = dk // 2
    pos = (q0 + jnp.arange(t)).astype(F32)
    inv = jnp.power(ROPE_BASE, -jnp.arange(half, dtype=F32) / half)
    ang = pos[:, None] * inv[None, :]
    cos, sin = jnp.cos(ang), jnp.sin(ang)
    log_g = jnp.log1p(-jnp.exp2(-5.0 - jnp.arange(nh, dtype=F32)))
    ii = jnp.arange(cs, dtype=F32)
    rel = ii[:, None] - ii[None, :]
    causal = rel >= 0
    dmat = jnp.where(causal, jnp.exp(jnp.where(causal, rel, 0.0) * log_g[:, None, None]), 0.0)
    q_dec = jnp.exp((ii + 1.0) * log_g[:, None])[:, :, None]
    k_dec = jnp.exp((cs - 1.0 - ii) * log_g[:, None])[:, :, None]
    c_dec = jnp.exp(cs * log_g)[:, None, None]
    hps = RET_HEADS_PER_STEP
    assert nh % hps == 0
    ng = nh // hps
    vb0 = 2 * nh * dk // (hps * dv)
    return pl.pallas_call(
        functools.partial(_retention_kernel, dk=dk, dv=dv, hps=hps),
        grid=(nb, ng, n),
        in_specs=[
            pl.BlockSpec((cs, hps * dk), lambda b, h, c: (b * n + c, h)),
            pl.BlockSpec((cs, hps * dk), lambda b, h, c: (b * n + c, ng + h)),
            pl.BlockSpec((cs, hps * dv), lambda b, h, c: (b * n + c, vb0 + h)),
            pl.BlockSpec((cs, hps * dv), lambda b, h, c: (b * n + c, vb0 + ng + h)),
            pl.BlockSpec((cs, half), lambda b, h, c: (c, 0)),
            pl.BlockSpec((cs, half), lambda b, h, c: (c, 0)),
            pl.BlockSpec((hps, cs, cs), lambda b, h, c: (h, 0, 0)),
            pl.BlockSpec((hps, cs, 1), lambda b, h, c: (h, 0, 0)),
            pl.BlockSpec((hps, cs, 1), lambda b, h, c: (h, 0, 0)),
            pl.BlockSpec((hps, 1, 1), lambda b, h, c: (h, 0, 0)),
        ] + ([] if state0 is None else [pl.BlockSpec((None, hps, dk, dv), lambda b, h, c: (b, h, 0, 0))]),
        out_specs=[
            pl.BlockSpec((cs, hps * dv), lambda b, h, c: (b * n + c, h)),
            pl.BlockSpec((None, hps, dk, dv), lambda b, h, c: (b, h, 0, 0)),
        ],
        out_shape=[
            jax.ShapeDtypeStruct((nb * t, nh * dv), BF16 if cs % 16 == 0 else F32),
            jax.ShapeDtypeStruct(state_shape, F32),
        ],
        compiler_params=_cparams(("arbitrary", "arbitrary", "arbitrary")),
        name="retention",
    )(proj, proj, proj, proj, cos, sin, dmat, q_dec, k_dec, c_dec, *([] if state0 is None else [state0]))


def _t5_bucket_np(dist):
    n = np.maximum(dist, 0)
    exact = N_BUCKETS // 2
    logv = np.log(np.maximum(n, 1).astype(np.float32) / np.float32(exact)) / np.float32(
        math.log(MAX_DISTANCE / exact))
    large = np.minimum(exact + (logv * np.float32(N_BUCKETS - exact)).astype(np.int32), N_BUCKETS - 1)
    return np.where(n < exact, n, large).astype(np.int32)


def _bucket_starts():
    bk = _t5_bucket_np(np.arange(4 * MAX_DISTANCE))
    assert np.all(np.diff(bk) >= 0) and bk[-1] == N_BUCKETS - 1
    return [int(np.argmax(bk >= b)) for b in range(N_BUCKETS)]


_BUCKET_START = _bucket_starts()


def _sel_matrix(n_cmp, n_cmp_pad, n_sel_pad):
    c = np.arange(n_cmp_pad)[:, None]
    j = np.arange(n_sel_pad)[None, :]
    a = (c >= SEL_RATIO * j) & (c < SEL_RATIO * j + SEL_RATIO)
    b = (c >= SEL_RATIO * j - 1) & (c < SEL_RATIO * j + SEL_RATIO - 1)
    m = (a.astype(np.float32) + b.astype(np.float32)) * (c < n_cmp)
    return jnp.asarray(m, BF16)


def _expand_matrix(n_blk_pad, n_keys):
    blk = np.arange(n_blk_pad)[:, None]
    key = np.arange(n_keys)[None, :]
    return jnp.asarray((key // SEL_BLOCK == blk).astype(np.float32), BF16)


def _w1_pairs(w1):
    w = w1.reshape(2, 2, CMP_STRIDE // 2, 2, HD, HD)
    w = jnp.transpose(w, (2, 3, 4, 0, 1, 5))
    return w.reshape(CMP_STRIDE // 2, 2 * HD, 4 * HD).astype(BF16)


def kernel(x_prompt, x_sample, c_prompt, c_sample, cache_nsa_kv, cache_nsa_win, state_ret, page_table,
           norm_g, ada_w, ada_b, rel_bias, a_w_in, a_w_out, a_cmp_pe, a_cmp_w1, a_cmp_w2, a_qk_g,
           r_w_in, r_w_out):
    nbp, t, d = x_prompt.shape
    nbs, ts, _ = x_sample.shape
    n_tab = page_table.shape[1]
    past_len = n_tab * PAGE
    kvw = A_KV * HD
    qw = A_HEADS * HD
    assert t % (2 * LANE) == 0 and t >= WINDOW and ts == 8 and past_len >= WINDOW
    assert cache_nsa_win.shape[2] == WINDOW

    n_c = nbp + nbs
    pad_c = (-n_c) % 8
    c_all = jnp.concatenate([c_prompt, c_sample, jnp.zeros((pad_c, d), F32)], axis=0)
    mod = ada_modulation(c_all, ada_w, ada_b).reshape(ada_w.shape[0], n_c + pad_c, 3, d)

    def mods(layer):
        m = mod[layer]
        return (m[:nbp, 0], m[:nbp, 1], m[:nbp, 2]), (m[nbp:n_c, 0], m[nbp:n_c, 1], m[nbp:n_c, 2])

    (sh_p, sc_p, gt_p), (sh_s, sc_s, gt_s) = mods(0)
    hp = norm_modulate(x_prompt, norm_g[0], sc_p, sh_p)
    hs = norm_modulate(x_sample, norm_g[0], sc_s, sh_s)
    w_in_t = jnp.swapaxes(a_w_in[0], 0, 1)
    n_main = w_in_t.shape[0] - 3 * A_HEADS
    n_qkv = qw + 6 * kvw
    proj_p, proj_s = project(hp, hs, w_in_t, n_cols=n_qkv, w_is_nk=True)
    zt_p, z_s = project_gate_t(hp, hs, w_in_t, n_qkv, n_main - n_qkv)
    w_gate = jnp.pad(w_in_t[n_main:], ((0, LANE - 3 * A_HEADS), (0, 0)))
    gl_p, gl_s = project(hp, hs, w_gate, w_is_nk=True)
    gl_p, gl_s = gl_p[:, :3 * A_HEADS], gl_s[:, :3 * A_HEADS]
    qk_g = a_qk_g[0]

    qn_p, rows_p, win_p, rows_t_p, win_t_p = nsa_prep(proj_p, qk_g)
    qn_s, rows_s, _, rows_t_s, win_t_s = nsa_prep(proj_s, qk_g)

    w_pairs = _w1_pairs(a_cmp_w1[0])
    pe = a_cmp_pe[0]
    w1 = a_cmp_w1[0].reshape(2, CMP_LEN, HD, HD)
    w2 = a_cmp_w2[0]
    cpp = PAGE // CMP_STRIDE

    pages_p = t // PAGE
    ident = jnp.arange(nbp * pages_p, dtype=jnp.int32).reshape(nbp, pages_p)
    ab_p = compress_stage_a(rows_t_p.reshape(nbp * pages_p, PAGE, 4 * A_KV, HD), ident, w_pairs,
                            min(PAGES_PER_STEP, pages_p))
    zero_tail = jnp.zeros((nbp, cpp, NCK, 2 * HD), F32)
    cmp_p = compress_stage_b(ab_p, zero_tail, pe, w1, w2, qk_g)

    nq = t // Q_BLOCK
    n_ch_p = t // CMP_STRIDE
    tsel, twin, tcmp = prompt_bias_tables(rel_bias, nq, n_ch_p)
    n_cmp_p = t // CMP_STRIDE - 1
    n_sel_p = -(-t // SEL_BLOCK)
    n_sel_pad_p = -(-n_sel_p // 16) * 16
    msel_p = _sel_matrix(n_cmp_p, n_ch_p, n_sel_pad_p).T
    expand_p = _expand_matrix(n_sel_pad_p, t).T
    gates_g = jnp.transpose(gl_p.reshape(nbp * t, 3, A_KV, A_HPG), (2, 1, 3, 0)).reshape(A_KV, 3 * A_HPG, nbp * t)
    mixed_p = nsa_prompt_attention(qn_p, cmp_p, rows_p, win_p, zt_p, gates_g,
                                   tsel, twin, tcmp, msel_p, expand_p, nbp, t)

    n_pool = cache_nsa_kv.shape[1]
    cache4d = cache_nsa_kv[0].reshape(n_pool, PAGE, 4 * A_KV, HD)
    n_pages = min(PAGES_PER_STEP, n_tab)
    ab_s = compress_stage_a(cache4d, page_table, w_pairs, n_pages)
    tail_rows = jnp.pad(rows_s.reshape(nbs, ts, 4 * kvw), ((0, 0), (0, PAGE - ts), (0, 0)))
    ident_s = jnp.arange(nbs, dtype=jnp.int32).reshape(nbs, 1)
    ab_tail = compress_stage_a(tail_rows.reshape(nbs, PAGE, 4 * A_KV, HD), ident_s, w_pairs, 1)
    cmp_s = compress_stage_b(ab_s, ab_tail, pe, w1, w2, qk_g)

    n_cmp_s = cmp_s.shape[1] // NCK
    n_steps = n_tab // n_pages
    bps = n_pages * PAGE // SEL_BLOCK
    n_sel_pad = (n_steps + 1) * bps
    nkeys = (n_steps + 1) * n_pages * PAGE
    nkw = WINDOW + LANE
    b_cmp, b_sel, b_win = sample_bias_rows(rel_bias, past_len, ts, n_cmp_s, nkeys, nkw)
    msel_s = _sel_matrix(n_cmp_s, n_cmp_s, n_sel_pad)
    expand_s = _expand_matrix(bps, n_pages * PAGE)
    q5 = qn_s.reshape(nbs, ts, A_KV, A_HPG, HD)
    eye = jnp.eye(A_KV, dtype=BF16)
    qbd = jnp.einsum('bqghd,ge->bghqed', q5, eye).reshape(nbs, A_HEADS * ts, kvw)
    win_new = win_t_s.reshape(nbs, ts, 2 * A_KV, HD)
    win_cache = cache_nsa_win[0].reshape(nbs, WINDOW, 2 * A_KV, HD)
    win_all = jnp.concatenate([win_cache, win_new], axis=1)
    win_pad = jnp.pad(win_all, ((0, 0), (0, nkw - WINDOW - ts), (0, 0), (0, 0)))
    tail_sel = tail_rows[:, :, 2 * kvw:]
    mixed_s = nsa_sample_attention(qbd, cmp_s, cache4d, page_table, tail_sel, win_pad, z_s, gl_s,
                                   b_cmp, b_sel, b_win, msel_s, expand_s, past_len, ts).astype(BF16)

    xs_flat = x_sample.reshape(nbs * ts, d)
    gate_s_rows = jnp.repeat(gt_s, ts, axis=0)
    x1p, x1s = project_residual(mixed_p, mixed_s, a_w_out[0], x_prompt.reshape(nbp * t, d), gt_p,
                                xs_flat, gate_s_rows, t)

    (sh_p, sc_p, gt_p), (sh_s, sc_s, gt_s) = mods(1)
    hp = norm_modulate(x1p.reshape(nbp, t, d), norm_g[1], sc_p, sh_p)
    hs = norm_modulate(x1s.reshape(nbs, ts, d), norm_g[1], sc_s, sh_s)
    rp, rs = project(hp, hs, r_w_in[0])
    op, ret_p = retention(rp, None, 0, nbp, t, state_shape=(nbp,) + state_ret.shape[2:])
    os_, ret_s = retention(rs, state_ret[0], past_len, nbs, ts)
    os_ = os_.astype(BF16)
    gate_s_rows = jnp.repeat(gt_s, ts, axis=0)
    x2p, x2s = project_residual(op, os_, r_w_out[0], x1p, gt_p, x1s, gate_s_rows, t)

    kv_p = rows_t_p.reshape(1, nbp, t, 4, A_KV, HD)
    kv_s = rows_t_s.reshape(1, nbs, ts, 4, A_KV, HD)
    wst_p = win_t_p.reshape(nbp, t, 2, A_KV, HD)[None, :, t - WINDOW:]
    wst_s = win_all[:, ts:].reshape(1, nbs, WINDOW, 2, A_KV, HD)
    return (x2p.reshape(nbp, t, d), x2s.reshape(nbs, ts, d), kv_p, kv_s, wst_p, wst_s,
            ret_p[None], ret_s[None])
```
